```python
import math
import jax, jax.numpy as jnp
from jax import lax
import numpy as np

D_MODEL = 1024
BATCH = 8
SEQ = 4096
DEPTH = 4

CHUNK = 64
Q_BLOCK = 128
EPS = 1e-6

SSM_WIDTH = 256
SSM_GROUP = 16
N_SSM_GROUPS = SSM_WIDTH // SSM_GROUP
SSM_STATE = 64
DT_MIN = 1e-3
DT_MAX = 1e-1

MLA_HEADS = 6
MLA_Q_RANK = 256
MLA_KV_RANK = 128
MLA_NOPE = 64
MLA_ROPE = 32
MLA_V = 64
MLA_QK = MLA_NOPE + MLA_ROPE
MLA_WIDTH = MLA_HEADS * MLA_V
ROPE_BASE = 10000.0

FOX_HEADS = 6
FOX_HEAD_DIM = 64
FOX_WIDTH = FOX_HEADS * FOX_HEAD_DIM

D_MIX = SSM_WIDTH + MLA_WIDTH + FOX_WIDTH
IN_SPLITS = (
    SSM_WIDTH,
    SSM_WIDTH + MLA_Q_RANK,
    SSM_WIDTH + MLA_Q_RANK + MLA_KV_RANK,
    SSM_WIDTH + MLA_Q_RANK + MLA_KV_RANK + MLA_ROPE,
    SSM_WIDTH + MLA_Q_RANK + MLA_KV_RANK + MLA_ROPE + FOX_WIDTH,
    SSM_WIDTH + MLA_Q_RANK + MLA_KV_RANK + MLA_ROPE + 2 * FOX_WIDTH,
    SSM_WIDTH + MLA_Q_RANK + MLA_KV_RANK + MLA_ROPE + 3 * FOX_WIDTH,
)
IN_COLS = SSM_WIDTH + MLA_Q_RANK + MLA_KV_RANK + MLA_ROPE + 3 * FOX_WIDTH + FOX_HEADS

D_FF = 2816
N_EXPERTS = 8
TOP_K = 2
D_FF_EXPERT = 1408
N_DENSE = (DEPTH + 1) // 2
N_MOE = DEPTH // 2

kernel_name = "hybrid_s5_mla_fox_moe_encoder"


def rms_norm(x, g):
    xf = x.astype(jnp.float32)
    y = xf * lax.rsqrt(jnp.mean(xf * xf, axis=-1, keepdims=True) + EPS)
    return (y * g.astype(jnp.float32)).astype(x.dtype)


def modulate(h, shift, scale):
    return h * (1.0 + scale[:, None, :]) + shift[:, None, :]


def rope_tables(positions):
    half = MLA_ROPE // 2
    inv = ROPE_BASE ** (-jnp.arange(half, dtype=jnp.float32) / half)
    ang = positions.astype(jnp.float32)[..., None] * inv
    return jnp.cos(ang), jnp.sin(ang)


def apply_rope(x, cos, sin):
    half = x.shape[-1] // 2
    x1 = x[..., :half].astype(jnp.float32)
    x2 = x[..., half:].astype(jnp.float32)
    cs = cos[:, :, None, :]
    sn = sin[:, :, None, :]
    return jnp.concatenate([x1 * cs - x2 * sn, x1 * sn + x2 * cs], axis=-1).astype(x.dtype)


def chunk_causal(t_pos, s_pos):
    return (s_pos // CHUNK) <= (t_pos // CHUNK)


def frame_causal(t_pos, s_pos):
    return s_pos <= t_pos


def block_sweep(q, k, v, scale, mask_fn, log_decay=None):
    L = q.shape[2]
    outs = []
    for qb in range(L // Q_BLOCK):
        q0, q1 = qb * Q_BLOCK, (qb + 1) * Q_BLOCK
        s = jnp.einsum("bhqd,bhkd->bhqk", q[:, :, q0:q1], k[:, :, :q1]).astype(jnp.float32) * scale
        if log_decay is not None:
            s = s + log_decay[:, :, q0:q1, None] - log_decay[:, :, None, :q1]
        t_pos = jnp.arange(q0, q1)[:, None]
        s_pos = jnp.arange(q1)[None, :]
        s = jnp.where(mask_fn(t_pos, s_pos), s, -jnp.inf)
        p = jax.nn.softmax(s, axis=-1)
        outs.append(jnp.einsum("bhqk,bhkd->bhqd", p.astype(v.dtype), v[:, :, :q1]))
    return jnp.concatenate(outs, axis=2)


def s5_mixer(u, lam_re, lam_im, log_dt, b_re, b_im, c_re, c_im, d_skip, w_glu, b_glu):
    Bn, L, _ = u.shape
    f32 = jnp.float32
    uf = u.astype(f32)
    ug = uf.reshape(Bn, L, N_SSM_GROUPS, SSM_GROUP)
    lam = lax.complex(lam_re.astype(f32), lam_im.astype(f32))
    dt = jnp.exp(log_dt.astype(f32))[:, None]
    lam_bar = jnp.exp(lam * dt)
    b = lax.complex(b_re.astype(f32), b_im.astype(f32))
    b_bar = ((lam_bar - 1.0) / lam)[..., None] * b
    bu = jnp.einsum("gpc,blgc->blgp", b_bar, ug)
    a = jnp.broadcast_to(lam_bar, bu.shape)

    def combine(left, right):
        a_l, b_l = left
        a_r, b_r = right
        return a_r * a_l, a_r * b_l + b_r

    _, states = lax.associative_scan(combine, (a, bu), axis=1)
    cmat = lax.complex(c_re.astype(f32), c_im.astype(f32))
    y = jnp.real(jnp.einsum("gcp,blgp->blgc", cmat, states)).reshape(Bn, L, SSM_WIDTH)
    y = jax.nn.gelu(y + d_skip.astype(f32) * uf)
    out = y * jax.nn.sigmoid(y @ w_glu.astype(f32) + b_glu.astype(f32))
    return out.astype(u.dtype)


def hybrid_mixer(h, cos, sin, w_in, lam_re, lam_im, log_dt, b_re, b_im, c_re, c_im, d_skip,
                 w_glu, b_glu, q_norm, kv_norm, w_uq, w_ukv, mla_gq, mla_gk,
                 fox_bf, fox_gq, fox_gk, out_norm, w_out):
    Bn, L, _ = h.shape
    proj = h @ w_in
    u, cq, ckv, kr, fq, fk, fv, fg = jnp.split(proj, IN_SPLITS, axis=-1)

    o_ssm = s5_mixer(u, lam_re, lam_im, log_dt, b_re, b_im, c_re, c_im, d_skip, w_glu, b_glu)

    q = (rms_norm(cq, q_norm) @ w_uq).reshape(Bn, L, MLA_HEADS, MLA_QK)
    kv = (rms_norm(ckv, kv_norm) @ w_ukv).reshape(Bn, L, MLA_HEADS, MLA_NOPE + MLA_V)
    k_nope, v_mla = kv[..., :MLA_NOPE], kv[..., MLA_NOPE:]
    k_rope = jnp.broadcast_to(kr[:, :, None, :], (Bn, L, MLA_HEADS, MLA_ROPE))
    k = jnp.concatenate([k_nope, k_rope], axis=-1)
    q = rms_norm(q, mla_gq)
    k = rms_norm(k, mla_gk)
    q = jnp.concatenate([q[..., :MLA_NOPE], apply_rope(q[..., MLA_NOPE:], cos, sin)], axis=-1)
    k = jnp.concatenate([k[..., :MLA_NOPE], apply_rope(k[..., MLA_NOPE:], cos, sin)], axis=-1)
    o_mla = block_sweep(q.transpose(0, 2, 1, 3), k.transpose(0, 2, 1, 3), v_mla.transpose(0, 2, 1, 3),
                        1.0 / math.sqrt(MLA_QK), chunk_causal)
    o_mla = o_mla.transpose(0, 2, 1, 3).reshape(Bn, L, MLA_WIDTH)

    fqh = rms_norm(fq.reshape(Bn, L, FOX_HEADS, FOX_HEAD_DIM), fox_gq)
    fkh = rms_norm(fk.reshape(Bn, L, FOX_HEADS, FOX_HEAD_DIM), fox_gk)
    fvh = fv.reshape(Bn, L, FOX_HEADS, FOX_HEAD_DIM)
    log_f = jax.nn.log_sigmoid(fg.astype(jnp.float32) + fox_bf.astype(jnp.float32))
    cum_log_f = jnp.cumsum(log_f, axis=1).transpose(0, 2, 1)
    o_fox = block_sweep(fqh.transpose(0, 2, 1, 3), fkh.transpose(0, 2, 1, 3), fvh.transpose(0, 2, 1, 3),
                        1.0 / math.sqrt(FOX_HEAD_DIM), frame_causal, log_decay=cum_log_f)
    o_fox = o_fox.transpose(0, 2, 1, 3).reshape(Bn, L, FOX_WIDTH)

    e1 = SSM_WIDTH
    e2 = SSM_WIDTH + MLA_WIDTH
    merged = jnp.concatenate([
        rms_norm(o_ssm, out_norm[:e1]),
        rms_norm(o_mla.astype(h.dtype), out_norm[e1:e2]),
        rms_norm(o_fox.astype(h.dtype), out_norm[e2:]),
    ], axis=-1)
    return merged @ w_out


def swiglu(t, w_gate, w_up, w_down):
    return (jax.nn.silu(t @ w_gate) * (t @ w_up)) @ w_down


def moe_ffn(h, w_router, b_router, w_gate, w_up, w_down):
    Bn, L, D = h.shape
    t = h.reshape(Bn * L, D)
    logits = (t @ w_router).astype(jnp.float32) + b_router.astype(jnp.float32)
    top_v, top_i = lax.top_k(logits, TOP_K)
    p = jax.nn.softmax(top_v, axis=-1)
    combine = jnp.sum(jax.nn.one_hot(top_i, N_EXPERTS, dtype=jnp.float32) * p[..., None], axis=1)
    out = jnp.zeros_like(t)
    for e in range(N_EXPERTS):
        out = out + combine[:, e:e + 1].astype(t.dtype) * swiglu(t, w_gate[e], w_up[e], w_down[e])
    return out.reshape(Bn, L, D)


def setup_inputs(seed: int = 0) -> dict:
    key = jax.random.key(seed)
    keys = iter(jax.random.split(key, 48))
    f32 = jnp.float32

    def nrm(shape, std):
        return std * jax.random.normal(next(keys), shape, f32)

    def gain(shape):
        return 1.0 + nrm(shape, 0.02)

    x = jax.random.normal(next(keys), (BATCH, SEQ, D_MODEL), f32)
    c = jax.random.normal(next(keys), (BATCH, D_MODEL), f32)
    offset = jax.random.randint(next(keys), (BATCH,), 0, 64, jnp.int32) * CHUNK
    positions = (offset[:, None] + jnp.arange(SEQ, dtype=jnp.int32)[None, :]).astype(jnp.int32)

    G, P = N_SSM_GROUPS, SSM_STATE
    lam_im_base = jnp.pi * jnp.arange(P, dtype=f32)
    log_dt = jax.random.uniform(next(keys), (DEPTH, G), f32, math.log(DT_MIN), math.log(DT_MAX))

    return {
        "x": x,
        "c": c,
        "positions": positions,
        "norm_mix": gain((DEPTH, D_MODEL)),
        "norm_ffn": gain((DEPTH, D_MODEL)),
        "w_ada": nrm((DEPTH, D_MODEL, 6 * D_MODEL), 0.5 * D_MODEL ** -0.5),
        "b_ada": nrm((DEPTH, 6 * D_MODEL), 0.02),
        "w_in": nrm((DEPTH, D_MODEL, IN_COLS), D_MODEL ** -0.5),
        "ssm_lam_re": -0.5 + nrm((DEPTH, G, P), 0.01),
        "ssm_lam_im": lam_im_base + nrm((DEPTH, G, P), 0.01),
        "ssm_log_dt": log_dt,
        "ssm_b_re": nrm((DEPTH, G, P, SSM_GROUP), (2 * SSM_GROUP) ** -0.5),
        "ssm_b_im": nrm((DEPTH, G, P, SSM_GROUP), (2 * SSM_GROUP) ** -0.5),
        "ssm_c_re": nrm((DEPTH, G, SSM_GROUP, P), 0.5),
        "ssm_c_im": nrm((DEPTH, G, SSM_GROUP, P), 0.5),
        "ssm_d": nrm((DEPTH, SSM_WIDTH), 1.0),
        "ssm_w_glu": nrm((DEPTH, SSM_WIDTH, SSM_WIDTH), SSM_WIDTH ** -0.5),
        "ssm_b_glu": nrm((DEPTH, SSM_WIDTH), 0.02),
        "mla_q_norm": gain((DEPTH, MLA_Q_RANK)),
        "mla_kv_norm": gain((DEPTH, MLA_KV_RANK)),
        "mla_w_uq": nrm((DEPTH, MLA_Q_RANK, MLA_HEADS * MLA_QK), MLA_Q_RANK ** -0.5),
        "mla_w_ukv": nrm((DEPTH, MLA_KV_RANK, MLA_HEADS * (MLA_NOPE + MLA_V)), MLA_KV_RANK ** -0.5),
        "mla_qk_gq": gain((DEPTH, MLA_QK)),
        "mla_qk_gk": gain((DEPTH, MLA_QK)),
        "fox_b_f": 3.0 + nrm((DEPTH, FOX_HEADS), 0.5),
        "fox_qk_gq": gain((DEPTH, FOX_HEAD_DIM)),
        "fox_qk_gk": gain((DEPTH, FOX_HEAD_DIM)),
        "out_norm": gain((DEPTH, D_MIX)),
        "w_out": nrm((DEPTH, D_MIX, D_MODEL), D_MIX ** -0.5),
        "ffn_w_gate": nrm((N_DENSE, D_MODEL, D_FF), D_MODEL ** -0.5),
        "ffn_w_up": nrm((N_DENSE, D_MODEL, D_FF), D_MODEL ** -0.5),
        "ffn_w_down": nrm((N_DENSE, D_FF, D_MODEL), D_FF ** -0.5),
        "moe_w_router": nrm((N_MOE, D_MODEL, N_EXPERTS), D_MODEL ** -0.5),
        "moe_b_router": nrm((N_MOE, N_EXPERTS), 0.01),
        "moe_w_gate": nrm((N_MOE, N_EXPERTS, D_MODEL, D_FF_EXPERT), D_MODEL ** -0.5),
        "moe_w_up": nrm((N_MOE, N_EXPERTS, D_MODEL, D_FF_EXPERT), D_MODEL ** -0.5),
        "moe_w_down": nrm((N_MOE, N_EXPERTS, D_FF_EXPERT, D_MODEL), D_FF_EXPERT ** -0.5),
    }


def reference(x, c, positions, norm_mix, norm_ffn, w_ada, b_ada, w_in,
              ssm_lam_re, ssm_lam_im, ssm_log_dt, ssm_b_re, ssm_b_im, ssm_c_re, ssm_c_im,
              ssm_d, ssm_w_glu, ssm_b_glu,
              mla_q_norm, mla_kv_norm, mla_w_uq, mla_w_ukv, mla_qk_gq, mla_qk_gk,
              fox_b_f, fox_qk_gq, fox_qk_gk, out_norm, w_out,
              ffn_w_gate, ffn_w_up, ffn_w_down,
              moe_w_router, moe_b_router, moe_w_gate, moe_w_up, moe_w_down):
    cos, sin = rope_tables(positions)
    c_act = jax.nn.silu(c)
    for i in range(DEPTH):
        ada = c_act @ w_ada[i] + b_ada[i]
        sh1, sc1, g1, sh2, sc2, g2 = jnp.split(ada, 6, axis=-1)

        h = modulate(rms_norm(x, norm_mix[i]), sh1, sc1)
        mix = hybrid_mixer(h, cos, sin, w_in[i],
                           ssm_lam_re[i], ssm_lam_im[i], ssm_log_dt[i], ssm_b_re[i], ssm_b_im[i],
                           ssm_c_re[i], ssm_c_im[i], ssm_d[i], ssm_w_glu[i], ssm_b_glu[i],
                           mla_q_norm[i], mla_kv_norm[i], mla_w_uq[i], mla_w_ukv[i],
                           mla_qk_gq[i], mla_qk_gk[i],
                           fox_b_f[i], fox_qk_gq[i], fox_qk_gk[i], out_norm[i], w_out[i])
        x = x + g1[:, None, :] * mix

        h = modulate(rms_norm(x, norm_ffn[i]), sh2, sc2)
        j = i // 2
        if i % 2 == 0:
            ff = swiglu(h, ffn_w_gate[j], ffn_w_up[j], ffn_w_down[j])
        else:
            ff = moe_ffn(h, moe_w_router[j], moe_b_router[j], moe_w_gate[j], moe_w_up[j], moe_w_down[j])
        x = x + g2[:, None, :] * ff
    return x
```

```python
import functools
import math

import jax
import jax.numpy as jnp
from jax import lax
from jax.experimental import pallas as pl
from jax.experimental.pallas import tpu as pltpu

F32 = jnp.float32
BF16 = jnp.bfloat16

D_MODEL = 1024
BATCH = 8
SEQ = 4096
DEPTH = 4
CHUNK = 64
EPS = 1e-6

SSM_WIDTH = 256
SSM_GROUP = 16
N_SSM_GROUPS = 16
SSM_STATE = 64
N_STATE = N_SSM_GROUPS * SSM_STATE

MLA_HEADS = 6
MLA_Q_RANK = 256
MLA_KV_RANK = 128
MLA_NOPE = 64
MLA_ROPE = 32
MLA_V = 64
MLA_QK = 96
ROPE_BASE = 10000.0

FOX_HEADS = 6
FOX_HEAD_DIM = 64
ATT_WIDTH = 384

D_FF = 2816
N_EXPERTS = 8
D_FF_EXPERT = 1408

LANE = 128
SUBLANE = 8
HEAD_PAD = LANE
ONES_LANE = 64
NEG = -1e30

IN_PAD = 1920
KR_LANE = 64

ROW_TILE = 512
S5_STEPS = 64
ATT_TILE = 512
FF_CHUNK = 1408
VMEM_LIMIT = 56 * 1024 * 1024


def _params(sem):
    return pltpu.CompilerParams(dimension_semantics=sem, vmem_limit_bytes=VMEM_LIMIT)


def _rms_mod(x, g, sc, sh):
    ms = jnp.mean(x * x, axis=-1, keepdims=True)
    h = x * lax.rsqrt(ms + EPS) * g
    return h * (1.0 + sc) + sh


def _split3(x):
    hi = x.astype(BF16).astype(F32)
    r = x - hi
    mid = r.astype(BF16).astype(F32)
    lo = (r - mid).astype(BF16).astype(F32)
    return hi, mid, lo


def _ada_kernel(c_ref, w_ref, b_ref, o_ref):
    c = c_ref[...]
    ca = (c * jax.nn.sigmoid(c)).astype(BF16)
    o_ref[0] = jnp.dot(ca, w_ref[0].astype(BF16), preferred_element_type=F32) + b_ref[0]


def _ada_call(c, w_ada, b_ada):
    tn = 1536
    return pl.pallas_call(
        _ada_kernel,
        grid=(DEPTH, 6 * D_MODEL // tn),
        in_specs=[pl.BlockSpec((BATCH, D_MODEL), lambda i, j: (0, 0)),
                  pl.BlockSpec((1, D_MODEL, tn), lambda i, j: (i, 0, j)),
                  pl.BlockSpec((1, 1, tn), lambda i, j: (i, 0, j))],
        out_specs=pl.BlockSpec((1, BATCH, tn), lambda i, j: (i, 0, j)),
        out_shape=jax.ShapeDtypeStruct((DEPTH, BATCH, 6 * D_MODEL), F32),
        compiler_params=_params(("arbitrary", "arbitrary")),
        name="ada",
    )(c, w_ada, b_ada.reshape(DEPTH, 1, 6 * D_MODEL))


_IN_GROUPS = ((0, 256), (256, 512), (512, 640), (640, 768), (768, 1152), (1152, 1536), (1536, 1920))


def _inproj_kernel(x_ref, g_ref, sh_ref, sc_ref, w_ref, *out_refs):
    h = _rms_mod(x_ref[0], g_ref[...], sc_ref[0], sh_ref[0]).astype(BF16)
    for ref, (c0, c1) in zip(out_refs, _IN_GROUPS):
        ref[0] = jnp.dot(h, w_ref[:, c0:c1], preferred_element_type=F32)


def _inproj_call(x, g, sh, sc, w):
    tm = ROW_TILE
    row = lambda b, i: (b, i, 0)
    per_b = lambda b, i: (b, 0, 0)
    const = lambda b, i: (0, 0)
    widths = [c1 - c0 for c0, c1 in _IN_GROUPS]
    return pl.pallas_call(
        _inproj_kernel,
        grid=(BATCH, SEQ // tm),
        in_specs=[pl.BlockSpec((1, tm, D_MODEL), row),
                  pl.BlockSpec((1, D_MODEL), const),
                  pl.BlockSpec((1, 1, D_MODEL), per_b),
                  pl.BlockSpec((1, 1, D_MODEL), per_b),
                  pl.BlockSpec((D_MODEL, IN_PAD), const)],
        out_specs=[pl.BlockSpec((1, tm, wd), row) for wd in widths],
        out_shape=[jax.ShapeDtypeStruct((BATCH, SEQ, wd), F32) for wd in widths],
        compiler_params=_params(("arbitrary", "arbitrary")),
        name="inproj",
    )(x, g, sh, sc, w)


def _pack_w_in(w):
    u, cq, ckv, kr, fq, fk, fv, fg = jnp.split(
        w, (256, 512, 640, 672, 1056, 1440, 1824), axis=1)
    z = lambda n: jnp.zeros((D_MODEL, n), w.dtype)
    krfg = jnp.concatenate([fg, z(KR_LANE - FOX_HEADS), kr, z(LANE - KR_LANE - MLA_ROPE)], axis=1)
    return jnp.concatenate([u, cq, ckv, krfg, fq, fk, fv], axis=1).astype(BF16)


def _s5_kernel(u_ref, bmat_ref, lam_ref, cmat_ref, d_ref, wglu_ref, bglu_ref, gn_ref,
               o_ref, bu_ref, state_ref, *, steps):
    @pl.when(pl.program_id(0) == 0)
    def _():
        state_ref[...] = jnp.zeros_like(state_ref)

    u = u_ref[...]
    bu_ref[...] = jnp.dot(u.astype(BF16), bmat_ref[...], preferred_element_type=F32)
    lr = jnp.broadcast_to(lam_ref[0:1, :], (SUBLANE, N_STATE))
    li = jnp.broadcast_to(lam_ref[1:2, :], (SUBLANE, N_STATE))

    def step(t, carry):
        sr, si = carry
        r0 = pl.multiple_of(t * SUBLANE, SUBLANE)
        nr = lr * sr - li * si + bu_ref[pl.ds(r0, SUBLANE), 0:N_STATE]
        ni = lr * si + li * sr + bu_ref[pl.ds(r0, SUBLANE), N_STATE:2 * N_STATE]
        bu_ref[pl.ds(r0, SUBLANE), 0:N_STATE] = nr
        bu_ref[pl.ds(r0, SUBLANE), N_STATE:2 * N_STATE] = ni
        return nr, ni

    sr, si = lax.fori_loop(0, steps, step,
                           (state_ref[:, 0:N_STATE], state_ref[:, N_STATE:2 * N_STATE]))
    state_ref[:, 0:N_STATE] = sr
    state_ref[:, N_STATE:2 * N_STATE] = si

    y = jnp.dot(bu_ref[...].astype(BF16), cmat_ref[...], preferred_element_type=F32)
    y = jax.nn.gelu(y + d_ref[...] * u)
    gate = jnp.dot(y.astype(BF16), wglu_ref[...], preferred_element_type=F32) + bglu_ref[...]
    o = y * jax.nn.sigmoid(gate)
    ms = jnp.mean(o * o, axis=-1, keepdims=True)
    o_ref[...] = o * lax.rsqrt(ms + EPS) * gn_ref[...]


def _s5_call(u_t, bmat, lam, cmat, d_skip, wglu, bglu, gn):
    rows = S5_STEPS * BATCH
    const = lambda i: (0, 0)
    return pl.pallas_call(
        functools.partial(_s5_kernel, steps=S5_STEPS),
        grid=(SEQ // S5_STEPS,),
        in_specs=[pl.BlockSpec((rows, SSM_WIDTH), lambda i: (i, 0)),
                  pl.BlockSpec((SSM_WIDTH, 2 * N_STATE), const),
                  pl.BlockSpec((2, N_STATE), const),
                  pl.BlockSpec((2 * N_STATE, SSM_WIDTH), const),
                  pl.BlockSpec((1, SSM_WIDTH), const),
                  pl.BlockSpec((SSM_WIDTH, SSM_WIDTH), const),
                  pl.BlockSpec((1, SSM_WIDTH), const),
                  pl.BlockSpec((1, SSM_WIDTH), const)],
        out_specs=pl.BlockSpec((rows, SSM_WIDTH), lambda i: (i, 0)),
        out_shape=jax.ShapeDtypeStruct((SEQ * BATCH, SSM_WIDTH), F32),
        scratch_shapes=[pltpu.VMEM((rows, 2 * N_STATE), F32),
                        pltpu.VMEM((SUBLANE, 2 * N_STATE), F32)],
        compiler_params=_params(("arbitrary",)),
        name="s5",
    )(u_t, bmat, lam, cmat, d_skip, wglu, bglu, gn)


def _s5_operands(lam_re, lam_im, log_dt, b_re, b_im, c_re, c_im):
    dt = jnp.exp(log_dt)[:, None]
    mag = jnp.exp(lam_re * dt)
    lb_re = mag * jnp.cos(lam_im * dt)
    lb_im = mag * jnp.sin(lam_im * dt)
    den = lam_re * lam_re + lam_im * lam_im
    co_re = ((lb_re - 1.0) * lam_re + lb_im * lam_im) / den
    co_im = (lb_im * lam_re - (lb_re - 1.0) * lam_im) / den
    bb_re = co_re[..., None] * b_re - co_im[..., None] * b_im
    bb_im = co_re[..., None] * b_im + co_im[..., None] * b_re
    eye = jnp.eye(N_SSM_GROUPS, dtype=F32)
    blk_b = lambda m: jnp.einsum("gpc,gh->gchp", m, eye).reshape(SSM_WIDTH, N_STATE)
    bmat = jnp.concatenate([blk_b(bb_re), blk_b(bb_im)], axis=1).astype(BF16)
    blk_c = lambda m: jnp.einsum("gcp,gh->gphc", m, eye).reshape(N_STATE, SSM_WIDTH)
    cmat = jnp.concatenate([blk_c(c_re), -blk_c(c_im)], axis=0).astype(BF16)
    lam = jnp.stack([lb_re.reshape(N_STATE), lb_im.reshape(N_STATE)], axis=0)
    return bmat, lam, cmat


def _rope(x, cos, sina, sinb):
    return x * cos + pltpu.roll(x, LANE - 16, 1) * sina + pltpu.roll(x, 16, 1) * sinb


def _rope_tables(positions):
    half = MLA_ROPE // 2
    inv = ROPE_BASE ** (-jnp.arange(half, dtype=F32) / half)
    ang = positions.astype(F32)[..., None] * inv
    cos, sin = jnp.cos(ang), jnp.sin(ang)
    shp = positions.shape
    one = lambda n: jnp.ones(shp + (n,), F32)
    zero = lambda n: jnp.zeros(shp + (n,), F32)
    cos_t = jnp.concatenate([one(MLA_NOPE), cos, cos, zero(LANE - MLA_QK)], axis=-1)
    sina_t = jnp.concatenate([zero(MLA_NOPE), -sin, zero(LANE - MLA_NOPE - half)], axis=-1)
    sinb_t = jnp.concatenate([zero(MLA_NOPE + half), sin, zero(LANE - MLA_QK)], axis=-1)
    return cos_t, sina_t, sinb_t


def _mla_prep_kernel(cq_ref, ckv_ref, krfg_ref, cos_ref, sina_ref, sinb_ref,
                     qn_ref, kvn_ref, wq_ref, wk_ref, wv_ref, gq_ref, gk_ref,
                     q_ref, k_ref, v_ref):
    tl = cq_ref.shape[1]
    lane = lax.broadcasted_iota(jnp.int32, (tl, LANE), 1)
    cos, sina, sinb = cos_ref[0], sina_ref[0], sinb_ref[0]

    cq = cq_ref[0]
    cqn = (cq * lax.rsqrt(jnp.mean(cq * cq, axis=-1, keepdims=True) + EPS) * qn_ref[...]).astype(BF16)
    ckv = ckv_ref[0]
    ckvn = (ckv * lax.rsqrt(jnp.mean(ckv * ckv, axis=-1, keepdims=True) + EPS) * kvn_ref[...]).astype(BF16)
    kr = jnp.where((lane >= KR_LANE) & (lane < KR_LANE + MLA_ROPE), krfg_ref[0], 0.0)

    for h in range(MLA_HEADS):
        q = jnp.dot(cqn, wq_ref[h], preferred_element_type=F32)
        q = q * lax.rsqrt(jnp.sum(q * q, axis=-1, keepdims=True) / MLA_QK + EPS) * gq_ref[...]
        q = _rope(q, cos, sina, sinb) * (1.0 / math.sqrt(MLA_QK))
        q_ref[0, h] = q.astype(BF16)

        k = jnp.dot(ckvn, wk_ref[h], preferred_element_type=F32) + kr
        k = k * lax.rsqrt(jnp.sum(k * k, axis=-1, keepdims=True) / MLA_QK + EPS) * gk_ref[...]
        k_ref[0, h] = _rope(k, cos, sina, sinb).astype(BF16)

        v = jnp.dot(ckvn, wv_ref[h], preferred_element_type=F32)
        v_ref[0, h] = jnp.where(lane == ONES_LANE, 1.0, v).astype(BF16)


def _mla_prep_call(cq, ckv, krfg, tabs, qn, kvn, wq, wk, wv, gq, gk):
    tl = ROW_TILE
    row = lambda b, i: (b, i, 0)
    c2 = lambda b, i: (0, 0)
    c3 = lambda b, i: (0, 0, 0)
    head_out = pl.BlockSpec((1, MLA_HEADS, tl, HEAD_PAD), lambda b, i: (b, 0, i, 0))
    head_shape = jax.ShapeDtypeStruct((BATCH, MLA_HEADS, SEQ, HEAD_PAD), BF16)
    return pl.pallas_call(
        _mla_prep_kernel,
        grid=(BATCH, SEQ // tl),
        in_specs=[pl.BlockSpec((1, tl, MLA_Q_RANK), row),
                  pl.BlockSpec((1, tl, MLA_KV_RANK), row),
                  pl.BlockSpec((1, tl, LANE), row),
                  pl.BlockSpec((1, tl, LANE), row),
                  pl.BlockSpec((1, tl, LANE), row),
                  pl.BlockSpec((1, tl, LANE), row),
                  pl.BlockSpec((1, MLA_Q_RANK), c2),
                  pl.BlockSpec((1, MLA_KV_RANK), c2),
                  pl.BlockSpec((MLA_HEADS, MLA_Q_RANK, HEAD_PAD), c3),
                  pl.BlockSpec((MLA_HEADS, MLA_KV_RANK, HEAD_PAD), c3),
                  pl.BlockSpec((MLA_HEADS, MLA_KV_RANK, HEAD_PAD), c3),
                  pl.BlockSpec((1, HEAD_PAD), c2),
                  pl.BlockSpec((1, HEAD_PAD), c2)],
        out_specs=[head_out, head_out, head_out],
        out_shape=[head_shape, head_shape, head_shape],
        compiler_params=_params(("arbitrary", "arbitrary")),
        name="mla_prep",
    )(cq, ckv, krfg, *tabs, qn, kvn, wq, wk, wv, gq, gk)


def _pad_lanes(a, n=HEAD_PAD):
    return jnp.pad(a, [(0, 0)] * (a.ndim - 1) + [(0, n - a.shape[-1])])


def _mla_weights(w_uq, w_ukv, gq, gk):
    wq = _pad_lanes(w_uq.reshape(MLA_Q_RANK, MLA_HEADS, MLA_QK).transpose(1, 0, 2)).astype(BF16)
    wkv = w_ukv.reshape(MLA_KV_RANK, MLA_HEADS, MLA_NOPE + MLA_V).transpose(1, 0, 2)
    wk = _pad_lanes(wkv[..., :MLA_NOPE]).astype(BF16)
    wv = _pad_lanes(wkv[..., MLA_NOPE:]).astype(BF16)
    return wq, wk, wv, _pad_lanes(gq[None, :]), _pad_lanes(gk[None, :])


def _fox_prep_kernel(fq_ref, fk_ref, fv_ref, krfg_ref, bf_ref, gq_ref, gk_ref,
                     q_ref, k_ref, v_ref, carry_ref):
    tl = fq_ref.shape[1]
    lane = lax.broadcasted_iota(jnp.int32, (tl, LANE), 1)

    @pl.when(pl.program_id(1) == 0)
    def _():
        carry_ref[...] = jnp.zeros_like(carry_ref)

    logf = jax.nn.log_sigmoid(krfg_ref[0] + bf_ref[...])
    logf = jnp.where(lane < FOX_HEADS, logf, 0.0)
    r_i = lax.broadcasted_iota(jnp.int32, (tl, tl), 0)
    c_i = lax.broadcasted_iota(jnp.int32, (tl, tl), 1)
    tri = jnp.where(c_i <= r_i, 1.0, 0.0).astype(BF16)
    cum = carry_ref[0:1, :]
    for piece in _split3(logf):
        cum = cum + jnp.dot(tri, piece.astype(BF16), preferred_element_type=F32)
    carry_ref[0:1, :] = cum[tl - 1:tl, :]

    def head_lanes(ref, h):
        x = ref[0, :, (h // 2) * LANE:(h // 2 + 1) * LANE]
        if h % 2:
            x = pltpu.roll(x, FOX_HEAD_DIM, 1)
        return jnp.where(lane < FOX_HEAD_DIM, x, 0.0)

    def normed(x, g):
        return x * lax.rsqrt(jnp.sum(x * x, axis=-1, keepdims=True) / FOX_HEAD_DIM + EPS) * g

    for h in range(FOX_HEADS):
        c_hi, c_mid, c_lo = _split3(cum[:, h:h + 1])
        q = normed(head_lanes(fq_ref, h), gq_ref[...]) * (1.0 / math.sqrt(FOX_HEAD_DIM))
        q = jnp.where(lane == 64, c_hi, q)
        q = jnp.where(lane == 65, c_mid, q)
        q = jnp.where(lane == 66, c_lo, q)
        q = jnp.where((lane >= 67) & (lane < 70), 1.0, q)
        q_ref[0, h] = q.astype(BF16)

        k = normed(head_lanes(fk_ref, h), gk_ref[...])
        k = jnp.where((lane >= 64) & (lane < 67), 1.0, k)
        k = jnp.where(lane == 67, -c_hi, k)
        k = jnp.where(lane == 68, -c_mid, k)
        k = jnp.where(lane == 69, -c_lo, k)
        k_ref[0, h] = k.astype(BF16)

        v = head_lanes(fv_ref, h)
        v_ref[0, h] = jnp.where(lane == ONES_LANE, 1.0, v).astype(BF16)


def _fox_prep_call(fq, fk, fv, krfg, bf, gq, gk):
    tl = ROW_TILE
    row = lambda b, i: (b, i, 0)
    c2 = lambda b, i: (0, 0)
    head_out = pl.BlockSpec((1, FOX_HEADS, tl, HEAD_PAD), lambda b, i: (b, 0, i, 0))
    head_shape = jax.ShapeDtypeStruct((BATCH, FOX_HEADS, SEQ, HEAD_PAD), BF16)
    return pl.pallas_call(
        _fox_prep_kernel,
        grid=(BATCH, SEQ // tl),
        in_specs=[pl.BlockSpec((1, tl, ATT_WIDTH), row),
                  pl.BlockSpec((1, tl, ATT_WIDTH), row),
                  pl.BlockSpec((1, tl, ATT_WIDTH), row),
                  pl.BlockSpec((1, tl, LANE), row),
                  pl.BlockSpec((1, LANE), c2),
                  pl.BlockSpec((1, LANE), c2),
                  pl.BlockSpec((1, LANE), c2)],
        out_specs=[head_out, head_out, head_out],
        out_shape=[head_shape, head_shape, head_shape],
        scratch_shapes=[pltpu.VMEM((SUBLANE, LANE), F32)],
        compiler_params=_params(("arbitrary", "arbitrary")),
        name="fox_prep",
    )(fq, fk, fv, krfg, bf, gq, gk)


def _flash_kernel(q_ref, k_ref, v_ref, o_ref, *, tile, chunk):
    qi = pl.program_id(2)
    r_i = lax.broadcasted_iota(jnp.int32, (tile, tile), 0)
    c_i = lax.broadcasted_iota(jnp.int32, (tile, tile), 1)
    diag_mask = (c_i // chunk) <= (r_i // chunk)
    lane = lax.broadcasted_iota(jnp.int32, (tile, HEAD_PAD), 1)

    outs = []
    for hh in range(2):
        q = q_ref[0, hh]

        def block(j, carry, masked):
            m, acc = carry
            k0 = pl.multiple_of(j * tile, tile)
            k = k_ref[0, hh, pl.ds(k0, tile), :]
            v = v_ref[0, hh, pl.ds(k0, tile), :]
            s = lax.dot_general(q, k, (((1,), (1,)), ((), ())), preferred_element_type=F32)
            if masked:
                s = jnp.where(diag_mask, s, NEG)
            m_new = jnp.maximum(m, jnp.max(s, axis=1, keepdims=True))
            p = jnp.exp(s - m_new)
            acc = jnp.exp(m - m_new) * acc + jnp.dot(p.astype(BF16), v, preferred_element_type=F32)
            return m_new, acc

        init = (jnp.full((tile, 1), NEG, F32), jnp.zeros((tile, HEAD_PAD), F32))
        carry = lax.fori_loop(0, qi, functools.partial(block, masked=False), init)
        _, acc = block(qi, carry, True)
        outs.append(acc / acc[:, ONES_LANE:ONES_LANE + 1])

    o_ref[0] = jnp.where(lane < 64, outs[0], pltpu.roll(outs[1], 64, 1))


def _flash_call(q, k, v, chunk):
    tile = ATT_TILE
    heads = q.shape[1]
    kv_spec = pl.BlockSpec((1, 2, SEQ, HEAD_PAD), lambda b, hp, i: (b, hp, 0, 0))
    return pl.pallas_call(
        functools.partial(_flash_kernel, tile=tile, chunk=chunk),
        grid=(BATCH, heads // 2, SEQ // tile),
        in_specs=[pl.BlockSpec((1, 2, tile, HEAD_PAD), lambda b, hp, i: (b, hp, i, 0)),
                  kv_spec, kv_spec],
        out_specs=pl.BlockSpec((1, tile, LANE), lambda b, hp, i: (b, i, hp)),
        out_shape=jax.ShapeDtypeStruct((BATCH, SEQ, ATT_WIDTH), F32),
        compiler_params=_params(("arbitrary", "arbitrary", "arbitrary")),
        name="flash_chunk%d" % chunk,
    )(q, k, v)


def _merge_kernel(ssm_ref, mla_ref, fox_ref, x_ref, g1_ref, gm_ref, gf_ref,
                  ws_ref, wm_ref, wf_ref, o_ref):
    def normed(ref, g_ref):
        a = ref[0]
        return (a * lax.rsqrt(jnp.mean(a * a, axis=-1, keepdims=True) + EPS) * g_ref[...]).astype(BF16)

    mix = jnp.dot(ssm_ref[0].astype(BF16), ws_ref[...], preferred_element_type=F32)
    mix = mix + jnp.dot(normed(mla_ref, gm_ref), wm_ref[...], preferred_element_type=F32)
    mix = mix + jnp.dot(normed(fox_ref, gf_ref), wf_ref[...], preferred_element_type=F32)
    o_ref[0] = x_ref[0] + g1_ref[0] * mix


def _merge_call(o_ssm, o_mla, o_fox, x, g1, gm, gf, ws, wm, wf):
    tm = ROW_TILE
    row = lambda b, i: (b, i, 0)
    c2 = lambda b, i: (0, 0)
    return pl.pallas_call(
        _merge_kernel,
        grid=(BATCH, SEQ // tm),
        in_specs=[pl.BlockSpec((1, tm, SSM_WIDTH), row),
                  pl.BlockSpec((1, tm, ATT_WIDTH), row),
                  pl.BlockSpec((1, tm, ATT_WIDTH), row),
                  pl.BlockSpec((1, tm, D_MODEL), row),
                  pl.BlockSpec((1, 1, D_MODEL), lambda b, i: (b, 0, 0)),
                  pl.BlockSpec((1, ATT_WIDTH), c2),
                  pl.BlockSpec((1, ATT_WIDTH), c2),
                  pl.BlockSpec((SSM_WIDTH, D_MODEL), c2),
                  pl.BlockSpec((ATT_WIDTH, D_MODEL), c2),
                  pl.BlockSpec((ATT_WIDTH, D_MODEL), c2)],
        out_specs=pl.BlockSpec((1, tm, D_MODEL), row),
        out_shape=jax.ShapeDtypeStruct((BATCH, SEQ, D_MODEL), F32),
        compiler_params=_params(("arbitrary", "arbitrary")),
        name="merge",
    )(o_ssm, o_mla, o_fox, x, g1, gm, gf, ws, wm, wf)


def _ffn_kernel(*refs, routed):
    if routed:
        x_ref, g_ref, sh_ref, sc_ref, g2_ref, comb_ref, wg_ref, wu_ref, wd_ref, o_ref, h_ref, acc_ref = refs
    else:
        x_ref, g_ref, sh_ref, sc_ref, g2_ref, wg_ref, wu_ref, wd_ref, o_ref, h_ref, acc_ref = refs
    c = pl.program_id(2)

    @pl.when(c == 0)
    def _():
        h_ref[...] = _rms_mod(x_ref[0], g_ref[...], sc_ref[0], sh_ref[0]).astype(BF16)
        acc_ref[...] = jnp.zeros_like(acc_ref)

    h = h_ref[...]
    gate = jnp.dot(h, wg_ref[0], preferred_element_type=F32)
    up = jnp.dot(h, wu_ref[0], preferred_element_type=F32)
    a = (gate * jax.nn.sigmoid(gate) * up).astype(BF16)
    y = jnp.dot(a, wd_ref[0], preferred_element_type=F32)
    if routed:
        lane = lax.broadcasted_iota(jnp.int32, comb_ref.shape[1:], 1)
        y = y * jnp.sum(jnp.where(lane == c, comb_ref[0], 0.0), axis=-1, keepdims=True)
    acc_ref[...] += y

    @pl.when(c == pl.num_programs(2) - 1)
    def _():
        o_ref[0] = x_ref[0] + g2_ref[0] * acc_ref[...]


def _ffn_call(x, g, sh, sc, g2, wg, wu, wd, comb=None):
    tm = 1024
    n_chunks = wg.shape[0]
    row = lambda b, i, c: (b, i, 0)
    per_b = lambda b, i, c: (b, 0, 0)
    routed = comb is not None
    in_specs = [pl.BlockSpec((1, tm, D_MODEL), row),
                pl.BlockSpec((1, D_MODEL), lambda b, i, c: (0, 0)),
                pl.BlockSpec((1, 1, D_MODEL), per_b),
                pl.BlockSpec((1, 1, D_MODEL), per_b),
                pl.BlockSpec((1, 1, D_MODEL), per_b)]
    args = [x, g, sh, sc, g2]
    if routed:
        in_specs.append(pl.BlockSpec((1, tm, LANE), row))
        args.append(comb)
    in_specs += [pl.BlockSpec((1, D_MODEL, FF_CHUNK), lambda b, i, c: (c, 0, 0)),
                 pl.BlockSpec((1, D_MODEL, FF_CHUNK), lambda b, i, c: (c, 0, 0)),
                 pl.BlockSpec((1, FF_CHUNK, D_MODEL), lambda b, i, c: (c, 0, 0))]
    args += [wg, wu, wd]
    return pl.pallas_call(
        functools.partial(_ffn_kernel, routed=routed),
        grid=(BATCH, SEQ // tm, n_chunks),
        in_specs=in_specs,
        out_specs=pl.BlockSpec((1, tm, D_MODEL), row),
        out_shape=jax.ShapeDtypeStruct((BATCH, SEQ, D_MODEL), F32),
        scratch_shapes=[pltpu.VMEM((tm, D_MODEL), BF16), pltpu.VMEM((tm, D_MODEL), F32)],
        compiler_params=_params(("arbitrary", "arbitrary", "arbitrary")),
        name="ffn_routed" if routed else "ffn_dense",
    )(*args)


def _router_kernel(x_ref, g_ref, sh_ref, sc_ref, w_ref, b_ref, comb_ref):
    h = _rms_mod(x_ref[0], g_ref[...], sc_ref[0], sh_ref[0])
    h_hi = h.astype(BF16)
    h_lo = (h - h_hi.astype(F32)).astype(BF16)
    w_hi, w_lo = w_ref[0], w_ref[1]
    logits = (jnp.dot(h_hi, w_hi, preferred_element_type=F32)
              + jnp.dot(h_lo, w_hi, preferred_element_type=F32)
              + jnp.dot(h_hi, w_lo, preferred_element_type=F32)) + b_ref[...]
    lane = lax.broadcasted_iota(jnp.int32, logits.shape, 1)
    logits = jnp.where(lane < N_EXPERTS, logits, -jnp.inf)
    m1 = jnp.max(logits, axis=-1, keepdims=True)
    i1 = jnp.min(jnp.where(logits == m1, lane, LANE), axis=-1, keepdims=True)
    rest = jnp.where(lane == i1, -jnp.inf, logits)
    m2 = jnp.max(rest, axis=-1, keepdims=True)
    i2 = jnp.min(jnp.where(rest == m2, lane, LANE), axis=-1, keepdims=True)
    e = jnp.exp(m2 - m1)
    p1 = 1.0 / (1.0 + e)
    comb_ref[0] = jnp.where(lane == i1, p1, 0.0) + jnp.where(lane == i2, e * p1, 0.0)


def _router_call(x, g, sh, sc, w, b):
    tm = ROW_TILE
    row = lambda b_, i: (b_, i, 0)
    per_b = lambda b_, i: (b_, 0, 0)
    return pl.pallas_call(
        _router_kernel,
        grid=(BATCH, SEQ // tm),
        in_specs=[pl.BlockSpec((1, tm, D_MODEL), row),
                  pl.BlockSpec((1, D_MODEL), lambda b_, i: (0, 0)),
                  pl.BlockSpec((1, 1, D_MODEL), per_b),
                  pl.BlockSpec((1, 1, D_MODEL), per_b),
                  pl.BlockSpec((2, D_MODEL, LANE), lambda b_, i: (0, 0, 0)),
                  pl.BlockSpec((1, LANE), lambda b_, i: (0, 0))],
        out_specs=pl.BlockSpec((1, tm, LANE), row),
        out_shape=jax.ShapeDtypeStruct((BATCH, SEQ, LANE), F32),
        compiler_params=_params(("arbitrary", "arbitrary")),
        name="router",
    )(x, g, sh, sc, w, b)


def kernel(x, c, positions, norm_mix, norm_ffn, w_ada, b_ada, w_in, ssm_lam_re, ssm_lam_im, ssm_log_dt, ssm_b_re, ssm_b_im, ssm_c_re, ssm_c_im, ssm_d, ssm_w_glu, ssm_b_glu, mla_q_norm, mla_kv_norm, mla_w_uq, mla_w_ukv, mla_qk_gq, mla_qk_gk, fox_b_f, fox_qk_gq, fox_qk_gk, out_norm, w_out, ffn_w_gate, ffn_w_up, ffn_w_down, moe_w_router, moe_b_router, moe_w_gate, moe_w_up, moe_w_down):
    tabs = _rope_tables(positions)
    ada = _ada_call(c, w_ada, b_ada)
    ada = ada.reshape(DEPTH, BATCH, 6, 1, D_MODEL)
    row2 = lambda a: a[None, :]

    for i in range(DEPTH):
        sh1, sc1, g1, sh2, sc2, g2 = (ada[i, :, n] for n in range(6))

        u, cq, ckv, krfg, fq, fk, fv = _inproj_call(x, row2(norm_mix[i]), sh1, sc1, _pack_w_in(w_in[i]))

        bmat, lam, cmat = _s5_operands(ssm_lam_re[i], ssm_lam_im[i], ssm_log_dt[i],
                                       ssm_b_re[i], ssm_b_im[i], ssm_c_re[i], ssm_c_im[i])
        u_t = u.transpose(1, 0, 2).reshape(SEQ * BATCH, SSM_WIDTH)
        o_ssm = _s5_call(u_t, bmat, lam, cmat, row2(ssm_d[i]), ssm_w_glu[i].astype(BF16),
                         row2(ssm_b_glu[i]), row2(out_norm[i, :SSM_WIDTH]))
        o_ssm = o_ssm.reshape(SEQ, BATCH, SSM_WIDTH).transpose(1, 0, 2)

        wq, wk, wv, gq, gk = _mla_weights(mla_w_uq[i], mla_w_ukv[i], mla_qk_gq[i], mla_qk_gk[i])
        mq, mk, mv = _mla_prep_call(cq, ckv, krfg, tabs, row2(mla_q_norm[i]), row2(mla_kv_norm[i]),
                                    wq, wk, wv, gq, gk)
        o_mla = _flash_call(mq, mk, mv, CHUNK)

        xq, xk, xv = _fox_prep_call(fq, fk, fv, krfg, _pad_lanes(row2(fox_b_f[i]), LANE),
                                    _pad_lanes(row2(fox_qk_gq[i]), LANE), _pad_lanes(row2(fox_qk_gk[i]), LANE))
        o_fox = _flash_call(xq, xk, xv, 1)

        e1, e2 = SSM_WIDTH, SSM_WIDTH + ATT_WIDTH
        wo = w_out[i].astype(BF16)
        x = _merge_call(o_ssm, o_mla, o_fox, x, g1, row2(out_norm[i, e1:e2]), row2(out_norm[i, e2:]),
                        wo[:e1], wo[e1:e2], wo[e2:])

        j = i // 2
        if i % 2 == 0:
            split = lambda w: w.reshape(D_MODEL, D_FF // FF_CHUNK, FF_CHUNK).transpose(1, 0, 2).astype(BF16)
            wd = ffn_w_down[j].reshape(D_FF // FF_CHUNK, FF_CHUNK, D_MODEL).astype(BF16)
            x = _ffn_call(x, row2(norm_ffn[i]), sh2, sc2, g2, split(ffn_w_gate[j]), split(ffn_w_up[j]), wd)
        else:
            wr = _pad_lanes(moe_w_router[j], LANE)
            wr_hi = wr.astype(BF16)
            wr_lo = (wr - wr_hi.astype(F32)).astype(BF16)
            comb = _router_call(x, row2(norm_ffn[i]), sh2, sc2, jnp.stack([wr_hi, wr_lo]),
                                _pad_lanes(row2(moe_b_router[j]), LANE))
            x = _ffn_call(x, row2(norm_ffn[i]), sh2, sc2, g2, moe_w_gate[j].astype(BF16),
                          moe_w_up[j].astype(BF16), moe_w_down[j].astype(BF16), comb=comb)
    return x
```

```python
import functools
import math

import jax
import jax.numpy as jnp
from jax import lax
from jax.experimental import pallas as pl
from jax.experimental.pallas import tpu as pltpu

F32 = jnp.float32
BF16 = jnp.bfloat16

D_MODEL = 1024
BATCH = 8
SEQ = 4096
DEPTH = 4
CHUNK = 64
EPS = 1e-6

SSM_WIDTH = 256
SSM_GROUP = 16
N_SSM_GROUPS = 16
SSM_STATE = 64
N_STATE = N_SSM_GROUPS * SSM_STATE

MLA_HEADS = 6
MLA_Q_RANK = 256
MLA_KV_RANK = 128
MLA_NOPE = 64
MLA_ROPE = 32
MLA_V = 64
MLA_QK = 96
ROPE_BASE = 10000.0

FOX_HEADS = 6
FOX_HEAD_DIM = 64
ATT_WIDTH = 384

D_FF = 2816
N_EXPERTS = 8
D_FF_EXPERT = 1408

LANE = 128
SUBLANE = 8
HEAD_PAD = LANE
ONES_LANE = 64
NEG = -1e30

IN_PAD = 1920
KR_LANE = 64

ROW_TILE = 512
S5_STEPS = 64
ATT_TILE = 512
LOG2E = math.log2(math.e)
FF_CHUNK = 1408
VMEM_LIMIT = 56 * 1024 * 1024


def _params(sem):
    return pltpu.CompilerParams(dimension_semantics=sem, vmem_limit_bytes=VMEM_LIMIT)


def _rms_mod(x, g, sc, sh):
    ms = jnp.mean(x * x, axis=-1, keepdims=True)
    h = x * lax.rsqrt(ms + EPS) * g
    return h * (1.0 + sc) + sh


def _split3(x):
    hi = x.astype(BF16).astype(F32)
    r = x - hi
    mid = r.astype(BF16).astype(F32)
    lo = (r - mid).astype(BF16).astype(F32)
    return hi, mid, lo


def _ada_kernel(c_ref, w_ref, b_ref, o_ref):
    c = c_ref[...]
    ca = (c * jax.nn.sigmoid(c)).astype(BF16)
    o_ref[0] = jnp.dot(ca, w_ref[0].astype(BF16), preferred_element_type=F32) + b_ref[0]


def _ada_call(c, w_ada, b_ada):
    tn = 1536
    return pl.pallas_call(
        _ada_kernel,
        grid=(DEPTH, 6 * D_MODEL // tn),
        in_specs=[pl.BlockSpec((BATCH, D_MODEL), lambda i, j: (0, 0)),
                  pl.BlockSpec((1, D_MODEL, tn), lambda i, j: (i, 0, j)),
                  pl.BlockSpec((1, 1, tn), lambda i, j: (i, 0, j))],
        out_specs=pl.BlockSpec((1, BATCH, tn), lambda i, j: (i, 0, j)),
        out_shape=jax.ShapeDtypeStruct((DEPTH, BATCH, 6 * D_MODEL), F32),
        compiler_params=_params(("arbitrary", "arbitrary")),
        name="ada",
    )(c, w_ada, b_ada.reshape(DEPTH, 1, 6 * D_MODEL))


_IN_GROUPS = ((0, 256), (256, 512), (512, 640), (640, 768), (768, 1152), (1152, 1536), (1536, 1920))


def _inproj_kernel(x_ref, g_ref, sh_ref, sc_ref, w_ref, *out_refs):
    h = _rms_mod(x_ref[0], g_ref[...], sc_ref[0], sh_ref[0]).astype(BF16)
    for ref, (c0, c1) in zip(out_refs, _IN_GROUPS):
        ref[0] = jnp.dot(h, w_ref[:, c0:c1], preferred_element_type=F32)


def _inproj_call(x, g, sh, sc, w):
    tm = ROW_TILE
    row = lambda b, i: (b, i, 0)
    per_b = lambda b, i: (b, 0, 0)
    const = lambda b, i: (0, 0)
    widths = [c1 - c0 for c0, c1 in _IN_GROUPS]
    return pl.pallas_call(
        _inproj_kernel,
        grid=(BATCH, SEQ // tm),
        in_specs=[pl.BlockSpec((1, tm, D_MODEL), row),
                  pl.BlockSpec((1, D_MODEL), const),
                  pl.BlockSpec((1, 1, D_MODEL), per_b),
                  pl.BlockSpec((1, 1, D_MODEL), per_b),
                  pl.BlockSpec((D_MODEL, IN_PAD), const)],
        out_specs=[pl.BlockSpec((1, tm, wd), row) for wd in widths],
        out_shape=[jax.ShapeDtypeStruct((BATCH, SEQ, wd), F32) for wd in widths],
        compiler_params=_params(("arbitrary", "arbitrary")),
        name="inproj",
    )(x, g, sh, sc, w)


def _pack_w_in(w):
    u, cq, ckv, kr, fq, fk, fv, fg = jnp.split(
        w, (256, 512, 640, 672, 1056, 1440, 1824), axis=1)
    z = lambda n: jnp.zeros((D_MODEL, n), w.dtype)
    krfg = jnp.concatenate([fg, z(KR_LANE - FOX_HEADS), kr, z(LANE - KR_LANE - MLA_ROPE)], axis=1)
    return jnp.concatenate([u, cq, ckv, krfg, fq, fk, fv], axis=1).astype(BF16)


def _s5_kernel(u_ref, bmat_ref, lam_ref, cmat_ref, d_ref, wglu_ref, bglu_ref, gn_ref,
               o_ref, bu_ref, state_ref, *, steps):
    @pl.when(pl.program_id(0) == 0)
    def _():
        state_ref[...] = jnp.zeros_like(state_ref)

    u = u_ref[...]
    bu_ref[...] = jnp.dot(u.astype(BF16), bmat_ref[...], preferred_element_type=F32)
    lr = jnp.broadcast_to(lam_ref[0:1, :], (SUBLANE, N_STATE))
    li = jnp.broadcast_to(lam_ref[1:2, :], (SUBLANE, N_STATE))

    def step(t, carry):
        sr, si = carry
        r0 = pl.multiple_of(t * SUBLANE, SUBLANE)
        nr = lr * sr - li * si + bu_ref[pl.ds(r0, SUBLANE), 0:N_STATE]
        ni = lr * si + li * sr + bu_ref[pl.ds(r0, SUBLANE), N_STATE:2 * N_STATE]
        bu_ref[pl.ds(r0, SUBLANE), 0:N_STATE] = nr
        bu_ref[pl.ds(r0, SUBLANE), N_STATE:2 * N_STATE] = ni
        return nr, ni

    sr, si = lax.fori_loop(0, steps, step,
                           (state_ref[:, 0:N_STATE], state_ref[:, N_STATE:2 * N_STATE]))
    state_ref[:, 0:N_STATE] = sr
    state_ref[:, N_STATE:2 * N_STATE] = si

    y = jnp.dot(bu_ref[...].astype(BF16), cmat_ref[...], preferred_element_type=F32)
    y = jax.nn.gelu(y + d_ref[...] * u)
    gate = jnp.dot(y.astype(BF16), wglu_ref[...], preferred_element_type=F32) + bglu_ref[...]
    o = y * jax.nn.sigmoid(gate)
    ms = jnp.mean(o * o, axis=-1, keepdims=True)
    o_ref[...] = o * lax.rsqrt(ms + EPS) * gn_ref[...]


def _s5_call(u_t, bmat, lam, cmat, d_skip, wglu, bglu, gn):
    rows = S5_STEPS * BATCH
    const = lambda i: (0, 0)
    return pl.pallas_call(
        functools.partial(_s5_kernel, steps=S5_STEPS),
        grid=(SEQ // S5_STEPS,),
        in_specs=[pl.BlockSpec((rows, SSM_WIDTH), lambda i: (i, 0)),
                  pl.BlockSpec((SSM_WIDTH, 2 * N_STATE), const),
                  pl.BlockSpec((2, N_STATE), const),
                  pl.BlockSpec((2 * N_STATE, SSM_WIDTH), const),
                  pl.BlockSpec((1, SSM_WIDTH), const),
                  pl.BlockSpec((SSM_WIDTH, SSM_WIDTH), const),
                  pl.BlockSpec((1, SSM_WIDTH), const),
                  pl.BlockSpec((1, SSM_WIDTH), const)],
        out_specs=pl.BlockSpec((rows, SSM_WIDTH), lambda i: (i, 0)),
        out_shape=jax.ShapeDtypeStruct((SEQ * BATCH, SSM_WIDTH), F32),
        scratch_shapes=[pltpu.VMEM((rows, 2 * N_STATE), F32),
                        pltpu.VMEM((SUBLANE, 2 * N_STATE), F32)],
        compiler_params=_params(("arbitrary",)),
        name="s5",
    )(u_t, bmat, lam, cmat, d_skip, wglu, bglu, gn)


def _s5_operands(lam_re, lam_im, log_dt, b_re, b_im, c_re, c_im):
    dt = jnp.exp(log_dt)[:, None]
    mag = jnp.exp(lam_re * dt)
    lb_re = mag * jnp.cos(lam_im * dt)
    lb_im = mag * jnp.sin(lam_im * dt)
    den = lam_re * lam_re + lam_im * lam_im
    co_re = ((lb_re - 1.0) * lam_re + lb_im * lam_im) / den
    co_im = (lb_im * lam_re - (lb_re - 1.0) * lam_im) / den
    bb_re = co_re[..., None] * b_re - co_im[..., None] * b_im
    bb_im = co_re[..., None] * b_im + co_im[..., None] * b_re
    eye = jnp.eye(N_SSM_GROUPS, dtype=F32)
    blk_b = lambda m: jnp.einsum("gpc,gh->gchp", m, eye).reshape(SSM_WIDTH, N_STATE)
    bmat = jnp.concatenate([blk_b(bb_re), blk_b(bb_im)], axis=1).astype(BF16)
    blk_c = lambda m: jnp.einsum("gcp,gh->gphc", m, eye).reshape(N_STATE, SSM_WIDTH)
    cmat = jnp.concatenate([blk_c(c_re), -blk_c(c_im)], axis=0).astype(BF16)
    lam = jnp.stack([lb_re.reshape(N_STATE), lb_im.reshape(N_STATE)], axis=0)
    return bmat, lam, cmat


def _rope(x, cos, sina, sinb):
    return x * cos + pltpu.roll(x, LANE - 16, 1) * sina + pltpu.roll(x, 16, 1) * sinb


def _rope_tables(positions):
    half = MLA_ROPE // 2
    inv = ROPE_BASE ** (-jnp.arange(half, dtype=F32) / half)
    ang = positions.astype(F32)[..., None] * inv
    cos, sin = jnp.cos(ang), jnp.sin(ang)
    shp = positions.shape
    one = lambda n: jnp.ones(shp + (n,), F32)
    zero = lambda n: jnp.zeros(shp + (n,), F32)
    cos_t = jnp.concatenate([one(MLA_NOPE), cos, cos, zero(LANE - MLA_QK)], axis=-1)
    sina_t = jnp.concatenate([zero(MLA_NOPE), -sin, zero(LANE - MLA_NOPE - half)], axis=-1)
    sinb_t = jnp.concatenate([zero(MLA_NOPE + half), sin, zero(LANE - MLA_QK)], axis=-1)
    return cos_t, sina_t, sinb_t


def _store_key_blocks(kt_ref, h, k):
    kt = k.T
    for s in range(k.shape[0] // ATT_TILE):
        kt_ref[0, h, s] = kt[:, s * ATT_TILE:(s + 1) * ATT_TILE].astype(BF16)


_KT_SPEC = lambda heads, tl: pl.BlockSpec((1, heads, tl // ATT_TILE, HEAD_PAD, ATT_TILE),
                                          lambda b, i: (b, 0, i, 0, 0))
_KT_SHAPE = lambda heads: jax.ShapeDtypeStruct((BATCH, heads, SEQ // ATT_TILE, HEAD_PAD, ATT_TILE), BF16)


def _mla_prep_kernel(cq_ref, ckv_ref, krfg_ref, cos_ref, sina_ref, sinb_ref,
                     qn_ref, kvn_ref, wq_ref, wk_ref, wv_ref, gq_ref, gk_ref,
                     q_ref, k_ref, v_ref):
    tl = cq_ref.shape[1]
    lane = lax.broadcasted_iota(jnp.int32, (tl, LANE), 1)
    cos, sina, sinb = cos_ref[0], sina_ref[0], sinb_ref[0]

    cq = cq_ref[0]
    cqn = (cq * lax.rsqrt(jnp.mean(cq * cq, axis=-1, keepdims=True) + EPS) * qn_ref[...]).astype(BF16)
    ckv = ckv_ref[0]
    ckvn = (ckv * lax.rsqrt(jnp.mean(ckv * ckv, axis=-1, keepdims=True) + EPS) * kvn_ref[...]).astype(BF16)
    kr = jnp.where((lane >= KR_LANE) & (lane < KR_LANE + MLA_ROPE), krfg_ref[0], 0.0)

    for h in range(MLA_HEADS):
        q = jnp.dot(cqn, wq_ref[h], preferred_element_type=F32)
        q = q * lax.rsqrt(jnp.sum(q * q, axis=-1, keepdims=True) / MLA_QK + EPS) * gq_ref[...]
        q = _rope(q, cos, sina, sinb) * (LOG2E / math.sqrt(MLA_QK))
        q_ref[0, h] = q.astype(BF16)

        k = jnp.dot(ckvn, wk_ref[h], preferred_element_type=F32) + kr
        k = k * lax.rsqrt(jnp.sum(k * k, axis=-1, keepdims=True) / MLA_QK + EPS) * gk_ref[...]
        _store_key_blocks(k_ref, h, _rope(k, cos, sina, sinb))

        v = jnp.dot(ckvn, wv_ref[h], preferred_element_type=F32)
        v_ref[0, h] = jnp.where(lane == ONES_LANE, 1.0, v).astype(BF16)


def _mla_prep_call(cq, ckv, krfg, tabs, qn, kvn, wq, wk, wv, gq, gk):
    tl = ROW_TILE
    row = lambda b, i: (b, i, 0)
    c2 = lambda b, i: (0, 0)
    c3 = lambda b, i: (0, 0, 0)
    head_out = pl.BlockSpec((1, MLA_HEADS, tl, HEAD_PAD), lambda b, i: (b, 0, i, 0))
    head_shape = jax.ShapeDtypeStruct((BATCH, MLA_HEADS, SEQ, HEAD_PAD), BF16)
    return pl.pallas_call(
        _mla_prep_kernel,
        grid=(BATCH, SEQ // tl),
        in_specs=[pl.BlockSpec((1, tl, MLA_Q_RANK), row),
                  pl.BlockSpec((1, tl, MLA_KV_RANK), row),
                  pl.BlockSpec((1, tl, LANE), row),
                  pl.BlockSpec((1, tl, LANE), row),
                  pl.BlockSpec((1, tl, LANE), row),
                  pl.BlockSpec((1, tl, LANE), row),
                  pl.BlockSpec((1, MLA_Q_RANK), c2),
                  pl.BlockSpec((1, MLA_KV_RANK), c2),
                  pl.BlockSpec((MLA_HEADS, MLA_Q_RANK, HEAD_PAD), c3),
                  pl.BlockSpec((MLA_HEADS, MLA_KV_RANK, HEAD_PAD), c3),
                  pl.BlockSpec((MLA_HEADS, MLA_KV_RANK, HEAD_PAD), c3),
                  pl.BlockSpec((1, HEAD_PAD), c2),
                  pl.BlockSpec((1, HEAD_PAD), c2)],
        out_specs=[head_out, _KT_SPEC(MLA_HEADS, tl), head_out],
        out_shape=[head_shape, _KT_SHAPE(MLA_HEADS), head_shape],
        compiler_params=_params(("arbitrary", "arbitrary")),
        name="mla_prep",
    )(cq, ckv, krfg, *tabs, qn, kvn, wq, wk, wv, gq, gk)


def _pad_lanes(a, n=HEAD_PAD):
    return jnp.pad(a, [(0, 0)] * (a.ndim - 1) + [(0, n - a.shape[-1])])


def _mla_weights(w_uq, w_ukv, gq, gk):
    wq = _pad_lanes(w_uq.reshape(MLA_Q_RANK, MLA_HEADS, MLA_QK).transpose(1, 0, 2)).astype(BF16)
    wkv = w_ukv.reshape(MLA_KV_RANK, MLA_HEADS, MLA_NOPE + MLA_V).transpose(1, 0, 2)
    wk = _pad_lanes(wkv[..., :MLA_NOPE]).astype(BF16)
    wv = _pad_lanes(wkv[..., MLA_NOPE:]).astype(BF16)
    return wq, wk, wv, _pad_lanes(gq[None, :]), _pad_lanes(gk[None, :])


def _fox_prep_kernel(fq_ref, fk_ref, fv_ref, krfg_ref, bf_ref, gq_ref, gk_ref,
                     q_ref, k_ref, v_ref, carry_ref):
    tl = fq_ref.shape[1]
    lane = lax.broadcasted_iota(jnp.int32, (tl, LANE), 1)

    @pl.when(pl.program_id(1) == 0)
    def _():
        carry_ref[...] = jnp.zeros_like(carry_ref)

    logf = jax.nn.log_sigmoid(krfg_ref[0] + bf_ref[...])
    logf = jnp.where(lane < FOX_HEADS, logf, 0.0)
    r_i = lax.broadcasted_iota(jnp.int32, (tl, tl), 0)
    c_i = lax.broadcasted_iota(jnp.int32, (tl, tl), 1)
    tri = jnp.where(c_i <= r_i, 1.0, 0.0).astype(BF16)
    cum = carry_ref[0:1, :]
    for piece in _split3(logf):
        cum = cum + jnp.dot(tri, piece.astype(BF16), preferred_element_type=F32)
    carry_ref[0:1, :] = cum[tl - 1:tl, :]

    def head_lanes(ref, h):
        x = ref[0, :, (h // 2) * LANE:(h // 2 + 1) * LANE]
        if h % 2:
            x = pltpu.roll(x, FOX_HEAD_DIM, 1)
        return jnp.where(lane < FOX_HEAD_DIM, x, 0.0)

    def normed(x, g):
        return x * lax.rsqrt(jnp.sum(x * x, axis=-1, keepdims=True) / FOX_HEAD_DIM + EPS) * g

    for h in range(FOX_HEADS):
        c_hi, c_mid, c_lo = _split3(cum[:, h:h + 1] * LOG2E)
        q = normed(head_lanes(fq_ref, h), gq_ref[...]) * (LOG2E / math.sqrt(FOX_HEAD_DIM))
        q = jnp.where(lane == 64, c_hi, q)
        q = jnp.where(lane == 65, c_mid, q)
        q = jnp.where(lane == 66, c_lo, q)
        q = jnp.where((lane >= 67) & (lane < 70), 1.0, q)
        q_ref[0, h] = q.astype(BF16)

        k = normed(head_lanes(fk_ref, h), gk_ref[...])
        k = jnp.where((lane >= 64) & (lane < 67), 1.0, k)
        k = jnp.where(lane == 67, -c_hi, k)
        k = jnp.where(lane == 68, -c_mid, k)
        k = jnp.where(lane == 69, -c_lo, k)
        _store_key_blocks(k_ref, h, k)

        v = head_lanes(fv_ref, h)
        v_ref[0, h] = jnp.where(lane == ONES_LANE, 1.0, v).astype(BF16)


def _fox_prep_call(fq, fk, fv, krfg, bf, gq, gk):
    tl = ROW_TILE
    row = lambda b, i: (b, i, 0)
    c2 = lambda b, i: (0, 0)
    head_out = pl.BlockSpec((1, FOX_HEADS, tl, HEAD_PAD), lambda b, i: (b, 0, i, 0))
    head_shape = jax.ShapeDtypeStruct((BATCH, FOX_HEADS, SEQ, HEAD_PAD), BF16)
    return pl.pallas_call(
        _fox_prep_kernel,
        grid=(BATCH, SEQ // tl),
        in_specs=[pl.BlockSpec((1, tl, ATT_WIDTH), row),
                  pl.BlockSpec((1, tl, ATT_WIDTH), row),
                  pl.BlockSpec((1, tl, ATT_WIDTH), row),
                  pl.BlockSpec((1, tl, LANE), row),
                  pl.BlockSpec((1, LANE), c2),
                  pl.BlockSpec((1, LANE), c2),
                  pl.BlockSpec((1, LANE), c2)],
        out_specs=[head_out, _KT_SPEC(FOX_HEADS, tl), head_out],
        out_shape=[head_shape, _KT_SHAPE(FOX_HEADS), head_shape],
        scratch_shapes=[pltpu.VMEM((SUBLANE, LANE), F32)],
        compiler_params=_params(("arbitrary", "arbitrary")),
        name="fox_prep",
    )(fq, fk, fv, krfg, bf, gq, gk)


def _flash_kernel(q_ref, kt_ref, v_ref, o_ref, s_ref, m_ref, acc_ref, *, tile, chunk):
    qi = pl.program_id(2)
    n_blk = qi + 1
    per_chunk = tile // chunk
    qry_i = lax.broadcasted_iota(jnp.int32, (tile, tile), 0)
    key_i = lax.broadcasted_iota(jnp.int32, (tile, tile), 1)
    chunk_gap = key_i // chunk - qry_i // chunk
    lane = lax.broadcasted_iota(jnp.int32, (tile, HEAD_PAD), 1)

    def in_pairs(fn):
        def pair(jj, carry):
            fn(2 * jj)
            fn(2 * jj + 1)
            return carry
        lax.fori_loop(0, n_blk // 2, pair, 0)

        @pl.when(n_blk % 2 == 1)
        def _():
            fn(qi)

    def scores(j):
        allowed = chunk_gap <= (qi - j) * per_chunk
        for hh in range(2):
            s = jnp.dot(q_ref[0, hh], kt_ref[0, hh, j], preferred_element_type=F32)
            s = jnp.where(allowed, s, NEG)
            s_ref[hh, j] = s
            mr = m_ref[hh]
            for c in range(tile // LANE):
                mr = jnp.maximum(mr, s[:, c * LANE:(c + 1) * LANE])
            m_ref[hh] = mr

    m_ref[...] = jnp.full(m_ref.shape, NEG, F32)
    in_pairs(scores)
    ms = [jnp.max(m_ref[hh], axis=1, keepdims=True) for hh in range(2)]

    def weighted(j):
        k0 = pl.multiple_of(j * tile, tile)
        for hh in range(2):
            p = jnp.exp2(s_ref[hh, j] - ms[hh]).astype(BF16)
            acc_ref[hh] += jnp.dot(p, v_ref[0, hh, pl.ds(k0, tile), :], preferred_element_type=F32)

    acc_ref[...] = jnp.zeros(acc_ref.shape, F32)
    in_pairs(weighted)
    outs = [acc_ref[hh] / acc_ref[hh][:, ONES_LANE:ONES_LANE + 1] for hh in range(2)]
    o_ref[0] = jnp.where(lane < 64, outs[0], pltpu.roll(outs[1], 64, 1))


def _flash_call(q, kt, v, chunk):
    tile = ATT_TILE
    heads = q.shape[1]
    n_blocks = SEQ // tile
    return pl.pallas_call(
        functools.partial(_flash_kernel, tile=tile, chunk=chunk),
        grid=(BATCH, heads // 2, n_blocks),
        in_specs=[pl.BlockSpec((1, 2, tile, HEAD_PAD), lambda b, hp, i: (b, hp, i, 0)),
                  pl.BlockSpec((1, 2, n_blocks, HEAD_PAD, tile), lambda b, hp, i: (b, hp, 0, 0, 0)),
                  pl.BlockSpec((1, 2, SEQ, HEAD_PAD), lambda b, hp, i: (b, hp, 0, 0))],
        out_specs=pl.BlockSpec((1, tile, LANE), lambda b, hp, i: (b, i, hp)),
        out_shape=jax.ShapeDtypeStruct((BATCH, SEQ, ATT_WIDTH), F32),
        scratch_shapes=[pltpu.VMEM((2, n_blocks, tile, tile), F32),
                        pltpu.VMEM((2, tile, LANE), F32),
                        pltpu.VMEM((2, tile, HEAD_PAD), F32)],
        compiler_params=_params(("arbitrary", "arbitrary", "arbitrary")),
        name="flash_chunk%d" % chunk,
    )(q, kt, v)


def _merge_kernel(ssm_ref, mla_ref, fox_ref, x_ref, g1_ref, gm_ref, gf_ref,
                  ws_ref, wm_ref, wf_ref, o_ref):
    def normed(ref, g_ref):
        a = ref[0]
        return (a * lax.rsqrt(jnp.mean(a * a, axis=-1, keepdims=True) + EPS) * g_ref[...]).astype(BF16)

    mix = jnp.dot(ssm_ref[0].astype(BF16), ws_ref[...], preferred_element_type=F32)
    mix = mix + jnp.dot(normed(mla_ref, gm_ref), wm_ref[...], preferred_element_type=F32)
    mix = mix + jnp.dot(normed(fox_ref, gf_ref), wf_ref[...], preferred_element_type=F32)
    o_ref[0] = x_ref[0] + g1_ref[0] * mix


def _merge_call(o_ssm, o_mla, o_fox, x, g1, gm, gf, ws, wm, wf):
    tm = ROW_TILE
    row = lambda b, i: (b, i, 0)
    c2 = lambda b, i: (0, 0)
    return pl.pallas_call(
        _merge_kernel,
        grid=(BATCH, SEQ // tm),
        in_specs=[pl.BlockSpec((1, tm, SSM_WIDTH), row),
                  pl.BlockSpec((1, tm, ATT_WIDTH), row),
                  pl.BlockSpec((1, tm, ATT_WIDTH), row),
                  pl.BlockSpec((1, tm, D_MODEL), row),
                  pl.BlockSpec((1, 1, D_MODEL), lambda b, i: (b, 0, 0)),
                  pl.BlockSpec((1, ATT_WIDTH), c2),
                  pl.BlockSpec((1, ATT_WIDTH), c2),
                  pl.BlockSpec((SSM_WIDTH, D_MODEL), c2),
                  pl.BlockSpec((ATT_WIDTH, D_MODEL), c2),
                  pl.BlockSpec((ATT_WIDTH, D_MODEL), c2)],
        out_specs=pl.BlockSpec((1, tm, D_MODEL), row),
        out_shape=jax.ShapeDtypeStruct((BATCH, SEQ, D_MODEL), F32),
        compiler_params=_params(("arbitrary", "arbitrary")),
        name="merge",
    )(o_ssm, o_mla, o_fox, x, g1, gm, gf, ws, wm, wf)


def _ffn_kernel(*refs, routed):
    if routed:
        x_ref, g_ref, sh_ref, sc_ref, g2_ref, comb_ref, wg_ref, wu_ref, wd_ref, o_ref, h_ref, acc_ref = refs
    else:
        x_ref, g_ref, sh_ref, sc_ref, g2_ref, wg_ref, wu_ref, wd_ref, o_ref, h_ref, acc_ref = refs
    c = pl.program_id(2)

    @pl.when(c == 0)
    def _():
        h_ref[...] = _rms_mod(x_ref[0], g_ref[...], sc_ref[0], sh_ref[0]).astype(BF16)
        acc_ref[...] = jnp.zeros_like(acc_ref)

    h = h_ref[...]
    gate = jnp.dot(h, wg_ref[0], preferred_element_type=F32)
    up = jnp.dot(h, wu_ref[0], preferred_element_type=F32)
    a = (gate * jax.nn.sigmoid(gate) * up).astype(BF16)
    y = jnp.dot(a, wd_ref[0], preferred_element_type=F32)
    if routed:
        lane = lax.broadcasted_iota(jnp.int32, comb_ref.shape[1:], 1)
        y = y * jnp.sum(jnp.where(lane == c, comb_ref[0], 0.0), axis=-1, keepdims=True)
    acc_ref[...] += y

    @pl.when(c == pl.num_programs(2) - 1)
    def _():
        o_ref[0] = x_ref[0] + g2_ref[0] * acc_ref[...]


def _ffn_call(x, g, sh, sc, g2, wg, wu, wd, comb=None):
    tm = 1024
    n_chunks = wg.shape[0]
    row = lambda b, i, c: (b, i, 0)
    per_b = lambda b, i, c: (b, 0, 0)
    routed = comb is not None
    in_specs = [pl.BlockSpec((1, tm, D_MODEL), row),
                pl.BlockSpec((1, D_MODEL), lambda b, i, c: (0, 0)),
                pl.BlockSpec((1, 1, D_MODEL), per_b),
                pl.BlockSpec((1, 1, D_MODEL), per_b),
                pl.BlockSpec((1, 1, D_MODEL), per_b)]
    args = [x, g, sh, sc, g2]
    if routed:
        in_specs.append(pl.BlockSpec((1, tm, LANE), row))
        args.append(comb)
    in_specs += [pl.BlockSpec((1, D_MODEL, FF_CHUNK), lambda b, i, c: (c, 0, 0)),
                 pl.BlockSpec((1, D_MODEL, FF_CHUNK), lambda b, i, c: (c, 0, 0)),
                 pl.BlockSpec((1, FF_CHUNK, D_MODEL), lambda b, i, c: (c, 0, 0))]
    args += [wg, wu, wd]
    return pl.pallas_call(
        functools.partial(_ffn_kernel, routed=routed),
        grid=(BATCH, SEQ // tm, n_chunks),
        in_specs=in_specs,
        out_specs=pl.BlockSpec((1, tm, D_MODEL), row),
        out_shape=jax.ShapeDtypeStruct((BATCH, SEQ, D_MODEL), F32),
        scratch_shapes=[pltpu.VMEM((tm, D_MODEL), BF16), pltpu.VMEM((tm, D_MODEL), F32)],
        compiler_params=_params(("arbitrary", "arbitrary", "arbitrary")),
        name="ffn_routed" if routed else "ffn_dense",
    )(*args)


def _router_kernel(x_ref, g_ref, sh_ref, sc_ref, w_ref, b_ref, comb_ref):
    h = _rms_mod(x_ref[0], g_ref[...], sc_ref[0], sh_ref[0])
    h_hi = h.astype(BF16)
    h_lo = (h - h_hi.astype(F32)).astype(BF16)
    w_hi, w_lo = w_ref[0], w_ref[1]
    logits = (jnp.dot(h_hi, w_hi, preferred_element_type=F32)
              + jnp.dot(h_lo, w_hi, preferred_element_type=F32)
              + jnp.dot(h_hi, w_lo, preferred_element_type=F32)) + b_ref[...]
    lane = lax.broadcasted_iota(jnp.int32, logits.shape, 1)
    logits = jnp.where(lane < N_EXPERTS, logits, -jnp.inf)
    m1 = jnp.max(logits, axis=-1, keepdims=True)
    i1 = jnp.min(jnp.where(logits == m1, lane, LANE), axis=-1, keepdims=True)
    rest = jnp.where(lane == i1, -jnp.inf, logits)
    m2 = jnp.max(rest, axis=-1, keepdims=True)
    i2 = jnp.min(jnp.where(rest == m2, lane, LANE), axis=-1, keepdims=True)
    e = jnp.exp(m2 - m1)
    p1 = 1.0 / (1.0 + e)
    comb_ref[0] = jnp.where(lane == i1, p1, 0.0) + jnp.where(lane == i2, e * p1, 0.0)


def _router_call(x, g, sh, sc, w, b):
    tm = ROW_TILE
    row = lambda b_, i: (b_, i, 0)
    per_b = lambda b_, i: (b_, 0, 0)
    return pl.pallas_call(
        _router_kernel,
        grid=(BATCH, SEQ // tm),
        in_specs=[pl.BlockSpec((1, tm, D_MODEL), row),
                  pl.BlockSpec((1, D_MODEL), lambda b_, i: (0, 0)),
                  pl.BlockSpec((1, 1, D_MODEL), per_b),
                  pl.BlockSpec((1, 1, D_MODEL), per_b),
                  pl.BlockSpec((2, D_MODEL, LANE), lambda b_, i: (0, 0, 0)),
                  pl.BlockSpec((1, LANE), lambda b_, i: (0, 0))],
        out_specs=pl.BlockSpec((1, tm, LANE), row),
        out_shape=jax.ShapeDtypeStruct((BATCH, SEQ, LANE), F32),
        compiler_params=_params(("arbitrary", "arbitrary")),
        name="router",
    )(x, g, sh, sc, w, b)


def kernel(x, c, positions, norm_mix, norm_ffn, w_ada, b_ada, w_in, ssm_lam_re, ssm_lam_im, ssm_log_dt, ssm_b_re, ssm_b_im, ssm_c_re, ssm_c_im, ssm_d, ssm_w_glu, ssm_b_glu, mla_q_norm, mla_kv_norm, mla_w_uq, mla_w_ukv, mla_qk_gq, mla_qk_gk, fox_b_f, fox_qk_gq, fox_qk_gk, out_norm, w_out, ffn_w_gate, ffn_w_up, ffn_w_down, moe_w_router, moe_b_router, moe_w_gate, moe_w_up, moe_w_down):
    tabs = _rope_tables(positions)
    ada = _ada_call(c, w_ada, b_ada)
    ada = ada.reshape(DEPTH, BATCH, 6, 1, D_MODEL)
    row2 = lambda a: a[None, :]

    for i in range(DEPTH):
        sh1, sc1, g1, sh2, sc2, g2 = (ada[i, :, n] for n in range(6))

        u, cq, ckv, krfg, fq, fk, fv = _inproj_call(x, row2(norm_mix[i]), sh1, sc1, _pack_w_in(w_in[i]))

        bmat, lam, cmat = _s5_operands(ssm_lam_re[i], ssm_lam_im[i], ssm_log_dt[i],
                                       ssm_b_re[i], ssm_b_im[i], ssm_c_re[i], ssm_c_im[i])
        u_t = u.transpose(1, 0, 2).reshape(SEQ * BATCH, SSM_WIDTH)
        o_ssm = _s5_call(u_t, bmat, lam, cmat, row2(ssm_d[i]), ssm_w_glu[i].astype(BF16),
                         row2(ssm_b_glu[i]), row2(out_norm[i, :SSM_WIDTH]))
        o_ssm = o_ssm.reshape(SEQ, BATCH, SSM_WIDTH).transpose(1, 0, 2)

        wq, wk, wv, gq, gk = _mla_weights(mla_w_uq[i], mla_w_ukv[i], mla_qk_gq[i], mla_qk_gk[i])
        mq, mk, mv = _mla_prep_call(cq, ckv, krfg, tabs, row2(mla_q_norm[i]), row2(mla_kv_norm[i]),
                                    wq, wk, wv, gq, gk)
        o_mla = _flash_call(mq, mk, mv, CHUNK)

        xq, xk, xv = _fox_prep_call(fq, fk, fv, krfg, _pad_lanes(row2(fox_b_f[i]), LANE),
                                    _pad_lanes(row2(fox_qk_gq[i]), LANE), _pad_lanes(row2(fox_qk_gk[i]), LANE))
        o_fox = _flash_call(xq, xk, xv, 1)

        e1, e2 = SSM_WIDTH, SSM_WIDTH + ATT_WIDTH
        wo = w_out[i].astype(BF16)
        x = _merge_call(o_ssm, o_mla, o_fox, x, g1, row2(out_norm[i, e1:e2]), row2(out_norm[i, e2:]),
                        wo[:e1], wo[e1:e2], wo[e2:])

        j = i // 2
        if i % 2 == 0:
            split = lambda w: w.reshape(D_MODEL, D_FF // FF_CHUNK, FF_CHUNK).transpose(1, 0, 2).astype(BF16)
            wd = ffn_w_down[j].reshape(D_FF // FF_CHUNK, FF_CHUNK, D_MODEL).astype(BF16)
            x = _ffn_call(x, row2(norm_ffn[i]), sh2, sc2, g2, split(ffn_w_gate[j]), split(ffn_w_up[j]), wd)
        else:
            wr = _pad_lanes(moe_w_router[j], LANE)
            wr_hi = wr.astype(BF16)
            wr_lo = (wr - wr_hi.astype(F32)).astype(BF16)
            comb = _router_call(x, row2(norm_ffn[i]), sh2, sc2, jnp.stack([wr_hi, wr_lo]),
                                _pad_lanes(row2(moe_b_router[j]), LANE))
            x = _ffn_call(x, row2(norm_ffn[i]), sh2, sc2, g2, moe_w_gate[j].astype(BF16),
                          moe_w_up[j].astype(BF16), moe_w_down[j].astype(BF16), comb=comb)
    return x
```

```python
import functools
import math

import jax
import jax.numpy as jnp
from jax import lax
from jax.experimental import pallas as pl
from jax.experimental.pallas import tpu as pltpu

F32 = jnp.float32
BF16 = jnp.bfloat16

D_MODEL = 1024
BATCH = 8
SEQ = 4096
DEPTH = 4
CHUNK = 64
EPS = 1e-6

SSM_WIDTH = 256
SSM_GROUP = 16
N_SSM_GROUPS = 16
SSM_STATE = 64
N_STATE = N_SSM_GROUPS * SSM_STATE

MLA_HEADS = 6
MLA_Q_RANK = 256
MLA_KV_RANK = 128
MLA_NOPE = 64
MLA_ROPE = 32
MLA_V = 64
MLA_QK = 96
ROPE_BASE = 10000.0

FOX_HEADS = 6
FOX_HEAD_DIM = 64
ATT_WIDTH = 384

D_FF = 2816
N_EXPERTS = 8
D_FF_EXPERT = 1408

LANE = 128
SUBLANE = 8
HEAD_PAD = LANE
ONES_LANE = 64
NEG = -1e30

IN_PAD = 1920
KR_LANE = 64

ROW_TILE = 512
S5_STEPS = 64
ATT_TILE = 512
LOG2E = math.log2(math.e)
FF_CHUNK = 1408
MOE_TILE = 1024
MOE_ROWS = 256
VMEM_LIMIT = 56 * 1024 * 1024


def _params(sem):
    return pltpu.CompilerParams(dimension_semantics=sem, vmem_limit_bytes=VMEM_LIMIT)


def _rms_mod(x, g, sc, sh):
    ms = jnp.mean(x * x, axis=-1, keepdims=True)
    h = x * lax.rsqrt(ms + EPS) * g
    return h * (1.0 + sc) + sh


def _split3(x):
    hi = x.astype(BF16).astype(F32)
    r = x - hi
    mid = r.astype(BF16).astype(F32)
    lo = (r - mid).astype(BF16).astype(F32)
    return hi, mid, lo


def _ada_kernel(c_ref, w_ref, b_ref, o_ref):
    c = c_ref[...]
    ca = (c * jax.nn.sigmoid(c)).astype(BF16)
    o_ref[0] = jnp.dot(ca, w_ref[0].astype(BF16), preferred_element_type=F32) + b_ref[0]


def _ada_call(c, w_ada, b_ada):
    tn = 1536
    return pl.pallas_call(
        _ada_kernel,
        grid=(DEPTH, 6 * D_MODEL // tn),
        in_specs=[pl.BlockSpec((BATCH, D_MODEL), lambda i, j: (0, 0)),
                  pl.BlockSpec((1, D_MODEL, tn), lambda i, j: (i, 0, j)),
                  pl.BlockSpec((1, 1, tn), lambda i, j: (i, 0, j))],
        out_specs=pl.BlockSpec((1, BATCH, tn), lambda i, j: (i, 0, j)),
        out_shape=jax.ShapeDtypeStruct((DEPTH, BATCH, 6 * D_MODEL), F32),
        compiler_params=_params(("arbitrary", "arbitrary")),
        name="ada",
    )(c, w_ada, b_ada.reshape(DEPTH, 1, 6 * D_MODEL))


_IN_GROUPS = ((0, 256), (256, 512), (512, 640), (640, 768), (768, 1152), (1152, 1536), (1536, 1920))


def _inproj_kernel(x_ref, g_ref, sh_ref, sc_ref, w_ref, *out_refs):
    h = _rms_mod(x_ref[0], g_ref[...], sc_ref[0], sh_ref[0]).astype(BF16)
    for ref, (c0, c1) in zip(out_refs, _IN_GROUPS):
        ref[0] = jnp.dot(h, w_ref[:, c0:c1], preferred_element_type=F32)


def _inproj_call(x, g, sh, sc, w):
    tm = ROW_TILE
    row = lambda b, i: (b, i, 0)
    per_b = lambda b, i: (b, 0, 0)
    const = lambda b, i: (0, 0)
    widths = [c1 - c0 for c0, c1 in _IN_GROUPS]
    return pl.pallas_call(
        _inproj_kernel,
        grid=(BATCH, SEQ // tm),
        in_specs=[pl.BlockSpec((1, tm, D_MODEL), row),
                  pl.BlockSpec((1, D_MODEL), const),
                  pl.BlockSpec((1, 1, D_MODEL), per_b),
                  pl.BlockSpec((1, 1, D_MODEL), per_b),
                  pl.BlockSpec((D_MODEL, IN_PAD), const)],
        out_specs=[pl.BlockSpec((1, tm, wd), row) for wd in widths],
        out_shape=[jax.ShapeDtypeStruct((BATCH, SEQ, wd), F32) for wd in widths],
        compiler_params=_params(("arbitrary", "arbitrary")),
        name="inproj",
    )(x, g, sh, sc, w)


def _pack_w_in(w):
    u, cq, ckv, kr, fq, fk, fv, fg = jnp.split(
        w, (256, 512, 640, 672, 1056, 1440, 1824), axis=1)
    z = lambda n: jnp.zeros((D_MODEL, n), w.dtype)
    krfg = jnp.concatenate([fg, z(KR_LANE - FOX_HEADS), kr, z(LANE - KR_LANE - MLA_ROPE)], axis=1)
    return jnp.concatenate([u, cq, ckv, krfg, fq, fk, fv], axis=1).astype(BF16)


def _s5_kernel(u_ref, bmat_ref, lam_ref, cmat_ref, d_ref, wglu_ref, bglu_ref, gn_ref,
               o_ref, bu_ref, state_ref, *, steps):
    @pl.when(pl.program_id(0) == 0)
    def _():
        state_ref[...] = jnp.zeros_like(state_ref)

    u = u_ref[...]
    bu_ref[...] = jnp.dot(u.astype(BF16), bmat_ref[...], preferred_element_type=F32)
    lr = jnp.broadcast_to(lam_ref[0:1, :], (SUBLANE, N_STATE))
    li = jnp.broadcast_to(lam_ref[1:2, :], (SUBLANE, N_STATE))

    def step(t, carry):
        sr, si = carry
        r0 = pl.multiple_of(t * SUBLANE, SUBLANE)
        nr = lr * sr - li * si + bu_ref[pl.ds(r0, SUBLANE), 0:N_STATE]
        ni = lr * si + li * sr + bu_ref[pl.ds(r0, SUBLANE), N_STATE:2 * N_STATE]
        bu_ref[pl.ds(r0, SUBLANE), 0:N_STATE] = nr
        bu_ref[pl.ds(r0, SUBLANE), N_STATE:2 * N_STATE] = ni
        return nr, ni

    sr, si = lax.fori_loop(0, steps, step,
                           (state_ref[:, 0:N_STATE], state_ref[:, N_STATE:2 * N_STATE]))
    state_ref[:, 0:N_STATE] = sr
    state_ref[:, N_STATE:2 * N_STATE] = si

    y = jnp.dot(bu_ref[...].astype(BF16), cmat_ref[...], preferred_element_type=F32)
    y = jax.nn.gelu(y + d_ref[...] * u)
    gate = jnp.dot(y.astype(BF16), wglu_ref[...], preferred_element_type=F32) + bglu_ref[...]
    o = y * jax.nn.sigmoid(gate)
    ms = jnp.mean(o * o, axis=-1, keepdims=True)
    o_ref[...] = o * lax.rsqrt(ms + EPS) * gn_ref[...]


def _s5_call(u_t, bmat, lam, cmat, d_skip, wglu, bglu, gn):
    rows = S5_STEPS * BATCH
    const = lambda i: (0, 0)
    return pl.pallas_call(
        functools.partial(_s5_kernel, steps=S5_STEPS),
        grid=(SEQ // S5_STEPS,),
        in_specs=[pl.BlockSpec((rows, SSM_WIDTH), lambda i: (i, 0)),
                  pl.BlockSpec((SSM_WIDTH, 2 * N_STATE), const),
                  pl.BlockSpec((2, N_STATE), const),
                  pl.BlockSpec((2 * N_STATE, SSM_WIDTH), const),
                  pl.BlockSpec((1, SSM_WIDTH), const),
                  pl.BlockSpec((SSM_WIDTH, SSM_WIDTH), const),
                  pl.BlockSpec((1, SSM_WIDTH), const),
                  pl.BlockSpec((1, SSM_WIDTH), const)],
        out_specs=pl.BlockSpec((rows, SSM_WIDTH), lambda i: (i, 0)),
        out_shape=jax.ShapeDtypeStruct((SEQ * BATCH, SSM_WIDTH), F32),
        scratch_shapes=[pltpu.VMEM((rows, 2 * N_STATE), F32),
                        pltpu.VMEM((SUBLANE, 2 * N_STATE), F32)],
        compiler_params=_params(("arbitrary",)),
        name="s5",
    )(u_t, bmat, lam, cmat, d_skip, wglu, bglu, gn)


def _s5_operands(lam_re, lam_im, log_dt, b_re, b_im, c_re, c_im):
    dt = jnp.exp(log_dt)[:, None]
    mag = jnp.exp(lam_re * dt)
    lb_re = mag * jnp.cos(lam_im * dt)
    lb_im = mag * jnp.sin(lam_im * dt)
    den = lam_re * lam_re + lam_im * lam_im
    co_re = ((lb_re - 1.0) * lam_re + lb_im * lam_im) / den
    co_im = (lb_im * lam_re - (lb_re - 1.0) * lam_im) / den
    bb_re = co_re[..., None] * b_re - co_im[..., None] * b_im
    bb_im = co_re[..., None] * b_im + co_im[..., None] * b_re
    eye = jnp.eye(N_SSM_GROUPS, dtype=F32)
    blk_b = lambda m: jnp.einsum("gpc,gh->gchp", m, eye).reshape(SSM_WIDTH, N_STATE)
    bmat = jnp.concatenate([blk_b(bb_re), blk_b(bb_im)], axis=1).astype(BF16)
    blk_c = lambda m: jnp.einsum("gcp,gh->gphc", m, eye).reshape(N_STATE, SSM_WIDTH)
    cmat = jnp.concatenate([blk_c(c_re), -blk_c(c_im)], axis=0).astype(BF16)
    lam = jnp.stack([lb_re.reshape(N_STATE), lb_im.reshape(N_STATE)], axis=0)
    return bmat, lam, cmat


def _rope(x, cos, sina, sinb):
    return x * cos + pltpu.roll(x, LANE - 16, 1) * sina + pltpu.roll(x, 16, 1) * sinb


def _rope_tables(positions):
    half = MLA_ROPE // 2
    inv = ROPE_BASE ** (-jnp.arange(half, dtype=F32) / half)
    ang = positions.astype(F32)[..., None] * inv
    cos, sin = jnp.cos(ang), jnp.sin(ang)
    shp = positions.shape
    one = lambda n: jnp.ones(shp + (n,), F32)
    zero = lambda n: jnp.zeros(shp + (n,), F32)
    cos_t = jnp.concatenate([one(MLA_NOPE), cos, cos, zero(LANE - MLA_QK)], axis=-1)
    sina_t = jnp.concatenate([zero(MLA_NOPE), -sin, zero(LANE - MLA_NOPE - half)], axis=-1)
    sinb_t = jnp.concatenate([zero(MLA_NOPE + half), sin, zero(LANE - MLA_QK)], axis=-1)
    return cos_t, sina_t, sinb_t


def _store_key_blocks(kt_ref, h, k):
    kt = k.T
    for s in range(k.shape[0] // ATT_TILE):
        kt_ref[0, h, s] = kt[:, s * ATT_TILE:(s + 1) * ATT_TILE].astype(BF16)


_KT_SPEC = lambda heads, tl: pl.BlockSpec((1, heads, tl // ATT_TILE, HEAD_PAD, ATT_TILE),
                                          lambda b, i: (b, 0, i, 0, 0))
_KT_SHAPE = lambda heads: jax.ShapeDtypeStruct((BATCH, heads, SEQ // ATT_TILE, HEAD_PAD, ATT_TILE), BF16)


def _mla_prep_kernel(cq_ref, ckv_ref, krfg_ref, cos_ref, sina_ref, sinb_ref,
                     qn_ref, kvn_ref, wq_ref, wk_ref, wv_ref, gq_ref, gk_ref,
                     q_ref, k_ref, v_ref):
    tl = cq_ref.shape[1]
    lane = lax.broadcasted_iota(jnp.int32, (tl, LANE), 1)
    cos, sina, sinb = cos_ref[0], sina_ref[0], sinb_ref[0]

    cq = cq_ref[0]
    cqn = (cq * lax.rsqrt(jnp.mean(cq * cq, axis=-1, keepdims=True) + EPS) * qn_ref[...]).astype(BF16)
    ckv = ckv_ref[0]
    ckvn = (ckv * lax.rsqrt(jnp.mean(ckv * ckv, axis=-1, keepdims=True) + EPS) * kvn_ref[...]).astype(BF16)
    kr = jnp.where((lane >= KR_LANE) & (lane < KR_LANE + MLA_ROPE), krfg_ref[0], 0.0)

    for h in range(MLA_HEADS):
        q = jnp.dot(cqn, wq_ref[h], preferred_element_type=F32)
        q = q * lax.rsqrt(jnp.sum(q * q, axis=-1, keepdims=True) / MLA_QK + EPS) * gq_ref[...]
        q = _rope(q, cos, sina, sinb) * (LOG2E / math.sqrt(MLA_QK))
        q_ref[0, h] = q.astype(BF16)

        k = jnp.dot(ckvn, wk_ref[h], preferred_element_type=F32) + kr
        k = k * lax.rsqrt(jnp.sum(k * k, axis=-1, keepdims=True) / MLA_QK + EPS) * gk_ref[...]
        _store_key_blocks(k_ref, h, _rope(k, cos, sina, sinb))

        v = jnp.dot(ckvn, wv_ref[h], preferred_element_type=F32)
        v_ref[0, h] = jnp.where(lane == ONES_LANE, 1.0, v).astype(BF16)


def _mla_prep_call(cq, ckv, krfg, tabs, qn, kvn, wq, wk, wv, gq, gk):
    tl = ROW_TILE
    row = lambda b, i: (b, i, 0)
    c2 = lambda b, i: (0, 0)
    c3 = lambda b, i: (0, 0, 0)
    head_out = pl.BlockSpec((1, MLA_HEADS, tl, HEAD_PAD), lambda b, i: (b, 0, i, 0))
    head_shape = jax.ShapeDtypeStruct((BATCH, MLA_HEADS, SEQ, HEAD_PAD), BF16)
    return pl.pallas_call(
        _mla_prep_kernel,
        grid=(BATCH, SEQ // tl),
        in_specs=[pl.BlockSpec((1, tl, MLA_Q_RANK), row),
                  pl.BlockSpec((1, tl, MLA_KV_RANK), row),
                  pl.BlockSpec((1, tl, LANE), row),
                  pl.BlockSpec((1, tl, LANE), row),
                  pl.BlockSpec((1, tl, LANE), row),
                  pl.BlockSpec((1, tl, LANE), row),
                  pl.BlockSpec((1, MLA_Q_RANK), c2),
                  pl.BlockSpec((1, MLA_KV_RANK), c2),
                  pl.BlockSpec((MLA_HEADS, MLA_Q_RANK, HEAD_PAD), c3),
                  pl.BlockSpec((MLA_HEADS, MLA_KV_RANK, HEAD_PAD), c3),
                  pl.BlockSpec((MLA_HEADS, MLA_KV_RANK, HEAD_PAD), c3),
                  pl.BlockSpec((1, HEAD_PAD), c2),
                  pl.BlockSpec((1, HEAD_PAD), c2)],
        out_specs=[head_out, _KT_SPEC(MLA_HEADS, tl), head_out],
        out_shape=[head_shape, _KT_SHAPE(MLA_HEADS), head_shape],
        compiler_params=_params(("arbitrary", "arbitrary")),
        name="mla_prep",
    )(cq, ckv, krfg, *tabs, qn, kvn, wq, wk, wv, gq, gk)


def _pad_lanes(a, n=HEAD_PAD):
    return jnp.pad(a, [(0, 0)] * (a.ndim - 1) + [(0, n - a.shape[-1])])


def _mla_weights(w_uq, w_ukv, gq, gk):
    wq = _pad_lanes(w_uq.reshape(MLA_Q_RANK, MLA_HEADS, MLA_QK).transpose(1, 0, 2)).astype(BF16)
    wkv = w_ukv.reshape(MLA_KV_RANK, MLA_HEADS, MLA_NOPE + MLA_V).transpose(1, 0, 2)
    wk = _pad_lanes(wkv[..., :MLA_NOPE]).astype(BF16)
    wv = _pad_lanes(wkv[..., MLA_NOPE:]).astype(BF16)
    return wq, wk, wv, _pad_lanes(gq[None, :]), _pad_lanes(gk[None, :])


def _fox_prep_kernel(fq_ref, fk_ref, fv_ref, krfg_ref, bf_ref, gq_ref, gk_ref,
                     q_ref, k_ref, v_ref, carry_ref):
    tl = fq_ref.shape[1]
    lane = lax.broadcasted_iota(jnp.int32, (tl, LANE), 1)

    @pl.when(pl.program_id(1) == 0)
    def _():
        carry_ref[...] = jnp.zeros_like(carry_ref)

    logf = jax.nn.log_sigmoid(krfg_ref[0] + bf_ref[...])
    logf = jnp.where(lane < FOX_HEADS, logf, 0.0)
    r_i = lax.broadcasted_iota(jnp.int32, (tl, tl), 0)
    c_i = lax.broadcasted_iota(jnp.int32, (tl, tl), 1)
    tri = jnp.where(c_i <= r_i, 1.0, 0.0).astype(BF16)
    cum = carry_ref[0:1, :]
    for piece in _split3(logf):
        cum = cum + jnp.dot(tri, piece.astype(BF16), preferred_element_type=F32)
    carry_ref[0:1, :] = cum[tl - 1:tl, :]

    def head_lanes(ref, h):
        x = ref[0, :, (h // 2) * LANE:(h // 2 + 1) * LANE]
        if h % 2:
            x = pltpu.roll(x, FOX_HEAD_DIM, 1)
        return jnp.where(lane < FOX_HEAD_DIM, x, 0.0)

    def normed(x, g):
        return x * lax.rsqrt(jnp.sum(x * x, axis=-1, keepdims=True) / FOX_HEAD_DIM + EPS) * g

    for h in range(FOX_HEADS):
        c_hi, c_mid, c_lo = _split3(cum[:, h:h + 1] * LOG2E)
        q = normed(head_lanes(fq_ref, h), gq_ref[...]) * (LOG2E / math.sqrt(FOX_HEAD_DIM))
        q = jnp.where(lane == 64, c_hi, q)
        q = jnp.where(lane == 65, c_mid, q)
        q = jnp.where(lane == 66, c_lo, q)
        q = jnp.where((lane >= 67) & (lane < 70), 1.0, q)
        q_ref[0, h] = q.astype(BF16)

        k = normed(head_lanes(fk_ref, h), gk_ref[...])
        k = jnp.where((lane >= 64) & (lane < 67), 1.0, k)
        k = jnp.where(lane == 67, -c_hi, k)
        k = jnp.where(lane == 68, -c_mid, k)
        k = jnp.where(lane == 69, -c_lo, k)
        _store_key_blocks(k_ref, h, k)

        v = head_lanes(fv_ref, h)
        v_ref[0, h] = jnp.where(lane == ONES_LANE, 1.0, v).astype(BF16)


def _fox_prep_call(fq, fk, fv, krfg, bf, gq, gk):
    tl = ROW_TILE
    row = lambda b, i: (b, i, 0)
    c2 = lambda b, i: (0, 0)
    head_out = pl.BlockSpec((1, FOX_HEADS, tl, HEAD_PAD), lambda b, i: (b, 0, i, 0))
    head_shape = jax.ShapeDtypeStruct((BATCH, FOX_HEADS, SEQ, HEAD_PAD), BF16)
    return pl.pallas_call(
        _fox_prep_kernel,
        grid=(BATCH, SEQ // tl),
        in_specs=[pl.BlockSpec((1, tl, ATT_WIDTH), row),
                  pl.BlockSpec((1, tl, ATT_WIDTH), row),
                  pl.BlockSpec((1, tl, ATT_WIDTH), row),
                  pl.BlockSpec((1, tl, LANE), row),
                  pl.BlockSpec((1, LANE), c2),
                  pl.BlockSpec((1, LANE), c2),
                  pl.BlockSpec((1, LANE), c2)],
        out_specs=[head_out, _KT_SPEC(FOX_HEADS, tl), head_out],
        out_shape=[head_shape, _KT_SHAPE(FOX_HEADS), head_shape],
        scratch_shapes=[pltpu.VMEM((SUBLANE, LANE), F32)],
        compiler_params=_params(("arbitrary", "arbitrary")),
        name="fox_prep",
    )(fq, fk, fv, krfg, bf, gq, gk)


def _flash_kernel(q_ref, kt_ref, v_ref, o_ref, s_ref, m_ref, acc_ref, *, tile, chunk):
    qi = pl.program_id(2)
    n_blk = qi + 1
    per_chunk = tile // chunk
    qry_i = lax.broadcasted_iota(jnp.int32, (tile, tile), 0)
    key_i = lax.broadcasted_iota(jnp.int32, (tile, tile), 1)
    chunk_gap = key_i // chunk - qry_i // chunk
    lane = lax.broadcasted_iota(jnp.int32, (tile, HEAD_PAD), 1)

    def in_pairs(fn):
        def pair(jj, carry):
            fn(2 * jj)
            fn(2 * jj + 1)
            return carry
        lax.fori_loop(0, n_blk // 2, pair, 0)

        @pl.when(n_blk % 2 == 1)
        def _():
            fn(qi)

    def scores(j):
        allowed = chunk_gap <= (qi - j) * per_chunk
        for hh in range(2):
            s = jnp.dot(q_ref[0, hh], kt_ref[0, hh, j], preferred_element_type=F32)
            s = jnp.where(allowed, s, NEG)
            s_ref[hh, j] = s
            mr = m_ref[hh]
            for c in range(tile // LANE):
                mr = jnp.maximum(mr, s[:, c * LANE:(c + 1) * LANE])
            m_ref[hh] = mr

    m_ref[...] = jnp.full(m_ref.shape, NEG, F32)
    in_pairs(scores)
    ms = [jnp.max(m_ref[hh], axis=1, keepdims=True) for hh in range(2)]

    def weighted(j):
        k0 = pl.multiple_of(j * tile, tile)
        for hh in range(2):
            p = jnp.exp2(s_ref[hh, j] - ms[hh]).astype(BF16)
            acc_ref[hh] += jnp.dot(p, v_ref[0, hh, pl.ds(k0, tile), :], preferred_element_type=F32)

    acc_ref[...] = jnp.zeros(acc_ref.shape, F32)
    in_pairs(weighted)
    outs = [acc_ref[hh] / acc_ref[hh][:, ONES_LANE:ONES_LANE + 1] for hh in range(2)]
    o_ref[0] = jnp.where(lane < 64, outs[0], pltpu.roll(outs[1], 64, 1))


def _flash_call(q, kt, v, chunk):
    tile = ATT_TILE
    heads = q.shape[1]
    n_blocks = SEQ // tile
    return pl.pallas_call(
        functools.partial(_flash_kernel, tile=tile, chunk=chunk),
        grid=(BATCH, heads // 2, n_blocks),
        in_specs=[pl.BlockSpec((1, 2, tile, HEAD_PAD), lambda b, hp, i: (b, hp, i, 0)),
                  pl.BlockSpec((1, 2, n_blocks, HEAD_PAD, tile), lambda b, hp, i: (b, hp, 0, 0, 0)),
                  pl.BlockSpec((1, 2, SEQ, HEAD_PAD), lambda b, hp, i: (b, hp, 0, 0))],
        out_specs=pl.BlockSpec((1, tile, LANE), lambda b, hp, i: (b, i, hp)),
        out_shape=jax.ShapeDtypeStruct((BATCH, SEQ, ATT_WIDTH), F32),
        scratch_shapes=[pltpu.VMEM((2, n_blocks, tile, tile), F32),
                        pltpu.VMEM((2, tile, LANE), F32),
                        pltpu.VMEM((2, tile, HEAD_PAD), F32)],
        compiler_params=_params(("arbitrary", "arbitrary", "arbitrary")),
        name="flash_chunk%d" % chunk,
    )(q, kt, v)


def _merge_kernel(ssm_ref, mla_ref, fox_ref, x_ref, g1_ref, gm_ref, gf_ref,
                  ws_ref, wm_ref, wf_ref, o_ref):
    def normed(ref, g_ref):
        a = ref[0]
        return (a * lax.rsqrt(jnp.mean(a * a, axis=-1, keepdims=True) + EPS) * g_ref[...]).astype(BF16)

    mix = jnp.dot(ssm_ref[0].astype(BF16), ws_ref[...], preferred_element_type=F32)
    mix = mix + jnp.dot(normed(mla_ref, gm_ref), wm_ref[...], preferred_element_type=F32)
    mix = mix + jnp.dot(normed(fox_ref, gf_ref), wf_ref[...], preferred_element_type=F32)
    o_ref[0] = x_ref[0] + g1_ref[0] * mix


def _merge_call(o_ssm, o_mla, o_fox, x, g1, gm, gf, ws, wm, wf):
    tm = ROW_TILE
    row = lambda b, i: (b, i, 0)
    c2 = lambda b, i: (0, 0)
    return pl.pallas_call(
        _merge_kernel,
        grid=(BATCH, SEQ // tm),
        in_specs=[pl.BlockSpec((1, tm, SSM_WIDTH), row),
                  pl.BlockSpec((1, tm, ATT_WIDTH), row),
                  pl.BlockSpec((1, tm, ATT_WIDTH), row),
                  pl.BlockSpec((1, tm, D_MODEL), row),
                  pl.BlockSpec((1, 1, D_MODEL), lambda b, i: (b, 0, 0)),
                  pl.BlockSpec((1, ATT_WIDTH), c2),
                  pl.BlockSpec((1, ATT_WIDTH), c2),
                  pl.BlockSpec((SSM_WIDTH, D_MODEL), c2),
                  pl.BlockSpec((ATT_WIDTH, D_MODEL), c2),
                  pl.BlockSpec((ATT_WIDTH, D_MODEL), c2)],
        out_specs=pl.BlockSpec((1, tm, D_MODEL), row),
        out_shape=jax.ShapeDtypeStruct((BATCH, SEQ, D_MODEL), F32),
        compiler_params=_params(("arbitrary", "arbitrary")),
        name="merge",
    )(o_ssm, o_mla, o_fox, x, g1, gm, gf, ws, wm, wf)


def _ffn_kernel(x_ref, g_ref, sh_ref, sc_ref, g2_ref, wg_ref, wu_ref, wd_ref, o_ref, h_ref, acc_ref):
    c = pl.program_id(2)

    @pl.when(c == 0)
    def _():
        h_ref[...] = _rms_mod(x_ref[0], g_ref[...], sc_ref[0], sh_ref[0]).astype(BF16)
        acc_ref[...] = jnp.zeros_like(acc_ref)

    h = h_ref[...]
    gate = jnp.dot(h, wg_ref[0], preferred_element_type=F32)
    up = jnp.dot(h, wu_ref[0], preferred_element_type=F32)
    a = (gate * jax.nn.sigmoid(gate) * up).astype(BF16)
    acc_ref[...] += jnp.dot(a, wd_ref[0], preferred_element_type=F32)

    @pl.when(c == pl.num_programs(2) - 1)
    def _():
        o_ref[0] = x_ref[0] + g2_ref[0] * acc_ref[...]


def _ffn_call(x, g, sh, sc, g2, wg, wu, wd):
    tm = MOE_TILE
    n_chunks = wg.shape[0]
    row = lambda b, i, c: (b, i, 0)
    per_b = lambda b, i, c: (b, 0, 0)
    chunk = lambda b, i, c: (c, 0, 0)
    return pl.pallas_call(
        _ffn_kernel,
        grid=(BATCH, SEQ // tm, n_chunks),
        in_specs=[pl.BlockSpec((1, tm, D_MODEL), row),
                  pl.BlockSpec((1, D_MODEL), lambda b, i, c: (0, 0)),
                  pl.BlockSpec((1, 1, D_MODEL), per_b),
                  pl.BlockSpec((1, 1, D_MODEL), per_b),
                  pl.BlockSpec((1, 1, D_MODEL), per_b),
                  pl.BlockSpec((1, D_MODEL, FF_CHUNK), chunk),
                  pl.BlockSpec((1, D_MODEL, FF_CHUNK), chunk),
                  pl.BlockSpec((1, FF_CHUNK, D_MODEL), chunk)],
        out_specs=pl.BlockSpec((1, tm, D_MODEL), row),
        out_shape=jax.ShapeDtypeStruct((BATCH, SEQ, D_MODEL), F32),
        scratch_shapes=[pltpu.VMEM((tm, D_MODEL), BF16), pltpu.VMEM((tm, D_MODEL), F32)],
        compiler_params=_params(("arbitrary", "arbitrary", "arbitrary")),
        name="ffn_dense",
    )(x, g, sh, sc, g2, wg, wu, wd)


def _router_kernel(x_ref, g_ref, sh_ref, sc_ref, w_ref, b_ref, comb_ref, rank_ref, rankt_ref, count_ref):
    tm = x_ref.shape[1]
    h = _rms_mod(x_ref[0], g_ref[...], sc_ref[0], sh_ref[0])
    h_hi = h.astype(BF16)
    h_lo = (h - h_hi.astype(F32)).astype(BF16)
    w_hi, w_lo = w_ref[0], w_ref[1]
    logits = (jnp.dot(h_hi, w_hi, preferred_element_type=F32)
              + jnp.dot(h_lo, w_hi, preferred_element_type=F32)
              + jnp.dot(h_hi, w_lo, preferred_element_type=F32)) + b_ref[...]
    lane = lax.broadcasted_iota(jnp.int32, logits.shape, 1)
    logits = jnp.where(lane < N_EXPERTS, logits, -jnp.inf)
    m1 = jnp.max(logits, axis=-1, keepdims=True)
    i1 = jnp.min(jnp.where(logits == m1, lane, LANE), axis=-1, keepdims=True)
    rest = jnp.where(lane == i1, -jnp.inf, logits)
    m2 = jnp.max(rest, axis=-1, keepdims=True)
    i2 = jnp.min(jnp.where(rest == m2, lane, LANE), axis=-1, keepdims=True)
    e = jnp.exp(m2 - m1)
    p1 = 1.0 / (1.0 + e)
    comb_ref[0] = jnp.where(lane == i1, p1, 0.0) + jnp.where(lane == i2, e * p1, 0.0)

    chosen = (lane == i1) | (lane == i2)
    chosen_f = jnp.where(chosen, 1.0, 0.0)
    r_i = lax.broadcasted_iota(jnp.int32, (tm, tm), 0)
    c_i = lax.broadcasted_iota(jnp.int32, (tm, tm), 1)
    earlier = jnp.where(c_i < r_i, 1.0, 0.0).astype(BF16)
    rank = jnp.dot(earlier, chosen_f.astype(BF16), preferred_element_type=F32)
    rank = jnp.where(chosen, rank, -1.0)
    rank_ref[0] = rank
    rankt_ref[0] = rank.T[0:SUBLANE, :]
    count_ref[0] = jnp.sum(chosen_f, axis=0, keepdims=True)


def _router_call(x, g, sh, sc, w, b):
    tm = MOE_TILE
    tiles = SEQ // tm
    row = lambda b_, i: (b_, i, 0)
    per_b = lambda b_, i: (b_, 0, 0)
    per_tile = lambda b_, i: (b_ * tiles + i, 0, 0)
    return pl.pallas_call(
        _router_kernel,
        grid=(BATCH, tiles),
        in_specs=[pl.BlockSpec((1, tm, D_MODEL), row),
                  pl.BlockSpec((1, D_MODEL), lambda b_, i: (0, 0)),
                  pl.BlockSpec((1, 1, D_MODEL), per_b),
                  pl.BlockSpec((1, 1, D_MODEL), per_b),
                  pl.BlockSpec((2, D_MODEL, LANE), lambda b_, i: (0, 0, 0)),
                  pl.BlockSpec((1, LANE), lambda b_, i: (0, 0))],
        out_specs=[pl.BlockSpec((1, tm, LANE), row),
                   pl.BlockSpec((1, tm, LANE), row),
                   pl.BlockSpec((1, SUBLANE, tm), per_tile),
                   pl.BlockSpec((1, 1, LANE), per_tile)],
        out_shape=[jax.ShapeDtypeStruct((BATCH, SEQ, LANE), F32),
                   jax.ShapeDtypeStruct((BATCH, SEQ, LANE), F32),
                   jax.ShapeDtypeStruct((BATCH * tiles, SUBLANE, tm), F32),
                   jax.ShapeDtypeStruct((BATCH * tiles, 1, LANE), F32)],
        compiler_params=_params(("arbitrary", "arbitrary")),
        name="router",
    )(x, g, sh, sc, w, b)


def _moe_kernel(count_ref, x_ref, g_ref, sh_ref, sc_ref, g2_ref, comb_ref, rank_ref, rankt_ref,
                wg_ref, wu_ref, wd_ref, o_ref, h_ref):
    tm = x_ref.shape[1]
    e = pl.program_id(1)

    @pl.when(e == 0)
    def _():
        x = x_ref[0]
        h_ref[...] = _rms_mod(x, g_ref[...], sc_ref[0], sh_ref[0]).astype(BF16)
        o_ref[0] = x

    lane = lax.broadcasted_iota(jnp.int32, (tm, LANE), 1)
    mine = lane == e
    rank_col = jnp.sum(jnp.where(mine, rank_ref[0], 0.0), axis=-1, keepdims=True)
    gate_col = jnp.sum(jnp.where(mine, comb_ref[0], 0.0), axis=-1, keepdims=True)
    rank_row = rankt_ref[0, pl.ds(e, 1), :]
    slot_sub = lax.broadcasted_iota(jnp.int32, (MOE_ROWS, tm), 0).astype(F32)
    slot_lane = lax.broadcasted_iota(jnp.int32, (tm, MOE_ROWS), 1).astype(F32)
    count = count_ref[pl.program_id(0) * N_EXPERTS + e]

    def block(sb, carry):
        base = (sb * MOE_ROWS).astype(F32)
        pick = jnp.where(rank_row - base == slot_sub, 1.0, 0.0).astype(BF16)
        rows = jnp.dot(pick, h_ref[...], preferred_element_type=F32).astype(BF16)
        gate = jnp.dot(rows, wg_ref[0], preferred_element_type=F32)
        up = jnp.dot(rows, wu_ref[0], preferred_element_type=F32)
        a = (gate * jax.nn.sigmoid(gate) * up).astype(BF16)
        y = jnp.dot(a, wd_ref[0], preferred_element_type=F32).astype(BF16)
        place = jnp.where(rank_col - base == slot_lane, 1.0, 0.0).astype(BF16)
        back = jnp.dot(place, y, preferred_element_type=F32)
        o_ref[0] += g2_ref[0] * (gate_col * back)
        return carry

    lax.fori_loop(0, (count + MOE_ROWS - 1) // MOE_ROWS, block, 0)


def _moe_call(x, g, sh, sc, g2, comb, rank, rankt, counts, wg, wu, wd):
    tm = MOE_TILE
    tiles = SEQ // tm
    n_tiles = BATCH * tiles
    row = lambda i, e, cnt: (i, 0, 0)
    per_b = lambda i, e, cnt: (i // tiles, 0, 0)
    expert = lambda i, e, cnt: (e, 0, 0)
    as_tiles = lambda a: a.reshape(n_tiles, tm, a.shape[-1])
    grid_spec = pltpu.PrefetchScalarGridSpec(
        num_scalar_prefetch=1,
        grid=(n_tiles, N_EXPERTS),
        in_specs=[pl.BlockSpec((1, tm, D_MODEL), row),
                  pl.BlockSpec((1, D_MODEL), lambda i, e, cnt: (0, 0)),
                  pl.BlockSpec((1, 1, D_MODEL), per_b),
                  pl.BlockSpec((1, 1, D_MODEL), per_b),
                  pl.BlockSpec((1, 1, D_MODEL), per_b),
                  pl.BlockSpec((1, tm, LANE), row),
                  pl.BlockSpec((1, tm, LANE), row),
                  pl.BlockSpec((1, SUBLANE, tm), row),
                  pl.BlockSpec((1, D_MODEL, D_FF_EXPERT), expert),
                  pl.BlockSpec((1, D_MODEL, D_FF_EXPERT), expert),
                  pl.BlockSpec((1, D_FF_EXPERT, D_MODEL), expert)],
        out_specs=pl.BlockSpec((1, tm, D_MODEL), row),
        scratch_shapes=[pltpu.VMEM((tm, D_MODEL), BF16)],
    )
    out = pl.pallas_call(
        _moe_kernel,
        grid_spec=grid_spec,
        out_shape=jax.ShapeDtypeStruct((n_tiles, tm, D_MODEL), F32),
        compiler_params=_params(("arbitrary", "arbitrary")),
        name="moe_experts",
    )(counts, as_tiles(x), g, sh, sc, g2, as_tiles(comb), as_tiles(rank), rankt, wg, wu, wd)
    return out.reshape(BATCH, SEQ, D_MODEL)


def kernel(x, c, positions, norm_mix, norm_ffn, w_ada, b_ada, w_in, ssm_lam_re, ssm_lam_im, ssm_log_dt, ssm_b_re, ssm_b_im, ssm_c_re, ssm_c_im, ssm_d, ssm_w_glu, ssm_b_glu, mla_q_norm, mla_kv_norm, mla_w_uq, mla_w_ukv, mla_qk_gq, mla_qk_gk, fox_b_f, fox_qk_gq, fox_qk_gk, out_norm, w_out, ffn_w_gate, ffn_w_up, ffn_w_down, moe_w_router, moe_b_router, moe_w_gate, moe_w_up, moe_w_down):
    tabs = _rope_tables(positions)
    ada = _ada_call(c, w_ada, b_ada)
    ada = ada.reshape(DEPTH, BATCH, 6, 1, D_MODEL)
    row2 = lambda a: a[None, :]

    for i in range(DEPTH):
        sh1, sc1, g1, sh2, sc2, g2 = (ada[i, :, n] for n in range(6))

        u, cq, ckv, krfg, fq, fk, fv = _inproj_call(x, row2(norm_mix[i]), sh1, sc1, _pack_w_in(w_in[i]))

        bmat, lam, cmat = _s5_operands(ssm_lam_re[i], ssm_lam_im[i], ssm_log_dt[i],
                                       ssm_b_re[i], ssm_b_im[i], ssm_c_re[i], ssm_c_im[i])
        u_t = u.transpose(1, 0, 2).reshape(SEQ * BATCH, SSM_WIDTH)
        o_ssm = _s5_call(u_t, bmat, lam, cmat, row2(ssm_d[i]), ssm_w_glu[i].astype(BF16),
                         row2(ssm_b_glu[i]), row2(out_norm[i, :SSM_WIDTH]))
        o_ssm = o_ssm.reshape(SEQ, BATCH, SSM_WIDTH).transpose(1, 0, 2)

        wq, wk, wv, gq, gk = _mla_weights(mla_w_uq[i], mla_w_ukv[i], mla_qk_gq[i], mla_qk_gk[i])
        mq, mk, mv = _mla_prep_call(cq, ckv, krfg, tabs, row2(mla_q_norm[i]), row2(mla_kv_norm[i]),
                                    wq, wk, wv, gq, gk)
        o_mla = _flash_call(mq, mk, mv, CHUNK)

        xq, xk, xv = _fox_prep_call(fq, fk, fv, krfg, _pad_lanes(row2(fox_b_f[i]), LANE),
                                    _pad_lanes(row2(fox_qk_gq[i]), LANE), _pad_lanes(row2(fox_qk_gk[i]), LANE))
        o_fox = _flash_call(xq, xk, xv, 1)

        e1, e2 = SSM_WIDTH, SSM_WIDTH + ATT_WIDTH
        wo = w_out[i].astype(BF16)
        x = _merge_call(o_ssm, o_mla, o_fox, x, g1, row2(out_norm[i, e1:e2]), row2(out_norm[i, e2:]),
                        wo[:e1], wo[e1:e2], wo[e2:])

        j = i // 2
        if i % 2 == 0:
            split = lambda w: w.reshape(D_MODEL, D_FF // FF_CHUNK, FF_CHUNK).transpose(1, 0, 2).astype(BF16)
            wd = ffn_w_down[j].reshape(D_FF // FF_CHUNK, FF_CHUNK, D_MODEL).astype(BF16)
            x = _ffn_call(x, row2(norm_ffn[i]), sh2, sc2, g2, split(ffn_w_gate[j]), split(ffn_w_up[j]), wd)
        else:
            wr = _pad_lanes(moe_w_router[j], LANE)
            wr_hi = wr.astype(BF16)
            wr_lo = (wr - wr_hi.astype(F32)).astype(BF16)
            comb, rank, rankt, counts = _router_call(x, row2(norm_ffn[i]), sh2, sc2, jnp.stack([wr_hi, wr_lo]),
                                                     _pad_lanes(row2(moe_b_router[j]), LANE))
            counts = counts[:, 0, :N_EXPERTS].astype(jnp.int32).reshape(-1)
            x = _moe_call(x, row2(norm_ffn[i]), sh2, sc2, g2, comb, rank, rankt, counts,
                          moe_w_gate[j].astype(BF16), moe_w_up[j].astype(BF16), moe_w_down[j].astype(BF16))
    return x
```

```python
import functools
import math

import jax
import jax.numpy as jnp
from jax import lax
from jax.experimental import pallas as pl
from jax.experimental.pallas import tpu as pltpu

F32 = jnp.float32
BF16 = jnp.bfloat16

D_MODEL = 1024
BATCH = 8
SEQ = 4096
DEPTH = 4
CHUNK = 64
EPS = 1e-6

SSM_WIDTH = 256
SSM_GROUP = 16
N_SSM_GROUPS = 16
SSM_STATE = 64
N_STATE = N_SSM_GROUPS * SSM_STATE

MLA_HEADS = 6
MLA_Q_RANK = 256
MLA_KV_RANK = 128
MLA_NOPE = 64
MLA_ROPE = 32
MLA_V = 64
MLA_QK = 96
ROPE_BASE = 10000.0

FOX_HEADS = 6
FOX_HEAD_DIM = 64
ATT_WIDTH = 384

D_FF = 2816
N_EXPERTS = 8
D_FF_EXPERT = 1408

LANE = 128
SUBLANE = 8
HEAD_PAD = LANE
ONES_LANE = 64
NEG = -1e30

IN_PAD = 1920
KR_LANE = 64

ROW_TILE = 512
S5_STEPS = 64
ATT_TILE = 512
LOG2E = math.log2(math.e)
FF_CHUNK = 1408
MOE_TILE = 1024
MOE_ROWS = 320
VMEM_LIMIT = 56 * 1024 * 1024


def _params(sem):
    return pltpu.CompilerParams(dimension_semantics=sem, vmem_limit_bytes=VMEM_LIMIT)


def _rms_mod(x, g, sc, sh):
    ms = jnp.mean(x * x, axis=-1, keepdims=True)
    h = x * lax.rsqrt(ms + EPS) * g
    return h * (1.0 + sc) + sh


def _split3(x):
    hi = x.astype(BF16).astype(F32)
    r = x - hi
    mid = r.astype(BF16).astype(F32)
    lo = (r - mid).astype(BF16).astype(F32)
    return hi, mid, lo


def _ada_kernel(c_ref, w_ref, b_ref, o_ref):
    c = c_ref[...]
    ca = (c * jax.nn.sigmoid(c)).astype(BF16)
    o_ref[0] = jnp.dot(ca, w_ref[0].astype(BF16), preferred_element_type=F32) + b_ref[0]


def _ada_call(c, w_ada, b_ada):
    tn = 1536
    return pl.pallas_call(
        _ada_kernel,
        grid=(DEPTH, 6 * D_MODEL // tn),
        in_specs=[pl.BlockSpec((BATCH, D_MODEL), lambda i, j: (0, 0)),
                  pl.BlockSpec((1, D_MODEL, tn), lambda i, j: (i, 0, j)),
                  pl.BlockSpec((1, 1, tn), lambda i, j: (i, 0, j))],
        out_specs=pl.BlockSpec((1, BATCH, tn), lambda i, j: (i, 0, j)),
        out_shape=jax.ShapeDtypeStruct((DEPTH, BATCH, 6 * D_MODEL), F32),
        compiler_params=_params(("arbitrary", "arbitrary")),
        name="ada",
    )(c, w_ada, b_ada.reshape(DEPTH, 1, 6 * D_MODEL))


_IN_GROUPS = ((0, 256), (256, 512), (512, 640), (640, 768), (768, 1152), (1152, 1536), (1536, 1920))


def _inproj_kernel(x_ref, g_ref, sh_ref, sc_ref, w_ref, *out_refs):
    h = _rms_mod(x_ref[0], g_ref[...], sc_ref[0], sh_ref[0]).astype(BF16)
    for ref, (c0, c1) in zip(out_refs, _IN_GROUPS):
        ref[0] = jnp.dot(h, w_ref[:, c0:c1], preferred_element_type=F32)


def _inproj_call(x, g, sh, sc, w):
    tm = ROW_TILE
    row = lambda b, i: (b, i, 0)
    per_b = lambda b, i: (b, 0, 0)
    const = lambda b, i: (0, 0)
    widths = [c1 - c0 for c0, c1 in _IN_GROUPS]
    return pl.pallas_call(
        _inproj_kernel,
        grid=(BATCH, SEQ // tm),
        in_specs=[pl.BlockSpec((1, tm, D_MODEL), row),
                  pl.BlockSpec((1, D_MODEL), const),
                  pl.BlockSpec((1, 1, D_MODEL), per_b),
                  pl.BlockSpec((1, 1, D_MODEL), per_b),
                  pl.BlockSpec((D_MODEL, IN_PAD), const)],
        out_specs=[pl.BlockSpec((1, tm, wd), row) for wd in widths],
        out_shape=[jax.ShapeDtypeStruct((BATCH, SEQ, wd), F32) for wd in widths],
        compiler_params=_params(("arbitrary", "arbitrary")),
        name="inproj",
    )(x, g, sh, sc, w)


def _pack_w_in(w):
    u, cq, ckv, kr, fq, fk, fv, fg = jnp.split(
        w, (256, 512, 640, 672, 1056, 1440, 1824), axis=1)
    z = lambda n: jnp.zeros((D_MODEL, n), w.dtype)
    krfg = jnp.concatenate([fg, z(KR_LANE - FOX_HEADS), kr, z(LANE - KR_LANE - MLA_ROPE)], axis=1)
    return jnp.concatenate([u, cq, ckv, krfg, fq, fk, fv], axis=1).astype(BF16)


def _s5_kernel(u_ref, bmat_ref, lam_ref, cmat_ref, d_ref, wglu_ref, bglu_ref, gn_ref,
               o_ref, bu_ref, state_ref, *, steps):
    @pl.when(pl.program_id(0) == 0)
    def _():
        state_ref[...] = jnp.zeros_like(state_ref)

    u = u_ref[...]
    bu_ref[...] = jnp.dot(u.astype(BF16), bmat_ref[...], preferred_element_type=F32)
    lr = jnp.broadcast_to(lam_ref[0:1, :], (SUBLANE, N_STATE))
    li = jnp.broadcast_to(lam_ref[1:2, :], (SUBLANE, N_STATE))

    def step(t, carry):
        sr, si = carry
        r0 = pl.multiple_of(t * SUBLANE, SUBLANE)
        nr = lr * sr - li * si + bu_ref[pl.ds(r0, SUBLANE), 0:N_STATE]
        ni = lr * si + li * sr + bu_ref[pl.ds(r0, SUBLANE), N_STATE:2 * N_STATE]
        bu_ref[pl.ds(r0, SUBLANE), 0:N_STATE] = nr
        bu_ref[pl.ds(r0, SUBLANE), N_STATE:2 * N_STATE] = ni
        return nr, ni

    sr, si = lax.fori_loop(0, steps, step,
                           (state_ref[:, 0:N_STATE], state_ref[:, N_STATE:2 * N_STATE]))
    state_ref[:, 0:N_STATE] = sr
    state_ref[:, N_STATE:2 * N_STATE] = si

    y = jnp.dot(bu_ref[...].astype(BF16), cmat_ref[...], preferred_element_type=F32)
    y = jax.nn.gelu(y + d_ref[...] * u)
    gate = jnp.dot(y.astype(BF16), wglu_ref[...], preferred_element_type=F32) + bglu_ref[...]
    o = y * jax.nn.sigmoid(gate)
    ms = jnp.mean(o * o, axis=-1, keepdims=True)
    o_ref[...] = o * lax.rsqrt(ms + EPS) * gn_ref[...]


def _s5_call(u_t, bmat, lam, cmat, d_skip, wglu, bglu, gn):
    rows = S5_STEPS * BATCH
    const = lambda i: (0, 0)
    return pl.pallas_call(
        functools.partial(_s5_kernel, steps=S5_STEPS),
        grid=(SEQ // S5_STEPS,),
        in_specs=[pl.BlockSpec((rows, SSM_WIDTH), lambda i: (i, 0)),
                  pl.BlockSpec((SSM_WIDTH, 2 * N_STATE), const),
                  pl.BlockSpec((2, N_STATE), const),
                  pl.BlockSpec((2 * N_STATE, SSM_WIDTH), const),
                  pl.BlockSpec((1, SSM_WIDTH), const),
                  pl.BlockSpec((SSM_WIDTH, SSM_WIDTH), const),
                  pl.BlockSpec((1, SSM_WIDTH), const),
                  pl.BlockSpec((1, SSM_WIDTH), const)],
        out_specs=pl.BlockSpec((rows, SSM_WIDTH), lambda i: (i, 0)),
        out_shape=jax.ShapeDtypeStruct((SEQ * BATCH, SSM_WIDTH), F32),
        scratch_shapes=[pltpu.VMEM((rows, 2 * N_STATE), F32),
                        pltpu.VMEM((SUBLANE, 2 * N_STATE), F32)],
        compiler_params=_params(("arbitrary",)),
        name="s5",
    )(u_t, bmat, lam, cmat, d_skip, wglu, bglu, gn)


def _s5_operands(lam_re, lam_im, log_dt, b_re, b_im, c_re, c_im):
    dt = jnp.exp(log_dt)[:, None]
    mag = jnp.exp(lam_re * dt)
    lb_re = mag * jnp.cos(lam_im * dt)
    lb_im = mag * jnp.sin(lam_im * dt)
    den = lam_re * lam_re + lam_im * lam_im
    co_re = ((lb_re - 1.0) * lam_re + lb_im * lam_im) / den
    co_im = (lb_im * lam_re - (lb_re - 1.0) * lam_im) / den
    bb_re = co_re[..., None] * b_re - co_im[..., None] * b_im
    bb_im = co_re[..., None] * b_im + co_im[..., None] * b_re
    eye = jnp.eye(N_SSM_GROUPS, dtype=F32)
    blk_b = lambda m: jnp.einsum("gpc,gh->gchp", m, eye).reshape(SSM_WIDTH, N_STATE)
    bmat = jnp.concatenate([blk_b(bb_re), blk_b(bb_im)], axis=1).astype(BF16)
    blk_c = lambda m: jnp.einsum("gcp,gh->gphc", m, eye).reshape(N_STATE, SSM_WIDTH)
    cmat = jnp.concatenate([blk_c(c_re), -blk_c(c_im)], axis=0).astype(BF16)
    lam = jnp.stack([lb_re.reshape(N_STATE), lb_im.reshape(N_STATE)], axis=0)
    return bmat, lam, cmat


def _rope(x, cos, sina, sinb):
    return x * cos + pltpu.roll(x, LANE - 16, 1) * sina + pltpu.roll(x, 16, 1) * sinb


def _rope_tables(positions):
    half = MLA_ROPE // 2
    inv = ROPE_BASE ** (-jnp.arange(half, dtype=F32) / half)
    ang = positions.astype(F32)[..., None] * inv
    cos, sin = jnp.cos(ang), jnp.sin(ang)
    shp = positions.shape
    one = lambda n: jnp.ones(shp + (n,), F32)
    zero = lambda n: jnp.zeros(shp + (n,), F32)
    cos_t = jnp.concatenate([one(MLA_NOPE), cos, cos, zero(LANE - MLA_QK)], axis=-1)
    sina_t = jnp.concatenate([zero(MLA_NOPE), -sin, zero(LANE - MLA_NOPE - half)], axis=-1)
    sinb_t = jnp.concatenate([zero(MLA_NOPE + half), sin, zero(LANE - MLA_QK)], axis=-1)
    return cos_t, sina_t, sinb_t


def _store_key_blocks(kt_ref, h, k):
    kt = k.T
    for s in range(k.shape[0] // ATT_TILE):
        kt_ref[0, h, s] = kt[:, s * ATT_TILE:(s + 1) * ATT_TILE].astype(BF16)


_KT_SPEC = lambda heads, tl: pl.BlockSpec((1, heads, tl // ATT_TILE, HEAD_PAD, ATT_TILE),
                                          lambda b, i: (b, 0, i, 0, 0))
_KT_SHAPE = lambda heads: jax.ShapeDtypeStruct((BATCH, heads, SEQ // ATT_TILE, HEAD_PAD, ATT_TILE), BF16)


def _mla_prep_kernel(cq_ref, ckv_ref, krfg_ref, cos_ref, sina_ref, sinb_ref,
                     qn_ref, kvn_ref, wq_ref, wk_ref, wv_ref, gq_ref, gk_ref,
                     q_ref, k_ref, v_ref):
    tl = cq_ref.shape[1]
    lane = lax.broadcasted_iota(jnp.int32, (tl, LANE), 1)
    cos, sina, sinb = cos_ref[0], sina_ref[0], sinb_ref[0]

    cq = cq_ref[0]
    cqn = (cq * lax.rsqrt(jnp.mean(cq * cq, axis=-1, keepdims=True) + EPS) * qn_ref[...]).astype(BF16)
    ckv = ckv_ref[0]
    ckvn = (ckv * lax.rsqrt(jnp.mean(ckv * ckv, axis=-1, keepdims=True) + EPS) * kvn_ref[...]).astype(BF16)
    kr = jnp.where((lane >= KR_LANE) & (lane < KR_LANE + MLA_ROPE), krfg_ref[0], 0.0)

    for h in range(MLA_HEADS):
        q = jnp.dot(cqn, wq_ref[h], preferred_element_type=F32)
        q = q * lax.rsqrt(jnp.sum(q * q, axis=-1, keepdims=True) / MLA_QK + EPS) * gq_ref[...]
        q = _rope(q, cos, sina, sinb) * (LOG2E / math.sqrt(MLA_QK))
        q_ref[0, h] = q.astype(BF16)

        k = jnp.dot(ckvn, wk_ref[h], preferred_element_type=F32) + kr
        k = k * lax.rsqrt(jnp.sum(k * k, axis=-1, keepdims=True) / MLA_QK + EPS) * gk_ref[...]
        _store_key_blocks(k_ref, h, _rope(k, cos, sina, sinb))

        v = jnp.dot(ckvn, wv_ref[h], preferred_element_type=F32)
        v_ref[0, h] = jnp.where(lane == ONES_LANE, 1.0, v).astype(BF16)


def _mla_prep_call(cq, ckv, krfg, tabs, qn, kvn, wq, wk, wv, gq, gk):
    tl = ROW_TILE
    row = lambda b, i: (b, i, 0)
    c2 = lambda b, i: (0, 0)
    c3 = lambda b, i: (0, 0, 0)
    head_out = pl.BlockSpec((1, MLA_HEADS, tl, HEAD_PAD), lambda b, i: (b, 0, i, 0))
    head_shape = jax.ShapeDtypeStruct((BATCH, MLA_HEADS, SEQ, HEAD_PAD), BF16)
    return pl.pallas_call(
        _mla_prep_kernel,
        grid=(BATCH, SEQ // tl),
        in_specs=[pl.BlockSpec((1, tl, MLA_Q_RANK), row),
                  pl.BlockSpec((1, tl, MLA_KV_RANK), row),
                  pl.BlockSpec((1, tl, LANE), row),
                  pl.BlockSpec((1, tl, LANE), row),
                  pl.BlockSpec((1, tl, LANE), row),
                  pl.BlockSpec((1, tl, LANE), row),
                  pl.BlockSpec((1, MLA_Q_RANK), c2),
                  pl.BlockSpec((1, MLA_KV_RANK), c2),
                  pl.BlockSpec((MLA_HEADS, MLA_Q_RANK, HEAD_PAD), c3),
                  pl.BlockSpec((MLA_HEADS, MLA_KV_RANK, HEAD_PAD), c3),
                  pl.BlockSpec((MLA_HEADS, MLA_KV_RANK, HEAD_PAD), c3),
                  pl.BlockSpec((1, HEAD_PAD), c2),
                  pl.BlockSpec((1, HEAD_PAD), c2)],
        out_specs=[head_out, _KT_SPEC(MLA_HEADS, tl), head_out],
        out_shape=[head_shape, _KT_SHAPE(MLA_HEADS), head_shape],
        compiler_params=_params(("arbitrary", "arbitrary")),
        name="mla_prep",
    )(cq, ckv, krfg, *tabs, qn, kvn, wq, wk, wv, gq, gk)


def _pad_lanes(a, n=HEAD_PAD):
    return jnp.pad(a, [(0, 0)] * (a.ndim - 1) + [(0, n - a.shape[-1])])


def _mla_weights(w_uq, w_ukv, gq, gk):
    wq = _pad_lanes(w_uq.reshape(MLA_Q_RANK, MLA_HEADS, MLA_QK).transpose(1, 0, 2)).astype(BF16)
    wkv = w_ukv.reshape(MLA_KV_RANK, MLA_HEADS, MLA_NOPE + MLA_V).transpose(1, 0, 2)
    wk = _pad_lanes(wkv[..., :MLA_NOPE]).astype(BF16)
    wv = _pad_lanes(wkv[..., MLA_NOPE:]).astype(BF16)
    return wq, wk, wv, _pad_lanes(gq[None, :]), _pad_lanes(gk[None, :])


def _fox_prep_kernel(fq_ref, fk_ref, fv_ref, krfg_ref, bf_ref, gq_ref, gk_ref,
                     q_ref, k_ref, v_ref, carry_ref):
    tl = fq_ref.shape[1]
    lane = lax.broadcasted_iota(jnp.int32, (tl, LANE), 1)

    @pl.when(pl.program_id(1) == 0)
    def _():
        carry_ref[...] = jnp.zeros_like(carry_ref)

    logf = jax.nn.log_sigmoid(krfg_ref[0] + bf_ref[...])
    logf = jnp.where(lane < FOX_HEADS, logf, 0.0)
    r_i = lax.broadcasted_iota(jnp.int32, (tl, tl), 0)
    c_i = lax.broadcasted_iota(jnp.int32, (tl, tl), 1)
    tri = jnp.where(c_i <= r_i, 1.0, 0.0).astype(BF16)
    cum = carry_ref[0:1, :]
    for piece in _split3(logf):
        cum = cum + jnp.dot(tri, piece.astype(BF16), preferred_element_type=F32)
    carry_ref[0:1, :] = cum[tl - 1:tl, :]

    def head_lanes(ref, h):
        x = ref[0, :, (h // 2) * LANE:(h // 2 + 1) * LANE]
        if h % 2:
            x = pltpu.roll(x, FOX_HEAD_DIM, 1)
        return jnp.where(lane < FOX_HEAD_DIM, x, 0.0)

    def normed(x, g):
        return x * lax.rsqrt(jnp.sum(x * x, axis=-1, keepdims=True) / FOX_HEAD_DIM + EPS) * g

    for h in range(FOX_HEADS):
        c_hi, c_mid, c_lo = _split3(cum[:, h:h + 1] * LOG2E)
        q = normed(head_lanes(fq_ref, h), gq_ref[...]) * (LOG2E / math.sqrt(FOX_HEAD_DIM))
        q = jnp.where(lane == 64, c_hi, q)
        q = jnp.where(lane == 65, c_mid, q)
        q = jnp.where(lane == 66, c_lo, q)
        q = jnp.where((lane >= 67) & (lane < 70), 1.0, q)
        q_ref[0, h] = q.astype(BF16)

        k = normed(head_lanes(fk_ref, h), gk_ref[...])
        k = jnp.where((lane >= 64) & (lane < 67), 1.0, k)
        k = jnp.where(lane == 67, -c_hi, k)
        k = jnp.where(lane == 68, -c_mid, k)
        k = jnp.where(lane == 69, -c_lo, k)
        _store_key_blocks(k_ref, h, k)

        v = head_lanes(fv_ref, h)
        v_ref[0, h] = jnp.where(lane == ONES_LANE, 1.0, v).astype(BF16)


def _fox_prep_call(fq, fk, fv, krfg, bf, gq, gk):
    tl = ROW_TILE
    row = lambda b, i: (b, i, 0)
    c2 = lambda b, i: (0, 0)
    head_out = pl.BlockSpec((1, FOX_HEADS, tl, HEAD_PAD), lambda b, i: (b, 0, i, 0))
    head_shape = jax.ShapeDtypeStruct((BATCH, FOX_HEADS, SEQ, HEAD_PAD), BF16)
    return pl.pallas_call(
        _fox_prep_kernel,
        grid=(BATCH, SEQ // tl),
        in_specs=[pl.BlockSpec((1, tl, ATT_WIDTH), row),
                  pl.BlockSpec((1, tl, ATT_WIDTH), row),
                  pl.BlockSpec((1, tl, ATT_WIDTH), row),
                  pl.BlockSpec((1, tl, LANE), row),
                  pl.BlockSpec((1, LANE), c2),
                  pl.BlockSpec((1, LANE), c2),
                  pl.BlockSpec((1, LANE), c2)],
        out_specs=[head_out, _KT_SPEC(FOX_HEADS, tl), head_out],
        out_shape=[head_shape, _KT_SHAPE(FOX_HEADS), head_shape],
        scratch_shapes=[pltpu.VMEM((SUBLANE, LANE), F32)],
        compiler_params=_params(("arbitrary", "arbitrary")),
        name="fox_prep",
    )(fq, fk, fv, krfg, bf, gq, gk)


def _flash_kernel(q_ref, kt_ref, v_ref, o_ref, s_ref, m_ref, acc_ref, *, tile, chunk):
    qi = pl.program_id(2)
    n_blk = qi + 1
    per_chunk = tile // chunk
    qry_i = lax.broadcasted_iota(jnp.int32, (tile, tile), 0)
    key_i = lax.broadcasted_iota(jnp.int32, (tile, tile), 1)
    chunk_gap = key_i // chunk - qry_i // chunk
    lane = lax.broadcasted_iota(jnp.int32, (tile, HEAD_PAD), 1)

    def in_pairs(fn):
        def quad(jj, carry):
            for u in range(4):
                fn(4 * jj + u)
            return carry
        lax.fori_loop(0, n_blk // 4, quad, 0)
        done = (n_blk // 4) * 4

        @pl.when(n_blk % 4 >= 2)
        def _():
            fn(done)
            fn(done + 1)

        @pl.when(n_blk % 2 == 1)
        def _():
            fn(qi)

    def scores(j):
        allowed = chunk_gap <= (qi - j) * per_chunk
        for hh in range(2):
            s = jnp.dot(q_ref[0, hh], kt_ref[0, hh, j], preferred_element_type=F32)
            s = jnp.where(allowed, s, NEG)
            s_ref[hh, j] = s
            mr = m_ref[hh]
            for c in range(tile // LANE):
                mr = jnp.maximum(mr, s[:, c * LANE:(c + 1) * LANE])
            m_ref[hh] = mr

    m_ref[...] = jnp.full(m_ref.shape, NEG, F32)
    in_pairs(scores)
    ms = [jnp.max(m_ref[hh], axis=1, keepdims=True) for hh in range(2)]

    def weighted(j):
        k0 = pl.multiple_of(j * tile, tile)
        for hh in range(2):
            p = jnp.exp2(s_ref[hh, j] - ms[hh]).astype(BF16)
            acc_ref[hh] += jnp.dot(p, v_ref[0, hh, pl.ds(k0, tile), :], preferred_element_type=F32)

    acc_ref[...] = jnp.zeros(acc_ref.shape, F32)
    in_pairs(weighted)
    outs = [acc_ref[hh] / acc_ref[hh][:, ONES_LANE:ONES_LANE + 1] for hh in range(2)]
    o_ref[0] = jnp.where(lane < 64, outs[0], pltpu.roll(outs[1], 64, 1))


def _flash_call(q, kt, v, chunk):
    tile = ATT_TILE
    heads = q.shape[1]
    n_blocks = SEQ // tile
    return pl.pallas_call(
        functools.partial(_flash_kernel, tile=tile, chunk=chunk),
        grid=(BATCH, heads // 2, n_blocks),
        in_specs=[pl.BlockSpec((1, 2, tile, HEAD_PAD), lambda b, hp, i: (b, hp, i, 0)),
                  pl.BlockSpec((1, 2, n_blocks, HEAD_PAD, tile), lambda b, hp, i: (b, hp, 0, 0, 0)),
                  pl.BlockSpec((1, 2, SEQ, HEAD_PAD), lambda b, hp, i: (b, hp, 0, 0))],
        out_specs=pl.BlockSpec((1, tile, LANE), lambda b, hp, i: (b, i, hp)),
        out_shape=jax.ShapeDtypeStruct((BATCH, SEQ, ATT_WIDTH), F32),
        scratch_shapes=[pltpu.VMEM((2, n_blocks, tile, tile), F32),
                        pltpu.VMEM((2, tile, LANE), F32),
                        pltpu.VMEM((2, tile, HEAD_PAD), F32)],
        compiler_params=_params(("arbitrary", "arbitrary", "arbitrary")),
        name="flash_chunk%d" % chunk,
    )(q, kt, v)


def _merge_kernel(ssm_ref, mla_ref, fox_ref, x_ref, g1_ref, gm_ref, gf_ref,
                  ws_ref, wm_ref, wf_ref, o_ref):
    def normed(ref, g_ref):
        a = ref[0]
        return (a * lax.rsqrt(jnp.mean(a * a, axis=-1, keepdims=True) + EPS) * g_ref[...]).astype(BF16)

    mix = jnp.dot(ssm_ref[0].astype(BF16), ws_ref[...], preferred_element_type=F32)
    mix = mix + jnp.dot(normed(mla_ref, gm_ref), wm_ref[...], preferred_element_type=F32)
    mix = mix + jnp.dot(normed(fox_ref, gf_ref), wf_ref[...], preferred_element_type=F32)
    o_ref[0] = x_ref[0] + g1_ref[0] * mix


def _merge_call(o_ssm, o_mla, o_fox, x, g1, gm, gf, ws, wm, wf):
    tm = ROW_TILE
    row = lambda b, i: (b, i, 0)
    c2 = lambda b, i: (0, 0)
    return pl.pallas_call(
        _merge_kernel,
        grid=(BATCH, SEQ // tm),
        in_specs=[pl.BlockSpec((1, tm, SSM_WIDTH), row),
                  pl.BlockSpec((1, tm, ATT_WIDTH), row),
                  pl.BlockSpec((1, tm, ATT_WIDTH), row),
                  pl.BlockSpec((1, tm, D_MODEL), row),
                  pl.BlockSpec((1, 1, D_MODEL), lambda b, i: (b, 0, 0)),
                  pl.BlockSpec((1, ATT_WIDTH), c2),
                  pl.BlockSpec((1, ATT_WIDTH), c2),
                  pl.BlockSpec((SSM_WIDTH, D_MODEL), c2),
                  pl.BlockSpec((ATT_WIDTH, D_MODEL), c2),
                  pl.BlockSpec((ATT_WIDTH, D_MODEL), c2)],
        out_specs=pl.BlockSpec((1, tm, D_MODEL), row),
        out_shape=jax.ShapeDtypeStruct((BATCH, SEQ, D_MODEL), F32),
        compiler_params=_params(("arbitrary", "arbitrary")),
        name="merge",
    )(o_ssm, o_mla, o_fox, x, g1, gm, gf, ws, wm, wf)


def _ffn_kernel(x_ref, g_ref, sh_ref, sc_ref, g2_ref, wg_ref, wu_ref, wd_ref, o_ref, h_ref, acc_ref):
    c = pl.program_id(2)

    @pl.when(c == 0)
    def _():
        h_ref[...] = _rms_mod(x_ref[0], g_ref[...], sc_ref[0], sh_ref[0]).astype(BF16)
        acc_ref[...] = jnp.zeros_like(acc_ref)

    h = h_ref[...]
    gate = jnp.dot(h, wg_ref[0], preferred_element_type=F32)
    up = jnp.dot(h, wu_ref[0], preferred_element_type=F32)
    a = (gate * jax.nn.sigmoid(gate) * up).astype(BF16)
    acc_ref[...] += jnp.dot(a, wd_ref[0], preferred_element_type=F32)

    @pl.when(c == pl.num_programs(2) - 1)
    def _():
        o_ref[0] = x_ref[0] + g2_ref[0] * acc_ref[...]


def _ffn_call(x, g, sh, sc, g2, wg, wu, wd):
    tm = MOE_TILE
    n_chunks = wg.shape[0]
    row = lambda b, i, c: (b, i, 0)
    per_b = lambda b, i, c: (b, 0, 0)
    chunk = lambda b, i, c: (c, 0, 0)
    return pl.pallas_call(
        _ffn_kernel,
        grid=(BATCH, SEQ // tm, n_chunks),
        in_specs=[pl.BlockSpec((1, tm, D_MODEL), row),
                  pl.BlockSpec((1, D_MODEL), lambda b, i, c: (0, 0)),
                  pl.BlockSpec((1, 1, D_MODEL), per_b),
                  pl.BlockSpec((1, 1, D_MODEL), per_b),
                  pl.BlockSpec((1, 1, D_MODEL), per_b),
                  pl.BlockSpec((1, D_MODEL, FF_CHUNK), chunk),
                  pl.BlockSpec((1, D_MODEL, FF_CHUNK), chunk),
                  pl.BlockSpec((1, FF_CHUNK, D_MODEL), chunk)],
        out_specs=pl.BlockSpec((1, tm, D_MODEL), row),
        out_shape=jax.ShapeDtypeStruct((BATCH, SEQ, D_MODEL), F32),
        scratch_shapes=[pltpu.VMEM((tm, D_MODEL), BF16), pltpu.VMEM((tm, D_MODEL), F32)],
        compiler_params=_params(("arbitrary", "arbitrary", "arbitrary")),
        name="ffn_dense",
    )(x, g, sh, sc, g2, wg, wu, wd)


def _router_kernel(x_ref, g_ref, sh_ref, sc_ref, w_ref, b_ref, comb_ref, rank_ref, rankt_ref, count_ref):
    tm = x_ref.shape[1]
    h = _rms_mod(x_ref[0], g_ref[...], sc_ref[0], sh_ref[0])
    h_hi = h.astype(BF16)
    h_lo = (h - h_hi.astype(F32)).astype(BF16)
    w_hi, w_lo = w_ref[0], w_ref[1]
    logits = (jnp.dot(h_hi, w_hi, preferred_element_type=F32)
              + jnp.dot(h_lo, w_hi, preferred_element_type=F32)
              + jnp.dot(h_hi, w_lo, preferred_element_type=F32)) + b_ref[...]
    lane = lax.broadcasted_iota(jnp.int32, logits.shape, 1)
    logits = jnp.where(lane < N_EXPERTS, logits, -jnp.inf)
    m1 = jnp.max(logits, axis=-1, keepdims=True)
    i1 = jnp.min(jnp.where(logits == m1, lane, LANE), axis=-1, keepdims=True)
    rest = jnp.where(lane == i1, -jnp.inf, logits)
    m2 = jnp.max(rest, axis=-1, keepdims=True)
    i2 = jnp.min(jnp.where(rest == m2, lane, LANE), axis=-1, keepdims=True)
    e = jnp.exp(m2 - m1)
    p1 = 1.0 / (1.0 + e)
    comb_ref[0] = jnp.where(lane == i1, p1, 0.0) + jnp.where(lane == i2, e * p1, 0.0)

    chosen = (lane == i1) | (lane == i2)
    chosen_f = jnp.where(chosen, 1.0, 0.0)
    r_i = lax.broadcasted_iota(jnp.int32, (tm, tm), 0)
    c_i = lax.broadcasted_iota(jnp.int32, (tm, tm), 1)
    earlier = jnp.where(c_i < r_i, 1.0, 0.0).astype(BF16)
    rank = jnp.dot(earlier, chosen_f.astype(BF16), preferred_element_type=F32)
    rank = jnp.where(chosen, rank, -1.0)
    rank_ref[0] = rank
    rankt_ref[0] = rank.T[0:SUBLANE, :]
    count_ref[0] = jnp.sum(chosen_f, axis=0, keepdims=True)


def _router_call(x, g, sh, sc, w, b):
    tm = MOE_TILE
    tiles = SEQ // tm
    row = lambda b_, i: (b_, i, 0)
    per_b = lambda b_, i: (b_, 0, 0)
    per_tile = lambda b_, i: (b_ * tiles + i, 0, 0)
    return pl.pallas_call(
        _router_kernel,
        grid=(BATCH, tiles),
        in_specs=[pl.BlockSpec((1, tm, D_MODEL), row),
                  pl.BlockSpec((1, D_MODEL), lambda b_, i: (0, 0)),
                  pl.BlockSpec((1, 1, D_MODEL), per_b),
                  pl.BlockSpec((1, 1, D_MODEL), per_b),
                  pl.BlockSpec((2, D_MODEL, LANE), lambda b_, i: (0, 0, 0)),
                  pl.BlockSpec((1, LANE), lambda b_, i: (0, 0))],
        out_specs=[pl.BlockSpec((1, tm, LANE), row),
                   pl.BlockSpec((1, tm, LANE), row),
                   pl.BlockSpec((1, SUBLANE, tm), per_tile),
                   pl.BlockSpec((1, 1, LANE), per_tile)],
        out_shape=[jax.ShapeDtypeStruct((BATCH, SEQ, LANE), F32),
                   jax.ShapeDtypeStruct((BATCH, SEQ, LANE), F32),
                   jax.ShapeDtypeStruct((BATCH * tiles, SUBLANE, tm), F32),
                   jax.ShapeDtypeStruct((BATCH * tiles, 1, LANE), F32)],
        compiler_params=_params(("arbitrary", "arbitrary")),
        name="router",
    )(x, g, sh, sc, w, b)


def _moe_kernel(count_ref, x_ref, g_ref, sh_ref, sc_ref, g2_ref, comb_ref, rank_ref, rankt_ref,
                wg_ref, wu_ref, wd_ref, o_ref, h_ref):
    tm = x_ref.shape[1]
    e = pl.program_id(1)

    @pl.when(e == 0)
    def _():
        x = x_ref[0]
        h_ref[...] = _rms_mod(x, g_ref[...], sc_ref[0], sh_ref[0]).astype(BF16)
        o_ref[0] = x

    lane = lax.broadcasted_iota(jnp.int32, (tm, LANE), 1)
    mine = lane == e
    rank_col = jnp.sum(jnp.where(mine, rank_ref[0], 0.0), axis=-1, keepdims=True)
    gate_col = jnp.sum(jnp.where(mine, comb_ref[0], 0.0), axis=-1, keepdims=True)
    rank_row = rankt_ref[0, pl.ds(e, 1), :]
    slot_sub = lax.broadcasted_iota(jnp.int32, (MOE_ROWS, tm), 0).astype(F32)
    slot_lane = lax.broadcasted_iota(jnp.int32, (tm, MOE_ROWS), 1).astype(F32)
    count = count_ref[pl.program_id(0) * N_EXPERTS + e]

    def block(sb, carry):
        base = (sb * MOE_ROWS).astype(F32)
        pick = jnp.where(rank_row - base == slot_sub, 1.0, 0.0).astype(BF16)
        rows = jnp.dot(pick, h_ref[...], preferred_element_type=F32).astype(BF16)
        gate = jnp.dot(rows, wg_ref[0], preferred_element_type=F32)
        up = jnp.dot(rows, wu_ref[0], preferred_element_type=F32)
        a = (gate * jax.nn.sigmoid(gate) * up).astype(BF16)
        y = jnp.dot(a, wd_ref[0], preferred_element_type=F32).astype(BF16)
        place = jnp.where(rank_col - base == slot_lane, 1.0, 0.0).astype(BF16)
        back = jnp.dot(place, y, preferred_element_type=F32)
        o_ref[0] += g2_ref[0] * (gate_col * back)
        return carry

    lax.fori_loop(0, (count + MOE_ROWS - 1) // MOE_ROWS, block, 0)


def _moe_call(x, g, sh, sc, g2, comb, rank, rankt, counts, wg, wu, wd):
    tm = MOE_TILE
    tiles = SEQ // tm
    n_tiles = BATCH * tiles
    row = lambda i, e, cnt: (i, 0, 0)
    per_b = lambda i, e, cnt: (i // tiles, 0, 0)
    expert = lambda i, e, cnt: (e, 0, 0)
    as_tiles = lambda a: a.reshape(n_tiles, tm, a.shape[-1])
    grid_spec = pltpu.PrefetchScalarGridSpec(
        num_scalar_prefetch=1,
        grid=(n_tiles, N_EXPERTS),
        in_specs=[pl.BlockSpec((1, tm, D_MODEL), row),
                  pl.BlockSpec((1, D_MODEL), lambda i, e, cnt: (0, 0)),
                  pl.BlockSpec((1, 1, D_MODEL), per_b),
                  pl.BlockSpec((1, 1, D_MODEL), per_b),
                  pl.BlockSpec((1, 1, D_MODEL), per_b),
                  pl.BlockSpec((1, tm, LANE), row),
                  pl.BlockSpec((1, tm, LANE), row),
                  pl.BlockSpec((1, SUBLANE, tm), row),
                  pl.BlockSpec((1, D_MODEL, D_FF_EXPERT), expert),
                  pl.BlockSpec((1, D_MODEL, D_FF_EXPERT), expert),
                  pl.BlockSpec((1, D_FF_EXPERT, D_MODEL), expert)],
        out_specs=pl.BlockSpec((1, tm, D_MODEL), row),
        scratch_shapes=[pltpu.VMEM((tm, D_MODEL), BF16)],
    )
    out = pl.pallas_call(
        _moe_kernel,
        grid_spec=grid_spec,
        out_shape=jax.ShapeDtypeStruct((n_tiles, tm, D_MODEL), F32),
        compiler_params=_params(("arbitrary", "arbitrary")),
        name="moe_experts",
    )(counts, as_tiles(x), g, sh, sc, g2, as_tiles(comb), as_tiles(rank), rankt, wg, wu, wd)
    return out.reshape(BATCH, SEQ, D_MODEL)


def kernel(x, c, positions, norm_mix, norm_ffn, w_ada, b_ada, w_in, ssm_lam_re, ssm_lam_im, ssm_log_dt, ssm_b_re, ssm_b_im, ssm_c_re, ssm_c_im, ssm_d, ssm_w_glu, ssm_b_glu, mla_q_norm, mla_kv_norm, mla_w_uq, mla_w_ukv, mla_qk_gq, mla_qk_gk, fox_b_f, fox_qk_gq, fox_qk_gk, out_norm, w_out, ffn_w_gate, ffn_w_up, ffn_w_down, moe_w_router, moe_b_router, moe_w_gate, moe_w_up, moe_w_down):
    tabs = _rope_tables(positions)
    ada = _ada_call(c, w_ada, b_ada)
    ada = ada.reshape(DEPTH, BATCH, 6, 1, D_MODEL)
    row2 = lambda a: a[None, :]

    for i in range(DEPTH):
        sh1, sc1, g1, sh2, sc2, g2 = (ada[i, :, n] for n in range(6))

        u, cq, ckv, krfg, fq, fk, fv = _inproj_call(x, row2(norm_mix[i]), sh1, sc1, _pack_w_in(w_in[i]))

        bmat, lam, cmat = _s5_operands(ssm_lam_re[i], ssm_lam_im[i], ssm_log_dt[i],
                                       ssm_b_re[i], ssm_b_im[i], ssm_c_re[i], ssm_c_im[i])
        u_t = u.transpose(1, 0, 2).reshape(SEQ * BATCH, SSM_WIDTH)
        o_ssm = _s5_call(u_t, bmat, lam, cmat, row2(ssm_d[i]), ssm_w_glu[i].astype(BF16),
                         row2(ssm_b_glu[i]), row2(out_norm[i, :SSM_WIDTH]))
        o_ssm = o_ssm.reshape(SEQ, BATCH, SSM_WIDTH).transpose(1, 0, 2)

        wq, wk, wv, gq, gk = _mla_weights(mla_w_uq[i], mla_w_ukv[i], mla_qk_gq[i], mla_qk_gk[i])
        mq, mk, mv = _mla_prep_call(cq, ckv, krfg, tabs, row2(mla_q_norm[i]), row2(mla_kv_norm[i]),
                                    wq, wk, wv, gq, gk)
        o_mla = _flash_call(mq, mk, mv, CHUNK)

        xq, xk, xv = _fox_prep_call(fq, fk, fv, krfg, _pad_lanes(row2(fox_b_f[i]), LANE),
                                    _pad_lanes(row2(fox_qk_gq[i]), LANE), _pad_lanes(row2(fox_qk_gk[i]), LANE))
        o_fox = _flash_call(xq, xk, xv, 1)

        e1, e2 = SSM_WIDTH, SSM_WIDTH + ATT_WIDTH
        wo = w_out[i].astype(BF16)
        x = _merge_call(o_ssm, o_mla, o_fox, x, g1, row2(out_norm[i, e1:e2]), row2(out_norm[i, e2:]),
                        wo[:e1], wo[e1:e2], wo[e2:])

        j = i // 2
        if i % 2 == 0:
            split = lambda w: w.reshape(D_MODEL, D_FF // FF_CHUNK, FF_CHUNK).transpose(1, 0, 2).astype(BF16)
            wd = ffn_w_down[j].reshape(D_FF // FF_CHUNK, FF_CHUNK, D_MODEL).astype(BF16)
            x = _ffn_call(x, row2(norm_ffn[i]), sh2, sc2, g2, split(ffn_w_gate[j]), split(ffn_w_up[j]), wd)
        else:
            wr = _pad_lanes(moe_w_router[j], LANE)
            wr_hi = wr.astype(BF16)
            wr_lo = (wr - wr_hi.astype(F32)).astype(BF16)
            comb, rank, rankt, counts = _router_call(x, row2(norm_ffn[i]), sh2, sc2, jnp.stack([wr_hi, wr_lo]),
                                                     _pad_lanes(row2(moe_b_router[j]), LANE))
            counts = counts[:, 0, :N_EXPERTS].astype(jnp.int32).reshape(-1)
            x = _moe_call(x, row2(norm_ffn[i]), sh2, sc2, g2, comb, rank, rankt, counts,
                          moe_w_gate[j].astype(BF16), moe_w_up[j].astype(BF16), moe_w_down[j].astype(BF16))
    return x
```

```python
import functools
import math

import jax
import jax.numpy as jnp
import numpy as np
from jax import lax
from jax.experimental import pallas as pl
from jax.experimental.pallas import tpu as pltpu

F32 = jnp.float32
BF16 = jnp.bfloat16

D_MODEL = 1024
BATCH = 8
SEQ = 4096
DEPTH = 4
CHUNK = 64
EPS = 1e-6

SSM_WIDTH = 256
SSM_GROUP = 16
N_SSM_GROUPS = 16
SSM_STATE = 64
N_STATE = N_SSM_GROUPS * SSM_STATE

MLA_HEADS = 6
MLA_Q_RANK = 256
MLA_KV_RANK = 128
MLA_NOPE = 64
MLA_ROPE = 32
MLA_V = 64
MLA_QK = 96
ROPE_BASE = 10000.0

FOX_HEADS = 6
FOX_HEAD_DIM = 64
ATT_WIDTH = 384

D_FF = 2816
N_EXPERTS = 8
D_FF_EXPERT = 1408

LANE = 128
SUBLANE = 8
HEAD_PAD = LANE
ONES_LANE = 64
NEG = -1e30

IN_PAD = 1920
KR_LANE = 64

ROW_TILE = 512
S5_STEPS = 64
ATT_TILE = 512
LOG2E = math.log2(math.e)
FF_CHUNK = 1408
MOE_TILE = 1024
MOE_ROWS = 256
VMEM_LIMIT = 56 * 1024 * 1024


def _params(sem):
    return pltpu.CompilerParams(dimension_semantics=sem, vmem_limit_bytes=VMEM_LIMIT)


def _rms_mod(x, g, sc, sh):
    ms = jnp.mean(x * x, axis=-1, keepdims=True)
    h = x * lax.rsqrt(ms + EPS) * g
    return h * (1.0 + sc) + sh


def _split3(x):
    hi = x.astype(BF16).astype(F32)
    r = x - hi
    mid = r.astype(BF16).astype(F32)
    lo = (r - mid).astype(BF16).astype(F32)
    return hi, mid, lo


def _ada_kernel(c_ref, w_ref, b_ref, o_ref):
    c = c_ref[...]
    ca = (c * jax.nn.sigmoid(c)).astype(BF16)
    o_ref[0] = jnp.dot(ca, w_ref[0].astype(BF16), preferred_element_type=F32) + b_ref[0]


def _ada_call(c, w_ada, b_ada):
    tn = 1536
    return pl.pallas_call(
        _ada_kernel,
        grid=(DEPTH, 6 * D_MODEL // tn),
        in_specs=[pl.BlockSpec((BATCH, D_MODEL), lambda i, j: (0, 0)),
                  pl.BlockSpec((1, D_MODEL, tn), lambda i, j: (i, 0, j)),
                  pl.BlockSpec((1, 1, tn), lambda i, j: (i, 0, j))],
        out_specs=pl.BlockSpec((1, BATCH, tn), lambda i, j: (i, 0, j)),
        out_shape=jax.ShapeDtypeStruct((DEPTH, BATCH, 6 * D_MODEL), F32),
        compiler_params=_params(("arbitrary", "arbitrary")),
        name="ada",
    )(c, w_ada, b_ada.reshape(DEPTH, 1, 6 * D_MODEL))


_IN_GROUPS = ((0, 256), (256, 512), (512, 640), (640, 768), (768, 1152), (1152, 1536), (1536, 1920))


def _inproj_kernel(x_ref, g_ref, sh_ref, sc_ref, w_ref, *out_refs):
    parts = 4
    step = x_ref.shape[1] // parts
    rows = [slice(r * step, (r + 1) * step) for r in range(parts)]
    normed = lambda r: _rms_mod(x_ref[0, rows[r]], g_ref[...], sc_ref[0], sh_ref[0]).astype(BF16)
    h_next = normed(0)
    for r in range(parts):
        h = h_next
        if r + 1 < parts:
            h_next = normed(r + 1)
        proj = jnp.dot(h, w_ref[...], preferred_element_type=F32)
        for ref, (c0, c1) in zip(out_refs, _IN_GROUPS):
            ref[0, rows[r]] = proj[:, c0:c1]


def _inproj_call(x, g, sh, sc, w):
    tm = ROW_TILE
    row = lambda b, i: (b, i, 0)
    per_b = lambda b, i: (b, 0, 0)
    const = lambda b, i: (0, 0)
    widths = [c1 - c0 for c0, c1 in _IN_GROUPS]
    return pl.pallas_call(
        _inproj_kernel,
        grid=(BATCH, SEQ // tm),
        in_specs=[pl.BlockSpec((1, tm, D_MODEL), row),
                  pl.BlockSpec((1, D_MODEL), const),
                  pl.BlockSpec((1, 1, D_MODEL), per_b),
                  pl.BlockSpec((1, 1, D_MODEL), per_b),
                  pl.BlockSpec((D_MODEL, IN_PAD), const)],
        out_specs=[pl.BlockSpec((1, tm, wd), row) for wd in widths],
        out_shape=[jax.ShapeDtypeStruct((BATCH, SEQ, wd), F32) for wd in widths],
        compiler_params=_params(("arbitrary", "arbitrary")),
        name="inproj",
    )(x, g, sh, sc, w)


def _pack_w_in(w):
    u, cq, ckv, kr, fq, fk, fv, fg = jnp.split(
        w, (256, 512, 640, 672, 1056, 1440, 1824), axis=1)
    z = lambda n: jnp.zeros((D_MODEL, n), w.dtype)
    krfg = jnp.concatenate([fg, z(KR_LANE - FOX_HEADS), kr, z(LANE - KR_LANE - MLA_ROPE)], axis=1)
    return jnp.concatenate([u, cq, ckv, krfg, fq, fk, fv], axis=1).astype(BF16)


def _s5_kernel(u_ref, bmat_ref, lam_ref, cmat_ref, d_ref, wglu_ref, bglu_ref, gn_ref,
               o_ref, bu_ref, state_ref, *, steps):
    @pl.when(pl.program_id(0) == 0)
    def _():
        state_ref[...] = jnp.zeros_like(state_ref)

    u = u_ref[...]
    bu_ref[...] = jnp.dot(u.astype(BF16), bmat_ref[...], preferred_element_type=F32)
    lr = jnp.broadcast_to(lam_ref[0:1, :], (SUBLANE, N_STATE))
    li = jnp.broadcast_to(lam_ref[1:2, :], (SUBLANE, N_STATE))

    def step(t, carry):
        sr, si = carry
        r0 = pl.multiple_of(t * SUBLANE, SUBLANE)
        nr = lr * sr - li * si + bu_ref[pl.ds(r0, SUBLANE), 0:N_STATE]
        ni = lr * si + li * sr + bu_ref[pl.ds(r0, SUBLANE), N_STATE:2 * N_STATE]
        bu_ref[pl.ds(r0, SUBLANE), 0:N_STATE] = nr
        bu_ref[pl.ds(r0, SUBLANE), N_STATE:2 * N_STATE] = ni
        return nr, ni

    sr, si = lax.fori_loop(0, steps, step,
                           (state_ref[:, 0:N_STATE], state_ref[:, N_STATE:2 * N_STATE]))
    state_ref[:, 0:N_STATE] = sr
    state_ref[:, N_STATE:2 * N_STATE] = si

    y = jnp.dot(bu_ref[...].astype(BF16), cmat_ref[...], preferred_element_type=F32)
    y = jax.nn.gelu(y + d_ref[...] * u)
    gate = jnp.dot(y.astype(BF16), wglu_ref[...], preferred_element_type=F32) + bglu_ref[...]
    o = y * jax.nn.sigmoid(gate)
    ms = jnp.mean(o * o, axis=-1, keepdims=True)
    o_ref[...] = (o * lax.rsqrt(ms + EPS) * gn_ref[...]).astype(BF16)


def _s5_call(u_t, bmat, lam, cmat, d_skip, wglu, bglu, gn):
    rows = S5_STEPS * BATCH
    const = lambda i: (0, 0)
    return pl.pallas_call(
        functools.partial(_s5_kernel, steps=S5_STEPS),
        grid=(SEQ // S5_STEPS,),
        in_specs=[pl.BlockSpec((rows, SSM_WIDTH), lambda i: (i, 0)),
                  pl.BlockSpec((SSM_WIDTH, 2 * N_STATE), const),
                  pl.BlockSpec((2, N_STATE), const),
                  pl.BlockSpec((2 * N_STATE, SSM_WIDTH), const),
                  pl.BlockSpec((1, SSM_WIDTH), const),
                  pl.BlockSpec((SSM_WIDTH, SSM_WIDTH), const),
                  pl.BlockSpec((1, SSM_WIDTH), const),
                  pl.BlockSpec((1, SSM_WIDTH), const)],
        out_specs=pl.BlockSpec((rows, SSM_WIDTH), lambda i: (i, 0)),
        out_shape=jax.ShapeDtypeStruct((SEQ * BATCH, SSM_WIDTH), BF16),
        scratch_shapes=[pltpu.VMEM((rows, 2 * N_STATE), F32),
                        pltpu.VMEM((SUBLANE, 2 * N_STATE), F32)],
        compiler_params=_params(("arbitrary",)),
        name="s5",
    )(u_t, bmat, lam, cmat, d_skip, wglu, bglu, gn)


def _s5_operands(lam_re, lam_im, log_dt, b_re, b_im, c_re, c_im):
    dt = jnp.exp(log_dt)[:, None]
    mag = jnp.exp(lam_re * dt)
    lb_re = mag * jnp.cos(lam_im * dt)
    lb_im = mag * jnp.sin(lam_im * dt)
    den = lam_re * lam_re + lam_im * lam_im
    co_re = ((lb_re - 1.0) * lam_re + lb_im * lam_im) / den
    co_im = (lb_im * lam_re - (lb_re - 1.0) * lam_im) / den
    bb_re = co_re[..., None] * b_re - co_im[..., None] * b_im
    bb_im = co_re[..., None] * b_im + co_im[..., None] * b_re
    eye = jnp.eye(N_SSM_GROUPS, dtype=F32)
    blk_b = lambda m: jnp.einsum("gpc,gh->gchp", m, eye).reshape(SSM_WIDTH, N_STATE)
    bmat = jnp.concatenate([blk_b(bb_re), blk_b(bb_im)], axis=1).astype(BF16)
    blk_c = lambda m: jnp.einsum("gcp,gh->gphc", m, eye).reshape(N_STATE, SSM_WIDTH)
    cmat = jnp.concatenate([blk_c(c_re), -blk_c(c_im)], axis=0).astype(BF16)
    lam = jnp.stack([lb_re.reshape(N_STATE), lb_im.reshape(N_STATE)], axis=0)
    return bmat, lam, cmat


def _rope_tables(positions):
    half = MLA_ROPE // 2
    inv = ROPE_BASE ** (-jnp.arange(half, dtype=F32) / half)
    ang = positions.astype(F32)[..., None] * inv
    cos, sin = jnp.cos(ang), jnp.sin(ang)
    shp = positions.shape
    one = lambda n: jnp.ones(shp + (n,), F32)
    zero = lambda n: jnp.zeros(shp + (n,), F32)
    cos_t = jnp.concatenate([one(MLA_NOPE), cos, cos, zero(LANE - MLA_QK)], axis=-1)
    sin_t = jnp.concatenate([zero(MLA_NOPE), -sin, sin, zero(LANE - MLA_QK)], axis=-1)
    return cos_t, sin_t


def _swap_rope_halves(a):
    half = MLA_ROPE // 2
    lo, hi = a[..., MLA_NOPE:MLA_NOPE + half], a[..., MLA_NOPE + half:MLA_QK]
    return jnp.concatenate([jnp.zeros_like(a[..., :MLA_NOPE]), hi, lo, jnp.zeros_like(a[..., MLA_QK:])], axis=-1)


def _store_key_blocks(kt_ref, h, k):
    kt = k.T
    for s in range(k.shape[0] // ATT_TILE):
        kt_ref[0, h, s] = kt[:, s * ATT_TILE:(s + 1) * ATT_TILE].astype(BF16)


_KT_SPEC = lambda heads, tl: pl.BlockSpec((1, heads, tl // ATT_TILE, HEAD_PAD, ATT_TILE),
                                          lambda b, i: (b, 0, i, 0, 0))
_KT_SHAPE = lambda heads: jax.ShapeDtypeStruct((BATCH, heads, SEQ // ATT_TILE, HEAD_PAD, ATT_TILE), BF16)


def _mla_prep_kernel(cq_ref, ckv_ref, krfg_ref, cos_ref, sin_ref, qn_ref, kvn_ref, wq_ref, wk_ref, wv_ref,
                     gq_ref, gqs_ref, gk_ref, gks_ref, q_ref, k_ref, v_ref):
    tl = cq_ref.shape[1]
    lane = lax.broadcasted_iota(jnp.int32, (tl, LANE), 1)
    cos, sin = cos_ref[0], sin_ref[0]
    q_scale = LOG2E / math.sqrt(MLA_QK)
    q_cos, q_sin = gq_ref[...] * cos * q_scale, gqs_ref[...] * sin * q_scale
    k_cos, k_sin = gk_ref[...] * cos, gks_ref[...] * sin
    ones = jnp.ones((LANE, LANE), BF16)

    def inv_rms(x):
        ss = jnp.dot((x * x).astype(BF16), ones, preferred_element_type=F32)
        return lax.rsqrt(ss / MLA_QK + EPS)

    cq = cq_ref[0]
    cqn = (cq * lax.rsqrt(jnp.mean(cq * cq, axis=-1, keepdims=True) + EPS) * qn_ref[...]).astype(BF16)
    ckv = ckv_ref[0]
    ckvn = (ckv * lax.rsqrt(jnp.mean(ckv * ckv, axis=-1, keepdims=True) + EPS) * kvn_ref[...]).astype(BF16)
    kr = jnp.where((lane >= KR_LANE) & (lane < KR_LANE + MLA_ROPE), krfg_ref[0], 0.0)
    kr_swapped = jnp.where(lane < KR_LANE + MLA_ROPE // 2, pltpu.roll(kr, LANE - 16, 1), pltpu.roll(kr, 16, 1))
    k_rotary = kr_swapped * k_sin

    heads = range(MLA_HEADS)
    qqs = [jnp.dot(cqn, wq_ref[h], preferred_element_type=F32) for h in heads]
    ks = [jnp.dot(ckvn, wk_ref[h], preferred_element_type=F32) + kr for h in heads]
    q_inv = [inv_rms(qq[:, :LANE]) for qq in qqs]
    k_inv = [inv_rms(k) for k in ks]
    for h in heads:
        q, q_swapped = qqs[h][:, :LANE], qqs[h][:, LANE:]
        q_ref[0, h] = (q_inv[h] * (q * q_cos + q_swapped * q_sin)).astype(BF16)
        _store_key_blocks(k_ref, h, k_inv[h] * (ks[h] * k_cos + k_rotary))
        v = jnp.dot(ckvn, wv_ref[h], preferred_element_type=F32)
        v_ref[0, h] = jnp.where(lane == ONES_LANE, 1.0, v).astype(BF16)


def _mla_prep_call(cq, ckv, krfg, tabs, qn, kvn, wq, wk, wv, gq, gqs, gk, gks):
    tl = ROW_TILE
    row = lambda b, i: (b, i, 0)
    c2 = lambda b, i: (0, 0)
    c3 = lambda b, i: (0, 0, 0)
    head_out = pl.BlockSpec((1, MLA_HEADS, tl, HEAD_PAD), lambda b, i: (b, 0, i, 0))
    head_shape = jax.ShapeDtypeStruct((BATCH, MLA_HEADS, SEQ, HEAD_PAD), BF16)
    gain = pl.BlockSpec((1, HEAD_PAD), c2)
    return pl.pallas_call(
        _mla_prep_kernel,
        grid=(BATCH, SEQ // tl),
        in_specs=[pl.BlockSpec((1, tl, MLA_Q_RANK), row),
                  pl.BlockSpec((1, tl, MLA_KV_RANK), row),
                  pl.BlockSpec((1, tl, LANE), row),
                  pl.BlockSpec((1, tl, LANE), row),
                  pl.BlockSpec((1, tl, LANE), row),
                  pl.BlockSpec((1, MLA_Q_RANK), c2),
                  pl.BlockSpec((1, MLA_KV_RANK), c2),
                  pl.BlockSpec((MLA_HEADS, MLA_Q_RANK, 2 * HEAD_PAD), c3),
                  pl.BlockSpec((MLA_HEADS, MLA_KV_RANK, HEAD_PAD), c3),
                  pl.BlockSpec((MLA_HEADS, MLA_KV_RANK, HEAD_PAD), c3),
                  gain, gain, gain, gain],
        out_specs=[head_out, _KT_SPEC(MLA_HEADS, tl), head_out],
        out_shape=[head_shape, _KT_SHAPE(MLA_HEADS), head_shape],
        compiler_params=_params(("arbitrary", "arbitrary")),
        name="mla_prep",
    )(cq, ckv, krfg, *tabs, qn, kvn, wq, wk, wv, gq, gqs, gk, gks)


def _pad_lanes(a, n=HEAD_PAD):
    return jnp.pad(a, [(0, 0)] * (a.ndim - 1) + [(0, n - a.shape[-1])])


def _mla_weights(w_uq, w_ukv, gq, gk):
    wq = _pad_lanes(w_uq.reshape(MLA_Q_RANK, MLA_HEADS, MLA_QK).transpose(1, 0, 2))
    wq = jnp.concatenate([wq, _swap_rope_halves(wq)], axis=-1).astype(BF16)
    wkv = w_ukv.reshape(MLA_KV_RANK, MLA_HEADS, MLA_NOPE + MLA_V).transpose(1, 0, 2)
    wk = _pad_lanes(wkv[..., :MLA_NOPE]).astype(BF16)
    wv = _pad_lanes(wkv[..., MLA_NOPE:]).astype(BF16)
    gq, gk = _pad_lanes(gq[None, :]), _pad_lanes(gk[None, :])
    return wq, wk, wv, gq, _swap_rope_halves(gq), gk, _swap_rope_halves(gk)


GATE_MID_LANE = 8
GATE_LO_LANE = 16
GATE_ONE_LANE = LANE - 1
Q_GATE_LANE = FOX_HEAD_DIM
K_GATE_LANE = FOX_HEAD_DIM + 3


def _fox_prep_kernel(fq_ref, fk_ref, fv_ref, krfg_ref, bf_ref, gq_ref, gk_ref, pq_ref, pk_ref, pv_ref,
                     q_ref, k_ref, v_ref, carry_ref):
    tl = fq_ref.shape[1]
    lane = lax.broadcasted_iota(jnp.int32, (tl, LANE), 1)

    @pl.when(pl.program_id(1) == 0)
    def _():
        carry_ref[...] = jnp.zeros_like(carry_ref)

    logf = jax.nn.log_sigmoid(krfg_ref[0] + bf_ref[...])
    logf = jnp.where(lane < FOX_HEADS, logf, 0.0)
    r_i = lax.broadcasted_iota(jnp.int32, (tl, tl), 0)
    c_i = lax.broadcasted_iota(jnp.int32, (tl, tl), 1)
    tri = jnp.where(c_i <= r_i, 1.0, 0.0).astype(BF16)
    cum = carry_ref[0:1, :]
    for piece in _split3(logf):
        cum = cum + jnp.dot(tri, piece.astype(BF16), preferred_element_type=F32)
    carry_ref[0:1, :] = cum[tl - 1:tl, :]

    c_hi, c_mid, c_lo = _split3(cum * LOG2E)
    gate_row = (c_hi + pltpu.roll(c_mid, GATE_MID_LANE, 1) + pltpu.roll(c_lo, GATE_LO_LANE, 1)
                + jnp.where(lane == GATE_ONE_LANE, 1.0, 0.0)).astype(BF16)

    p_r = lax.broadcasted_iota(jnp.int32, (LANE, LANE), 0)
    p_c = lax.broadcasted_iota(jnp.int32, (LANE, LANE), 1)
    head_mean = jnp.where(p_r // FOX_HEAD_DIM == p_c // FOX_HEAD_DIM, 1.0 / FOX_HEAD_DIM, 0.0).astype(BF16)

    def mean_sq(ref, j):
        x = ref[0, :, j * LANE:(j + 1) * LANE]
        return jnp.dot((x * x).astype(BF16), head_mean, preferred_element_type=F32)

    def normed(ref, g_ref, j, ms):
        lanes = slice(j * LANE, (j + 1) * LANE)
        return (ref[0, :, lanes] * lax.rsqrt(ms + EPS) * g_ref[:, lanes]).astype(BF16)

    def placed(x, p_ref, j):
        return jnp.dot(jnp.concatenate([x, gate_row], axis=1), p_ref[j], preferred_element_type=F32)

    pairs = range(FOX_HEADS // 2)
    q_ms = [mean_sq(fq_ref, j) for j in pairs]
    k_ms = [mean_sq(fk_ref, j) for j in pairs]
    q_n = [normed(fq_ref, gq_ref, j, q_ms[j]) for j in pairs]
    k_n = [normed(fk_ref, gk_ref, j, k_ms[j]) for j in pairs]
    for j in pairs:
        q = placed(q_n[j], pq_ref, j)
        k = placed(k_n[j], pk_ref, j)
        v = placed(fv_ref[0, :, j * LANE:(j + 1) * LANE].astype(BF16), pv_ref, j)
        for hh in range(2):
            head = slice(hh * HEAD_PAD, (hh + 1) * HEAD_PAD)
            q_ref[0, 2 * j + hh] = q[:, head].astype(BF16)
            _store_key_blocks(k_ref, 2 * j + hh, k[:, head])
            v_ref[0, 2 * j + hh] = v[:, head].astype(BF16)


def _fox_prep_call(fq, fk, fv, krfg, bf, gq, gk, pq, pk, pv):
    tl = ROW_TILE
    row = lambda b, i: (b, i, 0)
    c2 = lambda b, i: (0, 0)
    c3 = lambda b, i: (0, 0, 0)
    head_out = pl.BlockSpec((1, FOX_HEADS, tl, HEAD_PAD), lambda b, i: (b, 0, i, 0))
    head_shape = jax.ShapeDtypeStruct((BATCH, FOX_HEADS, SEQ, HEAD_PAD), BF16)
    place = pl.BlockSpec((FOX_HEADS // 2, 2 * LANE, 2 * HEAD_PAD), c3)
    return pl.pallas_call(
        _fox_prep_kernel,
        grid=(BATCH, SEQ // tl),
        in_specs=[pl.BlockSpec((1, tl, ATT_WIDTH), row),
                  pl.BlockSpec((1, tl, ATT_WIDTH), row),
                  pl.BlockSpec((1, tl, ATT_WIDTH), row),
                  pl.BlockSpec((1, tl, LANE), row),
                  pl.BlockSpec((1, LANE), c2),
                  pl.BlockSpec((1, ATT_WIDTH), c2),
                  pl.BlockSpec((1, ATT_WIDTH), c2),
                  place, place, place],
        out_specs=[head_out, _KT_SPEC(FOX_HEADS, tl), head_out],
        out_shape=[head_shape, _KT_SHAPE(FOX_HEADS), head_shape],
        scratch_shapes=[pltpu.VMEM((SUBLANE, LANE), F32)],
        compiler_params=_params(("arbitrary", "arbitrary")),
        name="fox_prep",
    )(fq, fk, fv, krfg, bf, gq, gk, pq, pk, pv)


def _fox_placements():
    pq = np.zeros((FOX_HEADS // 2, 2 * LANE, 2 * HEAD_PAD), np.float32)
    pk = np.zeros_like(pq)
    pv = np.zeros_like(pq)
    one_row = LANE + GATE_ONE_LANE
    for j in range(FOX_HEADS // 2):
        for hh in range(2):
            h, col0 = 2 * j + hh, hh * HEAD_PAD
            for d in range(FOX_HEAD_DIM):
                for p in (pq, pk, pv):
                    p[j, hh * FOX_HEAD_DIM + d, col0 + d] = 1.0
            pv[j, one_row, col0 + ONES_LANE] = 1.0
            for n, piece_lane in enumerate((0, GATE_MID_LANE, GATE_LO_LANE)):
                pq[j, LANE + piece_lane + h, col0 + Q_GATE_LANE + n] = 1.0
                pq[j, one_row, col0 + K_GATE_LANE + n] = 1.0
                pk[j, one_row, col0 + Q_GATE_LANE + n] = 1.0
                pk[j, LANE + piece_lane + h, col0 + K_GATE_LANE + n] = -1.0
    return tuple(jnp.asarray(p, BF16) for p in (pq, pk, pv))


def _fox_operands(bf, gq, gk):
    q_scale = LOG2E / math.sqrt(FOX_HEAD_DIM)
    return (_pad_lanes(bf[None, :], LANE), jnp.tile(gq * q_scale, FOX_HEADS)[None, :],
            jnp.tile(gk, FOX_HEADS)[None, :]) + _fox_placements()


def _flash_kernel(q_ref, kt_ref, v_ref, o_ref, s_ref, m_ref, acc_ref, *, tile, chunk):
    qi = pl.program_id(2)
    n_blk = qi + 1
    per_chunk = tile // chunk
    qry_i = lax.broadcasted_iota(jnp.int32, (tile, tile), 0)
    key_i = lax.broadcasted_iota(jnp.int32, (tile, tile), 1)
    chunk_gap = key_i // chunk - qry_i // chunk
    lane = lax.broadcasted_iota(jnp.int32, (tile, HEAD_PAD), 1)

    def over_blocks(fn):
        def quad(jj, carry):
            for u in range(4):
                fn(4 * jj + u)
            return carry
        lax.fori_loop(0, n_blk // 4, quad, 0)
        done = (n_blk // 4) * 4

        @pl.when(n_blk % 4 >= 2)
        def _():
            fn(done)
            fn(done + 1)

        @pl.when(n_blk % 2 == 1)
        def _():
            fn(qi)

    def scores(j):
        allowed = chunk_gap <= (qi - j) * per_chunk
        for hh in range(2):
            s = jnp.dot(q_ref[0, hh], kt_ref[0, hh, j], preferred_element_type=F32)
            s = jnp.where(allowed, s, NEG)
            s_ref[hh, j] = s
            mr = m_ref[hh]
            for c in range(tile // LANE):
                mr = jnp.maximum(mr, s[:, c * LANE:(c + 1) * LANE])
            m_ref[hh] = mr

    m_ref[...] = jnp.full(m_ref.shape, NEG, F32)
    over_blocks(scores)
    ms = [jnp.max(m_ref[hh], axis=1, keepdims=True) for hh in range(2)]

    def weighted(j):
        k0 = pl.multiple_of(j * tile, tile)
        for hh in range(2):
            p = jnp.exp2(s_ref[hh, j] - ms[hh]).astype(BF16)
            acc_ref[hh] += jnp.dot(p, v_ref[0, hh, pl.ds(k0, tile), :], preferred_element_type=F32)

    acc_ref[...] = jnp.zeros(acc_ref.shape, F32)
    over_blocks(weighted)
    outs = [acc_ref[hh] / acc_ref[hh][:, ONES_LANE:ONES_LANE + 1] for hh in range(2)]
    o_ref[0] = jnp.where(lane < 64, outs[0], pltpu.roll(outs[1], 64, 1)).astype(BF16)


def _flash_call(q, kt, v, chunk):
    tile = ATT_TILE
    heads = q.shape[1]
    n_blocks = SEQ // tile
    return pl.pallas_call(
        functools.partial(_flash_kernel, tile=tile, chunk=chunk),
        grid=(BATCH, heads // 2, n_blocks),
        in_specs=[pl.BlockSpec((1, 2, tile, HEAD_PAD), lambda b, hp, i: (b, hp, i, 0)),
                  pl.BlockSpec((1, 2, n_blocks, HEAD_PAD, tile), lambda b, hp, i: (b, hp, 0, 0, 0)),
                  pl.BlockSpec((1, 2, SEQ, HEAD_PAD), lambda b, hp, i: (b, hp, 0, 0))],
        out_specs=pl.BlockSpec((1, tile, LANE), lambda b, hp, i: (b, i, hp)),
        out_shape=jax.ShapeDtypeStruct((BATCH, SEQ, ATT_WIDTH), BF16),
        scratch_shapes=[pltpu.VMEM((2, n_blocks, tile, tile), F32),
                        pltpu.VMEM((2, tile, LANE), F32),
                        pltpu.VMEM((2, tile, HEAD_PAD), F32)],
        compiler_params=_params(("arbitrary", "arbitrary", "arbitrary")),
        name="flash_chunk%d" % chunk,
    )(q, kt, v)


def _merge_kernel(ssm_ref, mla_ref, fox_ref, x_ref, g1_ref, gm_ref, gf_ref,
                  ws_ref, wm_ref, wf_ref, o_ref):
    def normed(ref, g_ref):
        a = ref[0].astype(F32)
        return (a * lax.rsqrt(jnp.mean(a * a, axis=-1, keepdims=True) + EPS) * g_ref[...]).astype(BF16)

    mix = jnp.dot(ssm_ref[0], ws_ref[...], preferred_element_type=F32)
    mix = mix + jnp.dot(normed(mla_ref, gm_ref), wm_ref[...], preferred_element_type=F32)
    mix = mix + jnp.dot(normed(fox_ref, gf_ref), wf_ref[...], preferred_element_type=F32)
    o_ref[0] = x_ref[0] + g1_ref[0] * mix


def _merge_call(o_ssm, o_mla, o_fox, x, g1, gm, gf, ws, wm, wf):
    tm = ROW_TILE
    row = lambda b, i: (b, i, 0)
    c2 = lambda b, i: (0, 0)
    return pl.pallas_call(
        _merge_kernel,
        grid=(BATCH, SEQ // tm),
        in_specs=[pl.BlockSpec((1, tm, SSM_WIDTH), row),
                  pl.BlockSpec((1, tm, ATT_WIDTH), row),
                  pl.BlockSpec((1, tm, ATT_WIDTH), row),
                  pl.BlockSpec((1, tm, D_MODEL), row),
                  pl.BlockSpec((1, 1, D_MODEL), lambda b, i: (b, 0, 0)),
                  pl.BlockSpec((1, ATT_WIDTH), c2),
                  pl.BlockSpec((1, ATT_WIDTH), c2),
                  pl.BlockSpec((SSM_WIDTH, D_MODEL), c2),
                  pl.BlockSpec((ATT_WIDTH, D_MODEL), c2),
                  pl.BlockSpec((ATT_WIDTH, D_MODEL), c2)],
        out_specs=pl.BlockSpec((1, tm, D_MODEL), row),
        out_shape=jax.ShapeDtypeStruct((BATCH, SEQ, D_MODEL), F32),
        compiler_params=_params(("arbitrary", "arbitrary")),
        name="merge",
    )(o_ssm, o_mla, o_fox, x, g1, gm, gf, ws, wm, wf)


def _ffn_kernel(x_ref, g_ref, sh_ref, sc_ref, g2_ref, wg_ref, wu_ref, wd_ref, o_ref, h_ref, acc_ref):
    c = pl.program_id(2)

    @pl.when(c == 0)
    def _():
        h_ref[...] = _rms_mod(x_ref[0], g_ref[...], sc_ref[0], sh_ref[0]).astype(BF16)
        acc_ref[...] = jnp.zeros_like(acc_ref)

    h = h_ref[...]
    gate = jnp.dot(h, wg_ref[0], preferred_element_type=F32)
    up = jnp.dot(h, wu_ref[0], preferred_element_type=F32)
    a = (gate * jax.nn.sigmoid(gate) * up).astype(BF16)
    acc_ref[...] += jnp.dot(a, wd_ref[0], preferred_element_type=F32)

    @pl.when(c == pl.num_programs(2) - 1)
    def _():
        o_ref[0] = x_ref[0] + g2_ref[0] * acc_ref[...]


def _ffn_call(x, g, sh, sc, g2, wg, wu, wd):
    tm = MOE_TILE
    n_chunks = wg.shape[0]
    row = lambda b, i, c: (b, i, 0)
    per_b = lambda b, i, c: (b, 0, 0)
    chunk = lambda b, i, c: (c, 0, 0)
    return pl.pallas_call(
        _ffn_kernel,
        grid=(BATCH, SEQ // tm, n_chunks),
        in_specs=[pl.BlockSpec((1, tm, D_MODEL), row),
                  pl.BlockSpec((1, D_MODEL), lambda b, i, c: (0, 0)),
                  pl.BlockSpec((1, 1, D_MODEL), per_b),
                  pl.BlockSpec((1, 1, D_MODEL), per_b),
                  pl.BlockSpec((1, 1, D_MODEL), per_b),
                  pl.BlockSpec((1, D_MODEL, FF_CHUNK), chunk),
                  pl.BlockSpec((1, D_MODEL, FF_CHUNK), chunk),
                  pl.BlockSpec((1, FF_CHUNK, D_MODEL), chunk)],
        out_specs=pl.BlockSpec((1, tm, D_MODEL), row),
        out_shape=jax.ShapeDtypeStruct((BATCH, SEQ, D_MODEL), F32),
        scratch_shapes=[pltpu.VMEM((tm, D_MODEL), BF16), pltpu.VMEM((tm, D_MODEL), F32)],
        compiler_params=_params(("arbitrary", "arbitrary", "arbitrary")),
        name="ffn_dense",
    )(x, g, sh, sc, g2, wg, wu, wd)


def _router_kernel(x_ref, g_ref, sh_ref, sc_ref, w_ref, b_ref, comb_ref, rank_ref, rankt_ref, count_ref):
    tm = x_ref.shape[1]
    h = _rms_mod(x_ref[0], g_ref[...], sc_ref[0], sh_ref[0])
    h_hi = h.astype(BF16)
    h_lo = (h - h_hi.astype(F32)).astype(BF16)
    w_hi, w_lo = w_ref[0], w_ref[1]
    logits = (jnp.dot(h_hi, w_hi, preferred_element_type=F32)
              + jnp.dot(h_lo, w_hi, preferred_element_type=F32)
              + jnp.dot(h_hi, w_lo, preferred_element_type=F32)) + b_ref[...]
    lane = lax.broadcasted_iota(jnp.int32, logits.shape, 1)
    logits = jnp.where(lane < N_EXPERTS, logits, -jnp.inf)
    m1 = jnp.max(logits, axis=-1, keepdims=True)
    i1 = jnp.min(jnp.where(logits == m1, lane, LANE), axis=-1, keepdims=True)
    rest = jnp.where(lane == i1, -jnp.inf, logits)
    m2 = jnp.max(rest, axis=-1, keepdims=True)
    i2 = jnp.min(jnp.where(rest == m2, lane, LANE), axis=-1, keepdims=True)
    e = jnp.exp(m2 - m1)
    p1 = 1.0 / (1.0 + e)
    comb_ref[0] = jnp.where(lane == i1, p1, 0.0) + jnp.where(lane == i2, e * p1, 0.0)

    chosen = (lane == i1) | (lane == i2)
    chosen_f = jnp.where(chosen, 1.0, 0.0)
    r_i = lax.broadcasted_iota(jnp.int32, (tm, tm), 0)
    c_i = lax.broadcasted_iota(jnp.int32, (tm, tm), 1)
    earlier = jnp.where(c_i < r_i, 1.0, 0.0).astype(BF16)
    rank = jnp.dot(earlier, chosen_f.astype(BF16), preferred_element_type=F32)
    rank = jnp.where(chosen, rank, -1.0)
    rank_ref[0] = rank
    rankt_ref[0] = rank.T[0:SUBLANE, :]
    count_ref[0] = jnp.sum(chosen_f, axis=0, keepdims=True)


def _router_call(x, g, sh, sc, w, b):
    tm = MOE_TILE
    tiles = SEQ // tm
    row = lambda b_, i: (b_, i, 0)
    per_b = lambda b_, i: (b_, 0, 0)
    per_tile = lambda b_, i: (b_ * tiles + i, 0, 0)
    return pl.pallas_call(
        _router_kernel,
        grid=(BATCH, tiles),
        in_specs=[pl.BlockSpec((1, tm, D_MODEL), row),
                  pl.BlockSpec((1, D_MODEL), lambda b_, i: (0, 0)),
                  pl.BlockSpec((1, 1, D_MODEL), per_b),
                  pl.BlockSpec((1, 1, D_MODEL), per_b),
                  pl.BlockSpec((2, D_MODEL, LANE), lambda b_, i: (0, 0, 0)),
                  pl.BlockSpec((1, LANE), lambda b_, i: (0, 0))],
        out_specs=[pl.BlockSpec((1, tm, LANE), row),
                   pl.BlockSpec((1, tm, LANE), row),
                   pl.BlockSpec((1, SUBLANE, tm), per_tile),
                   pl.BlockSpec((1, 1, LANE), per_tile)],
        out_shape=[jax.ShapeDtypeStruct((BATCH, SEQ, LANE), F32),
                   jax.ShapeDtypeStruct((BATCH, SEQ, LANE), F32),
                   jax.ShapeDtypeStruct((BATCH * tiles, SUBLANE, tm), F32),
                   jax.ShapeDtypeStruct((BATCH * tiles, 1, LANE), F32)],
        compiler_params=_params(("arbitrary", "arbitrary")),
        name="router",
    )(x, g, sh, sc, w, b)


def _moe_kernel(count_ref, x_ref, g_ref, sh_ref, sc_ref, g2_ref, comb_ref, rank_ref, rankt_ref,
                wg_ref, wu_ref, wd_ref, o_ref, h_ref):
    tm = x_ref.shape[1]
    e = pl.program_id(1)

    @pl.when(e == 0)
    def _():
        x = x_ref[0]
        h_ref[...] = _rms_mod(x, g_ref[...], sc_ref[0], sh_ref[0]).astype(BF16)
        o_ref[0] = x

    lane = lax.broadcasted_iota(jnp.int32, (tm, LANE), 1)
    mine = lane == e
    rank_col = jnp.sum(jnp.where(mine, rank_ref[0], 0.0), axis=-1, keepdims=True)
    gate_col = jnp.sum(jnp.where(mine, comb_ref[0], 0.0), axis=-1, keepdims=True)
    rank_row = rankt_ref[0, pl.ds(e, 1), :]
    slot_sub = lax.broadcasted_iota(jnp.int32, (MOE_ROWS, tm), 0).astype(F32)
    slot_lane = lax.broadcasted_iota(jnp.int32, (tm, MOE_ROWS), 1).astype(F32)
    count = count_ref[pl.program_id(0) * N_EXPERTS + e]

    def block(sb, carry):
        base = (sb * MOE_ROWS).astype(F32)
        pick = jnp.where(rank_row - base == slot_sub, 1.0, 0.0).astype(BF16)
        rows = jnp.dot(pick, h_ref[...], preferred_element_type=F32).astype(BF16)
        gate = jnp.dot(rows, wg_ref[0], preferred_element_type=F32)
        up = jnp.dot(rows, wu_ref[0], preferred_element_type=F32)
        a = (gate * jax.nn.sigmoid(gate) * up).astype(BF16)
        y = jnp.dot(a, wd_ref[0], preferred_element_type=F32).astype(BF16)
        place = jnp.where(rank_col - base == slot_lane, 1.0, 0.0).astype(BF16)
        back = jnp.dot(place, y, preferred_element_type=F32)
        o_ref[0] += g2_ref[0] * (gate_col * back)
        return carry

    lax.fori_loop(0, (count + MOE_ROWS - 1) // MOE_ROWS, block, 0)


def _moe_call(x, g, sh, sc, g2, comb, rank, rankt, counts, wg, wu, wd):
    tm = MOE_TILE
    tiles = SEQ // tm
    n_tiles = BATCH * tiles
    row = lambda i, e, cnt: (i, 0, 0)
    per_b = lambda i, e, cnt: (i // tiles, 0, 0)
    expert = lambda i, e, cnt: (e, 0, 0)
    as_tiles = lambda a: a.reshape(n_tiles, tm, a.shape[-1])
    grid_spec = pltpu.PrefetchScalarGridSpec(
        num_scalar_prefetch=1,
        grid=(n_tiles, N_EXPERTS),
        in_specs=[pl.BlockSpec((1, tm, D_MODEL), row),
                  pl.BlockSpec((1, D_MODEL), lambda i, e, cnt: (0, 0)),
                  pl.BlockSpec((1, 1, D_MODEL), per_b),
                  pl.BlockSpec((1, 1, D_MODEL), per_b),
                  pl.BlockSpec((1, 1, D_MODEL), per_b),
                  pl.BlockSpec((1, tm, LANE), row),
                  pl.BlockSpec((1, tm, LANE), row),
                  pl.BlockSpec((1, SUBLANE, tm), row),
                  pl.BlockSpec((1, D_MODEL, D_FF_EXPERT), expert),
                  pl.BlockSpec((1, D_MODEL, D_FF_EXPERT), expert),
                  pl.BlockSpec((1, D_FF_EXPERT, D_MODEL), expert)],
        out_specs=pl.BlockSpec((1, tm, D_MODEL), row),
        scratch_shapes=[pltpu.VMEM((tm, D_MODEL), BF16)],
    )
    out = pl.pallas_call(
        _moe_kernel,
        grid_spec=grid_spec,
        out_shape=jax.ShapeDtypeStruct((n_tiles, tm, D_MODEL), F32),
        compiler_params=_params(("arbitrary", "arbitrary")),
        name="moe_experts",
    )(counts, as_tiles(x), g, sh, sc, g2, as_tiles(comb), as_tiles(rank), rankt, wg, wu, wd)
    return out.reshape(BATCH, SEQ, D_MODEL)


def kernel(x, c, positions, norm_mix, norm_ffn, w_ada, b_ada, w_in, ssm_lam_re, ssm_lam_im, ssm_log_dt, ssm_b_re, ssm_b_im, ssm_c_re, ssm_c_im, ssm_d, ssm_w_glu, ssm_b_glu, mla_q_norm, mla_kv_norm, mla_w_uq, mla_w_ukv, mla_qk_gq, mla_qk_gk, fox_b_f, fox_qk_gq, fox_qk_gk, out_norm, w_out, ffn_w_gate, ffn_w_up, ffn_w_down, moe_w_router, moe_b_router, moe_w_gate, moe_w_up, moe_w_down):
    tabs = _rope_tables(positions)
    ada = _ada_call(c, w_ada, b_ada)
    ada = ada.reshape(DEPTH, BATCH, 6, 1, D_MODEL)
    row2 = lambda a: a[None, :]

    for i in range(DEPTH):
        sh1, sc1, g1, sh2, sc2, g2 = (ada[i, :, n] for n in range(6))

        u, cq, ckv, krfg, fq, fk, fv = _inproj_call(x, row2(norm_mix[i]), sh1, sc1, _pack_w_in(w_in[i]))

        bmat, lam, cmat = _s5_operands(ssm_lam_re[i], ssm_lam_im[i], ssm_log_dt[i],
                                       ssm_b_re[i], ssm_b_im[i], ssm_c_re[i], ssm_c_im[i])
        u_t = u.transpose(1, 0, 2).reshape(SEQ * BATCH, SSM_WIDTH)
        o_ssm = _s5_call(u_t, bmat, lam, cmat, row2(ssm_d[i]), ssm_w_glu[i].astype(BF16),
                         row2(ssm_b_glu[i]), row2(out_norm[i, :SSM_WIDTH]))
        o_ssm = o_ssm.reshape(SEQ, BATCH, SSM_WIDTH).transpose(1, 0, 2)

        mq, mk, mv = _mla_prep_call(cq, ckv, krfg, tabs, row2(mla_q_norm[i]), row2(mla_kv_norm[i]),
                                    *_mla_weights(mla_w_uq[i], mla_w_ukv[i], mla_qk_gq[i], mla_qk_gk[i]))
        o_mla = _flash_call(mq, mk, mv, CHUNK)

        xq, xk, xv = _fox_prep_call(fq, fk, fv, krfg, *_fox_operands(fox_b_f[i], fox_qk_gq[i], fox_qk_gk[i]))
        o_fox = _flash_call(xq, xk, xv, 1)

        e1, e2 = SSM_WIDTH, SSM_WIDTH + ATT_WIDTH
        wo = w_out[i].astype(BF16)
        x = _merge_call(o_ssm, o_mla, o_fox, x, g1, row2(out_norm[i, e1:e2]), row2(out_norm[i, e2:]),
                        wo[:e1], wo[e1:e2], wo[e2:])

        j = i // 2
        if i % 2 == 0:
            split = lambda w: w.reshape(D_MODEL, D_FF // FF_CHUNK, FF_CHUNK).transpose(1, 0, 2).astype(BF16)
            wd = ffn_w_down[j].reshape(D_FF // FF_CHUNK, FF_CHUNK, D_MODEL).astype(BF16)
            x = _ffn_call(x, row2(norm_ffn[i]), sh2, sc2, g2, split(ffn_w_gate[j]), split(ffn_w_up[j]), wd)
        else:
            wr = _pad_lanes(moe_w_router[j], LANE)
            wr_hi = wr.astype(BF16)
            wr_lo = (wr - wr_hi.astype(F32)).astype(BF16)
            comb, rank, rankt, counts = _router_call(x, row2(norm_ffn[i]), sh2, sc2, jnp.stack([wr_hi, wr_lo]),
                                                     _pad_lanes(row2(moe_b_router[j]), LANE))
            counts = counts[:, 0, :N_EXPERTS].astype(jnp.int32).reshape(-1)
            x = _moe_call(x, row2(norm_ffn[i]), sh2, sc2, g2, comb, rank, rankt, counts,
                          moe_w_gate[j].astype(BF16), moe_w_up[j].astype(BF16), moe_w_down[j].astype(BF16))
    return x
```

```python
import functools
import math

import jax
import jax.numpy as jnp
import numpy as np
from jax import lax
from jax.experimental import pallas as pl
from jax.experimental.pallas import tpu as pltpu

F32 = jnp.float32
BF16 = jnp.bfloat16

D_MODEL = 1024
BATCH = 8
SEQ = 4096
DEPTH = 4
CHUNK = 64
EPS = 1e-6

SSM_WIDTH = 256
SSM_GROUP = 16
N_SSM_GROUPS = 16
SSM_STATE = 64
N_STATE = N_SSM_GROUPS * SSM_STATE

MLA_HEADS = 6
MLA_Q_RANK = 256
MLA_KV_RANK = 128
MLA_NOPE = 64
MLA_ROPE = 32
MLA_V = 64
MLA_QK = 96
ROPE_BASE = 10000.0

FOX_HEADS = 6
FOX_HEAD_DIM = 64
ATT_WIDTH = 384

D_FF = 2816
N_EXPERTS = 8
D_FF_EXPERT = 1408

LANE = 128
SUBLANE = 8
HEAD_PAD = LANE
ONES_LANE = 64
NEG = -1e30

IN_PAD = 1920
KR_LANE = 64

ROW_TILE = 512
S5_STEPS = 64
ATT_TILE = 512
LOG2E = math.log2(math.e)
FF_CHUNK = 1408
MOE_TILE = 1024
MOE_ROWS = 256
VMEM_LIMIT = 56 * 1024 * 1024


def _params(sem):
    return pltpu.CompilerParams(dimension_semantics=sem, vmem_limit_bytes=VMEM_LIMIT)


def _rms_mod(x, g, sc, sh):
    ms = jnp.mean(x * x, axis=-1, keepdims=True)
    h = x * lax.rsqrt(ms + EPS) * g
    return h * (1.0 + sc) + sh


def _split3(x):
    hi = x.astype(BF16).astype(F32)
    r = x - hi
    mid = r.astype(BF16).astype(F32)
    lo = (r - mid).astype(BF16).astype(F32)
    return hi, mid, lo


def _ada_kernel(c_ref, w_ref, b_ref, o_ref):
    c = c_ref[...]
    ca = (c * jax.nn.sigmoid(c)).astype(BF16)
    o_ref[0] = jnp.dot(ca, w_ref[0].astype(BF16), preferred_element_type=F32) + b_ref[0]


def _ada_call(c, w_ada, b_ada):
    tn = 1536
    return pl.pallas_call(
        _ada_kernel,
        grid=(DEPTH, 6 * D_MODEL // tn),
        in_specs=[pl.BlockSpec((BATCH, D_MODEL), lambda i, j: (0, 0)),
                  pl.BlockSpec((1, D_MODEL, tn), lambda i, j: (i, 0, j)),
                  pl.BlockSpec((1, 1, tn), lambda i, j: (i, 0, j))],
        out_specs=pl.BlockSpec((1, BATCH, tn), lambda i, j: (i, 0, j)),
        out_shape=jax.ShapeDtypeStruct((DEPTH, BATCH, 6 * D_MODEL), F32),
        compiler_params=_params(("arbitrary", "arbitrary")),
        name="ada",
    )(c, w_ada, b_ada.reshape(DEPTH, 1, 6 * D_MODEL))


_IN_GROUPS = ((0, 256), (256, 512), (512, 640), (640, 768), (768, 1152), (1152, 1536), (1536, 1920))


def _inproj_kernel(x_ref, g_ref, sh_ref, sc_ref, w_ref, *out_refs):
    parts = 4
    step = x_ref.shape[1] // parts
    rows = [slice(r * step, (r + 1) * step) for r in range(parts)]
    normed = lambda r: _rms_mod(x_ref[0, rows[r]], g_ref[...], sc_ref[0], sh_ref[0]).astype(BF16)
    h_next = normed(0)
    for r in range(parts):
        h = h_next
        if r + 1 < parts:
            h_next = normed(r + 1)
        proj = jnp.dot(h, w_ref[...], preferred_element_type=F32)
        for ref, (c0, c1) in zip(out_refs, _IN_GROUPS):
            ref[0, rows[r]] = proj[:, c0:c1]


def _inproj_call(x, g, sh, sc, w):
    tm = ROW_TILE
    row = lambda b, i: (b, i, 0)
    per_b = lambda b, i: (b, 0, 0)
    const = lambda b, i: (0, 0)
    widths = [c1 - c0 for c0, c1 in _IN_GROUPS]
    return pl.pallas_call(
        _inproj_kernel,
        grid=(BATCH, SEQ // tm),
        in_specs=[pl.BlockSpec((1, tm, D_MODEL), row),
                  pl.BlockSpec((1, D_MODEL), const),
                  pl.BlockSpec((1, 1, D_MODEL), per_b),
                  pl.BlockSpec((1, 1, D_MODEL), per_b),
                  pl.BlockSpec((D_MODEL, IN_PAD), const)],
        out_specs=[pl.BlockSpec((1, tm, wd), row) for wd in widths],
        out_shape=[jax.ShapeDtypeStruct((BATCH, SEQ, wd), F32) for wd in widths],
        compiler_params=_params(("arbitrary", "arbitrary")),
        name="inproj",
    )(x, g, sh, sc, w)


def _pack_w_in(w):
    u, cq, ckv, kr, fq, fk, fv, fg = jnp.split(
        w, (256, 512, 640, 672, 1056, 1440, 1824), axis=1)
    z = lambda n: jnp.zeros((D_MODEL, n), w.dtype)
    krfg = jnp.concatenate([fg, z(KR_LANE - FOX_HEADS), kr, z(LANE - KR_LANE - MLA_ROPE)], axis=1)
    return jnp.concatenate([u, cq, ckv, krfg, fq, fk, fv], axis=1).astype(BF16)


def _s5_kernel(u_ref, bmat_ref, lam_ref, cmat_ref, d_ref, wglu_ref, bglu_ref, gn_ref,
               o_ref, bu_ref, state_ref, *, steps):
    @pl.when(pl.program_id(0) == 0)
    def _():
        state_ref[...] = jnp.zeros_like(state_ref)

    u = u_ref[...]
    bu_ref[...] = jnp.dot(u.astype(BF16), bmat_ref[...], preferred_element_type=F32)
    lr = jnp.broadcast_to(lam_ref[0:1, :], (SUBLANE, N_STATE))
    li = jnp.broadcast_to(lam_ref[1:2, :], (SUBLANE, N_STATE))

    def step(t, carry):
        sr, si = carry
        r0 = pl.multiple_of(t * SUBLANE, SUBLANE)
        nr = lr * sr - li * si + bu_ref[pl.ds(r0, SUBLANE), 0:N_STATE]
        ni = lr * si + li * sr + bu_ref[pl.ds(r0, SUBLANE), N_STATE:2 * N_STATE]
        bu_ref[pl.ds(r0, SUBLANE), 0:N_STATE] = nr
        bu_ref[pl.ds(r0, SUBLANE), N_STATE:2 * N_STATE] = ni
        return nr, ni

    sr, si = lax.fori_loop(0, steps, step,
                           (state_ref[:, 0:N_STATE], state_ref[:, N_STATE:2 * N_STATE]))
    state_ref[:, 0:N_STATE] = sr
    state_ref[:, N_STATE:2 * N_STATE] = si

    y = jnp.dot(bu_ref[...].astype(BF16), cmat_ref[...], preferred_element_type=F32)
    y = jax.nn.gelu(y + d_ref[...] * u)
    gate = jnp.dot(y.astype(BF16), wglu_ref[...], preferred_element_type=F32) + bglu_ref[...]
    o = y * jax.nn.sigmoid(gate)
    ms = jnp.mean(o * o, axis=-1, keepdims=True)
    o_ref[...] = (o * lax.rsqrt(ms + EPS) * gn_ref[...]).astype(BF16)


def _s5_call(u_t, bmat, lam, cmat, d_skip, wglu, bglu, gn):
    rows = S5_STEPS * BATCH
    const = lambda i: (0, 0)
    return pl.pallas_call(
        functools.partial(_s5_kernel, steps=S5_STEPS),
        grid=(SEQ // S5_STEPS,),
        in_specs=[pl.BlockSpec((rows, SSM_WIDTH), lambda i: (i, 0)),
                  pl.BlockSpec((SSM_WIDTH, 2 * N_STATE), const),
                  pl.BlockSpec((2, N_STATE), const),
                  pl.BlockSpec((2 * N_STATE, SSM_WIDTH), const),
                  pl.BlockSpec((1, SSM_WIDTH), const),
                  pl.BlockSpec((SSM_WIDTH, SSM_WIDTH), const),
                  pl.BlockSpec((1, SSM_WIDTH), const),
                  pl.BlockSpec((1, SSM_WIDTH), const)],
        out_specs=pl.BlockSpec((rows, SSM_WIDTH), lambda i: (i, 0)),
        out_shape=jax.ShapeDtypeStruct((SEQ * BATCH, SSM_WIDTH), BF16),
        scratch_shapes=[pltpu.VMEM((rows, 2 * N_STATE), F32),
                        pltpu.VMEM((SUBLANE, 2 * N_STATE), F32)],
        compiler_params=_params(("arbitrary",)),
        name="s5",
    )(u_t, bmat, lam, cmat, d_skip, wglu, bglu, gn)


def _s5_operands(lam_re, lam_im, log_dt, b_re, b_im, c_re, c_im):
    dt = jnp.exp(log_dt)[:, None]
    mag = jnp.exp(lam_re * dt)
    lb_re = mag * jnp.cos(lam_im * dt)
    lb_im = mag * jnp.sin(lam_im * dt)
    den = lam_re * lam_re + lam_im * lam_im
    co_re = ((lb_re - 1.0) * lam_re + lb_im * lam_im) / den
    co_im = (lb_im * lam_re - (lb_re - 1.0) * lam_im) / den
    bb_re = co_re[..., None] * b_re - co_im[..., None] * b_im
    bb_im = co_re[..., None] * b_im + co_im[..., None] * b_re
    eye = jnp.eye(N_SSM_GROUPS, dtype=F32)
    blk_b = lambda m: jnp.einsum("gpc,gh->gchp", m, eye).reshape(SSM_WIDTH, N_STATE)
    bmat = jnp.concatenate([blk_b(bb_re), blk_b(bb_im)], axis=1).astype(BF16)
    blk_c = lambda m: jnp.einsum("gcp,gh->gphc", m, eye).reshape(N_STATE, SSM_WIDTH)
    cmat = jnp.concatenate([blk_c(c_re), -blk_c(c_im)], axis=0).astype(BF16)
    lam = jnp.stack([lb_re.reshape(N_STATE), lb_im.reshape(N_STATE)], axis=0)
    return bmat, lam, cmat


def _rope_tables(positions):
    half = MLA_ROPE // 2
    inv = ROPE_BASE ** (-jnp.arange(half, dtype=F32) / half)
    ang = positions.astype(F32)[..., None] * inv
    cos, sin = jnp.cos(ang), jnp.sin(ang)
    shp = positions.shape
    one = lambda n: jnp.ones(shp + (n,), F32)
    zero = lambda n: jnp.zeros(shp + (n,), F32)
    cos_t = jnp.concatenate([one(MLA_NOPE), cos, cos, zero(LANE - MLA_QK)], axis=-1)
    sin_t = jnp.concatenate([zero(MLA_NOPE), -sin, sin, zero(LANE - MLA_QK)], axis=-1)
    return cos_t, sin_t


def _swap_rope_halves(a):
    half = MLA_ROPE // 2
    lo, hi = a[..., MLA_NOPE:MLA_NOPE + half], a[..., MLA_NOPE + half:MLA_QK]
    return jnp.concatenate([jnp.zeros_like(a[..., :MLA_NOPE]), hi, lo, jnp.zeros_like(a[..., MLA_QK:])], axis=-1)


def _store_key_blocks(kt_ref, h, k):
    kt = k.T
    for s in range(k.shape[0] // ATT_TILE):
        kt_ref[0, h, s] = kt[:, s * ATT_TILE:(s + 1) * ATT_TILE].astype(BF16)


_KT_SPEC = lambda heads, tl: pl.BlockSpec((1, heads, tl // ATT_TILE, HEAD_PAD, ATT_TILE),
                                          lambda b, i: (b, 0, i, 0, 0))
_KT_SHAPE = lambda heads: jax.ShapeDtypeStruct((BATCH, heads, SEQ // ATT_TILE, HEAD_PAD, ATT_TILE), BF16)


def _mla_prep_kernel(cq_ref, ckv_ref, krfg_ref, cos_ref, sin_ref, qn_ref, kvn_ref, wq_ref, wk_ref, wv_ref,
                     gq_ref, gqs_ref, gk_ref, gks_ref, q_ref, k_ref, v_ref):
    tl = cq_ref.shape[1]
    lane = lax.broadcasted_iota(jnp.int32, (tl, LANE), 1)
    cos, sin = cos_ref[0], sin_ref[0]
    q_scale = LOG2E / math.sqrt(MLA_QK)
    q_cos, q_sin = gq_ref[...] * cos * q_scale, gqs_ref[...] * sin * q_scale
    k_cos, k_sin = gk_ref[...] * cos, gks_ref[...] * sin
    ones = jnp.ones((LANE, LANE), BF16)

    def inv_rms(x):
        ss = jnp.dot((x * x).astype(BF16), ones, preferred_element_type=F32)
        return lax.rsqrt(ss / MLA_QK + EPS)

    cq = cq_ref[0]
    cqn = (cq * lax.rsqrt(jnp.mean(cq * cq, axis=-1, keepdims=True) + EPS) * qn_ref[...]).astype(BF16)
    ckv = ckv_ref[0]
    ckvn = (ckv * lax.rsqrt(jnp.mean(ckv * ckv, axis=-1, keepdims=True) + EPS) * kvn_ref[...]).astype(BF16)
    kr = jnp.where((lane >= KR_LANE) & (lane < KR_LANE + MLA_ROPE), krfg_ref[0], 0.0)
    kr_swapped = jnp.where(lane < KR_LANE + MLA_ROPE // 2, pltpu.roll(kr, LANE - 16, 1), pltpu.roll(kr, 16, 1))
    k_rotary = kr_swapped * k_sin

    heads = range(MLA_HEADS)
    qqs = [jnp.dot(cqn, wq_ref[h], preferred_element_type=F32) for h in heads]
    ks = [jnp.dot(ckvn, wk_ref[h], preferred_element_type=F32) + kr for h in heads]
    q_inv = [inv_rms(qq[:, :LANE]) for qq in qqs]
    k_inv = [inv_rms(k) for k in ks]
    for h in heads:
        q, q_swapped = qqs[h][:, :LANE], qqs[h][:, LANE:]
        q_ref[0, h] = (q_inv[h] * (q * q_cos + q_swapped * q_sin)).astype(BF16)
        _store_key_blocks(k_ref, h, k_inv[h] * (ks[h] * k_cos + k_rotary))
        v = jnp.dot(ckvn, wv_ref[h], preferred_element_type=F32)
        v_ref[0, h] = jnp.where(lane == ONES_LANE, 1.0, v).astype(BF16)


def _mla_prep_call(cq, ckv, krfg, tabs, qn, kvn, wq, wk, wv, gq, gqs, gk, gks):
    tl = ROW_TILE
    row = lambda b, i: (b, i, 0)
    c2 = lambda b, i: (0, 0)
    c3 = lambda b, i: (0, 0, 0)
    head_out = pl.BlockSpec((1, MLA_HEADS, tl, HEAD_PAD), lambda b, i: (b, 0, i, 0))
    head_shape = jax.ShapeDtypeStruct((BATCH, MLA_HEADS, SEQ, HEAD_PAD), BF16)
    gain = pl.BlockSpec((1, HEAD_PAD), c2)
    return pl.pallas_call(
        _mla_prep_kernel,
        grid=(BATCH, SEQ // tl),
        in_specs=[pl.BlockSpec((1, tl, MLA_Q_RANK), row),
                  pl.BlockSpec((1, tl, MLA_KV_RANK), row),
                  pl.BlockSpec((1, tl, LANE), row),
                  pl.BlockSpec((1, tl, LANE), row),
                  pl.BlockSpec((1, tl, LANE), row),
                  pl.BlockSpec((1, MLA_Q_RANK), c2),
                  pl.BlockSpec((1, MLA_KV_RANK), c2),
                  pl.BlockSpec((MLA_HEADS, MLA_Q_RANK, 2 * HEAD_PAD), c3),
                  pl.BlockSpec((MLA_HEADS, MLA_KV_RANK, HEAD_PAD), c3),
                  pl.BlockSpec((MLA_HEADS, MLA_KV_RANK, HEAD_PAD), c3),
                  gain, gain, gain, gain],
        out_specs=[head_out, _KT_SPEC(MLA_HEADS, tl), head_out],
        out_shape=[head_shape, _KT_SHAPE(MLA_HEADS), head_shape],
        compiler_params=_params(("arbitrary", "arbitrary")),
        name="mla_prep",
    )(cq, ckv, krfg, *tabs, qn, kvn, wq, wk, wv, gq, gqs, gk, gks)


def _pad_lanes(a, n=HEAD_PAD):
    return jnp.pad(a, [(0, 0)] * (a.ndim - 1) + [(0, n - a.shape[-1])])


def _mla_weights(w_uq, w_ukv, gq, gk):
    wq = _pad_lanes(w_uq.reshape(MLA_Q_RANK, MLA_HEADS, MLA_QK).transpose(1, 0, 2))
    wq = jnp.concatenate([wq, _swap_rope_halves(wq)], axis=-1).astype(BF16)
    wkv = w_ukv.reshape(MLA_KV_RANK, MLA_HEADS, MLA_NOPE + MLA_V).transpose(1, 0, 2)
    wk = _pad_lanes(wkv[..., :MLA_NOPE]).astype(BF16)
    wv = _pad_lanes(wkv[..., MLA_NOPE:]).astype(BF16)
    gq, gk = _pad_lanes(gq[None, :]), _pad_lanes(gk[None, :])
    return wq, wk, wv, gq, _swap_rope_halves(gq), gk, _swap_rope_halves(gk)


GATE_MID_LANE = 8
GATE_LO_LANE = 16
GATE_ONE_LANE = LANE - 1
Q_GATE_LANE = FOX_HEAD_DIM
K_GATE_LANE = FOX_HEAD_DIM + 3


def _fox_prep_kernel(fq_ref, fk_ref, fv_ref, krfg_ref, bf_ref, gq_ref, gk_ref, pq_ref, pk_ref, pv_ref,
                     q_ref, k_ref, v_ref, carry_ref):
    tl = fq_ref.shape[1]
    lane = lax.broadcasted_iota(jnp.int32, (tl, LANE), 1)

    @pl.when(pl.program_id(1) == 0)
    def _():
        carry_ref[...] = jnp.zeros_like(carry_ref)

    logf = jax.nn.log_sigmoid(krfg_ref[0] + bf_ref[...])
    logf = jnp.where(lane < FOX_HEADS, logf, 0.0)
    r_i = lax.broadcasted_iota(jnp.int32, (tl, tl), 0)
    c_i = lax.broadcasted_iota(jnp.int32, (tl, tl), 1)
    tri = jnp.where(c_i <= r_i, 1.0, 0.0).astype(BF16)
    cum = carry_ref[0:1, :]
    for piece in _split3(logf):
        cum = cum + jnp.dot(tri, piece.astype(BF16), preferred_element_type=F32)
    carry_ref[0:1, :] = cum[tl - 1:tl, :]

    c_hi, c_mid, c_lo = _split3(cum * LOG2E)
    gate_row = (c_hi + pltpu.roll(c_mid, GATE_MID_LANE, 1) + pltpu.roll(c_lo, GATE_LO_LANE, 1)
                + jnp.where(lane == GATE_ONE_LANE, 1.0, 0.0)).astype(BF16)

    p_r = lax.broadcasted_iota(jnp.int32, (LANE, LANE), 0)
    p_c = lax.broadcasted_iota(jnp.int32, (LANE, LANE), 1)
    head_mean = jnp.where(p_r // FOX_HEAD_DIM == p_c // FOX_HEAD_DIM, 1.0 / FOX_HEAD_DIM, 0.0).astype(BF16)

    def mean_sq(ref, j):
        x = ref[0, :, j * LANE:(j + 1) * LANE]
        return jnp.dot((x * x).astype(BF16), head_mean, preferred_element_type=F32)

    def normed(ref, g_ref, j, ms):
        lanes = slice(j * LANE, (j + 1) * LANE)
        return (ref[0, :, lanes] * lax.rsqrt(ms + EPS) * g_ref[:, lanes]).astype(BF16)

    def placed(x, p_ref, j):
        return jnp.dot(jnp.concatenate([x, gate_row], axis=1), p_ref[j], preferred_element_type=F32)

    pairs = range(FOX_HEADS // 2)
    q_ms = [mean_sq(fq_ref, j) for j in pairs]
    k_ms = [mean_sq(fk_ref, j) for j in pairs]
    q_n = [normed(fq_ref, gq_ref, j, q_ms[j]) for j in pairs]
    k_n = [normed(fk_ref, gk_ref, j, k_ms[j]) for j in pairs]
    for j in pairs:
        q = placed(q_n[j], pq_ref, j)
        k = placed(k_n[j], pk_ref, j)
        v = placed(fv_ref[0, :, j * LANE:(j + 1) * LANE].astype(BF16), pv_ref, j)
        for hh in range(2):
            head = slice(hh * HEAD_PAD, (hh + 1) * HEAD_PAD)
            q_ref[0, 2 * j + hh] = q[:, head].astype(BF16)
            _store_key_blocks(k_ref, 2 * j + hh, k[:, head])
            v_ref[0, 2 * j + hh] = v[:, head].astype(BF16)


def _fox_prep_call(fq, fk, fv, krfg, bf, gq, gk, pq, pk, pv):
    tl = ROW_TILE
    row = lambda b, i: (b, i, 0)
    c2 = lambda b, i: (0, 0)
    c3 = lambda b, i: (0, 0, 0)
    head_out = pl.BlockSpec((1, FOX_HEADS, tl, HEAD_PAD), lambda b, i: (b, 0, i, 0))
    head_shape = jax.ShapeDtypeStruct((BATCH, FOX_HEADS, SEQ, HEAD_PAD), BF16)
    place = pl.BlockSpec((FOX_HEADS // 2, 2 * LANE, 2 * HEAD_PAD), c3)
    return pl.pallas_call(
        _fox_prep_kernel,
        grid=(BATCH, SEQ // tl),
        in_specs=[pl.BlockSpec((1, tl, ATT_WIDTH), row),
                  pl.BlockSpec((1, tl, ATT_WIDTH), row),
                  pl.BlockSpec((1, tl, ATT_WIDTH), row),
                  pl.BlockSpec((1, tl, LANE), row),
                  pl.BlockSpec((1, LANE), c2),
                  pl.BlockSpec((1, ATT_WIDTH), c2),
                  pl.BlockSpec((1, ATT_WIDTH), c2),
                  place, place, place],
        out_specs=[head_out, _KT_SPEC(FOX_HEADS, tl), head_out],
        out_shape=[head_shape, _KT_SHAPE(FOX_HEADS), head_shape],
        scratch_shapes=[pltpu.VMEM((SUBLANE, LANE), F32)],
        compiler_params=_params(("arbitrary", "arbitrary")),
        name="fox_prep",
    )(fq, fk, fv, krfg, bf, gq, gk, pq, pk, pv)


def _fox_placements():
    pq = np.zeros((FOX_HEADS // 2, 2 * LANE, 2 * HEAD_PAD), np.float32)
    pk = np.zeros_like(pq)
    pv = np.zeros_like(pq)
    one_row = LANE + GATE_ONE_LANE
    for j in range(FOX_HEADS // 2):
        for hh in range(2):
            h, col0 = 2 * j + hh, hh * HEAD_PAD
            for d in range(FOX_HEAD_DIM):
                for p in (pq, pk, pv):
                    p[j, hh * FOX_HEAD_DIM + d, col0 + d] = 1.0
            pv[j, one_row, col0 + ONES_LANE] = 1.0
            for n, piece_lane in enumerate((0, GATE_MID_LANE, GATE_LO_LANE)):
                pq[j, LANE + piece_lane + h, col0 + Q_GATE_LANE + n] = 1.0
                pq[j, one_row, col0 + K_GATE_LANE + n] = 1.0
                pk[j, one_row, col0 + Q_GATE_LANE + n] = 1.0
                pk[j, LANE + piece_lane + h, col0 + K_GATE_LANE + n] = -1.0
    return tuple(jnp.asarray(p, BF16) for p in (pq, pk, pv))


def _fox_operands(bf, gq, gk):
    q_scale = LOG2E / math.sqrt(FOX_HEAD_DIM)
    return (_pad_lanes(bf[None, :], LANE), jnp.tile(gq * q_scale, FOX_HEADS)[None, :],
            jnp.tile(gk, FOX_HEADS)[None, :]) + _fox_placements()


def _flash_kernel(q_ref, kt_ref, v_ref, gap_ref, o_ref, s_ref, m_ref, acc_ref, *, tile, chunk):
    qi = pl.program_id(2)
    n_blk = qi + 1
    per_chunk = tile // chunk
    chunk_gap = gap_ref[...]
    lane = lax.broadcasted_iota(jnp.int32, (tile, HEAD_PAD), 1)

    def over_blocks(fn):
        def quad(jj, carry):
            for u in range(4):
                fn(4 * jj + u)
            return carry
        lax.fori_loop(0, n_blk // 4, quad, 0)
        done = (n_blk // 4) * 4

        @pl.when(n_blk % 4 >= 2)
        def _():
            fn(done)
            fn(done + 1)

        @pl.when(n_blk % 2 == 1)
        def _():
            fn(qi)

    def scores(j):
        allowed = chunk_gap <= (qi - j) * per_chunk
        for hh in range(2):
            s = jnp.dot(q_ref[0, hh], kt_ref[0, hh, j], preferred_element_type=F32)
            s = jnp.where(allowed, s, NEG)
            s_ref[hh, j] = s
            mr = m_ref[hh]
            for c in range(tile // LANE):
                mr = jnp.maximum(mr, s[:, c * LANE:(c + 1) * LANE])
            m_ref[hh] = mr

    m_ref[...] = jnp.full(m_ref.shape, NEG, F32)
    over_blocks(scores)
    ms = [jnp.max(m_ref[hh], axis=1, keepdims=True) for hh in range(2)]

    def weighted(j):
        k0 = pl.multiple_of(j * tile, tile)
        for hh in range(2):
            p = jnp.exp2(s_ref[hh, j] - ms[hh]).astype(BF16)
            acc_ref[hh] += jnp.dot(p, v_ref[0, hh, pl.ds(k0, tile), :], preferred_element_type=F32)

    acc_ref[...] = jnp.zeros(acc_ref.shape, F32)
    over_blocks(weighted)
    outs = [acc_ref[hh] / acc_ref[hh][:, ONES_LANE:ONES_LANE + 1] for hh in range(2)]
    o_ref[0] = jnp.where(lane < 64, outs[0], pltpu.roll(outs[1], 64, 1)).astype(BF16)


def _flash_call(q, kt, v, chunk):
    tile = ATT_TILE
    heads = q.shape[1]
    n_blocks = SEQ // tile
    pos = np.arange(tile, dtype=np.int32) // chunk
    gap = jnp.asarray(pos[None, :] - pos[:, None])
    return pl.pallas_call(
        functools.partial(_flash_kernel, tile=tile, chunk=chunk),
        grid=(BATCH, heads // 2, n_blocks),
        in_specs=[pl.BlockSpec((1, 2, tile, HEAD_PAD), lambda b, hp, i: (b, hp, i, 0)),
                  pl.BlockSpec((1, 2, n_blocks, HEAD_PAD, tile), lambda b, hp, i: (b, hp, 0, 0, 0)),
                  pl.BlockSpec((1, 2, SEQ, HEAD_PAD), lambda b, hp, i: (b, hp, 0, 0)),
                  pl.BlockSpec((tile, tile), lambda b, hp, i: (0, 0))],
        out_specs=pl.BlockSpec((1, tile, LANE), lambda b, hp, i: (b, i, hp)),
        out_shape=jax.ShapeDtypeStruct((BATCH, SEQ, ATT_WIDTH), BF16),
        scratch_shapes=[pltpu.VMEM((2, n_blocks, tile, tile), F32),
                        pltpu.VMEM((2, tile, LANE), F32),
                        pltpu.VMEM((2, tile, HEAD_PAD), F32)],
        compiler_params=_params(("arbitrary", "arbitrary", "arbitrary")),
        name="flash_chunk%d" % chunk,
    )(q, kt, v, gap)


def _merge_kernel(ssm_ref, mla_ref, fox_ref, x_ref, g1_ref, gm_ref, gf_ref,
                  ws_ref, wm_ref, wf_ref, o_ref):
    def normed(ref, g_ref):
        a = ref[0].astype(F32)
        return (a * lax.rsqrt(jnp.mean(a * a, axis=-1, keepdims=True) + EPS) * g_ref[...]).astype(BF16)

    mix = jnp.dot(ssm_ref[0], ws_ref[...], preferred_element_type=F32)
    mix = mix + jnp.dot(normed(mla_ref, gm_ref), wm_ref[...], preferred_element_type=F32)
    mix = mix + jnp.dot(normed(fox_ref, gf_ref), wf_ref[...], preferred_element_type=F32)
    o_ref[0] = x_ref[0] + g1_ref[0] * mix


def _merge_call(o_ssm, o_mla, o_fox, x, g1, gm, gf, ws, wm, wf):
    tm = ROW_TILE
    row = lambda b, i: (b, i, 0)
    c2 = lambda b, i: (0, 0)
    return pl.pallas_call(
        _merge_kernel,
        grid=(BATCH, SEQ // tm),
        in_specs=[pl.BlockSpec((1, tm, SSM_WIDTH), row),
                  pl.BlockSpec((1, tm, ATT_WIDTH), row),
                  pl.BlockSpec((1, tm, ATT_WIDTH), row),
                  pl.BlockSpec((1, tm, D_MODEL), row),
                  pl.BlockSpec((1, 1, D_MODEL), lambda b, i: (b, 0, 0)),
                  pl.BlockSpec((1, ATT_WIDTH), c2),
                  pl.BlockSpec((1, ATT_WIDTH), c2),
                  pl.BlockSpec((SSM_WIDTH, D_MODEL), c2),
                  pl.BlockSpec((ATT_WIDTH, D_MODEL), c2),
                  pl.BlockSpec((ATT_WIDTH, D_MODEL), c2)],
        out_specs=pl.BlockSpec((1, tm, D_MODEL), row),
        out_shape=jax.ShapeDtypeStruct((BATCH, SEQ, D_MODEL), F32),
        compiler_params=_params(("arbitrary", "arbitrary")),
        name="merge",
    )(o_ssm, o_mla, o_fox, x, g1, gm, gf, ws, wm, wf)


def _ffn_kernel(x_ref, g_ref, sh_ref, sc_ref, g2_ref, wg_ref, wu_ref, wd_ref, o_ref, h_ref, acc_ref):
    c = pl.program_id(2)

    @pl.when(c == 0)
    def _():
        h_ref[...] = _rms_mod(x_ref[0], g_ref[...], sc_ref[0], sh_ref[0]).astype(BF16)
        acc_ref[...] = jnp.zeros_like(acc_ref)

    h = h_ref[...]
    gate = jnp.dot(h, wg_ref[0], preferred_element_type=F32)
    up = jnp.dot(h, wu_ref[0], preferred_element_type=F32)
    a = (gate * jax.nn.sigmoid(gate) * up).astype(BF16)
    acc_ref[...] += jnp.dot(a, wd_ref[0], preferred_element_type=F32)

    @pl.when(c == pl.num_programs(2) - 1)
    def _():
        o_ref[0] = x_ref[0] + g2_ref[0] * acc_ref[...]


def _ffn_call(x, g, sh, sc, g2, wg, wu, wd):
    tm = MOE_TILE
    n_chunks = wg.shape[0]
    row = lambda b, i, c: (b, i, 0)
    per_b = lambda b, i, c: (b, 0, 0)
    chunk = lambda b, i, c: (c, 0, 0)
    return pl.pallas_call(
        _ffn_kernel,
        grid=(BATCH, SEQ // tm, n_chunks),
        in_specs=[pl.BlockSpec((1, tm, D_MODEL), row),
                  pl.BlockSpec((1, D_MODEL), lambda b, i, c: (0, 0)),
                  pl.BlockSpec((1, 1, D_MODEL), per_b),
                  pl.BlockSpec((1, 1, D_MODEL), per_b),
                  pl.BlockSpec((1, 1, D_MODEL), per_b),
                  pl.BlockSpec((1, D_MODEL, FF_CHUNK), chunk),
                  pl.BlockSpec((1, D_MODEL, FF_CHUNK), chunk),
                  pl.BlockSpec((1, FF_CHUNK, D_MODEL), chunk)],
        out_specs=pl.BlockSpec((1, tm, D_MODEL), row),
        out_shape=jax.ShapeDtypeStruct((BATCH, SEQ, D_MODEL), F32),
        scratch_shapes=[pltpu.VMEM((tm, D_MODEL), BF16), pltpu.VMEM((tm, D_MODEL), F32)],
        compiler_params=_params(("arbitrary", "arbitrary", "arbitrary")),
        name="ffn_dense",
    )(x, g, sh, sc, g2, wg, wu, wd)


def _router_kernel(x_ref, g_ref, sh_ref, sc_ref, w_ref, b_ref, comb_ref, rank_ref, rankt_ref, count_ref):
    tm = x_ref.shape[1]
    h = _rms_mod(x_ref[0], g_ref[...], sc_ref[0], sh_ref[0])
    h_hi = h.astype(BF16)
    h_lo = (h - h_hi.astype(F32)).astype(BF16)
    w_hi, w_lo = w_ref[0], w_ref[1]
    logits = (jnp.dot(h_hi, w_hi, preferred_element_type=F32)
              + jnp.dot(h_lo, w_hi, preferred_element_type=F32)
              + jnp.dot(h_hi, w_lo, preferred_element_type=F32)) + b_ref[...]
    lane = lax.broadcasted_iota(jnp.int32, logits.shape, 1)
    logits = jnp.where(lane < N_EXPERTS, logits, -jnp.inf)
    m1 = jnp.max(logits, axis=-1, keepdims=True)
    i1 = jnp.min(jnp.where(logits == m1, lane, LANE), axis=-1, keepdims=True)
    rest = jnp.where(lane == i1, -jnp.inf, logits)
    m2 = jnp.max(rest, axis=-1, keepdims=True)
    i2 = jnp.min(jnp.where(rest == m2, lane, LANE), axis=-1, keepdims=True)
    e = jnp.exp(m2 - m1)
    p1 = 1.0 / (1.0 + e)
    comb_ref[0] = jnp.where(lane == i1, p1, 0.0) + jnp.where(lane == i2, e * p1, 0.0)

    chosen = (lane == i1) | (lane == i2)
    chosen_f = jnp.where(chosen, 1.0, 0.0)
    r_i = lax.broadcasted_iota(jnp.int32, (tm, tm), 0)
    c_i = lax.broadcasted_iota(jnp.int32, (tm, tm), 1)
    earlier = jnp.where(c_i < r_i, 1.0, 0.0).astype(BF16)
    rank = jnp.dot(earlier, chosen_f.astype(BF16), preferred_element_type=F32)
    rank = jnp.where(chosen, rank, -1.0)
    rank_ref[0] = rank
    rankt_ref[0] = rank.T[0:SUBLANE, :]
    count_ref[0] = jnp.sum(chosen_f, axis=0, keepdims=True)


def _router_call(x, g, sh, sc, w, b):
    tm = MOE_TILE
    tiles = SEQ // tm
    row = lambda b_, i: (b_, i, 0)
    per_b = lambda b_, i: (b_, 0, 0)
    per_tile = lambda b_, i: (b_ * tiles + i, 0, 0)
    return pl.pallas_call(
        _router_kernel,
        grid=(BATCH, tiles),
        in_specs=[pl.BlockSpec((1, tm, D_MODEL), row),
                  pl.BlockSpec((1, D_MODEL), lambda b_, i: (0, 0)),
                  pl.BlockSpec((1, 1, D_MODEL), per_b),
                  pl.BlockSpec((1, 1, D_MODEL), per_b),
                  pl.BlockSpec((2, D_MODEL, LANE), lambda b_, i: (0, 0, 0)),
                  pl.BlockSpec((1, LANE), lambda b_, i: (0, 0))],
        out_specs=[pl.BlockSpec((1, tm, LANE), row),
                   pl.BlockSpec((1, tm, LANE), row),
                   pl.BlockSpec((1, SUBLANE, tm), per_tile),
                   pl.BlockSpec((1, 1, LANE), per_tile)],
        out_shape=[jax.ShapeDtypeStruct((BATCH, SEQ, LANE), F32),
                   jax.ShapeDtypeStruct((BATCH, SEQ, LANE), F32),
                   jax.ShapeDtypeStruct((BATCH * tiles, SUBLANE, tm), F32),
                   jax.ShapeDtypeStruct((BATCH * tiles, 1, LANE), F32)],
        compiler_params=_params(("arbitrary", "arbitrary")),
        name="router",
    )(x, g, sh, sc, w, b)


def _moe_kernel(count_ref, x_ref, g_ref, sh_ref, sc_ref, g2_ref, comb_ref, rank_ref, rankt_ref,
                wg_ref, wu_ref, wd_ref, o_ref, h_ref):
    tm = x_ref.shape[1]
    e = pl.program_id(1)

    @pl.when(e == 0)
    def _():
        x = x_ref[0]
        h_ref[...] = _rms_mod(x, g_ref[...], sc_ref[0], sh_ref[0]).astype(BF16)
        o_ref[0] = x

    lane = lax.broadcasted_iota(jnp.int32, (tm, LANE), 1)
    mine = lane == e
    rank_col = jnp.sum(jnp.where(mine, rank_ref[0], 0.0), axis=-1, keepdims=True)
    gate_col = jnp.sum(jnp.where(mine, comb_ref[0], 0.0), axis=-1, keepdims=True)
    rank_row = rankt_ref[0, pl.ds(e, 1), :]
    count = count_ref[pl.program_id(0) * N_EXPERTS + e]

    def expert_pass(first, n_rows):
        base = first.astype(F32)
        slot_sub = lax.broadcasted_iota(jnp.int32, (n_rows, tm), 0).astype(F32)
        slot_lane = lax.broadcasted_iota(jnp.int32, (tm, n_rows), 1).astype(F32)
        pick = jnp.where(rank_row - base == slot_sub, 1.0, 0.0).astype(BF16)
        rows = jnp.dot(pick, h_ref[...], preferred_element_type=F32).astype(BF16)
        gate = jnp.dot(rows, wg_ref[0], preferred_element_type=F32)
        up = jnp.dot(rows, wu_ref[0], preferred_element_type=F32)
        a = (gate * jax.nn.sigmoid(gate) * up).astype(BF16)
        y = jnp.dot(a, wd_ref[0], preferred_element_type=F32).astype(BF16)
        place = jnp.where(rank_col - base == slot_lane, 1.0, 0.0).astype(BF16)
        back = jnp.dot(place, y, preferred_element_type=F32)
        o_ref[0] += g2_ref[0] * (gate_col * back)

    def full_pass(sb, carry):
        expert_pass(sb * MOE_ROWS, MOE_ROWS)
        return carry

    n_full = count // MOE_ROWS
    lax.fori_loop(0, n_full, full_pass, 0)
    left = count - n_full * MOE_ROWS

    @pl.when(left > MOE_ROWS // 2)
    def _():
        expert_pass(n_full * MOE_ROWS, MOE_ROWS)

    @pl.when((left > 0) & (left <= MOE_ROWS // 2))
    def _():
        expert_pass(n_full * MOE_ROWS, MOE_ROWS // 2)


def _moe_call(x, g, sh, sc, g2, comb, rank, rankt, counts, wg, wu, wd):
    tm = MOE_TILE
    tiles = SEQ // tm
    n_tiles = BATCH * tiles
    row = lambda i, e, cnt: (i, 0, 0)
    per_b = lambda i, e, cnt: (i // tiles, 0, 0)
    expert = lambda i, e, cnt: (e, 0, 0)
    as_tiles = lambda a: a.reshape(n_tiles, tm, a.shape[-1])
    grid_spec = pltpu.PrefetchScalarGridSpec(
        num_scalar_prefetch=1,
        grid=(n_tiles, N_EXPERTS),
        in_specs=[pl.BlockSpec((1, tm, D_MODEL), row),
                  pl.BlockSpec((1, D_MODEL), lambda i, e, cnt: (0, 0)),
                  pl.BlockSpec((1, 1, D_MODEL), per_b),
                  pl.BlockSpec((1, 1, D_MODEL), per_b),
                  pl.BlockSpec((1, 1, D_MODEL), per_b),
                  pl.BlockSpec((1, tm, LANE), row),
                  pl.BlockSpec((1, tm, LANE), row),
                  pl.BlockSpec((1, SUBLANE, tm), row),
                  pl.BlockSpec((1, D_MODEL, D_FF_EXPERT), expert),
                  pl.BlockSpec((1, D_MODEL, D_FF_EXPERT), expert),
                  pl.BlockSpec((1, D_FF_EXPERT, D_MODEL), expert)],
        out_specs=pl.BlockSpec((1, tm, D_MODEL), row),
        scratch_shapes=[pltpu.VMEM((tm, D_MODEL), BF16)],
    )
    out = pl.pallas_call(
        _moe_kernel,
        grid_spec=grid_spec,
        out_shape=jax.ShapeDtypeStruct((n_tiles, tm, D_MODEL), F32),
        compiler_params=_params(("arbitrary", "arbitrary")),
        name="moe_experts",
    )(counts, as_tiles(x), g, sh, sc, g2, as_tiles(comb), as_tiles(rank), rankt, wg, wu, wd)
    return out.reshape(BATCH, SEQ, D_MODEL)


def kernel(x, c, positions, norm_mix, norm_ffn, w_ada, b_ada, w_in, ssm_lam_re, ssm_lam_im, ssm_log_dt, ssm_b_re, ssm_b_im, ssm_c_re, ssm_c_im, ssm_d, ssm_w_glu, ssm_b_glu, mla_q_norm, mla_kv_norm, mla_w_uq, mla_w_ukv, mla_qk_gq, mla_qk_gk, fox_b_f, fox_qk_gq, fox_qk_gk, out_norm, w_out, ffn_w_gate, ffn_w_up, ffn_w_down, moe_w_router, moe_b_router, moe_w_gate, moe_w_up, moe_w_down):
    tabs = _rope_tables(positions)
    ada = _ada_call(c, w_ada, b_ada)
    ada = ada.reshape(DEPTH, BATCH, 6, 1, D_MODEL)
    row2 = lambda a: a[None, :]

    for i in range(DEPTH):
        sh1, sc1, g1, sh2, sc2, g2 = (ada[i, :, n] for n in range(6))

        u, cq, ckv, krfg, fq, fk, fv = _inproj_call(x, row2(norm_mix[i]), sh1, sc1, _pack_w_in(w_in[i]))

        bmat, lam, cmat = _s5_operands(ssm_lam_re[i], ssm_lam_im[i], ssm_log_dt[i],
                                       ssm_b_re[i], ssm_b_im[i], ssm_c_re[i], ssm_c_im[i])
        u_t = u.transpose(1, 0, 2).reshape(SEQ * BATCH, SSM_WIDTH)
        o_ssm = _s5_call(u_t, bmat, lam, cmat, row2(ssm_d[i]), ssm_w_glu[i].astype(BF16),
                         row2(ssm_b_glu[i]), row2(out_norm[i, :SSM_WIDTH]))
        o_ssm = o_ssm.reshape(SEQ, BATCH, SSM_WIDTH).transpose(1, 0, 2)

        mq, mk, mv = _mla_prep_call(cq, ckv, krfg, tabs, row2(mla_q_norm[i]), row2(mla_kv_norm[i]),
                                    *_mla_weights(mla_w_uq[i], mla_w_ukv[i], mla_qk_gq[i], mla_qk_gk[i]))
        o_mla = _flash_call(mq, mk, mv, CHUNK)

        xq, xk, xv = _fox_prep_call(fq, fk, fv, krfg, *_fox_operands(fox_b_f[i], fox_qk_gq[i], fox_qk_gk[i]))
        o_fox = _flash_call(xq, xk, xv, 1)

        e1, e2 = SSM_WIDTH, SSM_WIDTH + ATT_WIDTH
        wo = w_out[i].astype(BF16)
        x = _merge_call(o_ssm, o_mla, o_fox, x, g1, row2(out_norm[i, e1:e2]), row2(out_norm[i, e2:]),
                        wo[:e1], wo[e1:e2], wo[e2:])

        j = i // 2
        if i % 2 == 0:
            split = lambda w: w.reshape(D_MODEL, D_FF // FF_CHUNK, FF_CHUNK).transpose(1, 0, 2).astype(BF16)
            wd = ffn_w_down[j].reshape(D_FF // FF_CHUNK, FF_CHUNK, D_MODEL).astype(BF16)
            x = _ffn_call(x, row2(norm_ffn[i]), sh2, sc2, g2, split(ffn_w_gate[j]), split(ffn_w_up[j]), wd)
        else:
            wr = _pad_lanes(moe_w_router[j], LANE)
            wr_hi = wr.astype(BF16)
            wr_lo = (wr - wr_hi.astype(F32)).astype(BF16)
            comb, rank, rankt, counts = _router_call(x, row2(norm_ffn[i]), sh2, sc2, jnp.stack([wr_hi, wr_lo]),
                                                     _pad_lanes(row2(moe_b_router[j]), LANE))
            counts = counts[:, 0, :N_EXPERTS].astype(jnp.int32).reshape(-1)
            x = _moe_call(x, row2(norm_ffn[i]), sh2, sc2, g2, comb, rank, rankt, counts,
                          moe_w_gate[j].astype(BF16), moe_w_up[j].astype(BF16), moe_w_down[j].astype(BF16))
    return x
```

```python
import functools
import math

import jax
import jax.numpy as jnp
import numpy as np
from jax import lax
from jax.experimental import pallas as pl
from jax.experimental.pallas import tpu as pltpu

F32 = jnp.float32
BF16 = jnp.bfloat16

D_MODEL = 1024
BATCH = 8
SEQ = 4096
DEPTH = 4
CHUNK = 64
EPS = 1e-6

SSM_WIDTH = 256
SSM_GROUP = 16
N_SSM_GROUPS = 16
SSM_STATE = 64
N_STATE = N_SSM_GROUPS * SSM_STATE

MLA_HEADS = 6
MLA_Q_RANK = 256
MLA_KV_RANK = 128
MLA_NOPE = 64
MLA_ROPE = 32
MLA_V = 64
MLA_QK = 96
ROPE_BASE = 10000.0

FOX_HEADS = 6
FOX_HEAD_DIM = 64
ATT_WIDTH = 384

D_FF = 2816
N_EXPERTS = 8
D_FF_EXPERT = 1408

LANE = 128
SUBLANE = 8
HEAD_PAD = LANE
ONES_LANE = 64
NEG = -1e30

IN_PAD = 1920
KR_LANE = 64

ROW_TILE = 512
S5_STEPS = 64
ATT_TILE = 512
LOG2E = math.log2(math.e)
FF_CHUNK = 1408
MOE_TILE = 1024
MOE_ROWS = 256
VMEM_LIMIT = 56 * 1024 * 1024


def _params(sem):
    return pltpu.CompilerParams(dimension_semantics=sem, vmem_limit_bytes=VMEM_LIMIT)


def _rms_mod(x, g, sc, sh):
    ms = jnp.mean(x * x, axis=-1, keepdims=True)
    h = x * lax.rsqrt(ms + EPS) * g
    return h * (1.0 + sc) + sh


def _split3(x):
    hi = x.astype(BF16).astype(F32)
    r = x - hi
    mid = r.astype(BF16).astype(F32)
    lo = (r - mid).astype(BF16).astype(F32)
    return hi, mid, lo


def _ada_kernel(c_ref, w_ref, b_ref, o_ref):
    c = c_ref[...]
    ca = (c * jax.nn.sigmoid(c)).astype(BF16)
    o_ref[0] = jnp.dot(ca, w_ref[0].astype(BF16), preferred_element_type=F32) + b_ref[0]


def _ada_call(c, w_ada, b_ada):
    tn = 1536
    return pl.pallas_call(
        _ada_kernel,
        grid=(DEPTH, 6 * D_MODEL // tn),
        in_specs=[pl.BlockSpec((BATCH, D_MODEL), lambda i, j: (0, 0)),
                  pl.BlockSpec((1, D_MODEL, tn), lambda i, j: (i, 0, j)),
                  pl.BlockSpec((1, 1, tn), lambda i, j: (i, 0, j))],
        out_specs=pl.BlockSpec((1, BATCH, tn), lambda i, j: (i, 0, j)),
        out_shape=jax.ShapeDtypeStruct((DEPTH, BATCH, 6 * D_MODEL), F32),
        compiler_params=_params(("arbitrary", "arbitrary")),
        name="ada",
    )(c, w_ada, b_ada.reshape(DEPTH, 1, 6 * D_MODEL))


_IN_GROUPS = ((0, 256), (256, 512), (512, 640), (640, 768), (768, 1152), (1152, 1536), (1536, 1920))


def _inproj_kernel(x_ref, g_ref, sh_ref, sc_ref, w_ref, *out_refs):
    parts = 4
    step = x_ref.shape[1] // parts
    rows = [slice(r * step, (r + 1) * step) for r in range(parts)]
    normed = lambda r: _rms_mod(x_ref[0, rows[r]], g_ref[...], sc_ref[0], sh_ref[0]).astype(BF16)
    h_next = normed(0)
    for r in range(parts):
        h = h_next
        if r + 1 < parts:
            h_next = normed(r + 1)
        proj = jnp.dot(h, w_ref[...], preferred_element_type=F32)
        for ref, (c0, c1) in zip(out_refs, _IN_GROUPS):
            ref[0, rows[r]] = proj[:, c0:c1]


def _inproj_call(x, g, sh, sc, w):
    tm = ROW_TILE
    row = lambda b, i: (b, i, 0)
    per_b = lambda b, i: (b, 0, 0)
    const = lambda b, i: (0, 0)
    widths = [c1 - c0 for c0, c1 in _IN_GROUPS]
    return pl.pallas_call(
        _inproj_kernel,
        grid=(BATCH, SEQ // tm),
        in_specs=[pl.BlockSpec((1, tm, D_MODEL), row),
                  pl.BlockSpec((1, D_MODEL), const),
                  pl.BlockSpec((1, 1, D_MODEL), per_b),
                  pl.BlockSpec((1, 1, D_MODEL), per_b),
                  pl.BlockSpec((D_MODEL, IN_PAD), const)],
        out_specs=[pl.BlockSpec((1, tm, wd), row) for wd in widths],
        out_shape=[jax.ShapeDtypeStruct((BATCH, SEQ, wd), F32) for wd in widths],
        compiler_params=_params(("arbitrary", "arbitrary")),
        name="inproj",
    )(x, g, sh, sc, w)


def _pack_w_in(w):
    u, cq, ckv, kr, fq, fk, fv, fg = jnp.split(
        w, (256, 512, 640, 672, 1056, 1440, 1824), axis=1)
    z = lambda n: jnp.zeros((D_MODEL, n), w.dtype)
    krfg = jnp.concatenate([fg, z(KR_LANE - FOX_HEADS), kr, z(LANE - KR_LANE - MLA_ROPE)], axis=1)
    return jnp.concatenate([u, cq, ckv, krfg, fq, fk, fv], axis=1).astype(BF16)


def _s5_kernel(u_ref, bmat_ref, lam_ref, cmat_ref, d_ref, wglu_ref, bglu_ref, gn_ref,
               o_ref, bu_ref, state_ref, *, steps):
    @pl.when(pl.program_id(0) == 0)
    def _():
        state_ref[...] = jnp.zeros_like(state_ref)

    u = u_ref[...]
    bu_ref[...] = jnp.dot(u.astype(BF16), bmat_ref[...], preferred_element_type=F32)
    lr = jnp.broadcast_to(lam_ref[0:1, :], (SUBLANE, N_STATE))
    li = jnp.broadcast_to(lam_ref[1:2, :], (SUBLANE, N_STATE))

    def step(t, carry):
        sr, si = carry
        r0 = pl.multiple_of(t * SUBLANE, SUBLANE)
        nr = lr * sr - li * si + bu_ref[pl.ds(r0, SUBLANE), 0:N_STATE]
        ni = lr * si + li * sr + bu_ref[pl.ds(r0, SUBLANE), N_STATE:2 * N_STATE]
        bu_ref[pl.ds(r0, SUBLANE), 0:N_STATE] = nr
        bu_ref[pl.ds(r0, SUBLANE), N_STATE:2 * N_STATE] = ni
        return nr, ni

    sr, si = lax.fori_loop(0, steps, step,
                           (state_ref[:, 0:N_STATE], state_ref[:, N_STATE:2 * N_STATE]))
    state_ref[:, 0:N_STATE] = sr
    state_ref[:, N_STATE:2 * N_STATE] = si

    y = jnp.dot(bu_ref[...].astype(BF16), cmat_ref[...], preferred_element_type=F32)
    y = jax.nn.gelu(y + d_ref[...] * u)
    gate = jnp.dot(y.astype(BF16), wglu_ref[...], preferred_element_type=F32) + bglu_ref[...]
    o = y * jax.nn.sigmoid(gate)
    ms = jnp.mean(o * o, axis=-1, keepdims=True)
    o_ref[...] = (o * lax.rsqrt(ms + EPS) * gn_ref[...]).astype(BF16)


def _s5_call(u_t, bmat, lam, cmat, d_skip, wglu, bglu, gn):
    rows = S5_STEPS * BATCH
    const = lambda i: (0, 0)
    return pl.pallas_call(
        functools.partial(_s5_kernel, steps=S5_STEPS),
        grid=(SEQ // S5_STEPS,),
        in_specs=[pl.BlockSpec((rows, SSM_WIDTH), lambda i: (i, 0)),
                  pl.BlockSpec((SSM_WIDTH, 2 * N_STATE), const),
                  pl.BlockSpec((2, N_STATE), const),
                  pl.BlockSpec((2 * N_STATE, SSM_WIDTH), const),
                  pl.BlockSpec((1, SSM_WIDTH), const),
                  pl.BlockSpec((SSM_WIDTH, SSM_WIDTH), const),
                  pl.BlockSpec((1, SSM_WIDTH), const),
                  pl.BlockSpec((1, SSM_WIDTH), const)],
        out_specs=pl.BlockSpec((rows, SSM_WIDTH), lambda i: (i, 0)),
        out_shape=jax.ShapeDtypeStruct((SEQ * BATCH, SSM_WIDTH), BF16),
        scratch_shapes=[pltpu.VMEM((rows, 2 * N_STATE), F32),
                        pltpu.VMEM((SUBLANE, 2 * N_STATE), F32)],
        compiler_params=_params(("arbitrary",)),
        name="s5",
    )(u_t, bmat, lam, cmat, d_skip, wglu, bglu, gn)


def _s5_operands(lam_re, lam_im, log_dt, b_re, b_im, c_re, c_im):
    dt = jnp.exp(log_dt)[:, None]
    mag = jnp.exp(lam_re * dt)
    lb_re = mag * jnp.cos(lam_im * dt)
    lb_im = mag * jnp.sin(lam_im * dt)
    den = lam_re * lam_re + lam_im * lam_im
    co_re = ((lb_re - 1.0) * lam_re + lb_im * lam_im) / den
    co_im = (lb_im * lam_re - (lb_re - 1.0) * lam_im) / den
    bb_re = co_re[..., None] * b_re - co_im[..., None] * b_im
    bb_im = co_re[..., None] * b_im + co_im[..., None] * b_re
    eye = jnp.eye(N_SSM_GROUPS, dtype=F32)
    blk_b = lambda m: jnp.einsum("gpc,gh->gchp", m, eye).reshape(SSM_WIDTH, N_STATE)
    bmat = jnp.concatenate([blk_b(bb_re), blk_b(bb_im)], axis=1).astype(BF16)
    blk_c = lambda m: jnp.einsum("gcp,gh->gphc", m, eye).reshape(N_STATE, SSM_WIDTH)
    cmat = jnp.concatenate([blk_c(c_re), -blk_c(c_im)], axis=0).astype(BF16)
    lam = jnp.stack([lb_re.reshape(N_STATE), lb_im.reshape(N_STATE)], axis=0)
    return bmat, lam, cmat


def _rope_tables(positions):
    half = MLA_ROPE // 2
    inv = ROPE_BASE ** (-jnp.arange(half, dtype=F32) / half)
    ang = positions.astype(F32)[..., None] * inv
    cos, sin = jnp.cos(ang), jnp.sin(ang)
    shp = positions.shape
    one = lambda n: jnp.ones(shp + (n,), F32)
    zero = lambda n: jnp.zeros(shp + (n,), F32)
    cos_t = jnp.concatenate([one(MLA_NOPE), cos, cos, zero(LANE - MLA_QK)], axis=-1)
    sin_t = jnp.concatenate([zero(MLA_NOPE), -sin, sin, zero(LANE - MLA_QK)], axis=-1)
    return cos_t, sin_t


def _swap_rope_halves(a):
    half = MLA_ROPE // 2
    lo, hi = a[..., MLA_NOPE:MLA_NOPE + half], a[..., MLA_NOPE + half:MLA_QK]
    return jnp.concatenate([jnp.zeros_like(a[..., :MLA_NOPE]), hi, lo, jnp.zeros_like(a[..., MLA_QK:])], axis=-1)


def _store_key_blocks(kt_ref, h, k):
    kt = k.T
    for s in range(k.shape[0] // ATT_TILE):
        kt_ref[0, h, s] = kt[:, s * ATT_TILE:(s + 1) * ATT_TILE].astype(BF16)


_KT_SPEC = lambda heads, tl: pl.BlockSpec((1, heads, tl // ATT_TILE, HEAD_PAD, ATT_TILE),
                                          lambda b, i: (b, 0, i, 0, 0))
_KT_SHAPE = lambda heads: jax.ShapeDtypeStruct((BATCH, heads, SEQ // ATT_TILE, HEAD_PAD, ATT_TILE), BF16)


def _mla_prep_kernel(cq_ref, ckv_ref, krfg_ref, cos_ref, sin_ref, qn_ref, kvn_ref, wq_ref, wk_ref, wv_ref,
                     gq_ref, gqs_ref, gk_ref, gks_ref, q_ref, k_ref, v_ref):
    tl = cq_ref.shape[1]
    lane = lax.broadcasted_iota(jnp.int32, (tl, LANE), 1)
    cos, sin = cos_ref[0], sin_ref[0]
    q_scale = LOG2E / math.sqrt(MLA_QK)
    q_cos, q_sin = gq_ref[...] * cos * q_scale, gqs_ref[...] * sin * q_scale
    k_cos, k_sin = gk_ref[...] * cos, gks_ref[...] * sin
    ones = jnp.ones((LANE, LANE), BF16)

    def inv_rms(x):
        ss = jnp.dot((x * x).astype(BF16), ones, preferred_element_type=F32)
        return lax.rsqrt(ss / MLA_QK + EPS)

    cq = cq_ref[0]
    cqn = (cq * lax.rsqrt(jnp.mean(cq * cq, axis=-1, keepdims=True) + EPS) * qn_ref[...]).astype(BF16)
    ckv = ckv_ref[0]
    ckvn = (ckv * lax.rsqrt(jnp.mean(ckv * ckv, axis=-1, keepdims=True) + EPS) * kvn_ref[...]).astype(BF16)
    kr = jnp.where((lane >= KR_LANE) & (lane < KR_LANE + MLA_ROPE), krfg_ref[0], 0.0)
    kr_swapped = jnp.where(lane < KR_LANE + MLA_ROPE // 2, pltpu.roll(kr, LANE - 16, 1), pltpu.roll(kr, 16, 1))
    k_rotary = kr_swapped * k_sin

    heads = range(MLA_HEADS)
    qqs = [jnp.dot(cqn, wq_ref[h], preferred_element_type=F32) for h in heads]
    ks = [jnp.dot(ckvn, wk_ref[h], preferred_element_type=F32) + kr for h in heads]
    q_inv = [inv_rms(qq[:, :LANE]) for qq in qqs]
    k_inv = [inv_rms(k) for k in ks]
    for h in heads:
        q, q_swapped = qqs[h][:, :LANE], qqs[h][:, LANE:]
        q_ref[0, h] = (q_inv[h] * (q * q_cos + q_swapped * q_sin)).astype(BF16)
        _store_key_blocks(k_ref, h, k_inv[h] * (ks[h] * k_cos + k_rotary))
        v = jnp.dot(ckvn, wv_ref[h], preferred_element_type=F32)
        v_ref[0, h] = jnp.where(lane == ONES_LANE, 1.0, v).astype(BF16)


def _mla_prep_call(cq, ckv, krfg, tabs, qn, kvn, wq, wk, wv, gq, gqs, gk, gks):
    tl = ROW_TILE
    row = lambda b, i: (b, i, 0)
    c2 = lambda b, i: (0, 0)
    c3 = lambda b, i: (0, 0, 0)
    head_out = pl.BlockSpec((1, MLA_HEADS, tl, HEAD_PAD), lambda b, i: (b, 0, i, 0))
    head_shape = jax.ShapeDtypeStruct((BATCH, MLA_HEADS, SEQ, HEAD_PAD), BF16)
    gain = pl.BlockSpec((1, HEAD_PAD), c2)
    return pl.pallas_call(
        _mla_prep_kernel,
        grid=(BATCH, SEQ // tl),
        in_specs=[pl.BlockSpec((1, tl, MLA_Q_RANK), row),
                  pl.BlockSpec((1, tl, MLA_KV_RANK), row),
                  pl.BlockSpec((1, tl, LANE), row),
                  pl.BlockSpec((1, tl, LANE), row),
                  pl.BlockSpec((1, tl, LANE), row),
                  pl.BlockSpec((1, MLA_Q_RANK), c2),
                  pl.BlockSpec((1, MLA_KV_RANK), c2),
                  pl.BlockSpec((MLA_HEADS, MLA_Q_RANK, 2 * HEAD_PAD), c3),
                  pl.BlockSpec((MLA_HEADS, MLA_KV_RANK, HEAD_PAD), c3),
                  pl.BlockSpec((MLA_HEADS, MLA_KV_RANK, HEAD_PAD), c3),
                  gain, gain, gain, gain],
        out_specs=[head_out, _KT_SPEC(MLA_HEADS, tl), head_out],
        out_shape=[head_shape, _KT_SHAPE(MLA_HEADS), head_shape],
        compiler_params=_params(("arbitrary", "arbitrary")),
        name="mla_prep",
    )(cq, ckv, krfg, *tabs, qn, kvn, wq, wk, wv, gq, gqs, gk, gks)


def _pad_lanes(a, n=HEAD_PAD):
    return jnp.pad(a, [(0, 0)] * (a.ndim - 1) + [(0, n - a.shape[-1])])


def _mla_weights(w_uq, w_ukv, gq, gk):
    wq = _pad_lanes(w_uq.reshape(MLA_Q_RANK, MLA_HEADS, MLA_QK).transpose(1, 0, 2))
    wq = jnp.concatenate([wq, _swap_rope_halves(wq)], axis=-1).astype(BF16)
    wkv = w_ukv.reshape(MLA_KV_RANK, MLA_HEADS, MLA_NOPE + MLA_V).transpose(1, 0, 2)
    wk = _pad_lanes(wkv[..., :MLA_NOPE]).astype(BF16)
    wv = _pad_lanes(wkv[..., MLA_NOPE:]).astype(BF16)
    gq, gk = _pad_lanes(gq[None, :]), _pad_lanes(gk[None, :])
    return wq, wk, wv, gq, _swap_rope_halves(gq), gk, _swap_rope_halves(gk)


GATE_MID_LANE = 8
GATE_LO_LANE = 16
GATE_ONE_LANE = LANE - 1
Q_GATE_LANE = FOX_HEAD_DIM
K_GATE_LANE = FOX_HEAD_DIM + 3


def _fox_prep_kernel(fq_ref, fk_ref, fv_ref, krfg_ref, bf_ref, gq_ref, gk_ref, pq_ref, pk_ref, pv_ref,
                     q_ref, k_ref, v_ref, carry_ref):
    tl = fq_ref.shape[1]
    lane = lax.broadcasted_iota(jnp.int32, (tl, LANE), 1)

    @pl.when(pl.program_id(1) == 0)
    def _():
        carry_ref[...] = jnp.zeros_like(carry_ref)

    logf = jax.nn.log_sigmoid(krfg_ref[0] + bf_ref[...])
    logf = jnp.where(lane < FOX_HEADS, logf, 0.0)
    r_i = lax.broadcasted_iota(jnp.int32, (tl, tl), 0)
    c_i = lax.broadcasted_iota(jnp.int32, (tl, tl), 1)
    tri = jnp.where(c_i <= r_i, 1.0, 0.0).astype(BF16)
    cum = carry_ref[0:1, :]
    for piece in _split3(logf):
        cum = cum + jnp.dot(tri, piece.astype(BF16), preferred_element_type=F32)
    carry_ref[0:1, :] = cum[tl - 1:tl, :]

    c_hi, c_mid, c_lo = _split3(cum * LOG2E)
    gate_row = (c_hi + pltpu.roll(c_mid, GATE_MID_LANE, 1) + pltpu.roll(c_lo, GATE_LO_LANE, 1)
                + jnp.where(lane == GATE_ONE_LANE, 1.0, 0.0)).astype(BF16)

    p_r = lax.broadcasted_iota(jnp.int32, (LANE, LANE), 0)
    p_c = lax.broadcasted_iota(jnp.int32, (LANE, LANE), 1)
    head_mean = jnp.where(p_r // FOX_HEAD_DIM == p_c // FOX_HEAD_DIM, 1.0 / FOX_HEAD_DIM, 0.0).astype(BF16)

    def mean_sq(ref, j):
        x = ref[0, :, j * LANE:(j + 1) * LANE]
        return jnp.dot((x * x).astype(BF16), head_mean, preferred_element_type=F32)

    def normed(ref, g_ref, j, ms):
        lanes = slice(j * LANE, (j + 1) * LANE)
        return (ref[0, :, lanes] * lax.rsqrt(ms + EPS) * g_ref[:, lanes]).astype(BF16)

    def placed(x, p_ref, j):
        return jnp.dot(jnp.concatenate([x, gate_row], axis=1), p_ref[j], preferred_element_type=F32)

    pairs = range(FOX_HEADS // 2)
    q_ms = [mean_sq(fq_ref, j) for j in pairs]
    k_ms = [mean_sq(fk_ref, j) for j in pairs]
    q_n = [normed(fq_ref, gq_ref, j, q_ms[j]) for j in pairs]
    k_n = [normed(fk_ref, gk_ref, j, k_ms[j]) for j in pairs]
    for j in pairs:
        q = placed(q_n[j], pq_ref, j)
        k = placed(k_n[j], pk_ref, j)
        v = placed(fv_ref[0, :, j * LANE:(j + 1) * LANE].astype(BF16), pv_ref, j)
        for hh in range(2):
            head = slice(hh * HEAD_PAD, (hh + 1) * HEAD_PAD)
            q_ref[0, 2 * j + hh] = q[:, head].astype(BF16)
            _store_key_blocks(k_ref, 2 * j + hh, k[:, head])
            v_ref[0, 2 * j + hh] = v[:, head].astype(BF16)


def _fox_prep_call(fq, fk, fv, krfg, bf, gq, gk, pq, pk, pv):
    tl = ROW_TILE
    row = lambda b, i: (b, i, 0)
    c2 = lambda b, i: (0, 0)
    c3 = lambda b, i: (0, 0, 0)
    head_out = pl.BlockSpec((1, FOX_HEADS, tl, HEAD_PAD), lambda b, i: (b, 0, i, 0))
    head_shape = jax.ShapeDtypeStruct((BATCH, FOX_HEADS, SEQ, HEAD_PAD), BF16)
    place = pl.BlockSpec((FOX_HEADS // 2, 2 * LANE, 2 * HEAD_PAD), c3)
    return pl.pallas_call(
        _fox_prep_kernel,
        grid=(BATCH, SEQ // tl),
        in_specs=[pl.BlockSpec((1, tl, ATT_WIDTH), row),
                  pl.BlockSpec((1, tl, ATT_WIDTH), row),
                  pl.BlockSpec((1, tl, ATT_WIDTH), row),
                  pl.BlockSpec((1, tl, LANE), row),
                  pl.BlockSpec((1, LANE), c2),
                  pl.BlockSpec((1, ATT_WIDTH), c2),
                  pl.BlockSpec((1, ATT_WIDTH), c2),
                  place, place, place],
        out_specs=[head_out, _KT_SPEC(FOX_HEADS, tl), head_out],
        out_shape=[head_shape, _KT_SHAPE(FOX_HEADS), head_shape],
        scratch_shapes=[pltpu.VMEM((SUBLANE, LANE), F32)],
        compiler_params=_params(("arbitrary", "arbitrary")),
        name="fox_prep",
    )(fq, fk, fv, krfg, bf, gq, gk, pq, pk, pv)


def _fox_placements():
    pq = np.zeros((FOX_HEADS // 2, 2 * LANE, 2 * HEAD_PAD), np.float32)
    pk = np.zeros_like(pq)
    pv = np.zeros_like(pq)
    one_row = LANE + GATE_ONE_LANE
    for j in range(FOX_HEADS // 2):
        for hh in range(2):
            h, col0 = 2 * j + hh, hh * HEAD_PAD
            for d in range(FOX_HEAD_DIM):
                for p in (pq, pk, pv):
                    p[j, hh * FOX_HEAD_DIM + d, col0 + d] = 1.0
            pv[j, one_row, col0 + ONES_LANE] = 1.0
            for n, piece_lane in enumerate((0, GATE_MID_LANE, GATE_LO_LANE)):
                pq[j, LANE + piece_lane + h, col0 + Q_GATE_LANE + n] = 1.0
                pq[j, one_row, col0 + K_GATE_LANE + n] = 1.0
                pk[j, one_row, col0 + Q_GATE_LANE + n] = 1.0
                pk[j, LANE + piece_lane + h, col0 + K_GATE_LANE + n] = -1.0
    return tuple(jnp.asarray(p, BF16) for p in (pq, pk, pv))


def _fox_operands(bf, gq, gk):
    q_scale = LOG2E / math.sqrt(FOX_HEAD_DIM)
    return (_pad_lanes(bf[None, :], LANE), jnp.tile(gq * q_scale, FOX_HEADS)[None, :],
            jnp.tile(gk, FOX_HEADS)[None, :]) + _fox_placements()


def _flash_kernel(qa_ref, qb_ref, kt_ref, v_ref, gap_ref, o_ref, q_scr, s_ref, m_ref, acc_ref,
                  *, tile, chunk, n_tiles):
    p = pl.program_id(2)
    n_first = p + 1
    n_tasks = n_tiles + 1
    per_chunk = tile // chunk
    chunk_gap = gap_ref[...]
    lane = lax.broadcasted_iota(jnp.int32, (tile, HEAD_PAD), 1)
    q_scr[0] = qa_ref[0]
    q_scr[1] = qb_ref[0]

    def task(t):
        second = t >= n_first
        which = second.astype(jnp.int32)
        key_block = jnp.where(second, t - n_first, t)
        query_tile = jnp.where(second, n_tiles - 1 - p, p)
        return second, which, key_block, query_tile

    def over_tasks(fn):
        def quad(tt, carry):
            for u in range(4):
                fn(4 * tt + u)
            return carry
        lax.fori_loop(0, n_tasks // 4, quad, 0)
        for t in range(n_tasks - n_tasks % 4, n_tasks):
            fn(t)

    def scores(t):
        _, which, j, qt = task(t)
        allowed = chunk_gap <= (qt - j) * per_chunk
        for hh in range(2):
            s = jnp.dot(q_scr[which, hh], kt_ref[0, hh, j], preferred_element_type=F32)
            s = jnp.where(allowed, s, NEG)
            s_ref[hh, t] = s
            mr = m_ref[which, hh]
            for c in range(tile // LANE):
                mr = jnp.maximum(mr, s[:, c * LANE:(c + 1) * LANE])
            m_ref[which, hh] = mr

    m_ref[...] = jnp.full(m_ref.shape, NEG, F32)
    over_tasks(scores)
    ms = [[jnp.max(m_ref[w, hh], axis=1, keepdims=True) for hh in range(2)] for w in range(2)]

    def weighted(t):
        second, which, j, _ = task(t)
        k0 = pl.multiple_of(j * tile, tile)
        for hh in range(2):
            row_max = jnp.where(second, ms[1][hh], ms[0][hh])
            pr = jnp.exp2(s_ref[hh, t] - row_max).astype(BF16)
            acc_ref[which, hh] += jnp.dot(pr, v_ref[0, hh, pl.ds(k0, tile), :], preferred_element_type=F32)

    acc_ref[...] = jnp.zeros(acc_ref.shape, F32)
    over_tasks(weighted)
    for w in range(2):
        outs = [acc_ref[w, hh] / acc_ref[w, hh][:, ONES_LANE:ONES_LANE + 1] for hh in range(2)]
        o_ref[0, w, 0] = jnp.where(lane < 64, outs[0], pltpu.roll(outs[1], 64, 1)).astype(BF16)


def _flash_call(q, kt, v, chunk):
    tile = ATT_TILE
    heads = q.shape[1]
    n_tiles = SEQ // tile
    pos = np.arange(tile, dtype=np.int32) // chunk
    gap = jnp.asarray(pos[None, :] - pos[:, None])
    return pl.pallas_call(
        functools.partial(_flash_kernel, tile=tile, chunk=chunk, n_tiles=n_tiles),
        grid=(BATCH, heads // 2, n_tiles // 2),
        in_specs=[pl.BlockSpec((1, 2, tile, HEAD_PAD), lambda b, hp, p: (b, hp, p, 0)),
                  pl.BlockSpec((1, 2, tile, HEAD_PAD), lambda b, hp, p: (b, hp, n_tiles - 1 - p, 0)),
                  pl.BlockSpec((1, 2, n_tiles, HEAD_PAD, tile), lambda b, hp, p: (b, hp, 0, 0, 0)),
                  pl.BlockSpec((1, 2, SEQ, HEAD_PAD), lambda b, hp, p: (b, hp, 0, 0)),
                  pl.BlockSpec((tile, tile), lambda b, hp, p: (0, 0))],
        out_specs=pl.BlockSpec((1, 2, 1, tile, LANE), lambda b, hp, p: (b, 0, p, 0, hp)),
        out_shape=jax.ShapeDtypeStruct((BATCH, 2, n_tiles // 2, tile, ATT_WIDTH), BF16),
        scratch_shapes=[pltpu.VMEM((2, 2, tile, HEAD_PAD), BF16),
                        pltpu.VMEM((2, n_tiles + 1, tile, tile), F32),
                        pltpu.VMEM((2, 2, tile, LANE), F32),
                        pltpu.VMEM((2, 2, tile, HEAD_PAD), F32)],
        compiler_params=_params(("arbitrary", "arbitrary", "arbitrary")),
        name="flash_chunk%d" % chunk,
    )(q, q, kt, v, gap)


def _merge_kernel(ssm_ref, mla_ref, fox_ref, x_ref, g1_ref, gm_ref, gf_ref,
                  ws_ref, wm_ref, wf_ref, o_ref):
    def normed(ref, g_ref):
        a = ref[0, 0, 0].astype(F32)
        return (a * lax.rsqrt(jnp.mean(a * a, axis=-1, keepdims=True) + EPS) * g_ref[...]).astype(BF16)

    mix = jnp.dot(ssm_ref[0], ws_ref[...], preferred_element_type=F32)
    mix = mix + jnp.dot(normed(mla_ref, gm_ref), wm_ref[...], preferred_element_type=F32)
    mix = mix + jnp.dot(normed(fox_ref, gf_ref), wf_ref[...], preferred_element_type=F32)
    o_ref[0] = x_ref[0] + g1_ref[0] * mix


def _merge_call(o_ssm, o_mla, o_fox, x, g1, gm, gf, ws, wm, wf):
    tm = ATT_TILE
    half = SEQ // tm // 2
    row = lambda b, i: (b, i, 0)
    c2 = lambda b, i: (0, 0)
    att = pl.BlockSpec((1, 1, 1, tm, ATT_WIDTH),
                       lambda b, i: (b, i // half, jnp.where(i < half, i, 2 * half - 1 - i), 0, 0))
    return pl.pallas_call(
        _merge_kernel,
        grid=(BATCH, SEQ // tm),
        in_specs=[pl.BlockSpec((1, tm, SSM_WIDTH), row),
                  att, att,
                  pl.BlockSpec((1, tm, D_MODEL), row),
                  pl.BlockSpec((1, 1, D_MODEL), lambda b, i: (b, 0, 0)),
                  pl.BlockSpec((1, ATT_WIDTH), c2),
                  pl.BlockSpec((1, ATT_WIDTH), c2),
                  pl.BlockSpec((SSM_WIDTH, D_MODEL), c2),
                  pl.BlockSpec((ATT_WIDTH, D_MODEL), c2),
                  pl.BlockSpec((ATT_WIDTH, D_MODEL), c2)],
        out_specs=pl.BlockSpec((1, tm, D_MODEL), row),
        out_shape=jax.ShapeDtypeStruct((BATCH, SEQ, D_MODEL), F32),
        compiler_params=_params(("arbitrary", "arbitrary")),
        name="merge",
    )(o_ssm, o_mla, o_fox, x, g1, gm, gf, ws, wm, wf)


def _ffn_kernel(x_ref, g_ref, sh_ref, sc_ref, g2_ref, wg_ref, wu_ref, wd_ref, o_ref, h_ref, acc_ref):
    c = pl.program_id(2)

    @pl.when(c == 0)
    def _():
        h_ref[...] = _rms_mod(x_ref[0], g_ref[...], sc_ref[0], sh_ref[0]).astype(BF16)
        acc_ref[...] = jnp.zeros_like(acc_ref)

    h = h_ref[...]
    gate = jnp.dot(h, wg_ref[0], preferred_element_type=F32)
    up = jnp.dot(h, wu_ref[0], preferred_element_type=F32)
    a = (gate * jax.nn.sigmoid(gate) * up).astype(BF16)
    acc_ref[...] += jnp.dot(a, wd_ref[0], preferred_element_type=F32)

    @pl.when(c == pl.num_programs(2) - 1)
    def _():
        o_ref[0] = x_ref[0] + g2_ref[0] * acc_ref[...]


def _ffn_call(x, g, sh, sc, g2, wg, wu, wd):
    tm = MOE_TILE
    n_chunks = wg.shape[0]
    row = lambda b, i, c: (b, i, 0)
    per_b = lambda b, i, c: (b, 0, 0)
    chunk = lambda b, i, c: (c, 0, 0)
    return pl.pallas_call(
        _ffn_kernel,
        grid=(BATCH, SEQ // tm, n_chunks),
        in_specs=[pl.BlockSpec((1, tm, D_MODEL), row),
                  pl.BlockSpec((1, D_MODEL), lambda b, i, c: (0, 0)),
                  pl.BlockSpec((1, 1, D_MODEL), per_b),
                  pl.BlockSpec((1, 1, D_MODEL), per_b),
                  pl.BlockSpec((1, 1, D_MODEL), per_b),
                  pl.BlockSpec((1, D_MODEL, FF_CHUNK), chunk),
                  pl.BlockSpec((1, D_MODEL, FF_CHUNK), chunk),
                  pl.BlockSpec((1, FF_CHUNK, D_MODEL), chunk)],
        out_specs=pl.BlockSpec((1, tm, D_MODEL), row),
        out_shape=jax.ShapeDtypeStruct((BATCH, SEQ, D_MODEL), F32),
        scratch_shapes=[pltpu.VMEM((tm, D_MODEL), BF16), pltpu.VMEM((tm, D_MODEL), F32)],
        compiler_params=_params(("arbitrary", "arbitrary", "arbitrary")),
        name="ffn_dense",
    )(x, g, sh, sc, g2, wg, wu, wd)


def _router_kernel(x_ref, g_ref, sh_ref, sc_ref, w_ref, b_ref, comb_ref, rank_ref, rankt_ref, count_ref):
    tm = x_ref.shape[1]
    h = _rms_mod(x_ref[0], g_ref[...], sc_ref[0], sh_ref[0])
    h_hi = h.astype(BF16)
    h_lo = (h - h_hi.astype(F32)).astype(BF16)
    w_hi, w_lo = w_ref[0], w_ref[1]
    logits = (jnp.dot(h_hi, w_hi, preferred_element_type=F32)
              + jnp.dot(h_lo, w_hi, preferred_element_type=F32)
              + jnp.dot(h_hi, w_lo, preferred_element_type=F32)) + b_ref[...]
    lane = lax.broadcasted_iota(jnp.int32, logits.shape, 1)
    logits = jnp.where(lane < N_EXPERTS, logits, -jnp.inf)
    m1 = jnp.max(logits, axis=-1, keepdims=True)
    i1 = jnp.min(jnp.where(logits == m1, lane, LANE), axis=-1, keepdims=True)
    rest = jnp.where(lane == i1, -jnp.inf, logits)
    m2 = jnp.max(rest, axis=-1, keepdims=True)
    i2 = jnp.min(jnp.where(rest == m2, lane, LANE), axis=-1, keepdims=True)
    e = jnp.exp(m2 - m1)
    p1 = 1.0 / (1.0 + e)
    comb_ref[0] = jnp.where(lane == i1, p1, 0.0) + jnp.where(lane == i2, e * p1, 0.0)

    chosen = (lane == i1) | (lane == i2)
    chosen_f = jnp.where(chosen, 1.0, 0.0)
    r_i = lax.broadcasted_iota(jnp.int32, (tm, tm), 0)
    c_i = lax.broadcasted_iota(jnp.int32, (tm, tm), 1)
    earlier = jnp.where(c_i < r_i, 1.0, 0.0).astype(BF16)
    rank = jnp.dot(earlier, chosen_f.astype(BF16), preferred_element_type=F32)
    rank = jnp.where(chosen, rank, -1.0)
    rank_ref[0] = rank
    rankt_ref[0] = rank.T[0:SUBLANE, :]
    count_ref[0] = jnp.sum(chosen_f, axis=0, keepdims=True)


def _router_call(x, g, sh, sc, w, b):
    tm = MOE_TILE
    tiles = SEQ // tm
    row = lambda b_, i: (b_, i, 0)
    per_b = lambda b_, i: (b_, 0, 0)
    per_tile = lambda b_, i: (b_ * tiles + i, 0, 0)
    return pl.pallas_call(
        _router_kernel,
        grid=(BATCH, tiles),
        in_specs=[pl.BlockSpec((1, tm, D_MODEL), row),
                  pl.BlockSpec((1, D_MODEL), lambda b_, i: (0, 0)),
                  pl.BlockSpec((1, 1, D_MODEL), per_b),
                  pl.BlockSpec((1, 1, D_MODEL), per_b),
                  pl.BlockSpec((2, D_MODEL, LANE), lambda b_, i: (0, 0, 0)),
                  pl.BlockSpec((1, LANE), lambda b_, i: (0, 0))],
        out_specs=[pl.BlockSpec((1, tm, LANE), row),
                   pl.BlockSpec((1, tm, LANE), row),
                   pl.BlockSpec((1, SUBLANE, tm), per_tile),
                   pl.BlockSpec((1, 1, LANE), per_tile)],
        out_shape=[jax.ShapeDtypeStruct((BATCH, SEQ, LANE), F32),
                   jax.ShapeDtypeStruct((BATCH, SEQ, LANE), F32),
                   jax.ShapeDtypeStruct((BATCH * tiles, SUBLANE, tm), F32),
                   jax.ShapeDtypeStruct((BATCH * tiles, 1, LANE), F32)],
        compiler_params=_params(("arbitrary", "arbitrary")),
        name="router",
    )(x, g, sh, sc, w, b)


def _moe_kernel(count_ref, x_ref, g_ref, sh_ref, sc_ref, g2_ref, comb_ref, rank_ref, rankt_ref,
                wg_ref, wu_ref, wd_ref, o_ref, h_ref):
    tm = x_ref.shape[1]
    e = pl.program_id(1)

    @pl.when(e == 0)
    def _():
        x = x_ref[0]
        h_ref[...] = _rms_mod(x, g_ref[...], sc_ref[0], sh_ref[0]).astype(BF16)
        o_ref[0] = x

    lane = lax.broadcasted_iota(jnp.int32, (tm, LANE), 1)
    mine = lane == e
    rank_col = jnp.sum(jnp.where(mine, rank_ref[0], 0.0), axis=-1, keepdims=True)
    gate_col = jnp.sum(jnp.where(mine, comb_ref[0], 0.0), axis=-1, keepdims=True)
    rank_row = rankt_ref[0, pl.ds(e, 1), :]
    count = count_ref[pl.program_id(0) * N_EXPERTS + e]

    def expert_pass(first, n_rows):
        base = first.astype(F32)
        slot_sub = lax.broadcasted_iota(jnp.int32, (n_rows, tm), 0).astype(F32)
        slot_lane = lax.broadcasted_iota(jnp.int32, (tm, n_rows), 1).astype(F32)
        pick = jnp.where(rank_row - base == slot_sub, 1.0, 0.0).astype(BF16)
        rows = jnp.dot(pick, h_ref[...], preferred_element_type=F32).astype(BF16)
        gate = jnp.dot(rows, wg_ref[0], preferred_element_type=F32)
        up = jnp.dot(rows, wu_ref[0], preferred_element_type=F32)
        a = (gate * jax.nn.sigmoid(gate) * up).astype(BF16)
        y = jnp.dot(a, wd_ref[0], preferred_element_type=F32).astype(BF16)
        place = jnp.where(rank_col - base == slot_lane, 1.0, 0.0).astype(BF16)
        back = jnp.dot(place, y, preferred_element_type=F32)
        o_ref[0] += g2_ref[0] * (gate_col * back)

    def full_pass(sb, carry):
        expert_pass(sb * MOE_ROWS, MOE_ROWS)
        return carry

    n_full = count // MOE_ROWS
    lax.fori_loop(0, n_full, full_pass, 0)
    left = count - n_full * MOE_ROWS

    @pl.when(left > MOE_ROWS // 2)
    def _():
        expert_pass(n_full * MOE_ROWS, MOE_ROWS)

    @pl.when((left > 0) & (left <= MOE_ROWS // 2))
    def _():
        expert_pass(n_full * MOE_ROWS, MOE_ROWS // 2)


def _moe_call(x, g, sh, sc, g2, comb, rank, rankt, counts, wg, wu, wd):
    tm = MOE_TILE
    tiles = SEQ // tm
    n_tiles = BATCH * tiles
    row = lambda i, e, cnt: (i, 0, 0)
    per_b = lambda i, e, cnt: (i // tiles, 0, 0)
    expert = lambda i, e, cnt: (e, 0, 0)
    as_tiles = lambda a: a.reshape(n_tiles, tm, a.shape[-1])
    grid_spec = pltpu.PrefetchScalarGridSpec(
        num_scalar_prefetch=1,
        grid=(n_tiles, N_EXPERTS),
        in_specs=[pl.BlockSpec((1, tm, D_MODEL), row),
                  pl.BlockSpec((1, D_MODEL), lambda i, e, cnt: (0, 0)),
                  pl.BlockSpec((1, 1, D_MODEL), per_b),
                  pl.BlockSpec((1, 1, D_MODEL), per_b),
                  pl.BlockSpec((1, 1, D_MODEL), per_b),
                  pl.BlockSpec((1, tm, LANE), row),
                  pl.BlockSpec((1, tm, LANE), row),
                  pl.BlockSpec((1, SUBLANE, tm), row),
                  pl.BlockSpec((1, D_MODEL, D_FF_EXPERT), expert),
                  pl.BlockSpec((1, D_MODEL, D_FF_EXPERT), expert),
                  pl.BlockSpec((1, D_FF_EXPERT, D_MODEL), expert)],
        out_specs=pl.BlockSpec((1, tm, D_MODEL), row),
        scratch_shapes=[pltpu.VMEM((tm, D_MODEL), BF16)],
    )
    out = pl.pallas_call(
        _moe_kernel,
        grid_spec=grid_spec,
        out_shape=jax.ShapeDtypeStruct((n_tiles, tm, D_MODEL), F32),
        compiler_params=_params(("arbitrary", "arbitrary")),
        name="moe_experts",
    )(counts, as_tiles(x), g, sh, sc, g2, as_tiles(comb), as_tiles(rank), rankt, wg, wu, wd)
    return out.reshape(BATCH, SEQ, D_MODEL)


def kernel(x, c, positions, norm_mix, norm_ffn, w_ada, b_ada, w_in, ssm_lam_re, ssm_lam_im, ssm_log_dt, ssm_b_re, ssm_b_im, ssm_c_re, ssm_c_im, ssm_d, ssm_w_glu, ssm_b_glu, mla_q_norm, mla_kv_norm, mla_w_uq, mla_w_ukv, mla_qk_gq, mla_qk_gk, fox_b_f, fox_qk_gq, fox_qk_gk, out_norm, w_out, ffn_w_gate, ffn_w_up, ffn_w_down, moe_w_router, moe_b_router, moe_w_gate, moe_w_up, moe_w_down):
    tabs = _rope_tables(positions)
    ada = _ada_call(c, w_ada, b_ada)
    ada = ada.reshape(DEPTH, BATCH, 6, 1, D_MODEL)
    row2 = lambda a: a[None, :]

    for i in range(DEPTH):
        sh1, sc1, g1, sh2, sc2, g2 = (ada[i, :, n] for n in range(6))

        u, cq, ckv, krfg, fq, fk, fv = _inproj_call(x, row2(norm_mix[i]), sh1, sc1, _pack_w_in(w_in[i]))

        bmat, lam, cmat = _s5_operands(ssm_lam_re[i], ssm_lam_im[i], ssm_log_dt[i],
                                       ssm_b_re[i], ssm_b_im[i], ssm_c_re[i], ssm_c_im[i])
        u_t = u.transpose(1, 0, 2).reshape(SEQ * BATCH, SSM_WIDTH)
        o_ssm = _s5_call(u_t, bmat, lam, cmat, row2(ssm_d[i]), ssm_w_glu[i].astype(BF16),
                         row2(ssm_b_glu[i]), row2(out_norm[i, :SSM_WIDTH]))
        o_ssm = o_ssm.reshape(SEQ, BATCH, SSM_WIDTH).transpose(1, 0, 2)

        mq, mk, mv = _mla_prep_call(cq, ckv, krfg, tabs, row2(mla_q_norm[i]), row2(mla_kv_norm[i]),
                                    *_mla_weights(mla_w_uq[i], mla_w_ukv[i], mla_qk_gq[i], mla_qk_gk[i]))
        o_mla = _flash_call(mq, mk, mv, CHUNK)

        xq, xk, xv = _fox_prep_call(fq, fk, fv, krfg, *_fox_operands(fox_b_f[i], fox_qk_gq[i], fox_qk_gk[i]))
        o_fox = _flash_call(xq, xk, xv, 1)

        e1, e2 = SSM_WIDTH, SSM_WIDTH + ATT_WIDTH
        wo = w_out[i].astype(BF16)
        x = _merge_call(o_ssm, o_mla, o_fox, x, g1, row2(out_norm[i, e1:e2]), row2(out_norm[i, e2:]),
                        wo[:e1], wo[e1:e2], wo[e2:])

        j = i // 2
        if i % 2 == 0:
            split = lambda w: w.reshape(D_MODEL, D_FF // FF_CHUNK, FF_CHUNK).transpose(1, 0, 2).astype(BF16)
            wd = ffn_w_down[j].reshape(D_FF // FF_CHUNK, FF_CHUNK, D_MODEL).astype(BF16)
            x = _ffn_call(x, row2(norm_ffn[i]), sh2, sc2, g2, split(ffn_w_gate[j]), split(ffn_w_up[j]), wd)
        else:
            wr = _pad_lanes(moe_w_router[j], LANE)
            wr_hi = wr.astype(BF16)
            wr_lo = (wr - wr_hi.astype(F32)).astype(BF16)
            comb, rank, rankt, counts = _router_call(x, row2(norm_ffn[i]), sh2, sc2, jnp.stack([wr_hi, wr_lo]),
                                                     _pad_lanes(row2(moe_b_router[j]), LANE))
            counts = counts[:, 0, :N_EXPERTS].astype(jnp.int32).reshape(-1)
            x = _moe_call(x, row2(norm_ffn[i]), sh2, sc2, g2, comb, rank, rankt, counts,
                          moe_w_gate[j].astype(BF16), moe_w_up[j].astype(BF16), moe_w_down[j].astype(BF16))
    return x
```

```python
import functools
import math

import jax
import jax.numpy as jnp
import numpy as np
from jax import lax
from jax.experimental import pallas as pl
from jax.experimental.pallas import tpu as pltpu

F32 = jnp.float32
BF16 = jnp.bfloat16

D_MODEL = 1024
BATCH = 8
SEQ = 4096
DEPTH = 4
CHUNK = 64
EPS = 1e-6

SSM_WIDTH = 256
SSM_GROUP = 16
N_SSM_GROUPS = 16
SSM_STATE = 64
N_STATE = N_SSM_GROUPS * SSM_STATE

MLA_HEADS = 6
MLA_Q_RANK = 256
MLA_KV_RANK = 128
MLA_NOPE = 64
MLA_ROPE = 32
MLA_V = 64
MLA_QK = 96
ROPE_BASE = 10000.0

FOX_HEADS = 6
FOX_HEAD_DIM = 64
ATT_WIDTH = 384

D_FF = 2816
N_EXPERTS = 8
D_FF_EXPERT = 1408

LANE = 128
SUBLANE = 8
HEAD_PAD = LANE
ONES_LANE = 64
NEG = -1e30

IN_PAD = 1920
KR_LANE = 64

ROW_TILE = 512
S5_STEPS = 64
ATT_TILE = 512
LOG2E = math.log2(math.e)
FF_CHUNK = 1408
MOE_TILE = 1024
MOE_ROWS = 256
VMEM_LIMIT = 56 * 1024 * 1024


def _params(sem):
    return pltpu.CompilerParams(dimension_semantics=sem, vmem_limit_bytes=VMEM_LIMIT)


def _rms_mod(x, g, sc, sh):
    ms = jnp.mean(x * x, axis=-1, keepdims=True)
    h = x * lax.rsqrt(ms + EPS) * g
    return h * (1.0 + sc) + sh


def _split3(x):
    hi = x.astype(BF16).astype(F32)
    r = x - hi
    mid = r.astype(BF16).astype(F32)
    lo = (r - mid).astype(BF16).astype(F32)
    return hi, mid, lo


def _ada_kernel(c_ref, w_ref, b_ref, o_ref):
    c = c_ref[...]
    ca = (c * jax.nn.sigmoid(c)).astype(BF16)
    o_ref[0] = jnp.dot(ca, w_ref[0].astype(BF16), preferred_element_type=F32) + b_ref[0]


def _ada_call(c, w_ada, b_ada):
    tn = 1536
    return pl.pallas_call(
        _ada_kernel,
        grid=(DEPTH, 6 * D_MODEL // tn),
        in_specs=[pl.BlockSpec((BATCH, D_MODEL), lambda i, j: (0, 0)),
                  pl.BlockSpec((1, D_MODEL, tn), lambda i, j: (i, 0, j)),
                  pl.BlockSpec((1, 1, tn), lambda i, j: (i, 0, j))],
        out_specs=pl.BlockSpec((1, BATCH, tn), lambda i, j: (i, 0, j)),
        out_shape=jax.ShapeDtypeStruct((DEPTH, BATCH, 6 * D_MODEL), F32),
        compiler_params=_params(("arbitrary", "arbitrary")),
        name="ada",
    )(c, w_ada, b_ada.reshape(DEPTH, 1, 6 * D_MODEL))


_IN_GROUPS = ((0, 256), (256, 512), (512, 640), (640, 768), (768, 1152), (1152, 1536), (1536, 1920))


def _inproj_kernel(x_ref, g_ref, sh_ref, sc_ref, w_ref, *out_refs):
    parts = 4
    step = x_ref.shape[1] // parts
    rows = [slice(r * step, (r + 1) * step) for r in range(parts)]
    normed = lambda r: _rms_mod(x_ref[0, rows[r]], g_ref[...], sc_ref[0], sh_ref[0]).astype(BF16)
    h_next = normed(0)
    for r in range(parts):
        h = h_next
        if r + 1 < parts:
            h_next = normed(r + 1)
        proj = jnp.dot(h, w_ref[...], preferred_element_type=F32)
        for ref, (c0, c1) in zip(out_refs, _IN_GROUPS):
            ref[0, rows[r]] = proj[:, c0:c1]


def _inproj_call(x, g, sh, sc, w):
    tm = ROW_TILE
    row = lambda b, i: (b, i, 0)
    per_b = lambda b, i: (b, 0, 0)
    const = lambda b, i: (0, 0)
    widths = [c1 - c0 for c0, c1 in _IN_GROUPS]
    return pl.pallas_call(
        _inproj_kernel,
        grid=(BATCH, SEQ // tm),
        in_specs=[pl.BlockSpec((1, tm, D_MODEL), row),
                  pl.BlockSpec((1, D_MODEL), const),
                  pl.BlockSpec((1, 1, D_MODEL), per_b),
                  pl.BlockSpec((1, 1, D_MODEL), per_b),
                  pl.BlockSpec((D_MODEL, IN_PAD), const)],
        out_specs=[pl.BlockSpec((1, tm, wd), row) for wd in widths],
        out_shape=[jax.ShapeDtypeStruct((BATCH, SEQ, wd), F32) for wd in widths],
        compiler_params=_params(("arbitrary", "arbitrary")),
        name="inproj",
    )(x, g, sh, sc, w)


def _pack_w_in(w):
    u, cq, ckv, kr, fq, fk, fv, fg = jnp.split(
        w, (256, 512, 640, 672, 1056, 1440, 1824), axis=1)
    z = lambda n: jnp.zeros((D_MODEL, n), w.dtype)
    krfg = jnp.concatenate([fg, z(KR_LANE - FOX_HEADS), kr, z(LANE - KR_LANE - MLA_ROPE)], axis=1)
    return jnp.concatenate([u, cq, ckv, krfg, fq, fk, fv], axis=1).astype(BF16)


def _s5_kernel(u_ref, bmat_ref, lam_ref, cmat_ref, d_ref, wglu_ref, bglu_ref, gn_ref,
               o_ref, bu0_ref, bu1_ref, state_ref, *, steps):
    rows = steps * BATCH

    @pl.when(pl.program_id(0) == 0)
    def _():
        state_ref[...] = jnp.zeros_like(state_ref)

    halves = ((bu0_ref, slice(0, rows)), (bu1_ref, slice(rows, 2 * rows)))
    for bu_ref, rs in halves:
        bu_ref[...] = jnp.dot(u_ref[rs, :].astype(BF16), bmat_ref[...], preferred_element_type=F32)
    lr = jnp.broadcast_to(lam_ref[0:1, :], (SUBLANE, N_STATE))
    li = jnp.broadcast_to(lam_ref[1:2, :], (SUBLANE, N_STATE))
    sr, si = state_ref[:, 0:N_STATE], state_ref[:, N_STATE:2 * N_STATE]

    for bu_ref, rs in halves:
        for t in range(steps):
            r = slice(t * SUBLANE, (t + 1) * SUBLANE)
            nr = lr * sr - li * si + bu_ref[r, 0:N_STATE]
            ni = lr * si + li * sr + bu_ref[r, N_STATE:2 * N_STATE]
            bu_ref[r, 0:N_STATE] = nr
            bu_ref[r, N_STATE:2 * N_STATE] = ni
            sr, si = nr, ni
        y = jnp.dot(bu_ref[...].astype(BF16), cmat_ref[...], preferred_element_type=F32)
        y = jax.nn.gelu(y + d_ref[...] * u_ref[rs, :])
        gate = jnp.dot(y.astype(BF16), wglu_ref[...], preferred_element_type=F32) + bglu_ref[...]
        o = y * jax.nn.sigmoid(gate)
        ms = jnp.mean(o * o, axis=-1, keepdims=True)
        o_ref[rs, :] = (o * lax.rsqrt(ms + EPS) * gn_ref[...]).astype(BF16)

    state_ref[:, 0:N_STATE] = sr
    state_ref[:, N_STATE:2 * N_STATE] = si


def _s5_call(u_t, bmat, lam, cmat, d_skip, wglu, bglu, gn):
    rows = S5_STEPS * BATCH
    const = lambda i: (0, 0)
    return pl.pallas_call(
        functools.partial(_s5_kernel, steps=S5_STEPS),
        grid=(SEQ // (2 * S5_STEPS),),
        in_specs=[pl.BlockSpec((2 * rows, SSM_WIDTH), lambda i: (i, 0)),
                  pl.BlockSpec((SSM_WIDTH, 2 * N_STATE), const),
                  pl.BlockSpec((2, N_STATE), const),
                  pl.BlockSpec((2 * N_STATE, SSM_WIDTH), const),
                  pl.BlockSpec((1, SSM_WIDTH), const),
                  pl.BlockSpec((SSM_WIDTH, SSM_WIDTH), const),
                  pl.BlockSpec((1, SSM_WIDTH), const),
                  pl.BlockSpec((1, SSM_WIDTH), const)],
        out_specs=pl.BlockSpec((2 * rows, SSM_WIDTH), lambda i: (i, 0)),
        out_shape=jax.ShapeDtypeStruct((SEQ * BATCH, SSM_WIDTH), BF16),
        scratch_shapes=[pltpu.VMEM((rows, 2 * N_STATE), F32),
                        pltpu.VMEM((rows, 2 * N_STATE), F32),
                        pltpu.VMEM((SUBLANE, 2 * N_STATE), F32)],
        compiler_params=_params(("arbitrary",)),
        name="s5",
    )(u_t, bmat, lam, cmat, d_skip, wglu, bglu, gn)


def _s5_operands(lam_re, lam_im, log_dt, b_re, b_im, c_re, c_im):
    dt = jnp.exp(log_dt)[:, None]
    mag = jnp.exp(lam_re * dt)
    lb_re = mag * jnp.cos(lam_im * dt)
    lb_im = mag * jnp.sin(lam_im * dt)
    den = lam_re * lam_re + lam_im * lam_im
    co_re = ((lb_re - 1.0) * lam_re + lb_im * lam_im) / den
    co_im = (lb_im * lam_re - (lb_re - 1.0) * lam_im) / den
    bb_re = co_re[..., None] * b_re - co_im[..., None] * b_im
    bb_im = co_re[..., None] * b_im + co_im[..., None] * b_re
    eye = jnp.eye(N_SSM_GROUPS, dtype=F32)
    blk_b = lambda m: jnp.einsum("gpc,gh->gchp", m, eye).reshape(SSM_WIDTH, N_STATE)
    bmat = jnp.concatenate([blk_b(bb_re), blk_b(bb_im)], axis=1).astype(BF16)
    blk_c = lambda m: jnp.einsum("gcp,gh->gphc", m, eye).reshape(N_STATE, SSM_WIDTH)
    cmat = jnp.concatenate([blk_c(c_re), -blk_c(c_im)], axis=0).astype(BF16)
    lam = jnp.stack([lb_re.reshape(N_STATE), lb_im.reshape(N_STATE)], axis=0)
    return bmat, lam, cmat


def _rope_tables(positions):
    half = MLA_ROPE // 2
    inv = ROPE_BASE ** (-jnp.arange(half, dtype=F32) / half)
    ang = positions.astype(F32)[..., None] * inv
    cos, sin = jnp.cos(ang), jnp.sin(ang)
    shp = positions.shape
    one = lambda n: jnp.ones(shp + (n,), F32)
    zero = lambda n: jnp.zeros(shp + (n,), F32)
    cos_t = jnp.concatenate([one(MLA_NOPE), cos, cos, zero(LANE - MLA_QK)], axis=-1)
    sin_t = jnp.concatenate([zero(MLA_NOPE), -sin, sin, zero(LANE - MLA_QK)], axis=-1)
    return cos_t, sin_t


def _swap_rope_halves(a):
    half = MLA_ROPE // 2
    lo, hi = a[..., MLA_NOPE:MLA_NOPE + half], a[..., MLA_NOPE + half:MLA_QK]
    return jnp.concatenate([jnp.zeros_like(a[..., :MLA_NOPE]), hi, lo, jnp.zeros_like(a[..., MLA_QK:])], axis=-1)


def _store_key_blocks(kt_ref, h, k):
    kt = k.T
    for s in range(k.shape[0] // ATT_TILE):
        kt_ref[0, h, s] = kt[:, s * ATT_TILE:(s + 1) * ATT_TILE].astype(BF16)


_KT_SPEC = lambda heads, tl: pl.BlockSpec((1, heads, tl // ATT_TILE, HEAD_PAD, ATT_TILE),
                                          lambda b, i: (b, 0, i, 0, 0))
_KT_SHAPE = lambda heads: jax.ShapeDtypeStruct((BATCH, heads, SEQ // ATT_TILE, HEAD_PAD, ATT_TILE), BF16)


def _mla_prep_kernel(cq_ref, ckv_ref, krfg_ref, cos_ref, sin_ref, qn_ref, kvn_ref, wq_ref, wk_ref, wv_ref,
                     gq_ref, gqs_ref, gk_ref, gks_ref, q_ref, k_ref, v_ref):
    tl = cq_ref.shape[1]
    lane = lax.broadcasted_iota(jnp.int32, (tl, LANE), 1)
    cos, sin = cos_ref[0], sin_ref[0]
    q_scale = LOG2E / math.sqrt(MLA_QK)
    q_cos, q_sin = gq_ref[...] * cos * q_scale, gqs_ref[...] * sin * q_scale
    k_cos, k_sin = gk_ref[...] * cos, gks_ref[...] * sin
    ones = jnp.ones((LANE, LANE), BF16)

    def inv_rms(x):
        ss = jnp.dot((x * x).astype(BF16), ones, preferred_element_type=F32)
        return lax.rsqrt(ss / MLA_QK + EPS)

    cq = cq_ref[0]
    cqn = (cq * lax.rsqrt(jnp.mean(cq * cq, axis=-1, keepdims=True) + EPS) * qn_ref[...]).astype(BF16)
    ckv = ckv_ref[0]
    ckvn = (ckv * lax.rsqrt(jnp.mean(ckv * ckv, axis=-1, keepdims=True) + EPS) * kvn_ref[...]).astype(BF16)
    kr = jnp.where((lane >= KR_LANE) & (lane < KR_LANE + MLA_ROPE), krfg_ref[0], 0.0)
    kr_swapped = jnp.where(lane < KR_LANE + MLA_ROPE // 2, pltpu.roll(kr, LANE - 16, 1), pltpu.roll(kr, 16, 1))
    k_rotary = kr_swapped * k_sin

    heads = range(MLA_HEADS)
    qqs = [jnp.dot(cqn, wq_ref[h], preferred_element_type=F32) for h in heads]
    ks = [jnp.dot(ckvn, wk_ref[h], preferred_element_type=F32) + kr for h in heads]
    q_inv = [inv_rms(qq[:, :LANE]) for qq in qqs]
    k_inv = [inv_rms(k) for k in ks]
    for h in heads:
        q, q_swapped = qqs[h][:, :LANE], qqs[h][:, LANE:]
        q_ref[0, h] = (q_inv[h] * (q * q_cos + q_swapped * q_sin)).astype(BF16)
        _store_key_blocks(k_ref, h, k_inv[h] * (ks[h] * k_cos + k_rotary))
        v = jnp.dot(ckvn, wv_ref[h], preferred_element_type=F32)
        v_ref[0, h] = jnp.where(lane == ONES_LANE, 1.0, v).astype(BF16)


def _mla_prep_call(cq, ckv, krfg, tabs, qn, kvn, wq, wk, wv, gq, gqs, gk, gks):
    tl = ROW_TILE
    row = lambda b, i: (b, i, 0)
    c2 = lambda b, i: (0, 0)
    c3 = lambda b, i: (0, 0, 0)
    head_out = pl.BlockSpec((1, MLA_HEADS, tl, HEAD_PAD), lambda b, i: (b, 0, i, 0))
    head_shape = jax.ShapeDtypeStruct((BATCH, MLA_HEADS, SEQ, HEAD_PAD), BF16)
    gain = pl.BlockSpec((1, HEAD_PAD), c2)
    return pl.pallas_call(
        _mla_prep_kernel,
        grid=(BATCH, SEQ // tl),
        in_specs=[pl.BlockSpec((1, tl, MLA_Q_RANK), row),
                  pl.BlockSpec((1, tl, MLA_KV_RANK), row),
                  pl.BlockSpec((1, tl, LANE), row),
                  pl.BlockSpec((1, tl, LANE), row),
                  pl.BlockSpec((1, tl, LANE), row),
                  pl.BlockSpec((1, MLA_Q_RANK), c2),
                  pl.BlockSpec((1, MLA_KV_RANK), c2),
                  pl.BlockSpec((MLA_HEADS, MLA_Q_RANK, 2 * HEAD_PAD), c3),
                  pl.BlockSpec((MLA_HEADS, MLA_KV_RANK, HEAD_PAD), c3),
                  pl.BlockSpec((MLA_HEADS, MLA_KV_RANK, HEAD_PAD), c3),
                  gain, gain, gain, gain],
        out_specs=[head_out, _KT_SPEC(MLA_HEADS, tl), head_out],
        out_shape=[head_shape, _KT_SHAPE(MLA_HEADS), head_shape],
        compiler_params=_params(("arbitrary", "arbitrary")),
        name="mla_prep",
    )(cq, ckv, krfg, *tabs, qn, kvn, wq, wk, wv, gq, gqs, gk, gks)


def _pad_lanes(a, n=HEAD_PAD):
    return jnp.pad(a, [(0, 0)] * (a.ndim - 1) + [(0, n - a.shape[-1])])


def _mla_weights(w_uq, w_ukv, gq, gk):
    wq = _pad_lanes(w_uq.reshape(MLA_Q_RANK, MLA_HEADS, MLA_QK).transpose(1, 0, 2))
    wq = jnp.concatenate([wq, _swap_rope_halves(wq)], axis=-1).astype(BF16)
    wkv = w_ukv.reshape(MLA_KV_RANK, MLA_HEADS, MLA_NOPE + MLA_V).transpose(1, 0, 2)
    wk = _pad_lanes(wkv[..., :MLA_NOPE]).astype(BF16)
    wv = _pad_lanes(wkv[..., MLA_NOPE:]).astype(BF16)
    gq, gk = _pad_lanes(gq[None, :]), _pad_lanes(gk[None, :])
    return wq, wk, wv, gq, _swap_rope_halves(gq), gk, _swap_rope_halves(gk)


GATE_MID_LANE = 8
GATE_LO_LANE = 16
GATE_ONE_LANE = LANE - 1
Q_GATE_LANE = FOX_HEAD_DIM
K_GATE_LANE = FOX_HEAD_DIM + 3


def _fox_prep_kernel(fq_ref, fk_ref, fv_ref, krfg_ref, bf_ref, gq_ref, gk_ref, pq_ref, pk_ref, pv_ref,
                     q_ref, k_ref, v_ref, carry_ref):
    tl = fq_ref.shape[1]
    lane = lax.broadcasted_iota(jnp.int32, (tl, LANE), 1)

    @pl.when(pl.program_id(1) == 0)
    def _():
        carry_ref[...] = jnp.zeros_like(carry_ref)

    logf = jax.nn.log_sigmoid(krfg_ref[0] + bf_ref[...])
    logf = jnp.where(lane < FOX_HEADS, logf, 0.0)
    r_i = lax.broadcasted_iota(jnp.int32, (tl, tl), 0)
    c_i = lax.broadcasted_iota(jnp.int32, (tl, tl), 1)
    tri = jnp.where(c_i <= r_i, 1.0, 0.0).astype(BF16)
    cum = carry_ref[0:1, :]
    for piece in _split3(logf):
        cum = cum + jnp.dot(tri, piece.astype(BF16), preferred_element_type=F32)
    carry_ref[0:1, :] = cum[tl - 1:tl, :]

    c_hi, c_mid, c_lo = _split3(cum * LOG2E)
    gate_row = (c_hi + pltpu.roll(c_mid, GATE_MID_LANE, 1) + pltpu.roll(c_lo, GATE_LO_LANE, 1)
                + jnp.where(lane == GATE_ONE_LANE, 1.0, 0.0)).astype(BF16)

    p_r = lax.broadcasted_iota(jnp.int32, (LANE, LANE), 0)
    p_c = lax.broadcasted_iota(jnp.int32, (LANE, LANE), 1)
    head_mean = jnp.where(p_r // FOX_HEAD_DIM == p_c // FOX_HEAD_DIM, 1.0 / FOX_HEAD_DIM, 0.0).astype(BF16)

    def mean_sq(ref, j):
        x = ref[0, :, j * LANE:(j + 1) * LANE]
        return jnp.dot((x * x).astype(BF16), head_mean, preferred_element_type=F32)

    def normed(ref, g_ref, j, ms):
        lanes = slice(j * LANE, (j + 1) * LANE)
        return (ref[0, :, lanes] * lax.rsqrt(ms + EPS) * g_ref[:, lanes]).astype(BF16)

    def placed(x, p_ref, j):
        return jnp.dot(jnp.concatenate([x, gate_row], axis=1), p_ref[j], preferred_element_type=F32)

    pairs = range(FOX_HEADS // 2)
    q_ms = [mean_sq(fq_ref, j) for j in pairs]
    k_ms = [mean_sq(fk_ref, j) for j in pairs]
    q_n = [normed(fq_ref, gq_ref, j, q_ms[j]) for j in pairs]
    k_n = [normed(fk_ref, gk_ref, j, k_ms[j]) for j in pairs]
    for j in pairs:
        q = placed(q_n[j], pq_ref, j)
        k = placed(k_n[j], pk_ref, j)
        v = placed(fv_ref[0, :, j * LANE:(j + 1) * LANE].astype(BF16), pv_ref, j)
        for hh in range(2):
            head = slice(hh * HEAD_PAD, (hh + 1) * HEAD_PAD)
            q_ref[0, 2 * j + hh] = q[:, head].astype(BF16)
            _store_key_blocks(k_ref, 2 * j + hh, k[:, head])
            v_ref[0, 2 * j + hh] = v[:, head].astype(BF16)


def _fox_prep_call(fq, fk, fv, krfg, bf, gq, gk, pq, pk, pv):
    tl = ROW_TILE
    row = lambda b, i: (b, i, 0)
    c2 = lambda b, i: (0, 0)
    c3 = lambda b, i: (0, 0, 0)
    head_out = pl.BlockSpec((1, FOX_HEADS, tl, HEAD_PAD), lambda b, i: (b, 0, i, 0))
    head_shape = jax.ShapeDtypeStruct((BATCH, FOX_HEADS, SEQ, HEAD_PAD), BF16)
    place = pl.BlockSpec((FOX_HEADS // 2, 2 * LANE, 2 * HEAD_PAD), c3)
    return pl.pallas_call(
        _fox_prep_kernel,
        grid=(BATCH, SEQ // tl),
        in_specs=[pl.BlockSpec((1, tl, ATT_WIDTH), row),
                  pl.BlockSpec((1, tl, ATT_WIDTH), row),
                  pl.BlockSpec((1, tl, ATT_WIDTH), row),
                  pl.BlockSpec((1, tl, LANE), row),
                  pl.BlockSpec((1, LANE), c2),
                  pl.BlockSpec((1, ATT_WIDTH), c2),
                  pl.BlockSpec((1, ATT_WIDTH), c2),
                  place, place, place],
        out_specs=[head_out, _KT_SPEC(FOX_HEADS, tl), head_out],
        out_shape=[head_shape, _KT_SHAPE(FOX_HEADS), head_shape],
        scratch_shapes=[pltpu.VMEM((SUBLANE, LANE), F32)],
        compiler_params=_params(("arbitrary", "arbitrary")),
        name="fox_prep",
    )(fq, fk, fv, krfg, bf, gq, gk, pq, pk, pv)


def _fox_placements():
    pq = np.zeros((FOX_HEADS // 2, 2 * LANE, 2 * HEAD_PAD), np.float32)
    pk = np.zeros_like(pq)
    pv = np.zeros_like(pq)
    one_row = LANE + GATE_ONE_LANE
    for j in range(FOX_HEADS // 2):
        for hh in range(2):
            h, col0 = 2 * j + hh, hh * HEAD_PAD
            for d in range(FOX_HEAD_DIM):
                for p in (pq, pk, pv):
                    p[j, hh * FOX_HEAD_DIM + d, col0 + d] = 1.0
            pv[j, one_row, col0 + ONES_LANE] = 1.0
            for n, piece_lane in enumerate((0, GATE_MID_LANE, GATE_LO_LANE)):
                pq[j, LANE + piece_lane + h, col0 + Q_GATE_LANE + n] = 1.0
                pq[j, one_row, col0 + K_GATE_LANE + n] = 1.0
                pk[j, one_row, col0 + Q_GATE_LANE + n] = 1.0
                pk[j, LANE + piece_lane + h, col0 + K_GATE_LANE + n] = -1.0
    return tuple(jnp.asarray(p, BF16) for p in (pq, pk, pv))


def _fox_operands(bf, gq, gk):
    q_scale = LOG2E / math.sqrt(FOX_HEAD_DIM)
    return (_pad_lanes(bf[None, :], LANE), jnp.tile(gq * q_scale, FOX_HEADS)[None, :],
            jnp.tile(gk, FOX_HEADS)[None, :]) + _fox_placements()


def _flash_kernel(qa_ref, qb_ref, kt_ref, v_ref, gap_ref, o_ref, q_scr, s_ref, m_ref, acc_ref,
                  *, tile, chunk, n_tiles):
    p = pl.program_id(2)
    n_first = p + 1
    n_tasks = n_tiles + 1
    per_chunk = tile // chunk
    chunk_gap = gap_ref[...]
    lane = lax.broadcasted_iota(jnp.int32, (tile, HEAD_PAD), 1)
    q_scr[0] = qa_ref[0]
    q_scr[1] = qb_ref[0]

    def task(t):
        second = t >= n_first
        which = second.astype(jnp.int32)
        key_block = jnp.where(second, t - n_first, t)
        query_tile = jnp.where(second, n_tiles - 1 - p, p)
        return second, which, key_block, query_tile

    def over_tasks(fn):
        for t in range(n_tasks):
            fn(jnp.int32(t))

    def scores(t):
        _, which, j, qt = task(t)
        allowed = chunk_gap <= (qt - j) * per_chunk
        for hh in range(2):
            s = jnp.dot(q_scr[which, hh], kt_ref[0, hh, j], preferred_element_type=F32)
            s = jnp.where(allowed, s, NEG)
            s_ref[hh, t] = s
            mr = m_ref[which, hh]
            for c in range(tile // LANE):
                mr = jnp.maximum(mr, s[:, c * LANE:(c + 1) * LANE])
            m_ref[which, hh] = mr

    m_ref[...] = jnp.full(m_ref.shape, NEG, F32)
    over_tasks(scores)
    ms = [[jnp.max(m_ref[w, hh], axis=1, keepdims=True) for hh in range(2)] for w in range(2)]

    def weighted(t):
        second, which, j, _ = task(t)
        k0 = pl.multiple_of(j * tile, tile)
        for hh in range(2):
            row_max = jnp.where(second, ms[1][hh], ms[0][hh])
            pr = jnp.exp2(s_ref[hh, t] - row_max).astype(BF16)
            acc_ref[which, hh] += jnp.dot(pr, v_ref[0, hh, pl.ds(k0, tile), :], preferred_element_type=F32)

    acc_ref[...] = jnp.zeros(acc_ref.shape, F32)
    over_tasks(weighted)
    for w in range(2):
        outs = [acc_ref[w, hh] / acc_ref[w, hh][:, ONES_LANE:ONES_LANE + 1] for hh in range(2)]
        o_ref[0, w, 0] = jnp.where(lane < 64, outs[0], pltpu.roll(outs[1], 64, 1)).astype(BF16)


def _flash_call(q, kt, v, chunk):
    tile = ATT_TILE
    heads = q.shape[1]
    n_tiles = SEQ // tile
    pos = np.arange(tile, dtype=np.int32) // chunk
    gap = jnp.asarray(pos[None, :] - pos[:, None])
    return pl.pallas_call(
        functools.partial(_flash_kernel, tile=tile, chunk=chunk, n_tiles=n_tiles),
        grid=(BATCH, heads // 2, n_tiles // 2),
        in_specs=[pl.BlockSpec((1, 2, tile, HEAD_PAD), lambda b, hp, p: (b, hp, p, 0)),
                  pl.BlockSpec((1, 2, tile, HEAD_PAD), lambda b, hp, p: (b, hp, n_tiles - 1 - p, 0)),
                  pl.BlockSpec((1, 2, n_tiles, HEAD_PAD, tile), lambda b, hp, p: (b, hp, 0, 0, 0)),
                  pl.BlockSpec((1, 2, SEQ, HEAD_PAD), lambda b, hp, p: (b, hp, 0, 0)),
                  pl.BlockSpec((tile, tile), lambda b, hp, p: (0, 0))],
        out_specs=pl.BlockSpec((1, 2, 1, tile, LANE), lambda b, hp, p: (b, 0, p, 0, hp)),
        out_shape=jax.ShapeDtypeStruct((BATCH, 2, n_tiles // 2, tile, ATT_WIDTH), BF16),
        scratch_shapes=[pltpu.VMEM((2, 2, tile, HEAD_PAD), BF16),
                        pltpu.VMEM((2, n_tiles + 1, tile, tile), F32),
                        pltpu.VMEM((2, 2, tile, LANE), F32),
                        pltpu.VMEM((2, 2, tile, HEAD_PAD), F32)],
        compiler_params=_params(("arbitrary", "arbitrary", "arbitrary")),
        name="flash_chunk%d" % chunk,
    )(q, q, kt, v, gap)


def _merge_kernel(ssm_ref, mla_ref, fox_ref, x_ref, g1_ref, gm_ref, gf_ref,
                  ws_ref, wm_ref, wf_ref, o_ref):
    def normed(ref, g_ref):
        a = ref[0, 0, 0].astype(F32)
        return (a * lax.rsqrt(jnp.mean(a * a, axis=-1, keepdims=True) + EPS) * g_ref[...]).astype(BF16)

    mix = jnp.dot(ssm_ref[0], ws_ref[...], preferred_element_type=F32)
    mix = mix + jnp.dot(normed(mla_ref, gm_ref), wm_ref[...], preferred_element_type=F32)
    mix = mix + jnp.dot(normed(fox_ref, gf_ref), wf_ref[...], preferred_element_type=F32)
    o_ref[0] = x_ref[0] + g1_ref[0] * mix


def _merge_call(o_ssm, o_mla, o_fox, x, g1, gm, gf, ws, wm, wf):
    tm = ATT_TILE
    half = SEQ // tm // 2
    row = lambda b, i: (b, i, 0)
    c2 = lambda b, i: (0, 0)
    att = pl.BlockSpec((1, 1, 1, tm, ATT_WIDTH),
                       lambda b, i: (b, i // half, jnp.where(i < half, i, 2 * half - 1 - i), 0, 0))
    return pl.pallas_call(
        _merge_kernel,
        grid=(BATCH, SEQ // tm),
        in_specs=[pl.BlockSpec((1, tm, SSM_WIDTH), row),
                  att, att,
                  pl.BlockSpec((1, tm, D_MODEL), row),
                  pl.BlockSpec((1, 1, D_MODEL), lambda b, i: (b, 0, 0)),
                  pl.BlockSpec((1, ATT_WIDTH), c2),
                  pl.BlockSpec((1, ATT_WIDTH), c2),
                  pl.BlockSpec((SSM_WIDTH, D_MODEL), c2),
                  pl.BlockSpec((ATT_WIDTH, D_MODEL), c2),
                  pl.BlockSpec((ATT_WIDTH, D_MODEL), c2)],
        out_specs=pl.BlockSpec((1, tm, D_MODEL), row),
        out_shape=jax.ShapeDtypeStruct((BATCH, SEQ, D_MODEL), F32),
        compiler_params=_params(("arbitrary", "arbitrary")),
        name="merge",
    )(o_ssm, o_mla, o_fox, x, g1, gm, gf, ws, wm, wf)


def _ffn_kernel(x_ref, g_ref, sh_ref, sc_ref, g2_ref, wg_ref, wu_ref, wd_ref, o_ref, h_ref, acc_ref):
    c = pl.program_id(2)

    @pl.when(c == 0)
    def _():
        h_ref[...] = _rms_mod(x_ref[0], g_ref[...], sc_ref[0], sh_ref[0]).astype(BF16)
        acc_ref[...] = jnp.zeros_like(acc_ref)

    h = h_ref[...]
    gate = jnp.dot(h, wg_ref[0], preferred_element_type=F32)
    up = jnp.dot(h, wu_ref[0], preferred_element_type=F32)
    a = (gate * jax.nn.sigmoid(gate) * up).astype(BF16)
    acc_ref[...] += jnp.dot(a, wd_ref[0], preferred_element_type=F32)

    @pl.when(c == pl.num_programs(2) - 1)
    def _():
        o_ref[0] = x_ref[0] + g2_ref[0] * acc_ref[...]


def _ffn_call(x, g, sh, sc, g2, wg, wu, wd):
    tm = MOE_TILE
    n_chunks = wg.shape[0]
    row = lambda b, i, c: (b, i, 0)
    per_b = lambda b, i, c: (b, 0, 0)
    chunk = lambda b, i, c: (c, 0, 0)
    return pl.pallas_call(
        _ffn_kernel,
        grid=(BATCH, SEQ // tm, n_chunks),
        in_specs=[pl.BlockSpec((1, tm, D_MODEL), row),
                  pl.BlockSpec((1, D_MODEL), lambda b, i, c: (0, 0)),
                  pl.BlockSpec((1, 1, D_MODEL), per_b),
                  pl.BlockSpec((1, 1, D_MODEL), per_b),
                  pl.BlockSpec((1, 1, D_MODEL), per_b),
                  pl.BlockSpec((1, D_MODEL, FF_CHUNK), chunk),
                  pl.BlockSpec((1, D_MODEL, FF_CHUNK), chunk),
                  pl.BlockSpec((1, FF_CHUNK, D_MODEL), chunk)],
        out_specs=pl.BlockSpec((1, tm, D_MODEL), row),
        out_shape=jax.ShapeDtypeStruct((BATCH, SEQ, D_MODEL), F32),
        scratch_shapes=[pltpu.VMEM((tm, D_MODEL), BF16), pltpu.VMEM((tm, D_MODEL), F32)],
        compiler_params=_params(("arbitrary", "arbitrary", "arbitrary")),
        name="ffn_dense",
    )(x, g, sh, sc, g2, wg, wu, wd)


def _router_kernel(x_ref, g_ref, sh_ref, sc_ref, w_ref, b_ref, comb_ref, rank_ref, rankt_ref, count_ref):
    tm = x_ref.shape[1]
    h = _rms_mod(x_ref[0], g_ref[...], sc_ref[0], sh_ref[0])
    h_hi = h.astype(BF16)
    h_lo = (h - h_hi.astype(F32)).astype(BF16)
    w_hi, w_lo = w_ref[0], w_ref[1]
    logits = (jnp.dot(h_hi, w_hi, preferred_element_type=F32)
              + jnp.dot(h_lo, w_hi, preferred_element_type=F32)
              + jnp.dot(h_hi, w_lo, preferred_element_type=F32)) + b_ref[...]
    lane = lax.broadcasted_iota(jnp.int32, logits.shape, 1)
    logits = jnp.where(lane < N_EXPERTS, logits, -jnp.inf)
    m1 = jnp.max(logits, axis=-1, keepdims=True)
    i1 = jnp.min(jnp.where(logits == m1, lane, LANE), axis=-1, keepdims=True)
    rest = jnp.where(lane == i1, -jnp.inf, logits)
    m2 = jnp.max(rest, axis=-1, keepdims=True)
    i2 = jnp.min(jnp.where(rest == m2, lane, LANE), axis=-1, keepdims=True)
    e = jnp.exp(m2 - m1)
    p1 = 1.0 / (1.0 + e)
    comb_ref[0] = jnp.where(lane == i1, p1, 0.0) + jnp.where(lane == i2, e * p1, 0.0)

    chosen = (lane == i1) | (lane == i2)
    chosen_f = jnp.where(chosen, 1.0, 0.0)
    r_i = lax.broadcasted_iota(jnp.int32, (tm, tm), 0)
    c_i = lax.broadcasted_iota(jnp.int32, (tm, tm), 1)
    earlier = jnp.where(c_i < r_i, 1.0, 0.0).astype(BF16)
    rank = jnp.dot(earlier, chosen_f.astype(BF16), preferred_element_type=F32)
    rank = jnp.where(chosen, rank, -1.0)
    rank_ref[0] = rank
    rankt_ref[0] = rank.T[0:SUBLANE, :]
    count_ref[0] = jnp.sum(chosen_f, axis=0, keepdims=True)


def _router_call(x, g, sh, sc, w, b):
    tm = MOE_TILE
    tiles = SEQ // tm
    row = lambda b_, i: (b_, i, 0)
    per_b = lambda b_, i: (b_, 0, 0)
    per_tile = lambda b_, i: (b_ * tiles + i, 0, 0)
    return pl.pallas_call(
        _router_kernel,
        grid=(BATCH, tiles),
        in_specs=[pl.BlockSpec((1, tm, D_MODEL), row),
                  pl.BlockSpec((1, D_MODEL), lambda b_, i: (0, 0)),
                  pl.BlockSpec((1, 1, D_MODEL), per_b),
                  pl.BlockSpec((1, 1, D_MODEL), per_b),
                  pl.BlockSpec((2, D_MODEL, LANE), lambda b_, i: (0, 0, 0)),
                  pl.BlockSpec((1, LANE), lambda b_, i: (0, 0))],
        out_specs=[pl.BlockSpec((1, tm, LANE), row),
                   pl.BlockSpec((1, tm, LANE), row),
                   pl.BlockSpec((1, SUBLANE, tm), per_tile),
                   pl.BlockSpec((1, 1, LANE), per_tile)],
        out_shape=[jax.ShapeDtypeStruct((BATCH, SEQ, LANE), F32),
                   jax.ShapeDtypeStruct((BATCH, SEQ, LANE), F32),
                   jax.ShapeDtypeStruct((BATCH * tiles, SUBLANE, tm), F32),
                   jax.ShapeDtypeStruct((BATCH * tiles, 1, LANE), F32)],
        compiler_params=_params(("arbitrary", "arbitrary")),
        name="router",
    )(x, g, sh, sc, w, b)


def _moe_kernel(count_ref, x_ref, g_ref, sh_ref, sc_ref, g2_ref, comb_ref, rank_ref, rankt_ref,
                wg_ref, wu_ref, wd_ref, o_ref, h_ref):
    tm = x_ref.shape[1]
    e = pl.program_id(1)

    @pl.when(e == 0)
    def _():
        x = x_ref[0]
        h_ref[...] = _rms_mod(x, g_ref[...], sc_ref[0], sh_ref[0]).astype(BF16)
        o_ref[0] = x

    lane = lax.broadcasted_iota(jnp.int32, (tm, LANE), 1)
    mine = lane == e
    rank_col = jnp.sum(jnp.where(mine, rank_ref[0], 0.0), axis=-1, keepdims=True)
    gate_col = jnp.sum(jnp.where(mine, comb_ref[0], 0.0), axis=-1, keepdims=True)
    rank_row = rankt_ref[0, pl.ds(e, 1), :]
    count = count_ref[pl.program_id(0) * N_EXPERTS + e]

    def expert_pass(first, n_rows):
        base = first.astype(F32)
        slot_sub = lax.broadcasted_iota(jnp.int32, (n_rows, tm), 0).astype(F32)
        slot_lane = lax.broadcasted_iota(jnp.int32, (tm, n_rows), 1).astype(F32)
        pick = jnp.where(rank_row - base == slot_sub, 1.0, 0.0).astype(BF16)
        rows = jnp.dot(pick, h_ref[...], preferred_element_type=F32).astype(BF16)
        gate = jnp.dot(rows, wg_ref[0], preferred_element_type=F32)
        up = jnp.dot(rows, wu_ref[0], preferred_element_type=F32)
        a = (gate * jax.nn.sigmoid(gate) * up).astype(BF16)
        y = jnp.dot(a, wd_ref[0], preferred_element_type=F32).astype(BF16)
        place = jnp.where(rank_col - base == slot_lane, 1.0, 0.0).astype(BF16)
        back = jnp.dot(place, y, preferred_element_type=F32)
        o_ref[0] += g2_ref[0] * (gate_col * back)

    def full_pass(sb, carry):
        expert_pass(sb * MOE_ROWS, MOE_ROWS)
        return carry

    n_full = count // MOE_ROWS
    lax.fori_loop(0, n_full, full_pass, 0)
    left = count - n_full * MOE_ROWS

    @pl.when(left > MOE_ROWS // 2)
    def _():
        expert_pass(n_full * MOE_ROWS, MOE_ROWS)

    @pl.when((left > 0) & (left <= MOE_ROWS // 2))
    def _():
        expert_pass(n_full * MOE_ROWS, MOE_ROWS // 2)


def _moe_call(x, g, sh, sc, g2, comb, rank, rankt, counts, wg, wu, wd):
    tm = MOE_TILE
    tiles = SEQ // tm
    n_tiles = BATCH * tiles
    row = lambda i, e, cnt: (i, 0, 0)
    per_b = lambda i, e, cnt: (i // tiles, 0, 0)
    expert = lambda i, e, cnt: (e, 0, 0)
    as_tiles = lambda a: a.reshape(n_tiles, tm, a.shape[-1])
    grid_spec = pltpu.PrefetchScalarGridSpec(
        num_scalar_prefetch=1,
        grid=(n_tiles, N_EXPERTS),
        in_specs=[pl.BlockSpec((1, tm, D_MODEL), row),
                  pl.BlockSpec((1, D_MODEL), lambda i, e, cnt: (0, 0)),
                  pl.BlockSpec((1, 1, D_MODEL), per_b),
                  pl.BlockSpec((1, 1, D_MODEL), per_b),
                  pl.BlockSpec((1, 1, D_MODEL), per_b),
                  pl.BlockSpec((1, tm, LANE), row),
                  pl.BlockSpec((1, tm, LANE), row),
                  pl.BlockSpec((1, SUBLANE, tm), row),
                  pl.BlockSpec((1, D_MODEL, D_FF_EXPERT), expert),
                  pl.BlockSpec((1, D_MODEL, D_FF_EXPERT), expert),
                  pl.BlockSpec((1, D_FF_EXPERT, D_MODEL), expert)],
        out_specs=pl.BlockSpec((1, tm, D_MODEL), row),
        scratch_shapes=[pltpu.VMEM((tm, D_MODEL), BF16)],
    )
    out = pl.pallas_call(
        _moe_kernel,
        grid_spec=grid_spec,
        out_shape=jax.ShapeDtypeStruct((n_tiles, tm, D_MODEL), F32),
        compiler_params=_params(("arbitrary", "arbitrary")),
        name="moe_experts",
    )(counts, as_tiles(x), g, sh, sc, g2, as_tiles(comb), as_tiles(rank), rankt, wg, wu, wd)
    return out.reshape(BATCH, SEQ, D_MODEL)


def kernel(x, c, positions, norm_mix, norm_ffn, w_ada, b_ada, w_in, ssm_lam_re, ssm_lam_im, ssm_log_dt, ssm_b_re, ssm_b_im, ssm_c_re, ssm_c_im, ssm_d, ssm_w_glu, ssm_b_glu, mla_q_norm, mla_kv_norm, mla_w_uq, mla_w_ukv, mla_qk_gq, mla_qk_gk, fox_b_f, fox_qk_gq, fox_qk_gk, out_norm, w_out, ffn_w_gate, ffn_w_up, ffn_w_down, moe_w_router, moe_b_router, moe_w_gate, moe_w_up, moe_w_down):
    tabs = _rope_tables(positions)
    ada = _ada_call(c, w_ada, b_ada)
    ada = ada.reshape(DEPTH, BATCH, 6, 1, D_MODEL)
    row2 = lambda a: a[None, :]

    for i in range(DEPTH):
        sh1, sc1, g1, sh2, sc2, g2 = (ada[i, :, n] for n in range(6))

        u, cq, ckv, krfg, fq, fk, fv = _inproj_call(x, row2(norm_mix[i]), sh1, sc1, _pack_w_in(w_in[i]))

        bmat, lam, cmat = _s5_operands(ssm_lam_re[i], ssm_lam_im[i], ssm_log_dt[i],
                                       ssm_b_re[i], ssm_b_im[i], ssm_c_re[i], ssm_c_im[i])
        u_t = u.transpose(1, 0, 2).reshape(SEQ * BATCH, SSM_WIDTH)
        o_ssm = _s5_call(u_t, bmat, lam, cmat, row2(ssm_d[i]), ssm_w_glu[i].astype(BF16),
                         row2(ssm_b_glu[i]), row2(out_norm[i, :SSM_WIDTH]))
        o_ssm = o_ssm.reshape(SEQ, BATCH, SSM_WIDTH).transpose(1, 0, 2)

        mq, mk, mv = _mla_prep_call(cq, ckv, krfg, tabs, row2(mla_q_norm[i]), row2(mla_kv_norm[i]),
                                    *_mla_weights(mla_w_uq[i], mla_w_ukv[i], mla_qk_gq[i], mla_qk_gk[i]))
        o_mla = _flash_call(mq, mk, mv, CHUNK)

        xq, xk, xv = _fox_prep_call(fq, fk, fv, krfg, *_fox_operands(fox_b_f[i], fox_qk_gq[i], fox_qk_gk[i]))
        o_fox = _flash_call(xq, xk, xv, 1)

        e1, e2 = SSM_WIDTH, SSM_WIDTH + ATT_WIDTH
        wo = w_out[i].astype(BF16)
        x = _merge_call(o_ssm, o_mla, o_fox, x, g1, row2(out_norm[i, e1:e2]), row2(out_norm[i, e2:]),
                        wo[:e1], wo[e1:e2], wo[e2:])

        j = i // 2
        if i % 2 == 0:
            split = lambda w: w.reshape(D_MODEL, D_FF // FF_CHUNK, FF_CHUNK).transpose(1, 0, 2).astype(BF16)
            wd = ffn_w_down[j].reshape(D_FF // FF_CHUNK, FF_CHUNK, D_MODEL).astype(BF16)
            x = _ffn_call(x, row2(norm_ffn[i]), sh2, sc2, g2, split(ffn_w_gate[j]), split(ffn_w_up[j]), wd)
        else:
            wr = _pad_lanes(moe_w_router[j], LANE)
            wr_hi = wr.astype(BF16)
            wr_lo = (wr - wr_hi.astype(F32)).astype(BF16)
            comb, rank, rankt, counts = _router_call(x, row2(norm_ffn[i]), sh2, sc2, jnp.stack([wr_hi, wr_lo]),
                                                     _pad_lanes(row2(moe_b_router[j]), LANE))
            counts = counts[:, 0, :N_EXPERTS].astype(jnp.int32).reshape(-1)
            x = _moe_call(x, row2(norm_ffn[i]), sh2, sc2, g2, comb, rank, rankt, counts,
                          moe_w_gate[j].astype(BF16), moe_w_up[j].astype(BF16), moe_w_down[j].astype(BF16))
    return x
```

```python
import functools
import math

import jax
import jax.numpy as jnp
import numpy as np
from jax import lax
from jax.experimental import pallas as pl
from jax.experimental.pallas import tpu as pltpu

F32 = jnp.float32
BF16 = jnp.bfloat16

D_MODEL = 1024
BATCH = 8
SEQ = 4096
DEPTH = 4
CHUNK = 64
EPS = 1e-6

SSM_WIDTH = 256
SSM_GROUP = 16
N_SSM_GROUPS = 16
SSM_STATE = 64
N_STATE = N_SSM_GROUPS * SSM_STATE

MLA_HEADS = 6
MLA_Q_RANK = 256
MLA_KV_RANK = 128
MLA_NOPE = 64
MLA_ROPE = 32
MLA_V = 64
MLA_QK = 96
ROPE_BASE = 10000.0

FOX_HEADS = 6
FOX_HEAD_DIM = 64
ATT_WIDTH = 384

D_FF = 2816
N_EXPERTS = 8
D_FF_EXPERT = 1408

LANE = 128
SUBLANE = 8
HEAD_PAD = LANE
ONES_LANE = 64
NEG = -1e30

IN_PAD = 1920
KR_LANE = 64

ROW_TILE = 512
S5_STEPS = 64
ATT_TILE = 512
LOG2E = math.log2(math.e)
FF_CHUNK = 1408
MOE_TILE = 1024
MOE_ROWS = 256
VMEM_LIMIT = 56 * 1024 * 1024


def _params(sem):
    return pltpu.CompilerParams(dimension_semantics=sem, vmem_limit_bytes=VMEM_LIMIT)


def _rms_mod(x, g, sc, sh):
    ms = jnp.mean(x * x, axis=-1, keepdims=True)
    h = x * lax.rsqrt(ms + EPS) * g
    return h * (1.0 + sc) + sh


def _split3(x):
    hi = x.astype(BF16).astype(F32)
    r = x - hi
    mid = r.astype(BF16).astype(F32)
    lo = (r - mid).astype(BF16).astype(F32)
    return hi, mid, lo


def _ada_kernel(c_ref, w_ref, b_ref, o_ref):
    c = c_ref[...]
    ca = (c * jax.nn.sigmoid(c)).astype(BF16)
    o_ref[0] = jnp.dot(ca, w_ref[0].astype(BF16), preferred_element_type=F32) + b_ref[0]


def _ada_call(c, w_ada, b_ada):
    tn = 1536
    return pl.pallas_call(
        _ada_kernel,
        grid=(DEPTH, 6 * D_MODEL // tn),
        in_specs=[pl.BlockSpec((BATCH, D_MODEL), lambda i, j: (0, 0)),
                  pl.BlockSpec((1, D_MODEL, tn), lambda i, j: (i, 0, j)),
                  pl.BlockSpec((1, 1, tn), lambda i, j: (i, 0, j))],
        out_specs=pl.BlockSpec((1, BATCH, tn), lambda i, j: (i, 0, j)),
        out_shape=jax.ShapeDtypeStruct((DEPTH, BATCH, 6 * D_MODEL), F32),
        compiler_params=_params(("arbitrary", "arbitrary")),
        name="ada",
    )(c, w_ada, b_ada.reshape(DEPTH, 1, 6 * D_MODEL))


_IN_GROUPS = ((0, 256), (256, 512), (512, 640), (640, 768), (768, 1152), (1152, 1536), (1536, 1920))


def _inproj_kernel(x_ref, g_ref, sh_ref, sc_ref, w_ref, *out_refs):
    parts = 4
    step = x_ref.shape[1] // parts
    rows = [slice(r * step, (r + 1) * step) for r in range(parts)]
    normed = lambda r: _rms_mod(x_ref[0, rows[r]], g_ref[...], sc_ref[0], sh_ref[0]).astype(BF16)
    h_next = normed(0)
    for r in range(parts):
        h = h_next
        if r + 1 < parts:
            h_next = normed(r + 1)
        proj = jnp.dot(h, w_ref[...], preferred_element_type=F32)
        for ref, (c0, c1) in zip(out_refs, _IN_GROUPS):
            ref[0, rows[r]] = proj[:, c0:c1]


def _inproj_call(x, g, sh, sc, w):
    tm = ROW_TILE
    row = lambda b, i: (b, i, 0)
    per_b = lambda b, i: (b, 0, 0)
    const = lambda b, i: (0, 0)
    widths = [c1 - c0 for c0, c1 in _IN_GROUPS]
    return pl.pallas_call(
        _inproj_kernel,
        grid=(BATCH, SEQ // tm),
        in_specs=[pl.BlockSpec((1, tm, D_MODEL), row),
                  pl.BlockSpec((1, D_MODEL), const),
                  pl.BlockSpec((1, 1, D_MODEL), per_b),
                  pl.BlockSpec((1, 1, D_MODEL), per_b),
                  pl.BlockSpec((D_MODEL, IN_PAD), const)],
        out_specs=[pl.BlockSpec((1, tm, wd), row) for wd in widths],
        out_shape=[jax.ShapeDtypeStruct((BATCH, SEQ, wd), F32) for wd in widths],
        compiler_params=_params(("arbitrary", "arbitrary")),
        name="inproj",
    )(x, g, sh, sc, w)


def _pack_w_in(w):
    u, cq, ckv, kr, fq, fk, fv, fg = jnp.split(
        w, (256, 512, 640, 672, 1056, 1440, 1824), axis=1)
    z = lambda n: jnp.zeros((D_MODEL, n), w.dtype)
    krfg = jnp.concatenate([fg, z(KR_LANE - FOX_HEADS), kr, z(LANE - KR_LANE - MLA_ROPE)], axis=1)
    return jnp.concatenate([u, cq, ckv, krfg, fq, fk, fv], axis=1).astype(BF16)


def _s5_kernel(u_ref, bmat_ref, lam_ref, cmat_ref, d_ref, wglu_ref, bglu_ref, gn_ref,
               o_ref, bu0_ref, bu1_ref, state_ref, *, steps):
    rows = steps * BATCH

    @pl.when(pl.program_id(0) == 0)
    def _():
        state_ref[...] = jnp.zeros_like(state_ref)

    halves = ((bu0_ref, slice(0, rows)), (bu1_ref, slice(rows, 2 * rows)))
    for bu_ref, rs in halves:
        bu_ref[...] = jnp.dot(u_ref[rs, :].astype(BF16), bmat_ref[...], preferred_element_type=F32)
    lr = jnp.broadcast_to(lam_ref[0:1, :], (SUBLANE, N_STATE))
    li = jnp.broadcast_to(lam_ref[1:2, :], (SUBLANE, N_STATE))
    sr, si = state_ref[:, 0:N_STATE], state_ref[:, N_STATE:2 * N_STATE]

    for bu_ref, rs in halves:
        for t in range(steps):
            r = slice(t * SUBLANE, (t + 1) * SUBLANE)
            nr = lr * sr - li * si + bu_ref[r, 0:N_STATE]
            ni = lr * si + li * sr + bu_ref[r, N_STATE:2 * N_STATE]
            bu_ref[r, 0:N_STATE] = nr
            bu_ref[r, N_STATE:2 * N_STATE] = ni
            sr, si = nr, ni
        y = jnp.dot(bu_ref[...].astype(BF16), cmat_ref[...], preferred_element_type=F32)
        y = jax.nn.gelu(y + d_ref[...] * u_ref[rs, :])
        gate = jnp.dot(y.astype(BF16), wglu_ref[...], preferred_element_type=F32) + bglu_ref[...]
        o = y * jax.nn.sigmoid(gate)
        ms = jnp.mean(o * o, axis=-1, keepdims=True)
        o_ref[rs, :] = (o * lax.rsqrt(ms + EPS) * gn_ref[...]).astype(BF16)

    state_ref[:, 0:N_STATE] = sr
    state_ref[:, N_STATE:2 * N_STATE] = si


def _s5_call(u_t, bmat, lam, cmat, d_skip, wglu, bglu, gn):
    rows = S5_STEPS * BATCH
    const = lambda i: (0, 0)
    return pl.pallas_call(
        functools.partial(_s5_kernel, steps=S5_STEPS),
        grid=(SEQ // (2 * S5_STEPS),),
        in_specs=[pl.BlockSpec((2 * rows, SSM_WIDTH), lambda i: (i, 0)),
                  pl.BlockSpec((SSM_WIDTH, 2 * N_STATE), const),
                  pl.BlockSpec((2, N_STATE), const),
                  pl.BlockSpec((2 * N_STATE, SSM_WIDTH), const),
                  pl.BlockSpec((1, SSM_WIDTH), const),
                  pl.BlockSpec((SSM_WIDTH, SSM_WIDTH), const),
                  pl.BlockSpec((1, SSM_WIDTH), const),
                  pl.BlockSpec((1, SSM_WIDTH), const)],
        out_specs=pl.BlockSpec((2 * rows, SSM_WIDTH), lambda i: (i, 0)),
        out_shape=jax.ShapeDtypeStruct((SEQ * BATCH, SSM_WIDTH), BF16),
        scratch_shapes=[pltpu.VMEM((rows, 2 * N_STATE), F32),
                        pltpu.VMEM((rows, 2 * N_STATE), F32),
                        pltpu.VMEM((SUBLANE, 2 * N_STATE), F32)],
        compiler_params=_params(("arbitrary",)),
        name="s5",
    )(u_t, bmat, lam, cmat, d_skip, wglu, bglu, gn)


def _s5_operands(lam_re, lam_im, log_dt, b_re, b_im, c_re, c_im):
    dt = jnp.exp(log_dt)[:, None]
    mag = jnp.exp(lam_re * dt)
    lb_re = mag * jnp.cos(lam_im * dt)
    lb_im = mag * jnp.sin(lam_im * dt)
    den = lam_re * lam_re + lam_im * lam_im
    co_re = ((lb_re - 1.0) * lam_re + lb_im * lam_im) / den
    co_im = (lb_im * lam_re - (lb_re - 1.0) * lam_im) / den
    bb_re = co_re[..., None] * b_re - co_im[..., None] * b_im
    bb_im = co_re[..., None] * b_im + co_im[..., None] * b_re
    eye = jnp.eye(N_SSM_GROUPS, dtype=F32)
    blk_b = lambda m: jnp.einsum("gpc,gh->gchp", m, eye).reshape(SSM_WIDTH, N_STATE)
    bmat = jnp.concatenate([blk_b(bb_re), blk_b(bb_im)], axis=1).astype(BF16)
    blk_c = lambda m: jnp.einsum("gcp,gh->gphc", m, eye).reshape(N_STATE, SSM_WIDTH)
    cmat = jnp.concatenate([blk_c(c_re), -blk_c(c_im)], axis=0).astype(BF16)
    lam = jnp.stack([lb_re.reshape(N_STATE), lb_im.reshape(N_STATE)], axis=0)
    return bmat, lam, cmat


def _rope_tables(positions):
    half = MLA_ROPE // 2
    inv = ROPE_BASE ** (-jnp.arange(half, dtype=F32) / half)
    ang = positions.astype(F32)[..., None] * inv
    cos, sin = jnp.cos(ang), jnp.sin(ang)
    shp = positions.shape
    one = lambda n: jnp.ones(shp + (n,), F32)
    zero = lambda n: jnp.zeros(shp + (n,), F32)
    cos_t = jnp.concatenate([one(MLA_NOPE), cos, cos, zero(LANE - MLA_QK)], axis=-1)
    sin_t = jnp.concatenate([zero(MLA_NOPE), -sin, sin, zero(LANE - MLA_QK)], axis=-1)
    return cos_t, sin_t


def _swap_rope_halves(a):
    half = MLA_ROPE // 2
    lo, hi = a[..., MLA_NOPE:MLA_NOPE + half], a[..., MLA_NOPE + half:MLA_QK]
    return jnp.concatenate([jnp.zeros_like(a[..., :MLA_NOPE]), hi, lo, jnp.zeros_like(a[..., MLA_QK:])], axis=-1)


def _store_key_blocks(kt_ref, h, k):
    kt = k.T
    for s in range(k.shape[0] // ATT_TILE):
        kt_ref[0, h, s] = kt[:, s * ATT_TILE:(s + 1) * ATT_TILE].astype(BF16)


_KT_SPEC = lambda heads, tl: pl.BlockSpec((1, heads, tl // ATT_TILE, HEAD_PAD, ATT_TILE),
                                          lambda b, i: (b, 0, i, 0, 0))
_KT_SHAPE = lambda heads: jax.ShapeDtypeStruct((BATCH, heads, SEQ // ATT_TILE, HEAD_PAD, ATT_TILE), BF16)


def _mla_prep_kernel(cq_ref, ckv_ref, krfg_ref, cos_ref, sin_ref, qn_ref, kvn_ref, wq_ref, wk_ref, wv_ref,
                     gq_ref, gqs_ref, gk_ref, gks_ref, q_ref, k_ref, v_ref):
    tl = cq_ref.shape[1]
    lane = lax.broadcasted_iota(jnp.int32, (tl, LANE), 1)
    cos, sin = cos_ref[0], sin_ref[0]
    q_scale = LOG2E / math.sqrt(MLA_QK)
    q_cos, q_sin = gq_ref[...] * cos * q_scale, gqs_ref[...] * sin * q_scale
    k_cos, k_sin = gk_ref[...] * cos, gks_ref[...] * sin
    ones = jnp.ones((LANE, LANE), BF16)

    def inv_rms(x):
        ss = jnp.dot((x * x).astype(BF16), ones, preferred_element_type=F32)
        return lax.rsqrt(ss / MLA_QK + EPS)

    cq = cq_ref[0]
    cqn = (cq * lax.rsqrt(jnp.mean(cq * cq, axis=-1, keepdims=True) + EPS) * qn_ref[...]).astype(BF16)
    ckv = ckv_ref[0]
    ckvn = (ckv * lax.rsqrt(jnp.mean(ckv * ckv, axis=-1, keepdims=True) + EPS) * kvn_ref[...]).astype(BF16)
    kr = jnp.where((lane >= KR_LANE) & (lane < KR_LANE + MLA_ROPE), krfg_ref[0], 0.0)
    kr_swapped = jnp.where(lane < KR_LANE + MLA_ROPE // 2, pltpu.roll(kr, LANE - 16, 1), pltpu.roll(kr, 16, 1))
    k_rotary = kr_swapped * k_sin

    heads = range(MLA_HEADS)
    qqs = [jnp.dot(cqn, wq_ref[h], preferred_element_type=F32) for h in heads]
    ks = [jnp.dot(ckvn, wk_ref[h], preferred_element_type=F32) + kr for h in heads]
    q_inv = [inv_rms(qq[:, :LANE]) for qq in qqs]
    k_inv = [inv_rms(k) for k in ks]
    for h in heads:
        q, q_swapped = qqs[h][:, :LANE], qqs[h][:, LANE:]
        q_ref[0, h] = (q_inv[h] * (q * q_cos + q_swapped * q_sin)).astype(BF16)
        _store_key_blocks(k_ref, h, k_inv[h] * (ks[h] * k_cos + k_rotary))
        v = jnp.dot(ckvn, wv_ref[h], preferred_element_type=F32)
        v_ref[0, h] = jnp.where(lane == ONES_LANE, 1.0, v).astype(BF16)


def _mla_prep_call(cq, ckv, krfg, tabs, qn, kvn, wq, wk, wv, gq, gqs, gk, gks):
    tl = ROW_TILE
    row = lambda b, i: (b, i, 0)
    c2 = lambda b, i: (0, 0)
    c3 = lambda b, i: (0, 0, 0)
    head_out = pl.BlockSpec((1, MLA_HEADS, tl, HEAD_PAD), lambda b, i: (b, 0, i, 0))
    head_shape = jax.ShapeDtypeStruct((BATCH, MLA_HEADS, SEQ, HEAD_PAD), BF16)
    gain = pl.BlockSpec((1, HEAD_PAD), c2)
    return pl.pallas_call(
        _mla_prep_kernel,
        grid=(BATCH, SEQ // tl),
        in_specs=[pl.BlockSpec((1, tl, MLA_Q_RANK), row),
                  pl.BlockSpec((1, tl, MLA_KV_RANK), row),
                  pl.BlockSpec((1, tl, LANE), row),
                  pl.BlockSpec((1, tl, LANE), row),
                  pl.BlockSpec((1, tl, LANE), row),
                  pl.BlockSpec((1, MLA_Q_RANK), c2),
                  pl.BlockSpec((1, MLA_KV_RANK), c2),
                  pl.BlockSpec((MLA_HEADS, MLA_Q_RANK, 2 * HEAD_PAD), c3),
                  pl.BlockSpec((MLA_HEADS, MLA_KV_RANK, HEAD_PAD), c3),
                  pl.BlockSpec((MLA_HEADS, MLA_KV_RANK, HEAD_PAD), c3),
                  gain, gain, gain, gain],
        out_specs=[head_out, _KT_SPEC(MLA_HEADS, tl), head_out],
        out_shape=[head_shape, _KT_SHAPE(MLA_HEADS), head_shape],
        compiler_params=_params(("arbitrary", "arbitrary")),
        name="mla_prep",
    )(cq, ckv, krfg, *tabs, qn, kvn, wq, wk, wv, gq, gqs, gk, gks)


def _pad_lanes(a, n=HEAD_PAD):
    return jnp.pad(a, [(0, 0)] * (a.ndim - 1) + [(0, n - a.shape[-1])])


def _mla_weights(w_uq, w_ukv, gq, gk):
    wq = _pad_lanes(w_uq.reshape(MLA_Q_RANK, MLA_HEADS, MLA_QK).transpose(1, 0, 2))
    wq = jnp.concatenate([wq, _swap_rope_halves(wq)], axis=-1).astype(BF16)
    wkv = w_ukv.reshape(MLA_KV_RANK, MLA_HEADS, MLA_NOPE + MLA_V).transpose(1, 0, 2)
    wk = _pad_lanes(wkv[..., :MLA_NOPE]).astype(BF16)
    wv = _pad_lanes(wkv[..., MLA_NOPE:]).astype(BF16)
    gq, gk = _pad_lanes(gq[None, :]), _pad_lanes(gk[None, :])
    return wq, wk, wv, gq, _swap_rope_halves(gq), gk, _swap_rope_halves(gk)


GATE_MID_LANE = 8
GATE_LO_LANE = 16
GATE_ONE_LANE = LANE - 1
Q_GATE_LANE = FOX_HEAD_DIM
K_GATE_LANE = FOX_HEAD_DIM + 3


def _fox_prep_kernel(fq_ref, fk_ref, fv_ref, krfg_ref, bf_ref, gq_ref, gk_ref, pq_ref, pk_ref, pv_ref,
                     q_ref, k_ref, v_ref, carry_ref):
    tl = fq_ref.shape[1]
    lane = lax.broadcasted_iota(jnp.int32, (tl, LANE), 1)

    @pl.when(pl.program_id(1) == 0)
    def _():
        carry_ref[...] = jnp.zeros_like(carry_ref)

    logf = jax.nn.log_sigmoid(krfg_ref[0] + bf_ref[...])
    logf = jnp.where(lane < FOX_HEADS, logf, 0.0)
    r_i = lax.broadcasted_iota(jnp.int32, (tl, tl), 0)
    c_i = lax.broadcasted_iota(jnp.int32, (tl, tl), 1)
    tri = jnp.where(c_i <= r_i, 1.0, 0.0).astype(BF16)
    cum = carry_ref[0:1, :]
    for piece in _split3(logf):
        cum = cum + jnp.dot(tri, piece.astype(BF16), preferred_element_type=F32)
    carry_ref[0:1, :] = cum[tl - 1:tl, :]

    c_hi, c_mid, c_lo = _split3(cum * LOG2E)
    gate_row = (c_hi + pltpu.roll(c_mid, GATE_MID_LANE, 1) + pltpu.roll(c_lo, GATE_LO_LANE, 1)
                + jnp.where(lane == GATE_ONE_LANE, 1.0, 0.0)).astype(BF16)

    p_r = lax.broadcasted_iota(jnp.int32, (LANE, LANE), 0)
    p_c = lax.broadcasted_iota(jnp.int32, (LANE, LANE), 1)
    head_mean = jnp.where(p_r // FOX_HEAD_DIM == p_c // FOX_HEAD_DIM, 1.0 / FOX_HEAD_DIM, 0.0).astype(BF16)

    def mean_sq(ref, j):
        x = ref[0, :, j * LANE:(j + 1) * LANE]
        return jnp.dot((x * x).astype(BF16), head_mean, preferred_element_type=F32)

    def normed(ref, g_ref, j, ms):
        lanes = slice(j * LANE, (j + 1) * LANE)
        return (ref[0, :, lanes] * lax.rsqrt(ms + EPS) * g_ref[:, lanes]).astype(BF16)

    def placed(x, p_ref, j):
        return jnp.dot(jnp.concatenate([x, gate_row], axis=1), p_ref[j], preferred_element_type=F32)

    pairs = range(FOX_HEADS // 2)
    q_ms = [mean_sq(fq_ref, j) for j in pairs]
    k_ms = [mean_sq(fk_ref, j) for j in pairs]
    q_n = [normed(fq_ref, gq_ref, j, q_ms[j]) for j in pairs]
    k_n = [normed(fk_ref, gk_ref, j, k_ms[j]) for j in pairs]
    for j in pairs:
        q = placed(q_n[j], pq_ref, j)
        k = placed(k_n[j], pk_ref, j)
        v = placed(fv_ref[0, :, j * LANE:(j + 1) * LANE].astype(BF16), pv_ref, j)
        for hh in range(2):
            head = slice(hh * HEAD_PAD, (hh + 1) * HEAD_PAD)
            q_ref[0, 2 * j + hh] = q[:, head].astype(BF16)
            _store_key_blocks(k_ref, 2 * j + hh, k[:, head])
            v_ref[0, 2 * j + hh] = v[:, head].astype(BF16)


def _fox_prep_call(fq, fk, fv, krfg, bf, gq, gk, pq, pk, pv):
    tl = ROW_TILE
    row = lambda b, i: (b, i, 0)
    c2 = lambda b, i: (0, 0)
    c3 = lambda b, i: (0, 0, 0)
    head_out = pl.BlockSpec((1, FOX_HEADS, tl, HEAD_PAD), lambda b, i: (b, 0, i, 0))
    head_shape = jax.ShapeDtypeStruct((BATCH, FOX_HEADS, SEQ, HEAD_PAD), BF16)
    place = pl.BlockSpec((FOX_HEADS // 2, 2 * LANE, 2 * HEAD_PAD), c3)
    return pl.pallas_call(
        _fox_prep_kernel,
        grid=(BATCH, SEQ // tl),
        in_specs=[pl.BlockSpec((1, tl, ATT_WIDTH), row),
                  pl.BlockSpec((1, tl, ATT_WIDTH), row),
                  pl.BlockSpec((1, tl, ATT_WIDTH), row),
                  pl.BlockSpec((1, tl, LANE), row),
                  pl.BlockSpec((1, LANE), c2),
                  pl.BlockSpec((1, ATT_WIDTH), c2),
                  pl.BlockSpec((1, ATT_WIDTH), c2),
                  place, place, place],
        out_specs=[head_out, _KT_SPEC(FOX_HEADS, tl), head_out],
        out_shape=[head_shape, _KT_SHAPE(FOX_HEADS), head_shape],
        scratch_shapes=[pltpu.VMEM((SUBLANE, LANE), F32)],
        compiler_params=_params(("arbitrary", "arbitrary")),
        name="fox_prep",
    )(fq, fk, fv, krfg, bf, gq, gk, pq, pk, pv)


def _fox_placements():
    pq = np.zeros((FOX_HEADS // 2, 2 * LANE, 2 * HEAD_PAD), np.float32)
    pk = np.zeros_like(pq)
    pv = np.zeros_like(pq)
    one_row = LANE + GATE_ONE_LANE
    for j in range(FOX_HEADS // 2):
        for hh in range(2):
            h, col0 = 2 * j + hh, hh * HEAD_PAD
            for d in range(FOX_HEAD_DIM):
                for p in (pq, pk, pv):
                    p[j, hh * FOX_HEAD_DIM + d, col0 + d] = 1.0
            pv[j, one_row, col0 + ONES_LANE] = 1.0
            for n, piece_lane in enumerate((0, GATE_MID_LANE, GATE_LO_LANE)):
                pq[j, LANE + piece_lane + h, col0 + Q_GATE_LANE + n] = 1.0
                pq[j, one_row, col0 + K_GATE_LANE + n] = 1.0
                pk[j, one_row, col0 + Q_GATE_LANE + n] = 1.0
                pk[j, LANE + piece_lane + h, col0 + K_GATE_LANE + n] = -1.0
    return tuple(jnp.asarray(p, BF16) for p in (pq, pk, pv))


def _fox_operands(bf, gq, gk):
    q_scale = LOG2E / math.sqrt(FOX_HEAD_DIM)
    return (_pad_lanes(bf[None, :], LANE), jnp.tile(gq * q_scale, FOX_HEADS)[None, :],
            jnp.tile(gk, FOX_HEADS)[None, :]) + _fox_placements()


def _flash_kernel(qa_ref, qb_ref, kt_ref, v_ref, gap_ref, o_ref, q_scr, s_ref, m_ref, acc_ref,
                  *, tile, chunk, n_tiles):
    p = pl.program_id(2)
    tiles = (p, n_tiles - 1 - p)
    n_tasks = n_tiles + 1
    half = tile // 2
    top, bottom = slice(0, half), slice(half, tile)
    lane = lax.broadcasted_iota(jnp.int32, (tile, HEAD_PAD), 1)
    q_scr[0] = qa_ref[0]
    q_scr[1] = qb_ref[0]

    def plain_task(t):
        second = t - 2 >= p
        return second, second.astype(jnp.int32), jnp.where(second, t - 2 - p, t - 2)

    def row_max_update(w, hh, rows, s):
        mr = m_ref[w, hh, rows]
        for c in range(s.shape[1] // LANE):
            mr = jnp.maximum(mr, s[:, c * LANE:(c + 1) * LANE])
        m_ref[w, hh, rows] = mr

    m_ref[...] = jnp.full(m_ref.shape, NEG, F32)
    for w in range(2):
        for hh in range(2):
            kt = kt_ref[0, hh, tiles[w]]
            s_top = jnp.dot(q_scr[w, hh, top], kt[:, top], preferred_element_type=F32)
            s_top = jnp.where(gap_ref[top, top] <= 0, s_top, NEG)
            s_ref[hh, w, top, top] = s_top
            row_max_update(w, hh, top, s_top)
            s_bot = jnp.dot(q_scr[w, hh, bottom], kt, preferred_element_type=F32)
            s_bot = jnp.where(gap_ref[bottom, :] <= 0, s_bot, NEG)
            s_ref[hh, w, bottom] = s_bot
            row_max_update(w, hh, bottom, s_bot)
    for t in range(2, n_tasks):
        _, which, j = plain_task(jnp.int32(t))
        for hh in range(2):
            s = jnp.dot(q_scr[which, hh], kt_ref[0, hh, j], preferred_element_type=F32)
            s_ref[hh, t] = s
            row_max_update(which, hh, slice(None), s)

    ms = [[jnp.max(m_ref[w, hh], axis=1, keepdims=True) for hh in range(2)] for w in range(2)]

    acc_ref[...] = jnp.zeros(acc_ref.shape, F32)
    for w in range(2):
        k0 = pl.multiple_of(tiles[w] * tile, tile)
        for hh in range(2):
            pr = jnp.exp2(s_ref[hh, w, top, top] - ms[w][hh][top]).astype(BF16)
            acc_ref[w, hh, top] += jnp.dot(pr, v_ref[0, hh, pl.ds(k0, half), :], preferred_element_type=F32)
            pr = jnp.exp2(s_ref[hh, w, bottom] - ms[w][hh][bottom]).astype(BF16)
            acc_ref[w, hh, bottom] += jnp.dot(pr, v_ref[0, hh, pl.ds(k0, tile), :], preferred_element_type=F32)
    for t in range(2, n_tasks):
        second, which, j = plain_task(jnp.int32(t))
        k0 = pl.multiple_of(j * tile, tile)
        for hh in range(2):
            row_max = jnp.where(second, ms[1][hh], ms[0][hh])
            pr = jnp.exp2(s_ref[hh, t] - row_max).astype(BF16)
            acc_ref[which, hh] += jnp.dot(pr, v_ref[0, hh, pl.ds(k0, tile), :], preferred_element_type=F32)

    for w in range(2):
        outs = [acc_ref[w, hh] / acc_ref[w, hh][:, ONES_LANE:ONES_LANE + 1] for hh in range(2)]
        o_ref[0, w, 0] = jnp.where(lane < 64, outs[0], pltpu.roll(outs[1], 64, 1)).astype(BF16)


def _flash_call(q, kt, v, chunk):
    tile = ATT_TILE
    heads = q.shape[1]
    n_tiles = SEQ // tile
    pos = np.arange(tile, dtype=np.int32) // chunk
    gap = jnp.asarray(pos[None, :] - pos[:, None])
    return pl.pallas_call(
        functools.partial(_flash_kernel, tile=tile, chunk=chunk, n_tiles=n_tiles),
        grid=(BATCH, heads // 2, n_tiles // 2),
        in_specs=[pl.BlockSpec((1, 2, tile, HEAD_PAD), lambda b, hp, p: (b, hp, p, 0)),
                  pl.BlockSpec((1, 2, tile, HEAD_PAD), lambda b, hp, p: (b, hp, n_tiles - 1 - p, 0)),
                  pl.BlockSpec((1, 2, n_tiles, HEAD_PAD, tile), lambda b, hp, p: (b, hp, 0, 0, 0)),
                  pl.BlockSpec((1, 2, SEQ, HEAD_PAD), lambda b, hp, p: (b, hp, 0, 0)),
                  pl.BlockSpec((tile, tile), lambda b, hp, p: (0, 0))],
        out_specs=pl.BlockSpec((1, 2, 1, tile, LANE), lambda b, hp, p: (b, 0, p, 0, hp)),
        out_shape=jax.ShapeDtypeStruct((BATCH, 2, n_tiles // 2, tile, ATT_WIDTH), BF16),
        scratch_shapes=[pltpu.VMEM((2, 2, tile, HEAD_PAD), BF16),
                        pltpu.VMEM((2, n_tiles + 1, tile, tile), F32),
                        pltpu.VMEM((2, 2, tile, LANE), F32),
                        pltpu.VMEM((2, 2, tile, HEAD_PAD), F32)],
        compiler_params=_params(("arbitrary", "arbitrary", "arbitrary")),
        name="flash_chunk%d" % chunk,
    )(q, q, kt, v, gap)


def _merge_kernel(ssm_ref, mla_ref, fox_ref, x_ref, g1_ref, gm_ref, gf_ref,
                  ws_ref, wm_ref, wf_ref, o_ref):
    def normed(ref, g_ref):
        a = ref[0, 0, 0].astype(F32)
        return (a * lax.rsqrt(jnp.mean(a * a, axis=-1, keepdims=True) + EPS) * g_ref[...]).astype(BF16)

    mix = jnp.dot(ssm_ref[0], ws_ref[...], preferred_element_type=F32)
    mix = mix + jnp.dot(normed(mla_ref, gm_ref), wm_ref[...], preferred_element_type=F32)
    mix = mix + jnp.dot(normed(fox_ref, gf_ref), wf_ref[...], preferred_element_type=F32)
    o_ref[0] = x_ref[0] + g1_ref[0] * mix


def _merge_call(o_ssm, o_mla, o_fox, x, g1, gm, gf, ws, wm, wf):
    tm = ATT_TILE
    half = SEQ // tm // 2
    row = lambda b, i: (b, i, 0)
    c2 = lambda b, i: (0, 0)
    att = pl.BlockSpec((1, 1, 1, tm, ATT_WIDTH),
                       lambda b, i: (b, i // half, jnp.where(i < half, i, 2 * half - 1 - i), 0, 0))
    return pl.pallas_call(
        _merge_kernel,
        grid=(BATCH, SEQ // tm),
        in_specs=[pl.BlockSpec((1, tm, SSM_WIDTH), row),
                  att, att,
                  pl.BlockSpec((1, tm, D_MODEL), row),
                  pl.BlockSpec((1, 1, D_MODEL), lambda b, i: (b, 0, 0)),
                  pl.BlockSpec((1, ATT_WIDTH), c2),
                  pl.BlockSpec((1, ATT_WIDTH), c2),
                  pl.BlockSpec((SSM_WIDTH, D_MODEL), c2),
                  pl.BlockSpec((ATT_WIDTH, D_MODEL), c2),
                  pl.BlockSpec((ATT_WIDTH, D_MODEL), c2)],
        out_specs=pl.BlockSpec((1, tm, D_MODEL), row),
        out_shape=jax.ShapeDtypeStruct((BATCH, SEQ, D_MODEL), F32),
        compiler_params=_params(("arbitrary", "arbitrary")),
        name="merge",
    )(o_ssm, o_mla, o_fox, x, g1, gm, gf, ws, wm, wf)


def _ffn_kernel(x_ref, g_ref, sh_ref, sc_ref, g2_ref, wg_ref, wu_ref, wd_ref, o_ref, h_ref, acc_ref):
    c = pl.program_id(2)

    @pl.when(c == 0)
    def _():
        h_ref[...] = _rms_mod(x_ref[0], g_ref[...], sc_ref[0], sh_ref[0]).astype(BF16)
        acc_ref[...] = jnp.zeros_like(acc_ref)

    h = h_ref[...]
    gate = jnp.dot(h, wg_ref[0], preferred_element_type=F32)
    up = jnp.dot(h, wu_ref[0], preferred_element_type=F32)
    a = (gate * jax.nn.sigmoid(gate) * up).astype(BF16)
    acc_ref[...] += jnp.dot(a, wd_ref[0], preferred_element_type=F32)

    @pl.when(c == pl.num_programs(2) - 1)
    def _():
        o_ref[0] = x_ref[0] + g2_ref[0] * acc_ref[...]


def _ffn_call(x, g, sh, sc, g2, wg, wu, wd):
    tm = MOE_TILE
    n_chunks = wg.shape[0]
    row = lambda b, i, c: (b, i, 0)
    per_b = lambda b, i, c: (b, 0, 0)
    chunk = lambda b, i, c: (c, 0, 0)
    return pl.pallas_call(
        _ffn_kernel,
        grid=(BATCH, SEQ // tm, n_chunks),
        in_specs=[pl.BlockSpec((1, tm, D_MODEL), row),
                  pl.BlockSpec((1, D_MODEL), lambda b, i, c: (0, 0)),
                  pl.BlockSpec((1, 1, D_MODEL), per_b),
                  pl.BlockSpec((1, 1, D_MODEL), per_b),
                  pl.BlockSpec((1, 1, D_MODEL), per_b),
                  pl.BlockSpec((1, D_MODEL, FF_CHUNK), chunk),
                  pl.BlockSpec((1, D_MODEL, FF_CHUNK), chunk),
                  pl.BlockSpec((1, FF_CHUNK, D_MODEL), chunk)],
        out_specs=pl.BlockSpec((1, tm, D_MODEL), row),
        out_shape=jax.ShapeDtypeStruct((BATCH, SEQ, D_MODEL), F32),
        scratch_shapes=[pltpu.VMEM((tm, D_MODEL), BF16), pltpu.VMEM((tm, D_MODEL), F32)],
        compiler_params=_params(("arbitrary", "arbitrary", "arbitrary")),
        name="ffn_dense",
    )(x, g, sh, sc, g2, wg, wu, wd)


def _router_kernel(x_ref, g_ref, sh_ref, sc_ref, w_ref, b_ref, comb_ref, rank_ref, rankt_ref, count_ref):
    tm = x_ref.shape[1]
    h = _rms_mod(x_ref[0], g_ref[...], sc_ref[0], sh_ref[0])
    h_hi = h.astype(BF16)
    h_lo = (h - h_hi.astype(F32)).astype(BF16)
    w_hi, w_lo = w_ref[0], w_ref[1]
    logits = (jnp.dot(h_hi, w_hi, preferred_element_type=F32)
              + jnp.dot(h_lo, w_hi, preferred_element_type=F32)
              + jnp.dot(h_hi, w_lo, preferred_element_type=F32)) + b_ref[...]
    lane = lax.broadcasted_iota(jnp.int32, logits.shape, 1)
    logits = jnp.where(lane < N_EXPERTS, logits, -jnp.inf)
    m1 = jnp.max(logits, axis=-1, keepdims=True)
    i1 = jnp.min(jnp.where(logits == m1, lane, LANE), axis=-1, keepdims=True)
    rest = jnp.where(lane == i1, -jnp.inf, logits)
    m2 = jnp.max(rest, axis=-1, keepdims=True)
    i2 = jnp.min(jnp.where(rest == m2, lane, LANE), axis=-1, keepdims=True)
    e = jnp.exp(m2 - m1)
    p1 = 1.0 / (1.0 + e)
    comb_ref[0] = jnp.where(lane == i1, p1, 0.0) + jnp.where(lane == i2, e * p1, 0.0)

    chosen = (lane == i1) | (lane == i2)
    chosen_f = jnp.where(chosen, 1.0, 0.0)
    r_i = lax.broadcasted_iota(jnp.int32, (tm, tm), 0)
    c_i = lax.broadcasted_iota(jnp.int32, (tm, tm), 1)
    earlier = jnp.where(c_i < r_i, 1.0, 0.0).astype(BF16)
    rank = jnp.dot(earlier, chosen_f.astype(BF16), preferred_element_type=F32)
    rank = jnp.where(chosen, rank, -1.0)
    rank_ref[0] = rank
    rankt_ref[0] = rank.T[0:SUBLANE, :]
    count_ref[0] = jnp.sum(chosen_f, axis=0, keepdims=True)


def _router_call(x, g, sh, sc, w, b):
    tm = MOE_TILE
    tiles = SEQ // tm
    row = lambda b_, i: (b_, i, 0)
    per_b = lambda b_, i: (b_, 0, 0)
    per_tile = lambda b_, i: (b_ * tiles + i, 0, 0)
    return pl.pallas_call(
        _router_kernel,
        grid=(BATCH, tiles),
        in_specs=[pl.BlockSpec((1, tm, D_MODEL), row),
                  pl.BlockSpec((1, D_MODEL), lambda b_, i: (0, 0)),
                  pl.BlockSpec((1, 1, D_MODEL), per_b),
                  pl.BlockSpec((1, 1, D_MODEL), per_b),
                  pl.BlockSpec((2, D_MODEL, LANE), lambda b_, i: (0, 0, 0)),
                  pl.BlockSpec((1, LANE), lambda b_, i: (0, 0))],
        out_specs=[pl.BlockSpec((1, tm, LANE), row),
                   pl.BlockSpec((1, tm, LANE), row),
                   pl.BlockSpec((1, SUBLANE, tm), per_tile),
                   pl.BlockSpec((1, 1, LANE), per_tile)],
        out_shape=[jax.ShapeDtypeStruct((BATCH, SEQ, LANE), F32),
                   jax.ShapeDtypeStruct((BATCH, SEQ, LANE), F32),
                   jax.ShapeDtypeStruct((BATCH * tiles, SUBLANE, tm), F32),
                   jax.ShapeDtypeStruct((BATCH * tiles, 1, LANE), F32)],
        compiler_params=_params(("arbitrary", "arbitrary")),
        name="router",
    )(x, g, sh, sc, w, b)


def _moe_kernel(count_ref, x_ref, g_ref, sh_ref, sc_ref, g2_ref, comb_ref, rank_ref, rankt_ref,
                wg_ref, wu_ref, wd_ref, o_ref, h_ref):
    tm = x_ref.shape[1]
    e = pl.program_id(1)

    @pl.when(e == 0)
    def _():
        x = x_ref[0]
        h_ref[...] = _rms_mod(x, g_ref[...], sc_ref[0], sh_ref[0]).astype(BF16)
        o_ref[0] = x

    lane = lax.broadcasted_iota(jnp.int32, (tm, LANE), 1)
    mine = lane == e
    rank_col = jnp.sum(jnp.where(mine, rank_ref[0], 0.0), axis=-1, keepdims=True)
    gate_col = jnp.sum(jnp.where(mine, comb_ref[0], 0.0), axis=-1, keepdims=True)
    rank_row = rankt_ref[0, pl.ds(e, 1), :]
    count = count_ref[pl.program_id(0) * N_EXPERTS + e]

    def expert_pass(first, n_rows):
        base = first.astype(F32)
        slot_sub = lax.broadcasted_iota(jnp.int32, (n_rows, tm), 0).astype(F32)
        slot_lane = lax.broadcasted_iota(jnp.int32, (tm, n_rows), 1).astype(F32)
        pick = jnp.where(rank_row - base == slot_sub, 1.0, 0.0).astype(BF16)
        rows = jnp.dot(pick, h_ref[...], preferred_element_type=F32).astype(BF16)
        gate = jnp.dot(rows, wg_ref[0], preferred_element_type=F32)
        up = jnp.dot(rows, wu_ref[0], preferred_element_type=F32)
        a = (gate * jax.nn.sigmoid(gate) * up).astype(BF16)
        y = jnp.dot(a, wd_ref[0], preferred_element_type=F32).astype(BF16)
        place = jnp.where(rank_col - base == slot_lane, 1.0, 0.0).astype(BF16)
        back = jnp.dot(place, y, preferred_element_type=F32)
        o_ref[0] += g2_ref[0] * (gate_col * back)

    def full_pass(sb, carry):
        expert_pass(sb * MOE_ROWS, MOE_ROWS)
        return carry

    n_full = count // MOE_ROWS
    lax.fori_loop(0, n_full, full_pass, 0)
    left = count - n_full * MOE_ROWS

    @pl.when(left > MOE_ROWS // 2)
    def _():
        expert_pass(n_full * MOE_ROWS, MOE_ROWS)

    @pl.when((left > 0) & (left <= MOE_ROWS // 2))
    def _():
        expert_pass(n_full * MOE_ROWS, MOE_ROWS // 2)


def _moe_call(x, g, sh, sc, g2, comb, rank, rankt, counts, wg, wu, wd):
    tm = MOE_TILE
    tiles = SEQ // tm
    n_tiles = BATCH * tiles
    row = lambda i, e, cnt: (i, 0, 0)
    per_b = lambda i, e, cnt: (i // tiles, 0, 0)
    expert = lambda i, e, cnt: (e, 0, 0)
    as_tiles = lambda a: a.reshape(n_tiles, tm, a.shape[-1])
    grid_spec = pltpu.PrefetchScalarGridSpec(
        num_scalar_prefetch=1,
        grid=(n_tiles, N_EXPERTS),
        in_specs=[pl.BlockSpec((1, tm, D_MODEL), row),
                  pl.BlockSpec((1, D_MODEL), lambda i, e, cnt: (0, 0)),
                  pl.BlockSpec((1, 1, D_MODEL), per_b),
                  pl.BlockSpec((1, 1, D_MODEL), per_b),
                  pl.BlockSpec((1, 1, D_MODEL), per_b),
                  pl.BlockSpec((1, tm, LANE), row),
                  pl.BlockSpec((1, tm, LANE), row),
                  pl.BlockSpec((1, SUBLANE, tm), row),
                  pl.BlockSpec((1, D_MODEL, D_FF_EXPERT), expert),
                  pl.BlockSpec((1, D_MODEL, D_FF_EXPERT), expert),
                  pl.BlockSpec((1, D_FF_EXPERT, D_MODEL), expert)],
        out_specs=pl.BlockSpec((1, tm, D_MODEL), row),
        scratch_shapes=[pltpu.VMEM((tm, D_MODEL), BF16)],
    )
    out = pl.pallas_call(
        _moe_kernel,
        grid_spec=grid_spec,
        out_shape=jax.ShapeDtypeStruct((n_tiles, tm, D_MODEL), F32),
        compiler_params=_params(("arbitrary", "arbitrary")),
        name="moe_experts",
    )(counts, as_tiles(x), g, sh, sc, g2, as_tiles(comb), as_tiles(rank), rankt, wg, wu, wd)
    return out.reshape(BATCH, SEQ, D_MODEL)


def kernel(x, c, positions, norm_mix, norm_ffn, w_ada, b_ada, w_in, ssm_lam_re, ssm_lam_im, ssm_log_dt, ssm_b_re, ssm_b_im, ssm_c_re, ssm_c_im, ssm_d, ssm_w_glu, ssm_b_glu, mla_q_norm, mla_kv_norm, mla_w_uq, mla_w_ukv, mla_qk_gq, mla_qk_gk, fox_b_f, fox_qk_gq, fox_qk_gk, out_norm, w_out, ffn_w_gate, ffn_w_up, ffn_w_down, moe_w_router, moe_b_router, moe_w_gate, moe_w_up, moe_w_down):
    tabs = _rope_tables(positions)
    ada = _ada_call(c, w_ada, b_ada)
    ada = ada.reshape(DEPTH, BATCH, 6, 1, D_MODEL)
    row2 = lambda a: a[None, :]

    for i in range(DEPTH):
        sh1, sc1, g1, sh2, sc2, g2 = (ada[i, :, n] for n in range(6))

        u, cq, ckv, krfg, fq, fk, fv = _inproj_call(x, row2(norm_mix[i]), sh1, sc1, _pack_w_in(w_in[i]))

        bmat, lam, cmat = _s5_operands(ssm_lam_re[i], ssm_lam_im[i], ssm_log_dt[i],
                                       ssm_b_re[i], ssm_b_im[i], ssm_c_re[i], ssm_c_im[i])
        u_t = u.transpose(1, 0, 2).reshape(SEQ * BATCH, SSM_WIDTH)
        o_ssm = _s5_call(u_t, bmat, lam, cmat, row2(ssm_d[i]), ssm_w_glu[i].astype(BF16),
                         row2(ssm_b_glu[i]), row2(out_norm[i, :SSM_WIDTH]))
        o_ssm = o_ssm.reshape(SEQ, BATCH, SSM_WIDTH).transpose(1, 0, 2)

        mq, mk, mv = _mla_prep_call(cq, ckv, krfg, tabs, row2(mla_q_norm[i]), row2(mla_kv_norm[i]),
                                    *_mla_weights(mla_w_uq[i], mla_w_ukv[i], mla_qk_gq[i], mla_qk_gk[i]))
        o_mla = _flash_call(mq, mk, mv, CHUNK)

        xq, xk, xv = _fox_prep_call(fq, fk, fv, krfg, *_fox_operands(fox_b_f[i], fox_qk_gq[i], fox_qk_gk[i]))
        o_fox = _flash_call(xq, xk, xv, 1)

        e1, e2 = SSM_WIDTH, SSM_WIDTH + ATT_WIDTH
        wo = w_out[i].astype(BF16)
        x = _merge_call(o_ssm, o_mla, o_fox, x, g1, row2(out_norm[i, e1:e2]), row2(out_norm[i, e2:]),
                        wo[:e1], wo[e1:e2], wo[e2:])

        j = i // 2
        if i % 2 == 0:
            split = lambda w: w.reshape(D_MODEL, D_FF // FF_CHUNK, FF_CHUNK).transpose(1, 0, 2).astype(BF16)
            wd = ffn_w_down[j].reshape(D_FF // FF_CHUNK, FF_CHUNK, D_MODEL).astype(BF16)
            x = _ffn_call(x, row2(norm_ffn[i]), sh2, sc2, g2, split(ffn_w_gate[j]), split(ffn_w_up[j]), wd)
        else:
            wr = _pad_lanes(moe_w_router[j], LANE)
            wr_hi = wr.astype(BF16)
            wr_lo = (wr - wr_hi.astype(F32)).astype(BF16)
            comb, rank, rankt, counts = _router_call(x, row2(norm_ffn[i]), sh2, sc2, jnp.stack([wr_hi, wr_lo]),
                                                     _pad_lanes(row2(moe_b_router[j]), LANE))
            counts = counts[:, 0, :N_EXPERTS].astype(jnp.int32).reshape(-1)
            x = _moe_call(x, row2(norm_ffn[i]), sh2, sc2, g2, comb, rank, rankt, counts,
                          moe_w_gate[j].astype(BF16), moe_w_up[j].astype(BF16), moe_w_down[j].astype(BF16))
    return x
```

```python
import functools
import math

import jax
import jax.numpy as jnp
import numpy as np
from jax import lax
from jax.experimental import pallas as pl
from jax.experimental.pallas import tpu as pltpu

F32 = jnp.float32
BF16 = jnp.bfloat16

D_MODEL = 1024
BATCH = 8
SEQ = 4096
DEPTH = 4
CHUNK = 64
EPS = 1e-6

SSM_WIDTH = 256
SSM_GROUP = 16
N_SSM_GROUPS = 16
SSM_STATE = 64
N_STATE = N_SSM_GROUPS * SSM_STATE

MLA_HEADS = 6
MLA_Q_RANK = 256
MLA_KV_RANK = 128
MLA_NOPE = 64
MLA_ROPE = 32
MLA_V = 64
MLA_QK = 96
ROPE_BASE = 10000.0

FOX_HEADS = 6
FOX_HEAD_DIM = 64
ATT_WIDTH = 384

D_FF = 2816
N_EXPERTS = 8
D_FF_EXPERT = 1408

LANE = 128
SUBLANE = 8
HEAD_PAD = LANE
ONES_LANE = 64
NEG = -1e30

IN_PAD = 1920
KR_LANE = 64

ROW_TILE = 512
S5_STEPS = 64
ATT_TILE = 512
LOG2E = math.log2(math.e)
FF_CHUNK = 1408
MOE_TILE = 1024
MOE_ROWS = 256
VMEM_LIMIT = 56 * 1024 * 1024


def _params(sem):
    return pltpu.CompilerParams(dimension_semantics=sem, vmem_limit_bytes=VMEM_LIMIT)


def _rms_mod(x, g, sc, sh):
    ms = jnp.mean(x * x, axis=-1, keepdims=True)
    h = x * lax.rsqrt(ms + EPS) * g
    return h * (1.0 + sc) + sh


def _split3(x):
    hi = x.astype(BF16).astype(F32)
    r = x - hi
    mid = r.astype(BF16).astype(F32)
    lo = (r - mid).astype(BF16).astype(F32)
    return hi, mid, lo


def _ada_kernel(c_ref, w_ref, b_ref, o_ref):
    c = c_ref[...]
    ca = (c * jax.nn.sigmoid(c)).astype(BF16)
    o_ref[0] = jnp.dot(ca, w_ref[0].astype(BF16), preferred_element_type=F32) + b_ref[0]


def _ada_call(c, w_ada, b_ada):
    tn = 1536
    return pl.pallas_call(
        _ada_kernel,
        grid=(DEPTH, 6 * D_MODEL // tn),
        in_specs=[pl.BlockSpec((BATCH, D_MODEL), lambda i, j: (0, 0)),
                  pl.BlockSpec((1, D_MODEL, tn), lambda i, j: (i, 0, j)),
                  pl.BlockSpec((1, 1, tn), lambda i, j: (i, 0, j))],
        out_specs=pl.BlockSpec((1, BATCH, tn), lambda i, j: (i, 0, j)),
        out_shape=jax.ShapeDtypeStruct((DEPTH, BATCH, 6 * D_MODEL), F32),
        compiler_params=_params(("arbitrary", "arbitrary")),
        name="ada",
    )(c, w_ada, b_ada.reshape(DEPTH, 1, 6 * D_MODEL))


_IN_GROUPS = ((0, 256), (256, 512), (512, 640), (640, 768), (768, 1152), (1152, 1536), (1536, 1920))


def _inproj_into(x_ref, g_ref, sh_ref, sc_ref, w_ref, proj_ref, u_ref):
    parts = 4
    step = x_ref.shape[1] // parts
    rows = [slice(r * step, (r + 1) * step) for r in range(parts)]
    normed = lambda r: _rms_mod(x_ref[0, rows[r]], g_ref[...], sc_ref[0], sh_ref[0]).astype(BF16)
    h_next = normed(0)
    for r in range(parts):
        h = h_next
        if r + 1 < parts:
            h_next = normed(r + 1)
        proj = jnp.dot(h, w_ref[...], preferred_element_type=F32)
        proj_ref[0, rows[r]] = proj
        u_ref[0, rows[r]] = proj[:, _IN_GROUPS[0][0]:_IN_GROUPS[0][1]]


def _pack_w_in(w):
    u, cq, ckv, kr, fq, fk, fv, fg = jnp.split(
        w, (256, 512, 640, 672, 1056, 1440, 1824), axis=1)
    z = lambda n: jnp.zeros((D_MODEL, n), w.dtype)
    krfg = jnp.concatenate([fg, z(KR_LANE - FOX_HEADS), kr, z(LANE - KR_LANE - MLA_ROPE)], axis=1)
    return jnp.concatenate([u, cq, ckv, krfg, fq, fk, fv], axis=1).astype(BF16)


def _s5_kernel(u_ref, bmat_ref, lam_ref, cmat_ref, d_ref, wglu_ref, bglu_ref, gn_ref,
               o_ref, bu0_ref, bu1_ref, state_ref, *, steps):
    rows = steps * BATCH

    @pl.when(pl.program_id(0) == 0)
    def _():
        state_ref[...] = jnp.zeros_like(state_ref)

    halves = ((bu0_ref, slice(0, rows)), (bu1_ref, slice(rows, 2 * rows)))
    for bu_ref, rs in halves:
        bu_ref[...] = jnp.dot(u_ref[rs, :].astype(BF16), bmat_ref[...], preferred_element_type=F32)
    lr = jnp.broadcast_to(lam_ref[0:1, :], (SUBLANE, N_STATE))
    li = jnp.broadcast_to(lam_ref[1:2, :], (SUBLANE, N_STATE))
    sr, si = state_ref[:, 0:N_STATE], state_ref[:, N_STATE:2 * N_STATE]

    for bu_ref, rs in halves:
        for t in range(steps):
            r = slice(t * SUBLANE, (t + 1) * SUBLANE)
            nr = lr * sr - li * si + bu_ref[r, 0:N_STATE]
            ni = lr * si + li * sr + bu_ref[r, N_STATE:2 * N_STATE]
            bu_ref[r, 0:N_STATE] = nr
            bu_ref[r, N_STATE:2 * N_STATE] = ni
            sr, si = nr, ni
        y = jnp.dot(bu_ref[...].astype(BF16), cmat_ref[...], preferred_element_type=F32)
        y = jax.nn.gelu(y + d_ref[...] * u_ref[rs, :])
        gate = jnp.dot(y.astype(BF16), wglu_ref[...], preferred_element_type=F32) + bglu_ref[...]
        o = y * jax.nn.sigmoid(gate)
        ms = jnp.mean(o * o, axis=-1, keepdims=True)
        o_ref[rs, :] = (o * lax.rsqrt(ms + EPS) * gn_ref[...]).astype(BF16)

    state_ref[:, 0:N_STATE] = sr
    state_ref[:, N_STATE:2 * N_STATE] = si


def _s5_call(u_t, bmat, lam, cmat, d_skip, wglu, bglu, gn):
    rows = S5_STEPS * BATCH
    const = lambda i: (0, 0)
    return pl.pallas_call(
        functools.partial(_s5_kernel, steps=S5_STEPS),
        grid=(SEQ // (2 * S5_STEPS),),
        in_specs=[pl.BlockSpec((2 * rows, SSM_WIDTH), lambda i: (i, 0)),
                  pl.BlockSpec((SSM_WIDTH, 2 * N_STATE), const),
                  pl.BlockSpec((2, N_STATE), const),
                  pl.BlockSpec((2 * N_STATE, SSM_WIDTH), const),
                  pl.BlockSpec((1, SSM_WIDTH), const),
                  pl.BlockSpec((SSM_WIDTH, SSM_WIDTH), const),
                  pl.BlockSpec((1, SSM_WIDTH), const),
                  pl.BlockSpec((1, SSM_WIDTH), const)],
        out_specs=pl.BlockSpec((2 * rows, SSM_WIDTH), lambda i: (i, 0)),
        out_shape=jax.ShapeDtypeStruct((SEQ * BATCH, SSM_WIDTH), BF16),
        scratch_shapes=[pltpu.VMEM((rows, 2 * N_STATE), F32),
                        pltpu.VMEM((rows, 2 * N_STATE), F32),
                        pltpu.VMEM((SUBLANE, 2 * N_STATE), F32)],
        compiler_params=_params(("arbitrary",)),
        name="s5",
    )(u_t, bmat, lam, cmat, d_skip, wglu, bglu, gn)


def _s5_operands(lam_re, lam_im, log_dt, b_re, b_im, c_re, c_im):
    dt = jnp.exp(log_dt)[:, None]
    mag = jnp.exp(lam_re * dt)
    lb_re = mag * jnp.cos(lam_im * dt)
    lb_im = mag * jnp.sin(lam_im * dt)
    den = lam_re * lam_re + lam_im * lam_im
    co_re = ((lb_re - 1.0) * lam_re + lb_im * lam_im) / den
    co_im = (lb_im * lam_re - (lb_re - 1.0) * lam_im) / den
    bb_re = co_re[..., None] * b_re - co_im[..., None] * b_im
    bb_im = co_re[..., None] * b_im + co_im[..., None] * b_re
    eye = jnp.eye(N_SSM_GROUPS, dtype=F32)
    blk_b = lambda m: jnp.einsum("gpc,gh->gchp", m, eye).reshape(SSM_WIDTH, N_STATE)
    bmat = jnp.concatenate([blk_b(bb_re), blk_b(bb_im)], axis=1).astype(BF16)
    blk_c = lambda m: jnp.einsum("gcp,gh->gphc", m, eye).reshape(N_STATE, SSM_WIDTH)
    cmat = jnp.concatenate([blk_c(c_re), -blk_c(c_im)], axis=0).astype(BF16)
    lam = jnp.stack([lb_re.reshape(N_STATE), lb_im.reshape(N_STATE)], axis=0)
    return bmat, lam, cmat


def _rope_tables(positions):
    half = MLA_ROPE // 2
    inv = ROPE_BASE ** (-jnp.arange(half, dtype=F32) / half)
    ang = positions.astype(F32)[..., None] * inv
    cos, sin = jnp.cos(ang), jnp.sin(ang)
    shp = positions.shape
    one = lambda n: jnp.ones(shp + (n,), F32)
    zero = lambda n: jnp.zeros(shp + (n,), F32)
    cos_t = jnp.concatenate([one(MLA_NOPE), cos, cos, zero(LANE - MLA_QK)], axis=-1)
    sin_t = jnp.concatenate([zero(MLA_NOPE), -sin, sin, zero(LANE - MLA_QK)], axis=-1)
    return cos_t, sin_t


def _swap_rope_halves(a):
    half = MLA_ROPE // 2
    lo, hi = a[..., MLA_NOPE:MLA_NOPE + half], a[..., MLA_NOPE + half:MLA_QK]
    return jnp.concatenate([jnp.zeros_like(a[..., :MLA_NOPE]), hi, lo, jnp.zeros_like(a[..., MLA_QK:])], axis=-1)


def _store_key_blocks(kt_ref, h, k):
    kt = k.T
    for s in range(k.shape[0] // ATT_TILE):
        kt_ref[0, h, s] = kt[:, s * ATT_TILE:(s + 1) * ATT_TILE].astype(BF16)


_KT_SPEC = lambda heads, tl: pl.BlockSpec((1, heads, tl // ATT_TILE, HEAD_PAD, ATT_TILE),
                                          lambda b, i: (b, 0, i, 0, 0))
_KT_SHAPE = lambda heads: jax.ShapeDtypeStruct((BATCH, heads, SEQ // ATT_TILE, HEAD_PAD, ATT_TILE), BF16)


def _mla_prep_kernel(cq_ref, ckv_ref, krfg_ref, cos_ref, sin_ref, qn_ref, kvn_ref, wq_ref, wk_ref, wv_ref,
                     gq_ref, gqs_ref, gk_ref, gks_ref, q_ref, k_ref, v_ref):
    tl = cq_ref.shape[1]
    lane = lax.broadcasted_iota(jnp.int32, (tl, LANE), 1)
    cos, sin = cos_ref[0], sin_ref[0]
    q_scale = LOG2E / math.sqrt(MLA_QK)
    q_cos, q_sin = gq_ref[...] * cos * q_scale, gqs_ref[...] * sin * q_scale
    k_cos, k_sin = gk_ref[...] * cos, gks_ref[...] * sin
    ones = jnp.ones((LANE, LANE), BF16)

    def inv_rms(x):
        ss = jnp.dot((x * x).astype(BF16), ones, preferred_element_type=F32)
        return lax.rsqrt(ss / MLA_QK + EPS)

    cq = cq_ref[0]
    cqn = (cq * lax.rsqrt(jnp.mean(cq * cq, axis=-1, keepdims=True) + EPS) * qn_ref[...]).astype(BF16)
    ckv = ckv_ref[0]
    ckvn = (ckv * lax.rsqrt(jnp.mean(ckv * ckv, axis=-1, keepdims=True) + EPS) * kvn_ref[...]).astype(BF16)
    kr = jnp.where((lane >= KR_LANE) & (lane < KR_LANE + MLA_ROPE), krfg_ref[0], 0.0)
    kr_swapped = jnp.where(lane < KR_LANE + MLA_ROPE // 2, pltpu.roll(kr, LANE - 16, 1), pltpu.roll(kr, 16, 1))
    k_rotary = kr_swapped * k_sin

    heads = range(MLA_HEADS)
    qqs = [jnp.dot(cqn, wq_ref[h], preferred_element_type=F32) for h in heads]
    ks = [jnp.dot(ckvn, wk_ref[h], preferred_element_type=F32) + kr for h in heads]
    q_inv = [inv_rms(qq[:, :LANE]) for qq in qqs]
    k_inv = [inv_rms(k) for k in ks]
    for h in heads:
        q, q_swapped = qqs[h][:, :LANE], qqs[h][:, LANE:]
        q_ref[0, h] = (q_inv[h] * (q * q_cos + q_swapped * q_sin)).astype(BF16)
        _store_key_blocks(k_ref, h, k_inv[h] * (ks[h] * k_cos + k_rotary))
        v = jnp.dot(ckvn, wv_ref[h], preferred_element_type=F32)
        v_ref[0, h] = jnp.where(lane == ONES_LANE, 1.0, v).astype(BF16)


def _mla_prep_specs(tl):
    row = lambda b, i: (b, i, 0)
    c2 = lambda b, i: (0, 0)
    c3 = lambda b, i: (0, 0, 0)
    head_out = pl.BlockSpec((1, MLA_HEADS, tl, HEAD_PAD), lambda b, i: (b, 0, i, 0))
    head_shape = jax.ShapeDtypeStruct((BATCH, MLA_HEADS, SEQ, HEAD_PAD), BF16)
    gain = pl.BlockSpec((1, HEAD_PAD), c2)
    in_specs = [pl.BlockSpec((1, tl, MLA_Q_RANK), row),
                pl.BlockSpec((1, tl, MLA_KV_RANK), row),
                pl.BlockSpec((1, tl, LANE), row),
                pl.BlockSpec((1, tl, LANE), row),
                pl.BlockSpec((1, tl, LANE), row),
                pl.BlockSpec((1, MLA_Q_RANK), c2),
                pl.BlockSpec((1, MLA_KV_RANK), c2),
                pl.BlockSpec((MLA_HEADS, MLA_Q_RANK, 2 * HEAD_PAD), c3),
                pl.BlockSpec((MLA_HEADS, MLA_KV_RANK, HEAD_PAD), c3),
                pl.BlockSpec((MLA_HEADS, MLA_KV_RANK, HEAD_PAD), c3),
                gain, gain, gain, gain]
    return (in_specs, [head_out, _KT_SPEC(MLA_HEADS, tl), head_out],
            [head_shape, _KT_SHAPE(MLA_HEADS), head_shape])


def _pad_lanes(a, n=HEAD_PAD):
    return jnp.pad(a, [(0, 0)] * (a.ndim - 1) + [(0, n - a.shape[-1])])


def _mla_weights(w_uq, w_ukv, gq, gk):
    wq = _pad_lanes(w_uq.reshape(MLA_Q_RANK, MLA_HEADS, MLA_QK).transpose(1, 0, 2))
    wq = jnp.concatenate([wq, _swap_rope_halves(wq)], axis=-1).astype(BF16)
    wkv = w_ukv.reshape(MLA_KV_RANK, MLA_HEADS, MLA_NOPE + MLA_V).transpose(1, 0, 2)
    wk = _pad_lanes(wkv[..., :MLA_NOPE]).astype(BF16)
    wv = _pad_lanes(wkv[..., MLA_NOPE:]).astype(BF16)
    gq, gk = _pad_lanes(gq[None, :]), _pad_lanes(gk[None, :])
    return wq, wk, wv, gq, _swap_rope_halves(gq), gk, _swap_rope_halves(gk)


GATE_MID_LANE = 8
GATE_LO_LANE = 16
GATE_ONE_LANE = LANE - 1
Q_GATE_LANE = FOX_HEAD_DIM
K_GATE_LANE = FOX_HEAD_DIM + 3


def _fox_prep_kernel(fq_ref, fk_ref, fv_ref, krfg_ref, bf_ref, gq_ref, gk_ref, pq_ref, pk_ref, pv_ref,
                     q_ref, k_ref, v_ref, carry_ref):
    tl = fq_ref.shape[1]
    lane = lax.broadcasted_iota(jnp.int32, (tl, LANE), 1)

    @pl.when(pl.program_id(1) == 0)
    def _():
        carry_ref[...] = jnp.zeros_like(carry_ref)

    logf = jax.nn.log_sigmoid(krfg_ref[0] + bf_ref[...])
    logf = jnp.where(lane < FOX_HEADS, logf, 0.0)
    r_i = lax.broadcasted_iota(jnp.int32, (tl, tl), 0)
    c_i = lax.broadcasted_iota(jnp.int32, (tl, tl), 1)
    tri = jnp.where(c_i <= r_i, 1.0, 0.0).astype(BF16)
    cum = carry_ref[0:1, :]
    for piece in _split3(logf):
        cum = cum + jnp.dot(tri, piece.astype(BF16), preferred_element_type=F32)
    carry_ref[0:1, :] = cum[tl - 1:tl, :]

    c_hi, c_mid, c_lo = _split3(cum * LOG2E)
    gate_row = (c_hi + pltpu.roll(c_mid, GATE_MID_LANE, 1) + pltpu.roll(c_lo, GATE_LO_LANE, 1)
                + jnp.where(lane == GATE_ONE_LANE, 1.0, 0.0)).astype(BF16)

    p_r = lax.broadcasted_iota(jnp.int32, (LANE, LANE), 0)
    p_c = lax.broadcasted_iota(jnp.int32, (LANE, LANE), 1)
    head_mean = jnp.where(p_r // FOX_HEAD_DIM == p_c // FOX_HEAD_DIM, 1.0 / FOX_HEAD_DIM, 0.0).astype(BF16)

    def mean_sq(ref, j):
        x = ref[0, :, j * LANE:(j + 1) * LANE]
        return jnp.dot((x * x).astype(BF16), head_mean, preferred_element_type=F32)

    def normed(ref, g_ref, j, ms):
        lanes = slice(j * LANE, (j + 1) * LANE)
        return (ref[0, :, lanes] * lax.rsqrt(ms + EPS) * g_ref[:, lanes]).astype(BF16)

    def placed(x, p_ref, j):
        return jnp.dot(jnp.concatenate([x, gate_row], axis=1), p_ref[j], preferred_element_type=F32)

    pairs = range(FOX_HEADS // 2)
    q_ms = [mean_sq(fq_ref, j) for j in pairs]
    k_ms = [mean_sq(fk_ref, j) for j in pairs]
    q_n = [normed(fq_ref, gq_ref, j, q_ms[j]) for j in pairs]
    k_n = [normed(fk_ref, gk_ref, j, k_ms[j]) for j in pairs]
    for j in pairs:
        q = placed(q_n[j], pq_ref, j)
        k = placed(k_n[j], pk_ref, j)
        v = placed(fv_ref[0, :, j * LANE:(j + 1) * LANE].astype(BF16), pv_ref, j)
        for hh in range(2):
            head = slice(hh * HEAD_PAD, (hh + 1) * HEAD_PAD)
            q_ref[0, 2 * j + hh] = q[:, head].astype(BF16)
            _store_key_blocks(k_ref, 2 * j + hh, k[:, head])
            v_ref[0, 2 * j + hh] = v[:, head].astype(BF16)


def _fox_prep_specs(tl):
    row = lambda b, i: (b, i, 0)
    c2 = lambda b, i: (0, 0)
    c3 = lambda b, i: (0, 0, 0)
    head_out = pl.BlockSpec((1, FOX_HEADS, tl, HEAD_PAD), lambda b, i: (b, 0, i, 0))
    head_shape = jax.ShapeDtypeStruct((BATCH, FOX_HEADS, SEQ, HEAD_PAD), BF16)
    place = pl.BlockSpec((FOX_HEADS // 2, 2 * LANE, 2 * HEAD_PAD), c3)
    in_specs = [pl.BlockSpec((1, tl, ATT_WIDTH), row),
                pl.BlockSpec((1, tl, ATT_WIDTH), row),
                pl.BlockSpec((1, tl, ATT_WIDTH), row),
                pl.BlockSpec((1, tl, LANE), row),
                pl.BlockSpec((1, LANE), c2),
                pl.BlockSpec((1, ATT_WIDTH), c2),
                pl.BlockSpec((1, ATT_WIDTH), c2),
                place, place, place]
    return (in_specs, [head_out, _KT_SPEC(FOX_HEADS, tl), head_out],
            [head_shape, _KT_SHAPE(FOX_HEADS), head_shape])


def _front_kernel(*refs, n_mla, n_fox):
    x_ref, g_ref, sh_ref, sc_ref, w_ref = refs[:5]
    mla_rest, fox_rest = refs[5:5 + n_mla], refs[5 + n_mla:5 + n_mla + n_fox]
    u_ref = refs[5 + n_mla + n_fox]
    outs = refs[6 + n_mla + n_fox:12 + n_mla + n_fox]
    proj_ref, carry_ref = refs[-2:]
    _inproj_into(x_ref, g_ref, sh_ref, sc_ref, w_ref, proj_ref, u_ref)
    cols = [proj_ref.at[:, :, c0:c1] for c0, c1 in _IN_GROUPS]
    _mla_prep_kernel(cols[1], cols[2], cols[3], *mla_rest, *outs[:3])
    _fox_prep_kernel(cols[4], cols[5], cols[6], cols[3], *fox_rest, *outs[3:], carry_ref)


def _front_call(x, g, sh, sc, w, mla_rest, fox_rest):
    tl = ROW_TILE
    row = lambda b, i: (b, i, 0)
    per_b = lambda b, i: (b, 0, 0)
    const = lambda b, i: (0, 0)
    mla_in, mla_out, mla_shape = _mla_prep_specs(tl)
    fox_in, fox_out, fox_shape = _fox_prep_specs(tl)
    mla_in, fox_in = mla_in[3:], fox_in[4:]
    outs = pl.pallas_call(
        functools.partial(_front_kernel, n_mla=len(mla_in), n_fox=len(fox_in)),
        grid=(BATCH, SEQ // tl),
        in_specs=[pl.BlockSpec((1, tl, D_MODEL), row),
                  pl.BlockSpec((1, D_MODEL), const),
                  pl.BlockSpec((1, 1, D_MODEL), per_b),
                  pl.BlockSpec((1, 1, D_MODEL), per_b),
                  pl.BlockSpec((D_MODEL, IN_PAD), const)] + mla_in + fox_in,
        out_specs=[pl.BlockSpec((1, tl, SSM_WIDTH), row)] + mla_out + fox_out,
        out_shape=[jax.ShapeDtypeStruct((BATCH, SEQ, SSM_WIDTH), F32)] + mla_shape + fox_shape,
        scratch_shapes=[pltpu.VMEM((1, tl, IN_PAD), F32), pltpu.VMEM((SUBLANE, LANE), F32)],
        compiler_params=_params(("arbitrary", "arbitrary")),
        name="front",
    )(x, g, sh, sc, w, *mla_rest, *fox_rest)
    return outs[0], outs[1:4], outs[4:]


def _fox_placements():
    pq = np.zeros((FOX_HEADS // 2, 2 * LANE, 2 * HEAD_PAD), np.float32)
    pk = np.zeros_like(pq)
    pv = np.zeros_like(pq)
    one_row = LANE + GATE_ONE_LANE
    for j in range(FOX_HEADS // 2):
        for hh in range(2):
            h, col0 = 2 * j + hh, hh * HEAD_PAD
            for d in range(FOX_HEAD_DIM):
                for p in (pq, pk, pv):
                    p[j, hh * FOX_HEAD_DIM + d, col0 + d] = 1.0
            pv[j, one_row, col0 + ONES_LANE] = 1.0
            for n, piece_lane in enumerate((0, GATE_MID_LANE, GATE_LO_LANE)):
                pq[j, LANE + piece_lane + h, col0 + Q_GATE_LANE + n] = 1.0
                pq[j, one_row, col0 + K_GATE_LANE + n] = 1.0
                pk[j, one_row, col0 + Q_GATE_LANE + n] = 1.0
                pk[j, LANE + piece_lane + h, col0 + K_GATE_LANE + n] = -1.0
    return tuple(jnp.asarray(p, BF16) for p in (pq, pk, pv))


def _fox_operands(bf, gq, gk):
    q_scale = LOG2E / math.sqrt(FOX_HEAD_DIM)
    return (_pad_lanes(bf[None, :], LANE), jnp.tile(gq * q_scale, FOX_HEADS)[None, :],
            jnp.tile(gk, FOX_HEADS)[None, :]) + _fox_placements()


def _flash_kernel(qa_ref, qb_ref, kt_ref, v_ref, gap_ref, o_ref, q_scr, s_ref, m_ref, acc_ref,
                  *, tile, chunk, n_tiles):
    p = pl.program_id(2)
    tiles = (p, n_tiles - 1 - p)
    n_tasks = n_tiles + 1
    half = tile // 2
    top, bottom = slice(0, half), slice(half, tile)
    lane = lax.broadcasted_iota(jnp.int32, (tile, HEAD_PAD), 1)
    q_scr[0] = qa_ref[0]
    q_scr[1] = qb_ref[0]

    def plain_task(t):
        second = t - 2 >= p
        return second, second.astype(jnp.int32), jnp.where(second, t - 2 - p, t - 2)

    def row_max_update(w, hh, rows, s):
        mr = m_ref[w, hh, rows]
        for c in range(s.shape[1] // LANE):
            mr = jnp.maximum(mr, s[:, c * LANE:(c + 1) * LANE])
        m_ref[w, hh, rows] = mr

    m_ref[...] = jnp.full(m_ref.shape, NEG, F32)
    for w in range(2):
        for hh in range(2):
            kt = kt_ref[0, hh, tiles[w]]
            s_top = jnp.dot(q_scr[w, hh, top], kt[:, top], preferred_element_type=F32)
            s_top = jnp.where(gap_ref[top, top] <= 0, s_top, NEG)
            s_ref[hh, w, top, top] = s_top
            row_max_update(w, hh, top, s_top)
            s_bot = jnp.dot(q_scr[w, hh, bottom], kt, preferred_element_type=F32)
            s_bot = jnp.where(gap_ref[bottom, :] <= 0, s_bot, NEG)
            s_ref[hh, w, bottom] = s_bot
            row_max_update(w, hh, bottom, s_bot)
    for t in range(2, n_tasks):
        _, which, j = plain_task(jnp.int32(t))
        for hh in range(2):
            s = jnp.dot(q_scr[which, hh], kt_ref[0, hh, j], preferred_element_type=F32)
            s_ref[hh, t] = s
            row_max_update(which, hh, slice(None), s)

    ms = [[jnp.max(m_ref[w, hh], axis=1, keepdims=True) for hh in range(2)] for w in range(2)]

    acc_ref[...] = jnp.zeros(acc_ref.shape, F32)
    for w in range(2):
        k0 = pl.multiple_of(tiles[w] * tile, tile)
        for hh in range(2):
            pr = jnp.exp2(s_ref[hh, w, top, top] - ms[w][hh][top]).astype(BF16)
            acc_ref[w, hh, top] += jnp.dot(pr, v_ref[0, hh, pl.ds(k0, half), :], preferred_element_type=F32)
            pr = jnp.exp2(s_ref[hh, w, bottom] - ms[w][hh][bottom]).astype(BF16)
            acc_ref[w, hh, bottom] += jnp.dot(pr, v_ref[0, hh, pl.ds(k0, tile), :], preferred_element_type=F32)
    for t in range(2, n_tasks):
        second, which, j = plain_task(jnp.int32(t))
        k0 = pl.multiple_of(j * tile, tile)
        for hh in range(2):
            row_max = jnp.where(second, ms[1][hh], ms[0][hh])
            pr = jnp.exp2(s_ref[hh, t] - row_max).astype(BF16)
            acc_ref[which, hh] += jnp.dot(pr, v_ref[0, hh, pl.ds(k0, tile), :], preferred_element_type=F32)

    for w in range(2):
        outs = [acc_ref[w, hh] / acc_ref[w, hh][:, ONES_LANE:ONES_LANE + 1] for hh in range(2)]
        o_ref[0, w, 0] = jnp.where(lane < 64, outs[0], pltpu.roll(outs[1], 64, 1)).astype(BF16)


def _flash_call(q, kt, v, chunk):
    tile = ATT_TILE
    heads = q.shape[1]
    n_tiles = SEQ // tile
    pos = np.arange(tile, dtype=np.int32) // chunk
    gap = jnp.asarray(pos[None, :] - pos[:, None])
    return pl.pallas_call(
        functools.partial(_flash_kernel, tile=tile, chunk=chunk, n_tiles=n_tiles),
        grid=(BATCH, heads // 2, n_tiles // 2),
        in_specs=[pl.BlockSpec((1, 2, tile, HEAD_PAD), lambda b, hp, p: (b, hp, p, 0)),
                  pl.BlockSpec((1, 2, tile, HEAD_PAD), lambda b, hp, p: (b, hp, n_tiles - 1 - p, 0)),
                  pl.BlockSpec((1, 2, n_tiles, HEAD_PAD, tile), lambda b, hp, p: (b, hp, 0, 0, 0)),
                  pl.BlockSpec((1, 2, SEQ, HEAD_PAD), lambda b, hp, p: (b, hp, 0, 0)),
                  pl.BlockSpec((tile, tile), lambda b, hp, p: (0, 0))],
        out_specs=pl.BlockSpec((1, 2, 1, tile, LANE), lambda b, hp, p: (b, 0, p, 0, hp)),
        out_shape=jax.ShapeDtypeStruct((BATCH, 2, n_tiles // 2, tile, ATT_WIDTH), BF16),
        scratch_shapes=[pltpu.VMEM((2, 2, tile, HEAD_PAD), BF16),
                        pltpu.VMEM((2, n_tiles + 1, tile, tile), F32),
                        pltpu.VMEM((2, 2, tile, LANE), F32),
                        pltpu.VMEM((2, 2, tile, HEAD_PAD), F32)],
        compiler_params=_params(("arbitrary", "arbitrary", "arbitrary")),
        name="flash_chunk%d" % chunk,
    )(q, q, kt, v, gap)


def _merge_kernel(ssm_ref, mla_ref, fox_ref, x_ref, g1_ref, gm_ref, gf_ref,
                  ws_ref, wm_ref, wf_ref, o_ref):
    def normed(ref, g_ref):
        a = ref[0, 0, 0].astype(F32)
        return (a * lax.rsqrt(jnp.mean(a * a, axis=-1, keepdims=True) + EPS) * g_ref[...]).astype(BF16)

    mix = jnp.dot(ssm_ref[0], ws_ref[...], preferred_element_type=F32)
    mix = mix + jnp.dot(normed(mla_ref, gm_ref), wm_ref[...], preferred_element_type=F32)
    mix = mix + jnp.dot(normed(fox_ref, gf_ref), wf_ref[...], preferred_element_type=F32)
    o_ref[0] = x_ref[0] + g1_ref[0] * mix


def _merge_call(o_ssm, o_mla, o_fox, x, g1, gm, gf, ws, wm, wf):
    tm = ATT_TILE
    half = SEQ // tm // 2
    row = lambda b, i: (b, i, 0)
    c2 = lambda b, i: (0, 0)
    att = pl.BlockSpec((1, 1, 1, tm, ATT_WIDTH),
                       lambda b, i: (b, i // half, jnp.where(i < half, i, 2 * half - 1 - i), 0, 0))
    return pl.pallas_call(
        _merge_kernel,
        grid=(BATCH, SEQ // tm),
        in_specs=[pl.BlockSpec((1, tm, SSM_WIDTH), row),
                  att, att,
                  pl.BlockSpec((1, tm, D_MODEL), row),
                  pl.BlockSpec((1, 1, D_MODEL), lambda b, i: (b, 0, 0)),
                  pl.BlockSpec((1, ATT_WIDTH), c2),
                  pl.BlockSpec((1, ATT_WIDTH), c2),
                  pl.BlockSpec((SSM_WIDTH, D_MODEL), c2),
                  pl.BlockSpec((ATT_WIDTH, D_MODEL), c2),
                  pl.BlockSpec((ATT_WIDTH, D_MODEL), c2)],
        out_specs=pl.BlockSpec((1, tm, D_MODEL), row),
        out_shape=jax.ShapeDtypeStruct((BATCH, SEQ, D_MODEL), F32),
        compiler_params=_params(("arbitrary", "arbitrary")),
        name="merge",
    )(o_ssm, o_mla, o_fox, x, g1, gm, gf, ws, wm, wf)


def _ffn_kernel(x_ref, g_ref, sh_ref, sc_ref, g2_ref, wg_ref, wu_ref, wd_ref, o_ref, h_ref, acc_ref):
    c = pl.program_id(2)

    @pl.when(c == 0)
    def _():
        h_ref[...] = _rms_mod(x_ref[0], g_ref[...], sc_ref[0], sh_ref[0]).astype(BF16)
        acc_ref[...] = jnp.zeros_like(acc_ref)

    h = h_ref[...]
    gate = jnp.dot(h, wg_ref[0], preferred_element_type=F32)
    up = jnp.dot(h, wu_ref[0], preferred_element_type=F32)
    a = (gate * jax.nn.sigmoid(gate) * up).astype(BF16)
    acc_ref[...] += jnp.dot(a, wd_ref[0], preferred_element_type=F32)

    @pl.when(c == pl.num_programs(2) - 1)
    def _():
        o_ref[0] = x_ref[0] + g2_ref[0] * acc_ref[...]


def _ffn_call(x, g, sh, sc, g2, wg, wu, wd):
    tm = MOE_TILE
    n_chunks = wg.shape[0]
    row = lambda b, i, c: (b, i, 0)
    per_b = lambda b, i, c: (b, 0, 0)
    chunk = lambda b, i, c: (c, 0, 0)
    return pl.pallas_call(
        _ffn_kernel,
        grid=(BATCH, SEQ // tm, n_chunks),
        in_specs=[pl.BlockSpec((1, tm, D_MODEL), row),
                  pl.BlockSpec((1, D_MODEL), lambda b, i, c: (0, 0)),
                  pl.BlockSpec((1, 1, D_MODEL), per_b),
                  pl.BlockSpec((1, 1, D_MODEL), per_b),
                  pl.BlockSpec((1, 1, D_MODEL), per_b),
                  pl.BlockSpec((1, D_MODEL, FF_CHUNK), chunk),
                  pl.BlockSpec((1, D_MODEL, FF_CHUNK), chunk),
                  pl.BlockSpec((1, FF_CHUNK, D_MODEL), chunk)],
        out_specs=pl.BlockSpec((1, tm, D_MODEL), row),
        out_shape=jax.ShapeDtypeStruct((BATCH, SEQ, D_MODEL), F32),
        scratch_shapes=[pltpu.VMEM((tm, D_MODEL), BF16), pltpu.VMEM((tm, D_MODEL), F32)],
        compiler_params=_params(("arbitrary", "arbitrary", "arbitrary")),
        name="ffn_dense",
    )(x, g, sh, sc, g2, wg, wu, wd)


def _router_kernel(x_ref, g_ref, sh_ref, sc_ref, w_ref, b_ref, comb_ref, rank_ref, rankt_ref, count_ref):
    tm = x_ref.shape[1]
    h = _rms_mod(x_ref[0], g_ref[...], sc_ref[0], sh_ref[0])
    h_hi = h.astype(BF16)
    h_lo = (h - h_hi.astype(F32)).astype(BF16)
    w_hi, w_lo = w_ref[0], w_ref[1]
    logits = (jnp.dot(h_hi, w_hi, preferred_element_type=F32)
              + jnp.dot(h_lo, w_hi, preferred_element_type=F32)
              + jnp.dot(h_hi, w_lo, preferred_element_type=F32)) + b_ref[...]
    lane = lax.broadcasted_iota(jnp.int32, logits.shape, 1)
    logits = jnp.where(lane < N_EXPERTS, logits, -jnp.inf)
    m1 = jnp.max(logits, axis=-1, keepdims=True)
    i1 = jnp.min(jnp.where(logits == m1, lane, LANE), axis=-1, keepdims=True)
    rest = jnp.where(lane == i1, -jnp.inf, logits)
    m2 = jnp.max(rest, axis=-1, keepdims=True)
    i2 = jnp.min(jnp.where(rest == m2, lane, LANE), axis=-1, keepdims=True)
    e = jnp.exp(m2 - m1)
    p1 = 1.0 / (1.0 + e)
    comb_ref[0] = jnp.where(lane == i1, p1, 0.0) + jnp.where(lane == i2, e * p1, 0.0)

    chosen = (lane == i1) | (lane == i2)
    chosen_f = jnp.where(chosen, 1.0, 0.0)
    r_i = lax.broadcasted_iota(jnp.int32, (tm, tm), 0)
    c_i = lax.broadcasted_iota(jnp.int32, (tm, tm), 1)
    earlier = jnp.where(c_i < r_i, 1.0, 0.0).astype(BF16)
    rank = jnp.dot(earlier, chosen_f.astype(BF16), preferred_element_type=F32)
    rank = jnp.where(chosen, rank, -1.0)
    rank_ref[0] = rank
    rankt_ref[0] = rank.T[0:SUBLANE, :]
    count_ref[0] = jnp.sum(chosen_f, axis=0, keepdims=True)


def _router_call(x, g, sh, sc, w, b):
    tm = MOE_TILE
    tiles = SEQ // tm
    row = lambda b_, i: (b_, i, 0)
    per_b = lambda b_, i: (b_, 0, 0)
    per_tile = lambda b_, i: (b_ * tiles + i, 0, 0)
    return pl.pallas_call(
        _router_kernel,
        grid=(BATCH, tiles),
        in_specs=[pl.BlockSpec((1, tm, D_MODEL), row),
                  pl.BlockSpec((1, D_MODEL), lambda b_, i: (0, 0)),
                  pl.BlockSpec((1, 1, D_MODEL), per_b),
                  pl.BlockSpec((1, 1, D_MODEL), per_b),
                  pl.BlockSpec((2, D_MODEL, LANE), lambda b_, i: (0, 0, 0)),
                  pl.BlockSpec((1, LANE), lambda b_, i: (0, 0))],
        out_specs=[pl.BlockSpec((1, tm, LANE), row),
                   pl.BlockSpec((1, tm, LANE), row),
                   pl.BlockSpec((1, SUBLANE, tm), per_tile),
                   pl.BlockSpec((1, 1, LANE), per_tile)],
        out_shape=[jax.ShapeDtypeStruct((BATCH, SEQ, LANE), F32),
                   jax.ShapeDtypeStruct((BATCH, SEQ, LANE), F32),
                   jax.ShapeDtypeStruct((BATCH * tiles, SUBLANE, tm), F32),
                   jax.ShapeDtypeStruct((BATCH * tiles, 1, LANE), F32)],
        compiler_params=_params(("arbitrary", "arbitrary")),
        name="router",
    )(x, g, sh, sc, w, b)


def _moe_kernel(count_ref, x_ref, g_ref, sh_ref, sc_ref, g2_ref, comb_ref, rank_ref, rankt_ref,
                wg_ref, wu_ref, wd_ref, o_ref, h_ref):
    tm = x_ref.shape[1]
    e = pl.program_id(1)

    @pl.when(e == 0)
    def _():
        x = x_ref[0]
        h_ref[...] = _rms_mod(x, g_ref[...], sc_ref[0], sh_ref[0]).astype(BF16)
        o_ref[0] = x

    lane = lax.broadcasted_iota(jnp.int32, (tm, LANE), 1)
    mine = lane == e
    rank_col = jnp.sum(jnp.where(mine, rank_ref[0], 0.0), axis=-1, keepdims=True)
    gate_col = jnp.sum(jnp.where(mine, comb_ref[0], 0.0), axis=-1, keepdims=True)
    rank_row = rankt_ref[0, pl.ds(e, 1), :]
    count = count_ref[pl.program_id(0) * N_EXPERTS + e]

    def expert_pass(first, n_rows):
        base = first.astype(F32)
        slot_sub = lax.broadcasted_iota(jnp.int32, (n_rows, tm), 0).astype(F32)
        slot_lane = lax.broadcasted_iota(jnp.int32, (tm, n_rows), 1).astype(F32)
        pick = jnp.where(rank_row - base == slot_sub, 1.0, 0.0).astype(BF16)
        rows = jnp.dot(pick, h_ref[...], preferred_element_type=F32).astype(BF16)
        gate = jnp.dot(rows, wg_ref[0], preferred_element_type=F32)
        up = jnp.dot(rows, wu_ref[0], preferred_element_type=F32)
        a = (gate * jax.nn.sigmoid(gate) * up).astype(BF16)
        y = jnp.dot(a, wd_ref[0], preferred_element_type=F32).astype(BF16)
        place = jnp.where(rank_col - base == slot_lane, 1.0, 0.0).astype(BF16)
        back = jnp.dot(place, y, preferred_element_type=F32)
        o_ref[0] += g2_ref[0] * (gate_col * back)

    def full_pass(sb, carry):
        expert_pass(sb * MOE_ROWS, MOE_ROWS)
        return carry

    n_full = count // MOE_ROWS
    lax.fori_loop(0, n_full, full_pass, 0)
    left = count - n_full * MOE_ROWS

    @pl.when(left > MOE_ROWS // 2)
    def _():
        expert_pass(n_full * MOE_ROWS, MOE_ROWS)

    @pl.when((left > 0) & (left <= MOE_ROWS // 2))
    def _():
        expert_pass(n_full * MOE_ROWS, MOE_ROWS // 2)


def _moe_call(x, g, sh, sc, g2, comb, rank, rankt, counts, wg, wu, wd):
    tm = MOE_TILE
    tiles = SEQ // tm
    n_tiles = BATCH * tiles
    row = lambda i, e, cnt: (i, 0, 0)
    per_b = lambda i, e, cnt: (i // tiles, 0, 0)
    expert = lambda i, e, cnt: (e, 0, 0)
    as_tiles = lambda a: a.reshape(n_tiles, tm, a.shape[-1])
    grid_spec = pltpu.PrefetchScalarGridSpec(
        num_scalar_prefetch=1,
        grid=(n_tiles, N_EXPERTS),
        in_specs=[pl.BlockSpec((1, tm, D_MODEL), row),
                  pl.BlockSpec((1, D_MODEL), lambda i, e, cnt: (0, 0)),
                  pl.BlockSpec((1, 1, D_MODEL), per_b),
                  pl.BlockSpec((1, 1, D_MODEL), per_b),
                  pl.BlockSpec((1, 1, D_MODEL), per_b),
                  pl.BlockSpec((1, tm, LANE), row),
                  pl.BlockSpec((1, tm, LANE), row),
                  pl.BlockSpec((1, SUBLANE, tm), row),
                  pl.BlockSpec((1, D_MODEL, D_FF_EXPERT), expert),
                  pl.BlockSpec((1, D_MODEL, D_FF_EXPERT), expert),
                  pl.BlockSpec((1, D_FF_EXPERT, D_MODEL), expert)],
        out_specs=pl.BlockSpec((1, tm, D_MODEL), row),
        scratch_shapes=[pltpu.VMEM((tm, D_MODEL), BF16)],
    )
    out = pl.pallas_call(
        _moe_kernel,
        grid_spec=grid_spec,
        out_shape=jax.ShapeDtypeStruct((n_tiles, tm, D_MODEL), F32),
        compiler_params=_params(("arbitrary", "arbitrary")),
        name="moe_experts",
    )(counts, as_tiles(x), g, sh, sc, g2, as_tiles(comb), as_tiles(rank), rankt, wg, wu, wd)
    return out.reshape(BATCH, SEQ, D_MODEL)


def kernel(x, c, positions, norm_mix, norm_ffn, w_ada, b_ada, w_in, ssm_lam_re, ssm_lam_im, ssm_log_dt, ssm_b_re, ssm_b_im, ssm_c_re, ssm_c_im, ssm_d, ssm_w_glu, ssm_b_glu, mla_q_norm, mla_kv_norm, mla_w_uq, mla_w_ukv, mla_qk_gq, mla_qk_gk, fox_b_f, fox_qk_gq, fox_qk_gk, out_norm, w_out, ffn_w_gate, ffn_w_up, ffn_w_down, moe_w_router, moe_b_router, moe_w_gate, moe_w_up, moe_w_down):
    tabs = _rope_tables(positions)
    ada = _ada_call(c, w_ada, b_ada)
    ada = ada.reshape(DEPTH, BATCH, 6, 1, D_MODEL)
    row2 = lambda a: a[None, :]

    for i in range(DEPTH):
        sh1, sc1, g1, sh2, sc2, g2 = (ada[i, :, n] for n in range(6))

        u, mla_qkv, fox_qkv = _front_call(
            x, row2(norm_mix[i]), sh1, sc1, _pack_w_in(w_in[i]),
            (*tabs, row2(mla_q_norm[i]), row2(mla_kv_norm[i]),
             *_mla_weights(mla_w_uq[i], mla_w_ukv[i], mla_qk_gq[i], mla_qk_gk[i])),
            _fox_operands(fox_b_f[i], fox_qk_gq[i], fox_qk_gk[i]))

        bmat, lam, cmat = _s5_operands(ssm_lam_re[i], ssm_lam_im[i], ssm_log_dt[i],
                                       ssm_b_re[i], ssm_b_im[i], ssm_c_re[i], ssm_c_im[i])
        u_t = u.transpose(1, 0, 2).reshape(SEQ * BATCH, SSM_WIDTH)
        o_ssm = _s5_call(u_t, bmat, lam, cmat, row2(ssm_d[i]), ssm_w_glu[i].astype(BF16),
                         row2(ssm_b_glu[i]), row2(out_norm[i, :SSM_WIDTH]))
        o_ssm = o_ssm.reshape(SEQ, BATCH, SSM_WIDTH).transpose(1, 0, 2)

        o_mla = _flash_call(*mla_qkv, CHUNK)
        o_fox = _flash_call(*fox_qkv, 1)

        e1, e2 = SSM_WIDTH, SSM_WIDTH + ATT_WIDTH
        wo = w_out[i].astype(BF16)
        x = _merge_call(o_ssm, o_mla, o_fox, x, g1, row2(out_norm[i, e1:e2]), row2(out_norm[i, e2:]),
                        wo[:e1], wo[e1:e2], wo[e2:])

        j = i // 2
        if i % 2 == 0:
            split = lambda w: w.reshape(D_MODEL, D_FF // FF_CHUNK, FF_CHUNK).transpose(1, 0, 2).astype(BF16)
            wd = ffn_w_down[j].reshape(D_FF // FF_CHUNK, FF_CHUNK, D_MODEL).astype(BF16)
            x = _ffn_call(x, row2(norm_ffn[i]), sh2, sc2, g2, split(ffn_w_gate[j]), split(ffn_w_up[j]), wd)
        else:
            wr = _pad_lanes(moe_w_router[j], LANE)
            wr_hi = wr.astype(BF16)
            wr_lo = (wr - wr_hi.astype(F32)).astype(BF16)
            comb, rank, rankt, counts = _router_call(x, row2(norm_ffn[i]), sh2, sc2, jnp.stack([wr_hi, wr_lo]),
                                                     _pad_lanes(row2(moe_b_router[j]), LANE))
            counts = counts[:, 0, :N_EXPERTS].astype(jnp.int32).reshape(-1)
            x = _moe_call(x, row2(norm_ffn[i]), sh2, sc2, g2, comb, rank, rankt, counts,
                          moe_w_gate[j].astype(BF16), moe_w_up[j].astype(BF16), moe_w_down[j].astype(BF16))
    return x
```

```python
import functools
import math

import jax
import jax.numpy as jnp
import numpy as np
from jax import lax
from jax.experimental import pallas as pl
from jax.experimental.pallas import tpu as pltpu

F32 = jnp.float32
BF16 = jnp.bfloat16

D_MODEL = 1024
BATCH = 8
SEQ = 4096
DEPTH = 4
CHUNK = 64
EPS = 1e-6

SSM_WIDTH = 256
SSM_GROUP = 16
N_SSM_GROUPS = 16
SSM_STATE = 64
N_STATE = N_SSM_GROUPS * SSM_STATE

MLA_HEADS = 6
MLA_Q_RANK = 256
MLA_KV_RANK = 128
MLA_NOPE = 64
MLA_ROPE = 32
MLA_V = 64
MLA_QK = 96
ROPE_BASE = 10000.0

FOX_HEADS = 6
FOX_HEAD_DIM = 64
ATT_WIDTH = 384

D_FF = 2816
N_EXPERTS = 8
D_FF_EXPERT = 1408

LANE = 128
SUBLANE = 8
HEAD_PAD = LANE
ONES_LANE = 64
NEG = -1e30

IN_PAD = 1920
KR_LANE = 64

ROW_TILE = 512
S5_STEPS = 64
ATT_TILE = 512
LOG2E = math.log2(math.e)
FF_CHUNK = 1408
MOE_TILE = 1024
MOE_ROWS = 256
VMEM_LIMIT = 56 * 1024 * 1024


def _params(sem):
    return pltpu.CompilerParams(dimension_semantics=sem, vmem_limit_bytes=VMEM_LIMIT)


def _rms_mod(x, g, sc, sh):
    ms = jnp.mean(x * x, axis=-1, keepdims=True)
    h = x * lax.rsqrt(ms + EPS) * g
    return h * (1.0 + sc) + sh


def _split3(x):
    hi = x.astype(BF16).astype(F32)
    r = x - hi
    mid = r.astype(BF16).astype(F32)
    lo = (r - mid).astype(BF16).astype(F32)
    return hi, mid, lo


def _ada_kernel(c_ref, w_ref, b_ref, o_ref):
    c = c_ref[...]
    ca = (c * jax.nn.sigmoid(c)).astype(BF16)
    o_ref[0] = jnp.dot(ca, w_ref[0].astype(BF16), preferred_element_type=F32) + b_ref[0]


def _ada_call(c, w_ada, b_ada):
    tn = 1536
    return pl.pallas_call(
        _ada_kernel,
        grid=(DEPTH, 6 * D_MODEL // tn),
        in_specs=[pl.BlockSpec((BATCH, D_MODEL), lambda i, j: (0, 0)),
                  pl.BlockSpec((1, D_MODEL, tn), lambda i, j: (i, 0, j)),
                  pl.BlockSpec((1, 1, tn), lambda i, j: (i, 0, j))],
        out_specs=pl.BlockSpec((1, BATCH, tn), lambda i, j: (i, 0, j)),
        out_shape=jax.ShapeDtypeStruct((DEPTH, BATCH, 6 * D_MODEL), F32),
        compiler_params=_params(("arbitrary", "arbitrary")),
        name="ada",
    )(c, w_ada, b_ada.reshape(DEPTH, 1, 6 * D_MODEL))


_IN_GROUPS = ((0, 256), (256, 512), (512, 640), (640, 768), (768, 1152), (1152, 1536), (1536, 1920))


def _inproj_into(x_ref, g_ref, sh_ref, sc_ref, w_ref, proj_ref, u_ref):
    parts = 4
    step = x_ref.shape[1] // parts
    rows = [slice(r * step, (r + 1) * step) for r in range(parts)]
    normed = lambda r: _rms_mod(x_ref[0, rows[r]], g_ref[...], sc_ref[0], sh_ref[0]).astype(BF16)
    h_next = normed(0)
    for r in range(parts):
        h = h_next
        if r + 1 < parts:
            h_next = normed(r + 1)
        proj = jnp.dot(h, w_ref[...], preferred_element_type=F32)
        proj_ref[0, rows[r]] = proj
        u_ref[rows[r]] = proj[:, _IN_GROUPS[0][0]:_IN_GROUPS[0][1]]


def _pack_w_in(w):
    u, cq, ckv, kr, fq, fk, fv, fg = jnp.split(
        w, (256, 512, 640, 672, 1056, 1440, 1824), axis=1)
    z = lambda n: jnp.zeros((D_MODEL, n), w.dtype)
    krfg = jnp.concatenate([fg, z(KR_LANE - FOX_HEADS), kr, z(LANE - KR_LANE - MLA_ROPE)], axis=1)
    return jnp.concatenate([u, cq, ckv, krfg, fq, fk, fv], axis=1).astype(BF16)


def _s5_kernel(u2_ref, bmat_ref, lam_ref, cmat_ref, d_ref, wglu_ref, bglu_ref, gn_ref,
               o2_ref, u_ref, o_ref, bu0_ref, bu1_ref, state_ref, *, steps):
    rows = steps * BATCH
    lane_tiles = SSM_WIDTH // LANE
    for b in range(BATCH):
        for c in range(lane_tiles):
            lanes = slice(b * SSM_WIDTH + c * LANE, b * SSM_WIDTH + (c + 1) * LANE)
            u_ref.at[c][pl.ds(b, 2 * steps, stride=BATCH), :] = u2_ref[:, lanes]
    u_rows = lambda rs: jnp.concatenate([u_ref[c, rs, :] for c in range(lane_tiles)], axis=1)

    @pl.when(pl.program_id(0) == 0)
    def _():
        state_ref[...] = jnp.zeros_like(state_ref)

    halves = ((bu0_ref, slice(0, rows)), (bu1_ref, slice(rows, 2 * rows)))
    for bu_ref, rs in halves:
        bu_ref[...] = jnp.dot(u_rows(rs).astype(BF16), bmat_ref[...], preferred_element_type=F32)
    lr = jnp.broadcast_to(lam_ref[0:1, :], (SUBLANE, N_STATE))
    li = jnp.broadcast_to(lam_ref[1:2, :], (SUBLANE, N_STATE))
    sr, si = state_ref[:, 0:N_STATE], state_ref[:, N_STATE:2 * N_STATE]

    for bu_ref, rs in halves:
        for t in range(steps):
            r = slice(t * SUBLANE, (t + 1) * SUBLANE)
            nr = lr * sr - li * si + bu_ref[r, 0:N_STATE]
            ni = lr * si + li * sr + bu_ref[r, N_STATE:2 * N_STATE]
            bu_ref[r, 0:N_STATE] = nr
            bu_ref[r, N_STATE:2 * N_STATE] = ni
            sr, si = nr, ni
        y = jnp.dot(bu_ref[...].astype(BF16), cmat_ref[...], preferred_element_type=F32)
        y = jax.nn.gelu(y + d_ref[...] * u_rows(rs))
        gate = jnp.dot(y.astype(BF16), wglu_ref[...], preferred_element_type=F32) + bglu_ref[...]
        o = y * jax.nn.sigmoid(gate)
        ms = jnp.mean(o * o, axis=-1, keepdims=True)
        o = o * lax.rsqrt(ms + EPS) * gn_ref[...]
        for c in range(lane_tiles):
            o_ref[c, rs, :] = o[:, c * LANE:(c + 1) * LANE]

    state_ref[:, 0:N_STATE] = sr
    state_ref[:, N_STATE:2 * N_STATE] = si
    for b in range(BATCH):
        for c in range(lane_tiles):
            lanes = slice(b * SSM_WIDTH + c * LANE, b * SSM_WIDTH + (c + 1) * LANE)
            o2_ref[:, lanes] = o_ref.at[c][pl.ds(b, 2 * steps, stride=BATCH), :].astype(BF16)


def _s5_call(u2, bmat, lam, cmat, d_skip, wglu, bglu, gn):
    rows = S5_STEPS * BATCH
    const = lambda i: (0, 0)
    return pl.pallas_call(
        functools.partial(_s5_kernel, steps=S5_STEPS),
        grid=(SEQ // (2 * S5_STEPS),),
        in_specs=[pl.BlockSpec((2 * S5_STEPS, BATCH * SSM_WIDTH), lambda i: (i, 0)),
                  pl.BlockSpec((SSM_WIDTH, 2 * N_STATE), const),
                  pl.BlockSpec((2, N_STATE), const),
                  pl.BlockSpec((2 * N_STATE, SSM_WIDTH), const),
                  pl.BlockSpec((1, SSM_WIDTH), const),
                  pl.BlockSpec((SSM_WIDTH, SSM_WIDTH), const),
                  pl.BlockSpec((1, SSM_WIDTH), const),
                  pl.BlockSpec((1, SSM_WIDTH), const)],
        out_specs=pl.BlockSpec((2 * S5_STEPS, BATCH * SSM_WIDTH), lambda i: (i, 0)),
        out_shape=jax.ShapeDtypeStruct((SEQ, BATCH * SSM_WIDTH), BF16),
        scratch_shapes=[pltpu.VMEM((SSM_WIDTH // LANE, 2 * rows, LANE), F32),
                        pltpu.VMEM((SSM_WIDTH // LANE, 2 * rows, LANE), F32),
                        pltpu.VMEM((rows, 2 * N_STATE), F32),
                        pltpu.VMEM((rows, 2 * N_STATE), F32),
                        pltpu.VMEM((SUBLANE, 2 * N_STATE), F32)],
        compiler_params=_params(("arbitrary",)),
        name="s5",
    )(u2, bmat, lam, cmat, d_skip, wglu, bglu, gn)


def _s5_operands(lam_re, lam_im, log_dt, b_re, b_im, c_re, c_im):
    dt = jnp.exp(log_dt)[:, None]
    mag = jnp.exp(lam_re * dt)
    lb_re = mag * jnp.cos(lam_im * dt)
    lb_im = mag * jnp.sin(lam_im * dt)
    den = lam_re * lam_re + lam_im * lam_im
    co_re = ((lb_re - 1.0) * lam_re + lb_im * lam_im) / den
    co_im = (lb_im * lam_re - (lb_re - 1.0) * lam_im) / den
    bb_re = co_re[..., None] * b_re - co_im[..., None] * b_im
    bb_im = co_re[..., None] * b_im + co_im[..., None] * b_re
    eye = jnp.eye(N_SSM_GROUPS, dtype=F32)
    blk_b = lambda m: jnp.einsum("gpc,gh->gchp", m, eye).reshape(SSM_WIDTH, N_STATE)
    bmat = jnp.concatenate([blk_b(bb_re), blk_b(bb_im)], axis=1).astype(BF16)
    blk_c = lambda m: jnp.einsum("gcp,gh->gphc", m, eye).reshape(N_STATE, SSM_WIDTH)
    cmat = jnp.concatenate([blk_c(c_re), -blk_c(c_im)], axis=0).astype(BF16)
    lam = jnp.stack([lb_re.reshape(N_STATE), lb_im.reshape(N_STATE)], axis=0)
    return bmat, lam, cmat


def _rope_tables(positions):
    half = MLA_ROPE // 2
    inv = ROPE_BASE ** (-jnp.arange(half, dtype=F32) / half)
    ang = inv[:, None] * positions.astype(F32).reshape(1, -1)
    shp = positions.shape
    cos, sin = (f(ang).T.reshape(shp + (half,)) for f in (jnp.cos, jnp.sin))
    one = lambda n: jnp.ones(shp + (n,), F32)
    zero = lambda n: jnp.zeros(shp + (n,), F32)
    cos_t = jnp.concatenate([one(MLA_NOPE), cos, cos, zero(LANE - MLA_QK)], axis=-1)
    sin_t = jnp.concatenate([zero(MLA_NOPE), -sin, sin, zero(LANE - MLA_QK)], axis=-1)
    return cos_t, sin_t


def _swap_rope_halves(a):
    half = MLA_ROPE // 2
    lo, hi = a[..., MLA_NOPE:MLA_NOPE + half], a[..., MLA_NOPE + half:MLA_QK]
    return jnp.concatenate([jnp.zeros_like(a[..., :MLA_NOPE]), hi, lo, jnp.zeros_like(a[..., MLA_QK:])], axis=-1)


def _store_key_blocks(kt_ref, h, k):
    kt = k.T
    for s in range(k.shape[0] // ATT_TILE):
        kt_ref[0, h, s] = kt[:, s * ATT_TILE:(s + 1) * ATT_TILE].astype(BF16)


_KT_SPEC = lambda heads, tl: pl.BlockSpec((1, heads, tl // ATT_TILE, HEAD_PAD, ATT_TILE),
                                          lambda b, i: (b, 0, i, 0, 0))
_KT_SHAPE = lambda heads: jax.ShapeDtypeStruct((BATCH, heads, SEQ // ATT_TILE, HEAD_PAD, ATT_TILE), BF16)


def _mla_prep_kernel(cq_ref, ckv_ref, krfg_ref, cos_ref, sin_ref, qn_ref, kvn_ref, wq_ref, wk_ref, wv_ref,
                     gq_ref, gqs_ref, gk_ref, gks_ref, q_ref, k_ref, v_ref):
    tl = cq_ref.shape[1]
    lane = lax.broadcasted_iota(jnp.int32, (tl, LANE), 1)
    cos, sin = cos_ref[0], sin_ref[0]
    q_scale = LOG2E / math.sqrt(MLA_QK)
    q_cos, q_sin = gq_ref[...] * cos * q_scale, gqs_ref[...] * sin * q_scale
    k_cos, k_sin = gk_ref[...] * cos, gks_ref[...] * sin
    ones = jnp.ones((LANE, LANE), BF16)

    def inv_rms(x):
        ss = jnp.dot((x * x).astype(BF16), ones, preferred_element_type=F32)
        return lax.rsqrt(ss / MLA_QK + EPS)

    cq = cq_ref[0]
    cqn = (cq * lax.rsqrt(jnp.mean(cq * cq, axis=-1, keepdims=True) + EPS) * qn_ref[...]).astype(BF16)
    ckv = ckv_ref[0]
    ckvn = (ckv * lax.rsqrt(jnp.mean(ckv * ckv, axis=-1, keepdims=True) + EPS) * kvn_ref[...]).astype(BF16)
    kr = jnp.where((lane >= KR_LANE) & (lane < KR_LANE + MLA_ROPE), krfg_ref[0], 0.0)
    kr_swapped = jnp.where(lane < KR_LANE + MLA_ROPE // 2, pltpu.roll(kr, LANE - 16, 1), pltpu.roll(kr, 16, 1))
    k_rotary = kr_swapped * k_sin

    heads = range(MLA_HEADS)
    qqs = [jnp.dot(cqn, wq_ref[h], preferred_element_type=F32) for h in heads]
    ks = [jnp.dot(ckvn, wk_ref[h], preferred_element_type=F32) + kr for h in heads]
    q_inv = [inv_rms(qq[:, :LANE]) for qq in qqs]
    k_inv = [inv_rms(k) for k in ks]
    for h in heads:
        q, q_swapped = qqs[h][:, :LANE], qqs[h][:, LANE:]
        q_ref[0, h] = (q_inv[h] * (q * q_cos + q_swapped * q_sin)).astype(BF16)
        _store_key_blocks(k_ref, h, k_inv[h] * (ks[h] * k_cos + k_rotary))
        v = jnp.dot(ckvn, wv_ref[h], preferred_element_type=F32)
        v_ref[0, h] = jnp.where(lane == ONES_LANE, 1.0, v).astype(BF16)


def _mla_prep_specs(tl):
    row = lambda b, i: (b, i, 0)
    c2 = lambda b, i: (0, 0)
    c3 = lambda b, i: (0, 0, 0)
    head_out = pl.BlockSpec((1, MLA_HEADS, tl, HEAD_PAD), lambda b, i: (b, 0, i, 0))
    head_shape = jax.ShapeDtypeStruct((BATCH, MLA_HEADS, SEQ, HEAD_PAD), BF16)
    gain = pl.BlockSpec((1, HEAD_PAD), c2)
    in_specs = [pl.BlockSpec((1, tl, MLA_Q_RANK), row),
                pl.BlockSpec((1, tl, MLA_KV_RANK), row),
                pl.BlockSpec((1, tl, LANE), row),
                pl.BlockSpec((1, tl, LANE), row),
                pl.BlockSpec((1, tl, LANE), row),
                pl.BlockSpec((1, MLA_Q_RANK), c2),
                pl.BlockSpec((1, MLA_KV_RANK), c2),
                pl.BlockSpec((MLA_HEADS, MLA_Q_RANK, 2 * HEAD_PAD), c3),
                pl.BlockSpec((MLA_HEADS, MLA_KV_RANK, HEAD_PAD), c3),
                pl.BlockSpec((MLA_HEADS, MLA_KV_RANK, HEAD_PAD), c3),
                gain, gain, gain, gain]
    return (in_specs, [head_out, _KT_SPEC(MLA_HEADS, tl), head_out],
            [head_shape, _KT_SHAPE(MLA_HEADS), head_shape])


def _pad_lanes(a, n=HEAD_PAD):
    return jnp.pad(a, [(0, 0)] * (a.ndim - 1) + [(0, n - a.shape[-1])])


def _mla_weights(w_uq, w_ukv, gq, gk):
    wq = _pad_lanes(w_uq.reshape(MLA_Q_RANK, MLA_HEADS, MLA_QK).transpose(1, 0, 2))
    wq = jnp.concatenate([wq, _swap_rope_halves(wq)], axis=-1).astype(BF16)
    wkv = w_ukv.reshape(MLA_KV_RANK, MLA_HEADS, MLA_NOPE + MLA_V).transpose(1, 0, 2)
    wk = _pad_lanes(wkv[..., :MLA_NOPE]).astype(BF16)
    wv = _pad_lanes(wkv[..., MLA_NOPE:]).astype(BF16)
    gq, gk = _pad_lanes(gq[None, :]), _pad_lanes(gk[None, :])
    return wq, wk, wv, gq, _swap_rope_halves(gq), gk, _swap_rope_halves(gk)


GATE_MID_LANE = 8
GATE_LO_LANE = 16
GATE_ONE_LANE = LANE - 1
Q_GATE_LANE = FOX_HEAD_DIM
K_GATE_LANE = FOX_HEAD_DIM + 3


def _fox_prep_kernel(fq_ref, fk_ref, fv_ref, krfg_ref, bf_ref, gq_ref, gk_ref, pq_ref, pk_ref, pv_ref,
                     q_ref, k_ref, v_ref, carry_ref):
    tl = fq_ref.shape[1]
    lane = lax.broadcasted_iota(jnp.int32, (tl, LANE), 1)

    @pl.when(pl.program_id(1) == 0)
    def _():
        carry_ref[...] = jnp.zeros_like(carry_ref)

    logf = jax.nn.log_sigmoid(krfg_ref[0] + bf_ref[...])
    logf = jnp.where(lane < FOX_HEADS, logf, 0.0)
    r_i = lax.broadcasted_iota(jnp.int32, (tl, tl), 0)
    c_i = lax.broadcasted_iota(jnp.int32, (tl, tl), 1)
    tri = jnp.where(c_i <= r_i, 1.0, 0.0).astype(BF16)
    cum = carry_ref[0:1, :]
    for piece in _split3(logf):
        cum = cum + jnp.dot(tri, piece.astype(BF16), preferred_element_type=F32)
    carry_ref[0:1, :] = cum[tl - 1:tl, :]

    c_hi, c_mid, c_lo = _split3(cum * LOG2E)
    gate_row = (c_hi + pltpu.roll(c_mid, GATE_MID_LANE, 1) + pltpu.roll(c_lo, GATE_LO_LANE, 1)
                + jnp.where(lane == GATE_ONE_LANE, 1.0, 0.0)).astype(BF16)

    p_r = lax.broadcasted_iota(jnp.int32, (LANE, LANE), 0)
    p_c = lax.broadcasted_iota(jnp.int32, (LANE, LANE), 1)
    head_mean = jnp.where(p_r // FOX_HEAD_DIM == p_c // FOX_HEAD_DIM, 1.0 / FOX_HEAD_DIM, 0.0).astype(BF16)

    def mean_sq(ref, j):
        x = ref[0, :, j * LANE:(j + 1) * LANE]
        return jnp.dot((x * x).astype(BF16), head_mean, preferred_element_type=F32)

    def normed(ref, g_ref, j, ms):
        lanes = slice(j * LANE, (j + 1) * LANE)
        return (ref[0, :, lanes] * lax.rsqrt(ms + EPS) * g_ref[:, lanes]).astype(BF16)

    def placed(x, p_ref, j):
        return jnp.dot(jnp.concatenate([x, gate_row], axis=1), p_ref[j], preferred_element_type=F32)

    pairs = range(FOX_HEADS // 2)
    q_ms = [mean_sq(fq_ref, j) for j in pairs]
    k_ms = [mean_sq(fk_ref, j) for j in pairs]
    q_n = [normed(fq_ref, gq_ref, j, q_ms[j]) for j in pairs]
    k_n = [normed(fk_ref, gk_ref, j, k_ms[j]) for j in pairs]
    for j in pairs:
        q = placed(q_n[j], pq_ref, j)
        k = placed(k_n[j], pk_ref, j)
        v = placed(fv_ref[0, :, j * LANE:(j + 1) * LANE].astype(BF16), pv_ref, j)
        for hh in range(2):
            head = slice(hh * HEAD_PAD, (hh + 1) * HEAD_PAD)
            q_ref[0, 2 * j + hh] = q[:, head].astype(BF16)
            _store_key_blocks(k_ref, 2 * j + hh, k[:, head])
            v_ref[0, 2 * j + hh] = v[:, head].astype(BF16)


def _fox_prep_specs(tl):
    row = lambda b, i: (b, i, 0)
    c2 = lambda b, i: (0, 0)
    c3 = lambda b, i: (0, 0, 0)
    head_out = pl.BlockSpec((1, FOX_HEADS, tl, HEAD_PAD), lambda b, i: (b, 0, i, 0))
    head_shape = jax.ShapeDtypeStruct((BATCH, FOX_HEADS, SEQ, HEAD_PAD), BF16)
    place = pl.BlockSpec((FOX_HEADS // 2, 2 * LANE, 2 * HEAD_PAD), c3)
    in_specs = [pl.BlockSpec((1, tl, ATT_WIDTH), row),
                pl.BlockSpec((1, tl, ATT_WIDTH), row),
                pl.BlockSpec((1, tl, ATT_WIDTH), row),
                pl.BlockSpec((1, tl, LANE), row),
                pl.BlockSpec((1, LANE), c2),
                pl.BlockSpec((1, ATT_WIDTH), c2),
                pl.BlockSpec((1, ATT_WIDTH), c2),
                place, place, place]
    return (in_specs, [head_out, _KT_SPEC(FOX_HEADS, tl), head_out],
            [head_shape, _KT_SHAPE(FOX_HEADS), head_shape])


def _front_kernel(*refs, n_mla, n_fox):
    x_ref, g_ref, sh_ref, sc_ref, w_ref = refs[:5]
    mla_rest, fox_rest = refs[5:5 + n_mla], refs[5 + n_mla:5 + n_mla + n_fox]
    u_ref = refs[5 + n_mla + n_fox]
    outs = refs[6 + n_mla + n_fox:12 + n_mla + n_fox]
    proj_ref, carry_ref = refs[-2:]
    _inproj_into(x_ref, g_ref, sh_ref, sc_ref, w_ref, proj_ref, u_ref)
    cols = [proj_ref.at[:, :, c0:c1] for c0, c1 in _IN_GROUPS]
    _mla_prep_kernel(cols[1], cols[2], cols[3], *mla_rest, *outs[:3])
    _fox_prep_kernel(cols[4], cols[5], cols[6], cols[3], *fox_rest, *outs[3:], carry_ref)


def _front_call(x, g, sh, sc, w, mla_rest, fox_rest):
    tl = ROW_TILE
    row = lambda b, i: (b, i, 0)
    per_b = lambda b, i: (b, 0, 0)
    const = lambda b, i: (0, 0)
    mla_in, mla_out, mla_shape = _mla_prep_specs(tl)
    fox_in, fox_out, fox_shape = _fox_prep_specs(tl)
    mla_in, fox_in = mla_in[3:], fox_in[4:]
    outs = pl.pallas_call(
        functools.partial(_front_kernel, n_mla=len(mla_in), n_fox=len(fox_in)),
        grid=(BATCH, SEQ // tl),
        in_specs=[pl.BlockSpec((1, tl, D_MODEL), row),
                  pl.BlockSpec((1, D_MODEL), const),
                  pl.BlockSpec((1, 1, D_MODEL), per_b),
                  pl.BlockSpec((1, 1, D_MODEL), per_b),
                  pl.BlockSpec((D_MODEL, IN_PAD), const)] + mla_in + fox_in,
        out_specs=[pl.BlockSpec((tl, SSM_WIDTH), lambda b, i: (i, b))] + mla_out + fox_out,
        out_shape=[jax.ShapeDtypeStruct((SEQ, BATCH * SSM_WIDTH), F32)] + mla_shape + fox_shape,
        scratch_shapes=[pltpu.VMEM((1, tl, IN_PAD), F32), pltpu.VMEM((SUBLANE, LANE), F32)],
        compiler_params=_params(("arbitrary", "arbitrary")),
        name="front",
    )(x, g, sh, sc, w, *mla_rest, *fox_rest)
    return outs[0], outs[1:4], outs[4:]


def _fox_placements():
    pq = np.zeros((FOX_HEADS // 2, 2 * LANE, 2 * HEAD_PAD), np.float32)
    pk = np.zeros_like(pq)
    pv = np.zeros_like(pq)
    one_row = LANE + GATE_ONE_LANE
    for j in range(FOX_HEADS // 2):
        for hh in range(2):
            h, col0 = 2 * j + hh, hh * HEAD_PAD
            for d in range(FOX_HEAD_DIM):
                for p in (pq, pk, pv):
                    p[j, hh * FOX_HEAD_DIM + d, col0 + d] = 1.0
            pv[j, one_row, col0 + ONES_LANE] = 1.0
            for n, piece_lane in enumerate((0, GATE_MID_LANE, GATE_LO_LANE)):
                pq[j, LANE + piece_lane + h, col0 + Q_GATE_LANE + n] = 1.0
                pq[j, one_row, col0 + K_GATE_LANE + n] = 1.0
                pk[j, one_row, col0 + Q_GATE_LANE + n] = 1.0
                pk[j, LANE + piece_lane + h, col0 + K_GATE_LANE + n] = -1.0
    return tuple(jnp.asarray(p, BF16) for p in (pq, pk, pv))


def _fox_operands(bf, gq, gk):
    q_scale = LOG2E / math.sqrt(FOX_HEAD_DIM)
    return (_pad_lanes(bf[None, :], LANE), jnp.tile(gq * q_scale, FOX_HEADS)[None, :],
            jnp.tile(gk, FOX_HEADS)[None, :]) + _fox_placements()


def _flash_kernel(qa_ref, qb_ref, kt_ref, v_ref, gap_ref, o_ref, q_scr, s_ref, m_ref, acc_ref,
                  *, tile, chunk, n_tiles):
    p = pl.program_id(2)
    tiles = (p, n_tiles - 1 - p)
    n_tasks = n_tiles + 1
    half = tile // 2
    top, bottom = slice(0, half), slice(half, tile)
    lane = lax.broadcasted_iota(jnp.int32, (tile, HEAD_PAD), 1)
    q_scr[0] = qa_ref[0]
    q_scr[1] = qb_ref[0]

    def plain_task(t):
        second = t - 2 >= p
        return second, second.astype(jnp.int32), jnp.where(second, t - 2 - p, t - 2)

    def row_max_update(w, hh, rows, s):
        mr = m_ref[w, hh, rows]
        for c in range(s.shape[1] // LANE):
            mr = jnp.maximum(mr, s[:, c * LANE:(c + 1) * LANE])
        m_ref[w, hh, rows] = mr

    m_ref[...] = jnp.full(m_ref.shape, NEG, F32)
    for w in range(2):
        for hh in range(2):
            kt = kt_ref[0, hh, tiles[w]]
            s_top = jnp.dot(q_scr[w, hh, top], kt[:, top], preferred_element_type=F32)
            s_top = jnp.where(gap_ref[top, top] <= 0, s_top, NEG)
            s_ref[hh, w, top, top] = s_top
            row_max_update(w, hh, top, s_top)
            s_bot = jnp.dot(q_scr[w, hh, bottom], kt, preferred_element_type=F32)
            s_bot = jnp.where(gap_ref[bottom, :] <= 0, s_bot, NEG)
            s_ref[hh, w, bottom] = s_bot
            row_max_update(w, hh, bottom, s_bot)
    for t in range(2, n_tasks):
        _, which, j = plain_task(jnp.int32(t))
        for hh in range(2):
            s = jnp.dot(q_scr[which, hh], kt_ref[0, hh, j], preferred_element_type=F32)
            s_ref[hh, t] = s
            row_max_update(which, hh, slice(None), s)

    ms = [[jnp.max(m_ref[w, hh], axis=1, keepdims=True) for hh in range(2)] for w in range(2)]

    acc_ref[...] = jnp.zeros(acc_ref.shape, F32)
    for w in range(2):
        k0 = pl.multiple_of(tiles[w] * tile, tile)
        for hh in range(2):
            pr = jnp.exp2(s_ref[hh, w, top, top] - ms[w][hh][top]).astype(BF16)
            acc_ref[w, hh, top] += jnp.dot(pr, v_ref[0, hh, pl.ds(k0, half), :], preferred_element_type=F32)
            pr = jnp.exp2(s_ref[hh, w, bottom] - ms[w][hh][bottom]).astype(BF16)
            acc_ref[w, hh, bottom] += jnp.dot(pr, v_ref[0, hh, pl.ds(k0, tile), :], preferred_element_type=F32)
    for t in range(2, n_tasks):
        second, which, j = plain_task(jnp.int32(t))
        k0 = pl.multiple_of(j * tile, tile)
        for hh in range(2):
            row_max = jnp.where(second, ms[1][hh], ms[0][hh])
            pr = jnp.exp2(s_ref[hh, t] - row_max).astype(BF16)
            acc_ref[which, hh] += jnp.dot(pr, v_ref[0, hh, pl.ds(k0, tile), :], preferred_element_type=F32)

    for w in range(2):
        outs = [acc_ref[w, hh] / acc_ref[w, hh][:, ONES_LANE:ONES_LANE + 1] for hh in range(2)]
        o_ref[0, w, 0] = jnp.where(lane < 64, outs[0], pltpu.roll(outs[1], 64, 1)).astype(BF16)


def _flash_call(q, kt, v, chunk):
    tile = ATT_TILE
    heads = q.shape[1]
    n_tiles = SEQ // tile
    pos = np.arange(tile, dtype=np.int32) // chunk
    gap = jnp.asarray(pos[None, :] - pos[:, None])
    return pl.pallas_call(
        functools.partial(_flash_kernel, tile=tile, chunk=chunk, n_tiles=n_tiles),
        grid=(BATCH, heads // 2, n_tiles // 2),
        in_specs=[pl.BlockSpec((1, 2, tile, HEAD_PAD), lambda b, hp, p: (b, hp, p, 0)),
                  pl.BlockSpec((1, 2, tile, HEAD_PAD), lambda b, hp, p: (b, hp, n_tiles - 1 - p, 0)),
                  pl.BlockSpec((1, 2, n_tiles, HEAD_PAD, tile), lambda b, hp, p: (b, hp, 0, 0, 0)),
                  pl.BlockSpec((1, 2, SEQ, HEAD_PAD), lambda b, hp, p: (b, hp, 0, 0)),
                  pl.BlockSpec((tile, tile), lambda b, hp, p: (0, 0))],
        out_specs=pl.BlockSpec((1, 2, 1, tile, LANE), lambda b, hp, p: (b, 0, p, 0, hp)),
        out_shape=jax.ShapeDtypeStruct((BATCH, 2, n_tiles // 2, tile, ATT_WIDTH), BF16),
        scratch_shapes=[pltpu.VMEM((2, 2, tile, HEAD_PAD), BF16),
                        pltpu.VMEM((2, n_tiles + 1, tile, tile), F32),
                        pltpu.VMEM((2, 2, tile, LANE), F32),
                        pltpu.VMEM((2, 2, tile, HEAD_PAD), F32)],
        compiler_params=_params(("arbitrary", "arbitrary", "arbitrary")),
        name="flash_chunk%d" % chunk,
    )(q, q, kt, v, gap)


def _merge_kernel(ssm_ref, mla_ref, fox_ref, x_ref, g1_ref, gm_ref, gf_ref, w_ref, o_ref):
    def normed(ref, g_ref):
        a = ref[0, 0, 0].astype(F32)
        return (a * lax.rsqrt(jnp.mean(a * a, axis=-1, keepdims=True) + EPS) * g_ref[...]).astype(BF16)

    merged = jnp.concatenate([ssm_ref[...], normed(mla_ref, gm_ref), normed(fox_ref, gf_ref)], axis=1)
    mix = jnp.dot(merged, w_ref[...], preferred_element_type=F32)
    o_ref[0] = x_ref[0] + g1_ref[0] * mix


def _merge_call(o_ssm, o_mla, o_fox, x, g1, gm, gf, w):
    tm = ATT_TILE
    half = SEQ // tm // 2
    row = lambda b, i: (b, i, 0)
    c2 = lambda b, i: (0, 0)
    att = pl.BlockSpec((1, 1, 1, tm, ATT_WIDTH),
                       lambda b, i: (b, i // half, jnp.where(i < half, i, 2 * half - 1 - i), 0, 0))
    return pl.pallas_call(
        _merge_kernel,
        grid=(BATCH, SEQ // tm),
        in_specs=[pl.BlockSpec((tm, SSM_WIDTH), lambda b, i: (i, b)),
                  att, att,
                  pl.BlockSpec((1, tm, D_MODEL), row),
                  pl.BlockSpec((1, 1, D_MODEL), lambda b, i: (b, 0, 0)),
                  pl.BlockSpec((1, ATT_WIDTH), c2),
                  pl.BlockSpec((1, ATT_WIDTH), c2),
                  pl.BlockSpec((D_MODEL, D_MODEL), c2)],
        out_specs=pl.BlockSpec((1, tm, D_MODEL), row),
        out_shape=jax.ShapeDtypeStruct((BATCH, SEQ, D_MODEL), F32),
        compiler_params=_params(("arbitrary", "arbitrary")),
        name="merge",
    )(o_ssm, o_mla, o_fox, x, g1, gm, gf, w)


def _ffn_kernel(x_ref, g_ref, sh_ref, sc_ref, g2_ref, wg_ref, wu_ref, wd_ref, o_ref, h_ref, acc_ref):
    c = pl.program_id(2)

    @pl.when(c == 0)
    def _():
        h_ref[...] = _rms_mod(x_ref[0], g_ref[...], sc_ref[0], sh_ref[0]).astype(BF16)
        acc_ref[...] = jnp.zeros_like(acc_ref)

    h = h_ref[...]
    gate = jnp.dot(h, wg_ref[0], preferred_element_type=F32)
    up = jnp.dot(h, wu_ref[0], preferred_element_type=F32)
    a = (gate * jax.nn.sigmoid(gate) * up).astype(BF16)
    acc_ref[...] += jnp.dot(a, wd_ref[0], preferred_element_type=F32)

    @pl.when(c == pl.num_programs(2) - 1)
    def _():
        o_ref[0] = x_ref[0] + g2_ref[0] * acc_ref[...]


def _ffn_call(x, g, sh, sc, g2, wg, wu, wd):
    tm = MOE_TILE
    n_chunks = wg.shape[0]
    row = lambda b, i, c: (b, i, 0)
    per_b = lambda b, i, c: (b, 0, 0)
    chunk = lambda b, i, c: (c, 0, 0)
    return pl.pallas_call(
        _ffn_kernel,
        grid=(BATCH, SEQ // tm, n_chunks),
        in_specs=[pl.BlockSpec((1, tm, D_MODEL), row),
                  pl.BlockSpec((1, D_MODEL), lambda b, i, c: (0, 0)),
                  pl.BlockSpec((1, 1, D_MODEL), per_b),
                  pl.BlockSpec((1, 1, D_MODEL), per_b),
                  pl.BlockSpec((1, 1, D_MODEL), per_b),
                  pl.BlockSpec((1, D_MODEL, FF_CHUNK), chunk),
                  pl.BlockSpec((1, D_MODEL, FF_CHUNK), chunk),
                  pl.BlockSpec((1, FF_CHUNK, D_MODEL), chunk)],
        out_specs=pl.BlockSpec((1, tm, D_MODEL), row),
        out_shape=jax.ShapeDtypeStruct((BATCH, SEQ, D_MODEL), F32),
        scratch_shapes=[pltpu.VMEM((tm, D_MODEL), BF16), pltpu.VMEM((tm, D_MODEL), F32)],
        compiler_params=_params(("arbitrary", "arbitrary", "arbitrary")),
        name="ffn_dense",
    )(x, g, sh, sc, g2, wg, wu, wd)


def _router_kernel(x_ref, g_ref, sh_ref, sc_ref, w_ref, b_ref, comb_ref, rank_ref, rankt_ref, count_ref):
    tm = x_ref.shape[1]
    h = _rms_mod(x_ref[0], g_ref[...], sc_ref[0], sh_ref[0])
    h_hi = h.astype(BF16)
    h_lo = (h - h_hi.astype(F32)).astype(BF16)
    w_hi, w_lo = w_ref[0], w_ref[1]
    logits = (jnp.dot(h_hi, w_hi, preferred_element_type=F32)
              + jnp.dot(h_lo, w_hi, preferred_element_type=F32)
              + jnp.dot(h_hi, w_lo, preferred_element_type=F32)) + b_ref[...]
    lane = lax.broadcasted_iota(jnp.int32, logits.shape, 1)
    logits = jnp.where(lane < N_EXPERTS, logits, -jnp.inf)
    m1 = jnp.max(logits, axis=-1, keepdims=True)
    i1 = jnp.min(jnp.where(logits == m1, lane, LANE), axis=-1, keepdims=True)
    rest = jnp.where(lane == i1, -jnp.inf, logits)
    m2 = jnp.max(rest, axis=-1, keepdims=True)
    i2 = jnp.min(jnp.where(rest == m2, lane, LANE), axis=-1, keepdims=True)
    e = jnp.exp(m2 - m1)
    p1 = 1.0 / (1.0 + e)
    comb_ref[0] = jnp.where(lane == i1, p1, 0.0) + jnp.where(lane == i2, e * p1, 0.0)

    chosen = (lane == i1) | (lane == i2)
    chosen_f = jnp.where(chosen, 1.0, 0.0)
    r_i = lax.broadcasted_iota(jnp.int32, (tm, tm), 0)
    c_i = lax.broadcasted_iota(jnp.int32, (tm, tm), 1)
    earlier = jnp.where(c_i < r_i, 1.0, 0.0).astype(BF16)
    rank = jnp.dot(earlier, chosen_f.astype(BF16), preferred_element_type=F32)
    rank = jnp.where(chosen, rank, -1.0)
    rank_ref[0] = rank
    rankt_ref[0] = rank.T[0:SUBLANE, :]
    count_ref[0] = jnp.sum(chosen_f, axis=0, keepdims=True)


def _router_call(x, g, sh, sc, w, b):
    tm = MOE_TILE
    tiles = SEQ // tm
    row = lambda b_, i: (b_, i, 0)
    per_b = lambda b_, i: (b_, 0, 0)
    per_tile = lambda b_, i: (b_ * tiles + i, 0, 0)
    return pl.pallas_call(
        _router_kernel,
        grid=(BATCH, tiles),
        in_specs=[pl.BlockSpec((1, tm, D_MODEL), row),
                  pl.BlockSpec((1, D_MODEL), lambda b_, i: (0, 0)),
                  pl.BlockSpec((1, 1, D_MODEL), per_b),
                  pl.BlockSpec((1, 1, D_MODEL), per_b),
                  pl.BlockSpec((2, D_MODEL, LANE), lambda b_, i: (0, 0, 0)),
                  pl.BlockSpec((1, LANE), lambda b_, i: (0, 0))],
        out_specs=[pl.BlockSpec((1, tm, LANE), row),
                   pl.BlockSpec((1, tm, LANE), row),
                   pl.BlockSpec((1, SUBLANE, tm), per_tile),
                   pl.BlockSpec((1, 1, LANE), per_tile)],
        out_shape=[jax.ShapeDtypeStruct((BATCH, SEQ, LANE), F32),
                   jax.ShapeDtypeStruct((BATCH, SEQ, LANE), F32),
                   jax.ShapeDtypeStruct((BATCH * tiles, SUBLANE, tm), F32),
                   jax.ShapeDtypeStruct((BATCH * tiles, 1, LANE), F32)],
        compiler_params=_params(("arbitrary", "arbitrary")),
        name="router",
    )(x, g, sh, sc, w, b)


def _moe_kernel(count_ref, x_ref, g_ref, sh_ref, sc_ref, g2_ref, comb_ref, rank_ref, rankt_ref,
                wg_ref, wu_ref, wd_ref, o_ref, h_ref):
    tm = x_ref.shape[1]
    e = pl.program_id(1)

    @pl.when(e == 0)
    def _():
        x = x_ref[0]
        h_ref[...] = _rms_mod(x, g_ref[...], sc_ref[0], sh_ref[0]).astype(BF16)
        o_ref[0] = x

    lane = lax.broadcasted_iota(jnp.int32, (tm, LANE), 1)
    mine = lane == e
    rank_col = jnp.sum(jnp.where(mine, rank_ref[0], 0.0), axis=-1, keepdims=True)
    gate_col = jnp.sum(jnp.where(mine, comb_ref[0], 0.0), axis=-1, keepdims=True)
    rank_row = rankt_ref[0, pl.ds(e, 1), :]
    count = count_ref[pl.program_id(0) * N_EXPERTS + e]

    def expert_pass(first, n_rows):
        base = first.astype(F32)
        slot_sub = lax.broadcasted_iota(jnp.int32, (n_rows, tm), 0).astype(F32)
        slot_lane = lax.broadcasted_iota(jnp.int32, (tm, n_rows), 1).astype(F32)
        pick = jnp.where(rank_row - base == slot_sub, 1.0, 0.0).astype(BF16)
        rows = jnp.dot(pick, h_ref[...], preferred_element_type=F32).astype(BF16)
        gate = jnp.dot(rows, wg_ref[0], preferred_element_type=F32)
        up = jnp.dot(rows, wu_ref[0], preferred_element_type=F32)
        a = (gate * jax.nn.sigmoid(gate) * up).astype(BF16)
        y = jnp.dot(a, wd_ref[0], preferred_element_type=F32).astype(BF16)
        place = jnp.where(rank_col - base == slot_lane, 1.0, 0.0).astype(BF16)
        back = jnp.dot(place, y, preferred_element_type=F32)
        o_ref[0] += g2_ref[0] * (gate_col * back)

    def full_pass(sb, carry):
        expert_pass(sb * MOE_ROWS, MOE_ROWS)
        return carry

    n_full = count // MOE_ROWS
    lax.fori_loop(0, n_full, full_pass, 0)
    left = count - n_full * MOE_ROWS

    @pl.when(left > MOE_ROWS // 2)
    def _():
        expert_pass(n_full * MOE_ROWS, MOE_ROWS)

    @pl.when((left > 0) & (left <= MOE_ROWS // 2))
    def _():
        expert_pass(n_full * MOE_ROWS, MOE_ROWS // 2)


def _moe_call(x, g, sh, sc, g2, comb, rank, rankt, counts, wg, wu, wd):
    tm = MOE_TILE
    tiles = SEQ // tm
    n_tiles = BATCH * tiles
    row = lambda i, e, cnt: (i, 0, 0)
    per_b = lambda i, e, cnt: (i // tiles, 0, 0)
    expert = lambda i, e, cnt: (e, 0, 0)
    as_tiles = lambda a: a.reshape(n_tiles, tm, a.shape[-1])
    grid_spec = pltpu.PrefetchScalarGridSpec(
        num_scalar_prefetch=1,
        grid=(n_tiles, N_EXPERTS),
        in_specs=[pl.BlockSpec((1, tm, D_MODEL), row),
                  pl.BlockSpec((1, D_MODEL), lambda i, e, cnt: (0, 0)),
                  pl.BlockSpec((1, 1, D_MODEL), per_b),
                  pl.BlockSpec((1, 1, D_MODEL), per_b),
                  pl.BlockSpec((1, 1, D_MODEL), per_b),
                  pl.BlockSpec((1, tm, LANE), row),
                  pl.BlockSpec((1, tm, LANE), row),
                  pl.BlockSpec((1, SUBLANE, tm), row),
                  pl.BlockSpec((1, D_MODEL, D_FF_EXPERT), expert),
                  pl.BlockSpec((1, D_MODEL, D_FF_EXPERT), expert),
                  pl.BlockSpec((1, D_FF_EXPERT, D_MODEL), expert)],
        out_specs=pl.BlockSpec((1, tm, D_MODEL), row),
        scratch_shapes=[pltpu.VMEM((tm, D_MODEL), BF16)],
    )
    out = pl.pallas_call(
        _moe_kernel,
        grid_spec=grid_spec,
        out_shape=jax.ShapeDtypeStruct((n_tiles, tm, D_MODEL), F32),
        compiler_params=_params(("arbitrary", "arbitrary")),
        name="moe_experts",
    )(counts, as_tiles(x), g, sh, sc, g2, as_tiles(comb), as_tiles(rank), rankt, wg, wu, wd)
    return out.reshape(BATCH, SEQ, D_MODEL)


def kernel(x, c, positions, norm_mix, norm_ffn, w_ada, b_ada, w_in, ssm_lam_re, ssm_lam_im, ssm_log_dt, ssm_b_re, ssm_b_im, ssm_c_re, ssm_c_im, ssm_d, ssm_w_glu, ssm_b_glu, mla_q_norm, mla_kv_norm, mla_w_uq, mla_w_ukv, mla_qk_gq, mla_qk_gk, fox_b_f, fox_qk_gq, fox_qk_gk, out_norm, w_out, ffn_w_gate, ffn_w_up, ffn_w_down, moe_w_router, moe_b_router, moe_w_gate, moe_w_up, moe_w_down):
    tabs = _rope_tables(positions)
    ada = _ada_call(c, w_ada, b_ada)
    ada = ada.reshape(DEPTH, BATCH, 6, 1, D_MODEL)
    row2 = lambda a: a[None, :]

    for i in range(DEPTH):
        sh1, sc1, g1, sh2, sc2, g2 = (ada[i, :, n] for n in range(6))

        u, mla_qkv, fox_qkv = _front_call(
            x, row2(norm_mix[i]), sh1, sc1, _pack_w_in(w_in[i]),
            (*tabs, row2(mla_q_norm[i]), row2(mla_kv_norm[i]),
             *_mla_weights(mla_w_uq[i], mla_w_ukv[i], mla_qk_gq[i], mla_qk_gk[i])),
            _fox_operands(fox_b_f[i], fox_qk_gq[i], fox_qk_gk[i]))

        bmat, lam, cmat = _s5_operands(ssm_lam_re[i], ssm_lam_im[i], ssm_log_dt[i],
                                       ssm_b_re[i], ssm_b_im[i], ssm_c_re[i], ssm_c_im[i])
        o_ssm = _s5_call(u, bmat, lam, cmat, row2(ssm_d[i]), ssm_w_glu[i].astype(BF16),
                         row2(ssm_b_glu[i]), row2(out_norm[i, :SSM_WIDTH]))

        o_mla = _flash_call(*mla_qkv, CHUNK)
        o_fox = _flash_call(*fox_qkv, 1)

        e1, e2 = SSM_WIDTH, SSM_WIDTH + ATT_WIDTH
        x = _merge_call(o_ssm, o_mla, o_fox, x, g1, row2(out_norm[i, e1:e2]), row2(out_norm[i, e2:]),
                        w_out[i].astype(BF16))

        j = i // 2
        if i % 2 == 0:
            split = lambda w: w.reshape(D_MODEL, D_FF // FF_CHUNK, FF_CHUNK).transpose(1, 0, 2).astype(BF16)
            wd = ffn_w_down[j].reshape(D_FF // FF_CHUNK, FF_CHUNK, D_MODEL).astype(BF16)
            x = _ffn_call(x, row2(norm_ffn[i]), sh2, sc2, g2, split(ffn_w_gate[j]), split(ffn_w_up[j]), wd)
        else:
            wr = _pad_lanes(moe_w_router[j], LANE)
            wr_hi = wr.astype(BF16)
            wr_lo = (wr - wr_hi.astype(F32)).astype(BF16)
            comb, rank, rankt, counts = _router_call(x, row2(norm_ffn[i]), sh2, sc2, jnp.stack([wr_hi, wr_lo]),
                                                     _pad_lanes(row2(moe_b_router[j]), LANE))
            counts = counts[:, 0, :N_EXPERTS].astype(jnp.int32).reshape(-1)
            x = _moe_call(x, row2(norm_ffn[i]), sh2, sc2, g2, comb, rank, rankt, counts,
                          moe_w_gate[j].astype(BF16), moe_w_up[j].astype(BF16), moe_w_down[j].astype(BF16))
    return x
```

```python
import functools
import math

import jax
import jax.numpy as jnp
import numpy as np
from jax import lax
from jax.experimental import pallas as pl
from jax.experimental.pallas import tpu as pltpu

F32 = jnp.float32
BF16 = jnp.bfloat16

D_MODEL = 1024
BATCH = 8
SEQ = 4096
DEPTH = 4
CHUNK = 64
EPS = 1e-6

SSM_WIDTH = 256
SSM_GROUP = 16
N_SSM_GROUPS = 16
SSM_STATE = 64
N_STATE = N_SSM_GROUPS * SSM_STATE

MLA_HEADS = 6
MLA_Q_RANK = 256
MLA_KV_RANK = 128
MLA_NOPE = 64
MLA_ROPE = 32
MLA_V = 64
MLA_QK = 96
ROPE_BASE = 10000.0

FOX_HEADS = 6
FOX_HEAD_DIM = 64
ATT_WIDTH = 384

D_FF = 2816
N_EXPERTS = 8
D_FF_EXPERT = 1408

LANE = 128
SUBLANE = 8
HEAD_PAD = LANE
ONES_LANE = 64
NEG = -1e30

IN_PAD = 1920
KR_LANE = 64

ROW_TILE = 512
S5_STEPS = 64
ATT_TILE = 512
LOG2E = math.log2(math.e)
FF_CHUNK = 1408
MOE_TILE = 1024
MOE_ROWS = 256
VMEM_LIMIT = 56 * 1024 * 1024


def _params(sem):
    return pltpu.CompilerParams(dimension_semantics=sem, vmem_limit_bytes=VMEM_LIMIT)


def _rms_mod(x, g, sc, sh):
    ms = jnp.mean(x * x, axis=-1, keepdims=True)
    h = x * lax.rsqrt(ms + EPS) * g
    return h * (1.0 + sc) + sh


def _split3(x):
    hi = x.astype(BF16).astype(F32)
    r = x - hi
    mid = r.astype(BF16).astype(F32)
    lo = (r - mid).astype(BF16).astype(F32)
    return hi, mid, lo


def _ada_kernel(c_ref, w_ref, b_ref, o_ref):
    c = c_ref[...]
    ca = (c * jax.nn.sigmoid(c)).astype(BF16)
    o_ref[0] = jnp.dot(ca, w_ref[0].astype(BF16), preferred_element_type=F32) + b_ref[0]


def _ada_call(c, w_ada, b_ada):
    tn = 1536
    return pl.pallas_call(
        _ada_kernel,
        grid=(DEPTH, 6 * D_MODEL // tn),
        in_specs=[pl.BlockSpec((BATCH, D_MODEL), lambda i, j: (0, 0)),
                  pl.BlockSpec((1, D_MODEL, tn), lambda i, j: (i, 0, j)),
                  pl.BlockSpec((1, 1, tn), lambda i, j: (i, 0, j))],
        out_specs=pl.BlockSpec((1, BATCH, tn), lambda i, j: (i, 0, j)),
        out_shape=jax.ShapeDtypeStruct((DEPTH, BATCH, 6 * D_MODEL), F32),
        compiler_params=_params(("arbitrary", "arbitrary")),
        name="ada",
    )(c, w_ada, b_ada.reshape(DEPTH, 1, 6 * D_MODEL))


_IN_GROUPS = ((0, 256), (256, 512), (512, 640), (640, 768), (768, 1152), (1152, 1536), (1536, 1920))


def _inproj_into(x_ref, g_ref, sh_ref, sc_ref, w_ref, proj_ref, u_ref):
    parts = 4
    step = x_ref.shape[1] // parts
    rows = [slice(r * step, (r + 1) * step) for r in range(parts)]
    normed = lambda r: _rms_mod(x_ref[0, rows[r]], g_ref[...], sc_ref[0], sh_ref[0]).astype(BF16)
    h_next = normed(0)
    for r in range(parts):
        h = h_next
        if r + 1 < parts:
            h_next = normed(r + 1)
        proj = jnp.dot(h, w_ref[0], preferred_element_type=F32)
        proj_ref[0, rows[r]] = proj
        u_ref[rows[r]] = proj[:, _IN_GROUPS[0][0]:_IN_GROUPS[0][1]]


def _pack_w_in(w):
    u, cq, ckv, kr, fq, fk, fv, fg = jnp.split(
        w.astype(BF16), (256, 512, 640, 672, 1056, 1440, 1824), axis=2)
    z = lambda n: jnp.zeros(w.shape[:2] + (n,), BF16)
    krfg = jnp.concatenate([fg, z(KR_LANE - FOX_HEADS), kr, z(LANE - KR_LANE - MLA_ROPE)], axis=2)
    return jnp.concatenate([u, cq, ckv, krfg, fq, fk, fv], axis=2)


def _s5_kernel(u2_ref, bmat_ref, lam_ref, cmat_ref, d_ref, wglu_ref, bglu_ref, gn_ref,
               o2_ref, u_ref, o_ref, bu0_ref, bu1_ref, state_ref, *, steps):
    rows = steps * BATCH
    lane_tiles = SSM_WIDTH // LANE
    for b in range(BATCH):
        for c in range(lane_tiles):
            lanes = slice(b * SSM_WIDTH + c * LANE, b * SSM_WIDTH + (c + 1) * LANE)
            u_ref.at[c][pl.ds(b, 2 * steps, stride=BATCH), :] = u2_ref[:, lanes]
    u_rows = lambda rs: jnp.concatenate([u_ref[c, rs, :] for c in range(lane_tiles)], axis=1)

    @pl.when(pl.program_id(0) == 0)
    def _():
        state_ref[...] = jnp.zeros_like(state_ref)

    halves = ((bu0_ref, slice(0, rows)), (bu1_ref, slice(rows, 2 * rows)))
    for bu_ref, rs in halves:
        bu_ref[...] = jnp.dot(u_rows(rs).astype(BF16), bmat_ref[...], preferred_element_type=F32)
    lr = jnp.broadcast_to(lam_ref[0:1, :], (SUBLANE, N_STATE))
    li = jnp.broadcast_to(lam_ref[1:2, :], (SUBLANE, N_STATE))
    sr, si = state_ref[:, 0:N_STATE], state_ref[:, N_STATE:2 * N_STATE]

    for bu_ref, rs in halves:
        for t in range(steps):
            r = slice(t * SUBLANE, (t + 1) * SUBLANE)
            nr = lr * sr - li * si + bu_ref[r, 0:N_STATE]
            ni = lr * si + li * sr + bu_ref[r, N_STATE:2 * N_STATE]
            bu_ref[r, 0:N_STATE] = nr
            bu_ref[r, N_STATE:2 * N_STATE] = ni
            sr, si = nr, ni
        y = jnp.dot(bu_ref[...].astype(BF16), cmat_ref[...], preferred_element_type=F32)
        y = jax.nn.gelu(y + d_ref[...] * u_rows(rs))
        gate = jnp.dot(y.astype(BF16), wglu_ref[...], preferred_element_type=F32) + bglu_ref[...]
        o = y * jax.nn.sigmoid(gate)
        ms = jnp.mean(o * o, axis=-1, keepdims=True)
        o = o * lax.rsqrt(ms + EPS) * gn_ref[...]
        for c in range(lane_tiles):
            o_ref[c, rs, :] = o[:, c * LANE:(c + 1) * LANE]

    state_ref[:, 0:N_STATE] = sr
    state_ref[:, N_STATE:2 * N_STATE] = si
    for b in range(BATCH):
        for c in range(lane_tiles):
            lanes = slice(b * SSM_WIDTH + c * LANE, b * SSM_WIDTH + (c + 1) * LANE)
            o2_ref[:, lanes] = o_ref.at[c][pl.ds(b, 2 * steps, stride=BATCH), :].astype(BF16)


def _s5_call(u2, bmat, lam, cmat, d_skip, wglu, bglu, gn):
    rows = S5_STEPS * BATCH
    const = lambda i: (0, 0)
    return pl.pallas_call(
        functools.partial(_s5_kernel, steps=S5_STEPS),
        grid=(SEQ // (2 * S5_STEPS),),
        in_specs=[pl.BlockSpec((2 * S5_STEPS, BATCH * SSM_WIDTH), lambda i: (i, 0)),
                  pl.BlockSpec((SSM_WIDTH, 2 * N_STATE), const),
                  pl.BlockSpec((2, N_STATE), const),
                  pl.BlockSpec((2 * N_STATE, SSM_WIDTH), const),
                  pl.BlockSpec((1, SSM_WIDTH), const),
                  pl.BlockSpec((SSM_WIDTH, SSM_WIDTH), const),
                  pl.BlockSpec((1, SSM_WIDTH), const),
                  pl.BlockSpec((1, SSM_WIDTH), const)],
        out_specs=pl.BlockSpec((2 * S5_STEPS, BATCH * SSM_WIDTH), lambda i: (i, 0)),
        out_shape=jax.ShapeDtypeStruct((SEQ, BATCH * SSM_WIDTH), BF16),
        scratch_shapes=[pltpu.VMEM((SSM_WIDTH // LANE, 2 * rows, LANE), F32),
                        pltpu.VMEM((SSM_WIDTH // LANE, 2 * rows, LANE), F32),
                        pltpu.VMEM((rows, 2 * N_STATE), F32),
                        pltpu.VMEM((rows, 2 * N_STATE), F32),
                        pltpu.VMEM((SUBLANE, 2 * N_STATE), F32)],
        compiler_params=_params(("arbitrary",)),
        name="s5",
    )(u2, bmat, lam, cmat, d_skip, wglu, bglu, gn)


def _s5_operands(lam_re, lam_im, log_dt, b_re, b_im, c_re, c_im):
    dt = jnp.exp(log_dt)[:, None]
    mag = jnp.exp(lam_re * dt)
    lb_re = mag * jnp.cos(lam_im * dt)
    lb_im = mag * jnp.sin(lam_im * dt)
    den = lam_re * lam_re + lam_im * lam_im
    co_re = ((lb_re - 1.0) * lam_re + lb_im * lam_im) / den
    co_im = (lb_im * lam_re - (lb_re - 1.0) * lam_im) / den
    bb_re = co_re[..., None] * b_re - co_im[..., None] * b_im
    bb_im = co_re[..., None] * b_im + co_im[..., None] * b_re
    eye = jnp.eye(N_SSM_GROUPS, dtype=F32)
    blk_b = lambda m: jnp.einsum("gpc,gh->gchp", m, eye).reshape(SSM_WIDTH, N_STATE)
    bmat = jnp.concatenate([blk_b(bb_re), blk_b(bb_im)], axis=1).astype(BF16)
    blk_c = lambda m: jnp.einsum("gcp,gh->gphc", m, eye).reshape(N_STATE, SSM_WIDTH)
    cmat = jnp.concatenate([blk_c(c_re), -blk_c(c_im)], axis=0).astype(BF16)
    lam = jnp.stack([lb_re.reshape(N_STATE), lb_im.reshape(N_STATE)], axis=0)
    return bmat, lam, cmat


def _rope_tables(positions):
    half = MLA_ROPE // 2
    inv = ROPE_BASE ** (-jnp.arange(half, dtype=F32) / half)
    ang = inv[:, None] * positions.astype(F32).reshape(1, -1)
    shp = positions.shape
    cos, sin = (lax.optimization_barrier(f(ang)).T.reshape(shp + (half,)) for f in (jnp.cos, jnp.sin))
    one = lambda n: jnp.ones(shp + (n,), F32)
    zero = lambda n: jnp.zeros(shp + (n,), F32)
    cos_t = jnp.concatenate([one(MLA_NOPE), cos, cos, zero(LANE - MLA_QK)], axis=-1)
    sin_t = jnp.concatenate([zero(MLA_NOPE), -sin, sin, zero(LANE - MLA_QK)], axis=-1)
    return cos_t, sin_t


def _swap_rope_halves(a):
    half = MLA_ROPE // 2
    lo, hi = a[..., MLA_NOPE:MLA_NOPE + half], a[..., MLA_NOPE + half:MLA_QK]
    return jnp.concatenate([jnp.zeros_like(a[..., :MLA_NOPE]), hi, lo, jnp.zeros_like(a[..., MLA_QK:])], axis=-1)


def _store_key_blocks(kt_ref, h, k):
    kt = k.T
    for s in range(k.shape[0] // ATT_TILE):
        kt_ref[0, h, s] = kt[:, s * ATT_TILE:(s + 1) * ATT_TILE].astype(BF16)


_KT_SPEC = lambda heads, tl: pl.BlockSpec((1, heads, tl // ATT_TILE, HEAD_PAD, ATT_TILE),
                                          lambda b, i: (b, 0, i, 0, 0))
_KT_SHAPE = lambda heads: jax.ShapeDtypeStruct((BATCH, heads, SEQ // ATT_TILE, HEAD_PAD, ATT_TILE), BF16)


def _mla_prep_kernel(cq_ref, ckv_ref, krfg_ref, cos_ref, sin_ref, qn_ref, kvn_ref, wq_ref, wk_ref, wv_ref,
                     gq_ref, gqs_ref, gk_ref, gks_ref, q_ref, k_ref, v_ref):
    tl = cq_ref.shape[1]
    lane = lax.broadcasted_iota(jnp.int32, (tl, LANE), 1)
    cos, sin = cos_ref[0], sin_ref[0]
    q_scale = LOG2E / math.sqrt(MLA_QK)
    q_cos, q_sin = gq_ref[...] * cos * q_scale, gqs_ref[...] * sin * q_scale
    k_cos, k_sin = gk_ref[...] * cos, gks_ref[...] * sin
    ones = jnp.ones((LANE, LANE), BF16)

    def inv_rms(x):
        ss = jnp.dot((x * x).astype(BF16), ones, preferred_element_type=F32)
        return lax.rsqrt(ss / MLA_QK + EPS)

    cq = cq_ref[0]
    cqn = (cq * lax.rsqrt(jnp.mean(cq * cq, axis=-1, keepdims=True) + EPS) * qn_ref[...]).astype(BF16)
    ckv = ckv_ref[0]
    ckvn = (ckv * lax.rsqrt(jnp.mean(ckv * ckv, axis=-1, keepdims=True) + EPS) * kvn_ref[...]).astype(BF16)
    kr = jnp.where((lane >= KR_LANE) & (lane < KR_LANE + MLA_ROPE), krfg_ref[0], 0.0)
    kr_swapped = jnp.where(lane < KR_LANE + MLA_ROPE // 2, pltpu.roll(kr, LANE - 16, 1), pltpu.roll(kr, 16, 1))
    k_rotary = kr_swapped * k_sin

    heads = range(MLA_HEADS)
    qqs = [jnp.dot(cqn, wq_ref[h], preferred_element_type=F32) for h in heads]
    ks = [jnp.dot(ckvn, wk_ref[h], preferred_element_type=F32) + kr for h in heads]
    q_inv = [inv_rms(qq[:, :LANE]) for qq in qqs]
    k_inv = [inv_rms(k) for k in ks]
    for h in heads:
        q, q_swapped = qqs[h][:, :LANE], qqs[h][:, LANE:]
        q_ref[0, h] = (q_inv[h] * (q * q_cos + q_swapped * q_sin)).astype(BF16)
        _store_key_blocks(k_ref, h, k_inv[h] * (ks[h] * k_cos + k_rotary))
        v = jnp.dot(ckvn, wv_ref[h], preferred_element_type=F32)
        v_ref[0, h] = jnp.where(lane == ONES_LANE, 1.0, v).astype(BF16)


def _mla_prep_specs(tl):
    row = lambda b, i: (b, i, 0)
    c2 = lambda b, i: (0, 0)
    c3 = lambda b, i: (0, 0, 0)
    head_out = pl.BlockSpec((1, MLA_HEADS, tl, HEAD_PAD), lambda b, i: (b, 0, i, 0))
    head_shape = jax.ShapeDtypeStruct((BATCH, MLA_HEADS, SEQ, HEAD_PAD), BF16)
    gain = pl.BlockSpec((1, HEAD_PAD), c2)
    in_specs = [pl.BlockSpec((1, tl, MLA_Q_RANK), row),
                pl.BlockSpec((1, tl, MLA_KV_RANK), row),
                pl.BlockSpec((1, tl, LANE), row),
                pl.BlockSpec((1, tl, LANE), row),
                pl.BlockSpec((1, tl, LANE), row),
                pl.BlockSpec((1, MLA_Q_RANK), c2),
                pl.BlockSpec((1, MLA_KV_RANK), c2),
                pl.BlockSpec((MLA_HEADS, MLA_Q_RANK, 2 * HEAD_PAD), c3),
                pl.BlockSpec((MLA_HEADS, MLA_KV_RANK, HEAD_PAD), c3),
                pl.BlockSpec((MLA_HEADS, MLA_KV_RANK, HEAD_PAD), c3),
                gain, gain, gain, gain]
    return (in_specs, [head_out, _KT_SPEC(MLA_HEADS, tl), head_out],
            [head_shape, _KT_SHAPE(MLA_HEADS), head_shape])


def _pad_lanes(a, n=HEAD_PAD):
    return jnp.pad(a, [(0, 0)] * (a.ndim - 1) + [(0, n - a.shape[-1])])


def _mla_weights(w_uq, w_ukv, gq, gk):
    wq = _pad_lanes(w_uq.reshape(MLA_Q_RANK, MLA_HEADS, MLA_QK).transpose(1, 0, 2))
    wq = jnp.concatenate([wq, _swap_rope_halves(wq)], axis=-1).astype(BF16)
    wkv = w_ukv.reshape(MLA_KV_RANK, MLA_HEADS, MLA_NOPE + MLA_V).transpose(1, 0, 2)
    wk = _pad_lanes(wkv[..., :MLA_NOPE]).astype(BF16)
    wv = _pad_lanes(wkv[..., MLA_NOPE:]).astype(BF16)
    gq, gk = _pad_lanes(gq[None, :]), _pad_lanes(gk[None, :])
    return wq, wk, wv, gq, _swap_rope_halves(gq), gk, _swap_rope_halves(gk)


GATE_MID_LANE = 8
GATE_LO_LANE = 16
GATE_ONE_LANE = LANE - 1
Q_GATE_LANE = FOX_HEAD_DIM
K_GATE_LANE = FOX_HEAD_DIM + 3


def _fox_prep_kernel(fq_ref, fk_ref, fv_ref, krfg_ref, bf_ref, gq_ref, gk_ref, pq_ref, pk_ref, pv_ref,
                     q_ref, k_ref, v_ref, carry_ref):
    tl = fq_ref.shape[1]
    lane = lax.broadcasted_iota(jnp.int32, (tl, LANE), 1)

    @pl.when(pl.program_id(1) == 0)
    def _():
        carry_ref[...] = jnp.zeros_like(carry_ref)

    logf = jax.nn.log_sigmoid(krfg_ref[0] + bf_ref[...])
    logf = jnp.where(lane < FOX_HEADS, logf, 0.0)
    r_i = lax.broadcasted_iota(jnp.int32, (tl, tl), 0)
    c_i = lax.broadcasted_iota(jnp.int32, (tl, tl), 1)
    tri = jnp.where(c_i <= r_i, 1.0, 0.0).astype(BF16)
    cum = carry_ref[0:1, :]
    for piece in _split3(logf):
        cum = cum + jnp.dot(tri, piece.astype(BF16), preferred_element_type=F32)
    carry_ref[0:1, :] = cum[tl - 1:tl, :]

    c_hi, c_mid, c_lo = _split3(cum * LOG2E)
    gate_row = (c_hi + pltpu.roll(c_mid, GATE_MID_LANE, 1) + pltpu.roll(c_lo, GATE_LO_LANE, 1)
                + jnp.where(lane == GATE_ONE_LANE, 1.0, 0.0)).astype(BF16)

    p_r = lax.broadcasted_iota(jnp.int32, (LANE, LANE), 0)
    p_c = lax.broadcasted_iota(jnp.int32, (LANE, LANE), 1)
    head_mean = jnp.where(p_r // FOX_HEAD_DIM == p_c // FOX_HEAD_DIM, 1.0 / FOX_HEAD_DIM, 0.0).astype(BF16)

    def mean_sq(ref, j):
        x = ref[0, :, j * LANE:(j + 1) * LANE]
        return jnp.dot((x * x).astype(BF16), head_mean, preferred_element_type=F32)

    def normed(ref, g_ref, j, ms):
        lanes = slice(j * LANE, (j + 1) * LANE)
        return (ref[0, :, lanes] * lax.rsqrt(ms + EPS) * g_ref[:, lanes]).astype(BF16)

    def placed(x, p_ref, j):
        return jnp.dot(jnp.concatenate([x, gate_row], axis=1), p_ref[j], preferred_element_type=F32)

    pairs = range(FOX_HEADS // 2)
    q_ms = [mean_sq(fq_ref, j) for j in pairs]
    k_ms = [mean_sq(fk_ref, j) for j in pairs]
    q_n = [normed(fq_ref, gq_ref, j, q_ms[j]) for j in pairs]
    k_n = [normed(fk_ref, gk_ref, j, k_ms[j]) for j in pairs]
    for j in pairs:
        q = placed(q_n[j], pq_ref, j)
        k = placed(k_n[j], pk_ref, j)
        v = placed(fv_ref[0, :, j * LANE:(j + 1) * LANE].astype(BF16), pv_ref, j)
        for hh in range(2):
            head = slice(hh * HEAD_PAD, (hh + 1) * HEAD_PAD)
            q_ref[0, 2 * j + hh] = q[:, head].astype(BF16)
            _store_key_blocks(k_ref, 2 * j + hh, k[:, head])
            v_ref[0, 2 * j + hh] = v[:, head].astype(BF16)


def _fox_prep_specs(tl):
    row = lambda b, i: (b, i, 0)
    c2 = lambda b, i: (0, 0)
    c3 = lambda b, i: (0, 0, 0)
    head_out = pl.BlockSpec((1, FOX_HEADS, tl, HEAD_PAD), lambda b, i: (b, 0, i, 0))
    head_shape = jax.ShapeDtypeStruct((BATCH, FOX_HEADS, SEQ, HEAD_PAD), BF16)
    place = pl.BlockSpec((FOX_HEADS // 2, 2 * LANE, 2 * HEAD_PAD), c3)
    in_specs = [pl.BlockSpec((1, tl, ATT_WIDTH), row),
                pl.BlockSpec((1, tl, ATT_WIDTH), row),
                pl.BlockSpec((1, tl, ATT_WIDTH), row),
                pl.BlockSpec((1, tl, LANE), row),
                pl.BlockSpec((1, LANE), c2),
                pl.BlockSpec((1, ATT_WIDTH), c2),
                pl.BlockSpec((1, ATT_WIDTH), c2),
                place, place, place]
    return (in_specs, [head_out, _KT_SPEC(FOX_HEADS, tl), head_out],
            [head_shape, _KT_SHAPE(FOX_HEADS), head_shape])


def _front_kernel(*refs, n_mla, n_fox):
    x_ref, g_ref, sh_ref, sc_ref, w_ref = refs[:5]
    mla_rest, fox_rest = refs[5:5 + n_mla], refs[5 + n_mla:5 + n_mla + n_fox]
    u_ref = refs[5 + n_mla + n_fox]
    outs = refs[6 + n_mla + n_fox:12 + n_mla + n_fox]
    proj_ref, carry_ref = refs[-2:]
    _inproj_into(x_ref, g_ref, sh_ref, sc_ref, w_ref, proj_ref, u_ref)
    cols = [proj_ref.at[:, :, c0:c1] for c0, c1 in _IN_GROUPS]
    _mla_prep_kernel(cols[1], cols[2], cols[3], *mla_rest, *outs[:3])
    _fox_prep_kernel(cols[4], cols[5], cols[6], cols[3], *fox_rest, *outs[3:], carry_ref)


def _front_call(x, g, sh, sc, w, layer, mla_rest, fox_rest):
    tl = ROW_TILE
    row = lambda b, i: (b, i, 0)
    per_b = lambda b, i: (b, 0, 0)
    const = lambda b, i: (0, 0)
    mla_in, mla_out, mla_shape = _mla_prep_specs(tl)
    fox_in, fox_out, fox_shape = _fox_prep_specs(tl)
    mla_in, fox_in = mla_in[3:], fox_in[4:]
    outs = pl.pallas_call(
        functools.partial(_front_kernel, n_mla=len(mla_in), n_fox=len(fox_in)),
        grid=(BATCH, SEQ // tl),
        in_specs=[pl.BlockSpec((1, tl, D_MODEL), row),
                  pl.BlockSpec((1, D_MODEL), const),
                  pl.BlockSpec((1, 1, D_MODEL), per_b),
                  pl.BlockSpec((1, 1, D_MODEL), per_b),
                  pl.BlockSpec((1, D_MODEL, IN_PAD), lambda b, i: (layer, 0, 0))] + mla_in + fox_in,
        out_specs=[pl.BlockSpec((tl, SSM_WIDTH), lambda b, i: (i, b))] + mla_out + fox_out,
        out_shape=[jax.ShapeDtypeStruct((SEQ, BATCH * SSM_WIDTH), F32)] + mla_shape + fox_shape,
        scratch_shapes=[pltpu.VMEM((1, tl, IN_PAD), F32), pltpu.VMEM((SUBLANE, LANE), F32)],
        compiler_params=_params(("arbitrary", "arbitrary")),
        name="front",
    )(x, g, sh, sc, w, *mla_rest, *fox_rest)
    return outs[0], outs[1:4], outs[4:]


def _fox_placements():
    pq = np.zeros((FOX_HEADS // 2, 2 * LANE, 2 * HEAD_PAD), np.float32)
    pk = np.zeros_like(pq)
    pv = np.zeros_like(pq)
    one_row = LANE + GATE_ONE_LANE
    for j in range(FOX_HEADS // 2):
        for hh in range(2):
            h, col0 = 2 * j + hh, hh * HEAD_PAD
            for d in range(FOX_HEAD_DIM):
                for p in (pq, pk, pv):
                    p[j, hh * FOX_HEAD_DIM + d, col0 + d] = 1.0
            pv[j, one_row, col0 + ONES_LANE] = 1.0
            for n, piece_lane in enumerate((0, GATE_MID_LANE, GATE_LO_LANE)):
                pq[j, LANE + piece_lane + h, col0 + Q_GATE_LANE + n] = 1.0
                pq[j, one_row, col0 + K_GATE_LANE + n] = 1.0
                pk[j, one_row, col0 + Q_GATE_LANE + n] = 1.0
                pk[j, LANE + piece_lane + h, col0 + K_GATE_LANE + n] = -1.0
    return tuple(jnp.asarray(p, BF16) for p in (pq, pk, pv))


def _fox_operands(bf, gq, gk):
    q_scale = LOG2E / math.sqrt(FOX_HEAD_DIM)
    return (_pad_lanes(bf[None, :], LANE), jnp.tile(gq * q_scale, FOX_HEADS)[None, :],
            jnp.tile(gk, FOX_HEADS)[None, :]) + _fox_placements()


def _flash_kernel(qa_ref, qb_ref, kt_ref, v_ref, gap_ref, o_ref, q_scr, s_ref, m_ref, acc_ref,
                  *, tile, chunk, n_tiles):
    p = pl.program_id(2)
    tiles = (p, n_tiles - 1 - p)
    n_tasks = n_tiles + 1
    half = tile // 2
    top, bottom = slice(0, half), slice(half, tile)
    lane = lax.broadcasted_iota(jnp.int32, (tile, HEAD_PAD), 1)
    q_scr[0] = qa_ref[0]
    q_scr[1] = qb_ref[0]

    def plain_task(t):
        second = t - 2 >= p
        return second, second.astype(jnp.int32), jnp.where(second, t - 2 - p, t - 2)

    def row_max_update(w, hh, rows, s):
        mr = m_ref[w, hh, rows]
        for c in range(s.shape[1] // LANE):
            mr = jnp.maximum(mr, s[:, c * LANE:(c + 1) * LANE])
        m_ref[w, hh, rows] = mr

    m_ref[...] = jnp.full(m_ref.shape, NEG, F32)
    for w in range(2):
        for hh in range(2):
            kt = kt_ref[0, hh, tiles[w]]
            s_top = jnp.dot(q_scr[w, hh, top], kt[:, top], preferred_element_type=F32)
            s_top = jnp.where(gap_ref[top, top] <= 0, s_top, NEG)
            s_ref[hh, w, top, top] = s_top
            row_max_update(w, hh, top, s_top)
            s_bot = jnp.dot(q_scr[w, hh, bottom], kt, preferred_element_type=F32)
            s_bot = jnp.where(gap_ref[bottom, :] <= 0, s_bot, NEG)
            s_ref[hh, w, bottom] = s_bot
            row_max_update(w, hh, bottom, s_bot)
    for t in range(2, n_tasks):
        _, which, j = plain_task(jnp.int32(t))
        for hh in range(2):
            s = jnp.dot(q_scr[which, hh], kt_ref[0, hh, j], preferred_element_type=F32)
            s_ref[hh, t] = s
            row_max_update(which, hh, slice(None), s)

    ms = [[jnp.max(m_ref[w, hh], axis=1, keepdims=True) for hh in range(2)] for w in range(2)]

    acc_ref[...] = jnp.zeros(acc_ref.shape, F32)
    for w in range(2):
        k0 = pl.multiple_of(tiles[w] * tile, tile)
        for hh in range(2):
            pr = jnp.exp2(s_ref[hh, w, top, top] - ms[w][hh][top]).astype(BF16)
            acc_ref[w, hh, top] += jnp.dot(pr, v_ref[0, hh, pl.ds(k0, half), :], preferred_element_type=F32)
            pr = jnp.exp2(s_ref[hh, w, bottom] - ms[w][hh][bottom]).astype(BF16)
            acc_ref[w, hh, bottom] += jnp.dot(pr, v_ref[0, hh, pl.ds(k0, tile), :], preferred_element_type=F32)
    for t in range(2, n_tasks):
        second, which, j = plain_task(jnp.int32(t))
        k0 = pl.multiple_of(j * tile, tile)
        for hh in range(2):
            row_max = jnp.where(second, ms[1][hh], ms[0][hh])
            pr = jnp.exp2(s_ref[hh, t] - row_max).astype(BF16)
            acc_ref[which, hh] += jnp.dot(pr, v_ref[0, hh, pl.ds(k0, tile), :], preferred_element_type=F32)

    for w in range(2):
        outs = [acc_ref[w, hh] / acc_ref[w, hh][:, ONES_LANE:ONES_LANE + 1] for hh in range(2)]
        o_ref[0, w, 0] = jnp.where(lane < 64, outs[0], pltpu.roll(outs[1], 64, 1)).astype(BF16)


def _flash_call(q, kt, v, chunk):
    tile = ATT_TILE
    heads = q.shape[1]
    n_tiles = SEQ // tile
    pos = np.arange(tile, dtype=np.int32) // chunk
    gap = jnp.asarray(pos[None, :] - pos[:, None])
    return pl.pallas_call(
        functools.partial(_flash_kernel, tile=tile, chunk=chunk, n_tiles=n_tiles),
        grid=(BATCH, heads // 2, n_tiles // 2),
        in_specs=[pl.BlockSpec((1, 2, tile, HEAD_PAD), lambda b, hp, p: (b, hp, p, 0)),
                  pl.BlockSpec((1, 2, tile, HEAD_PAD), lambda b, hp, p: (b, hp, n_tiles - 1 - p, 0)),
                  pl.BlockSpec((1, 2, n_tiles, HEAD_PAD, tile), lambda b, hp, p: (b, hp, 0, 0, 0)),
                  pl.BlockSpec((1, 2, SEQ, HEAD_PAD), lambda b, hp, p: (b, hp, 0, 0)),
                  pl.BlockSpec((tile, tile), lambda b, hp, p: (0, 0))],
        out_specs=pl.BlockSpec((1, 2, 1, tile, LANE), lambda b, hp, p: (b, 0, p, 0, hp)),
        out_shape=jax.ShapeDtypeStruct((BATCH, 2, n_tiles // 2, tile, ATT_WIDTH), BF16),
        scratch_shapes=[pltpu.VMEM((2, 2, tile, HEAD_PAD), BF16),
                        pltpu.VMEM((2, n_tiles + 1, tile, tile), F32),
                        pltpu.VMEM((2, 2, tile, LANE), F32),
                        pltpu.VMEM((2, 2, tile, HEAD_PAD), F32)],
        compiler_params=_params(("arbitrary", "arbitrary", "arbitrary")),
        name="flash_chunk%d" % chunk,
    )(q, q, kt, v, gap)


def _merge_kernel(ssm_ref, mla_ref, fox_ref, x_ref, g1_ref, gm_ref, gf_ref, w_ref, o_ref):
    def normed(ref, g_ref):
        a = ref[0, 0, 0].astype(F32)
        return (a * lax.rsqrt(jnp.mean(a * a, axis=-1, keepdims=True) + EPS) * g_ref[...]).astype(BF16)

    merged = jnp.concatenate([ssm_ref[...], normed(mla_ref, gm_ref), normed(fox_ref, gf_ref)], axis=1)
    mix = jnp.dot(merged, w_ref[...], preferred_element_type=F32)
    o_ref[0] = x_ref[0] + g1_ref[0] * mix


def _merge_call(o_ssm, o_mla, o_fox, x, g1, gm, gf, w):
    tm = ATT_TILE
    half = SEQ // tm // 2
    row = lambda b, i: (b, i, 0)
    c2 = lambda b, i: (0, 0)
    att = pl.BlockSpec((1, 1, 1, tm, ATT_WIDTH),
                       lambda b, i: (b, i // half, jnp.where(i < half, i, 2 * half - 1 - i), 0, 0))
    return pl.pallas_call(
        _merge_kernel,
        grid=(BATCH, SEQ // tm),
        in_specs=[pl.BlockSpec((tm, SSM_WIDTH), lambda b, i: (i, b)),
                  att, att,
                  pl.BlockSpec((1, tm, D_MODEL), row),
                  pl.BlockSpec((1, 1, D_MODEL), lambda b, i: (b, 0, 0)),
                  pl.BlockSpec((1, ATT_WIDTH), c2),
                  pl.BlockSpec((1, ATT_WIDTH), c2),
                  pl.BlockSpec((D_MODEL, D_MODEL), c2)],
        out_specs=pl.BlockSpec((1, tm, D_MODEL), row),
        out_shape=jax.ShapeDtypeStruct((BATCH, SEQ, D_MODEL), F32),
        compiler_params=_params(("arbitrary", "arbitrary")),
        name="merge",
    )(o_ssm, o_mla, o_fox, x, g1, gm, gf, w)


def _ffn_kernel(x_ref, g_ref, sh_ref, sc_ref, g2_ref, wg_ref, wu_ref, wd_ref, o_ref, h_ref, acc_ref):
    c = pl.program_id(2)

    @pl.when(c == 0)
    def _():
        h_ref[...] = _rms_mod(x_ref[0], g_ref[...], sc_ref[0], sh_ref[0]).astype(BF16)
        acc_ref[...] = jnp.zeros_like(acc_ref)

    h = h_ref[...]
    gate = jnp.dot(h, wg_ref[0], preferred_element_type=F32)
    up = jnp.dot(h, wu_ref[0], preferred_element_type=F32)
    a = (gate * jax.nn.sigmoid(gate) * up).astype(BF16)
    acc_ref[...] += jnp.dot(a, wd_ref[0], preferred_element_type=F32)

    @pl.when(c == pl.num_programs(2) - 1)
    def _():
        o_ref[0] = x_ref[0] + g2_ref[0] * acc_ref[...]


def _ffn_call(x, g, sh, sc, g2, wg, wu, wd):
    tm = MOE_TILE
    n_chunks = wg.shape[0]
    row = lambda b, i, c: (b, i, 0)
    per_b = lambda b, i, c: (b, 0, 0)
    chunk = lambda b, i, c: (c, 0, 0)
    return pl.pallas_call(
        _ffn_kernel,
        grid=(BATCH, SEQ // tm, n_chunks),
        in_specs=[pl.BlockSpec((1, tm, D_MODEL), row),
                  pl.BlockSpec((1, D_MODEL), lambda b, i, c: (0, 0)),
                  pl.BlockSpec((1, 1, D_MODEL), per_b),
                  pl.BlockSpec((1, 1, D_MODEL), per_b),
                  pl.BlockSpec((1, 1, D_MODEL), per_b),
                  pl.BlockSpec((1, D_MODEL, FF_CHUNK), chunk),
                  pl.BlockSpec((1, D_MODEL, FF_CHUNK), chunk),
                  pl.BlockSpec((1, FF_CHUNK, D_MODEL), chunk)],
        out_specs=pl.BlockSpec((1, tm, D_MODEL), row),
        out_shape=jax.ShapeDtypeStruct((BATCH, SEQ, D_MODEL), F32),
        scratch_shapes=[pltpu.VMEM((tm, D_MODEL), BF16), pltpu.VMEM((tm, D_MODEL), F32)],
        compiler_params=_params(("arbitrary", "arbitrary", "arbitrary")),
        name="ffn_dense",
    )(x, g, sh, sc, g2, wg, wu, wd)


def _router_kernel(x_ref, g_ref, sh_ref, sc_ref, w_ref, b_ref, comb_ref, rank_ref, rankt_ref, count_ref):
    tm = x_ref.shape[1]
    h = _rms_mod(x_ref[0], g_ref[...], sc_ref[0], sh_ref[0])
    h_hi = h.astype(BF16)
    h_lo = (h - h_hi.astype(F32)).astype(BF16)
    w_hi, w_lo = w_ref[0], w_ref[1]
    logits = (jnp.dot(h_hi, w_hi, preferred_element_type=F32)
              + jnp.dot(h_lo, w_hi, preferred_element_type=F32)
              + jnp.dot(h_hi, w_lo, preferred_element_type=F32)) + b_ref[...]
    lane = lax.broadcasted_iota(jnp.int32, logits.shape, 1)
    logits = jnp.where(lane < N_EXPERTS, logits, -jnp.inf)
    m1 = jnp.max(logits, axis=-1, keepdims=True)
    i1 = jnp.min(jnp.where(logits == m1, lane, LANE), axis=-1, keepdims=True)
    rest = jnp.where(lane == i1, -jnp.inf, logits)
    m2 = jnp.max(rest, axis=-1, keepdims=True)
    i2 = jnp.min(jnp.where(rest == m2, lane, LANE), axis=-1, keepdims=True)
    e = jnp.exp(m2 - m1)
    p1 = 1.0 / (1.0 + e)
    comb_ref[0] = jnp.where(lane == i1, p1, 0.0) + jnp.where(lane == i2, e * p1, 0.0)

    chosen = (lane == i1) | (lane == i2)
    chosen_f = jnp.where(chosen, 1.0, 0.0)
    r_i = lax.broadcasted_iota(jnp.int32, (tm, tm), 0)
    c_i = lax.broadcasted_iota(jnp.int32, (tm, tm), 1)
    earlier = jnp.where(c_i < r_i, 1.0, 0.0).astype(BF16)
    rank = jnp.dot(earlier, chosen_f.astype(BF16), preferred_element_type=F32)
    rank = jnp.where(chosen, rank, -1.0)
    rank_ref[0] = rank
    rankt_ref[0] = rank.T[0:SUBLANE, :]
    count_ref[0] = jnp.sum(chosen_f, axis=0, keepdims=True)


def _router_call(x, g, sh, sc, w, b):
    tm = MOE_TILE
    tiles = SEQ // tm
    row = lambda b_, i: (b_, i, 0)
    per_b = lambda b_, i: (b_, 0, 0)
    per_tile = lambda b_, i: (b_ * tiles + i, 0, 0)
    return pl.pallas_call(
        _router_kernel,
        grid=(BATCH, tiles),
        in_specs=[pl.BlockSpec((1, tm, D_MODEL), row),
                  pl.BlockSpec((1, D_MODEL), lambda b_, i: (0, 0)),
                  pl.BlockSpec((1, 1, D_MODEL), per_b),
                  pl.BlockSpec((1, 1, D_MODEL), per_b),
                  pl.BlockSpec((2, D_MODEL, LANE), lambda b_, i: (0, 0, 0)),
                  pl.BlockSpec((1, LANE), lambda b_, i: (0, 0))],
        out_specs=[pl.BlockSpec((1, tm, LANE), row),
                   pl.BlockSpec((1, tm, LANE), row),
                   pl.BlockSpec((1, SUBLANE, tm), per_tile),
                   pl.BlockSpec((1, 1, LANE), per_tile)],
        out_shape=[jax.ShapeDtypeStruct((BATCH, SEQ, LANE), F32),
                   jax.ShapeDtypeStruct((BATCH, SEQ, LANE), F32),
                   jax.ShapeDtypeStruct((BATCH * tiles, SUBLANE, tm), F32),
                   jax.ShapeDtypeStruct((BATCH * tiles, 1, LANE), F32)],
        compiler_params=_params(("arbitrary", "arbitrary")),
        name="router",
    )(x, g, sh, sc, w, b)


def _moe_kernel(count_ref, x_ref, g_ref, sh_ref, sc_ref, g2_ref, comb_ref, rank_ref, rankt_ref,
                wg_ref, wu_ref, wd_ref, o_ref, h_ref):
    tm = x_ref.shape[1]
    e = pl.program_id(1)

    @pl.when(e == 0)
    def _():
        x = x_ref[0]
        h_ref[...] = _rms_mod(x, g_ref[...], sc_ref[0], sh_ref[0]).astype(BF16)
        o_ref[0] = x

    lane = lax.broadcasted_iota(jnp.int32, (tm, LANE), 1)
    mine = lane == e
    rank_col = jnp.sum(jnp.where(mine, rank_ref[0], 0.0), axis=-1, keepdims=True)
    gate_col = jnp.sum(jnp.where(mine, comb_ref[0], 0.0), axis=-1, keepdims=True)
    rank_row = rankt_ref[0, pl.ds(e, 1), :]
    count = count_ref[pl.program_id(0) * N_EXPERTS + e]

    def expert_pass(first, n_rows):
        base = first.astype(F32)
        slot_sub = lax.broadcasted_iota(jnp.int32, (n_rows, tm), 0).astype(F32)
        slot_lane = lax.broadcasted_iota(jnp.int32, (tm, n_rows), 1).astype(F32)
        pick = jnp.where(rank_row - base == slot_sub, 1.0, 0.0).astype(BF16)
        rows = jnp.dot(pick, h_ref[...], preferred_element_type=F32).astype(BF16)
        gate = jnp.dot(rows, wg_ref[0, 0], preferred_element_type=F32)
        up = jnp.dot(rows, wu_ref[0, 0], preferred_element_type=F32)
        a = (gate * jax.nn.sigmoid(gate) * up).astype(BF16)
        y = jnp.dot(a, wd_ref[0, 0], preferred_element_type=F32).astype(BF16)
        place = jnp.where(rank_col - base == slot_lane, 1.0, 0.0).astype(BF16)
        back = jnp.dot(place, y, preferred_element_type=F32)
        o_ref[0] += g2_ref[0] * (gate_col * back)

    def full_pass(sb, carry):
        expert_pass(sb * MOE_ROWS, MOE_ROWS)
        return carry

    n_full = count // MOE_ROWS
    lax.fori_loop(0, n_full, full_pass, 0)
    left = count - n_full * MOE_ROWS

    @pl.when(left > MOE_ROWS // 2)
    def _():
        expert_pass(n_full * MOE_ROWS, MOE_ROWS)

    @pl.when((left > 0) & (left <= MOE_ROWS // 2))
    def _():
        expert_pass(n_full * MOE_ROWS, MOE_ROWS // 2)


def _moe_call(x, g, sh, sc, g2, comb, rank, rankt, counts, wg, wu, wd, layer):
    tm = MOE_TILE
    tiles = SEQ // tm
    n_tiles = BATCH * tiles
    row = lambda i, e, cnt: (i, 0, 0)
    per_b = lambda i, e, cnt: (i // tiles, 0, 0)
    expert = lambda i, e, cnt: (layer, e, 0, 0)
    as_tiles = lambda a: a.reshape(n_tiles, tm, a.shape[-1])
    grid_spec = pltpu.PrefetchScalarGridSpec(
        num_scalar_prefetch=1,
        grid=(n_tiles, N_EXPERTS),
        in_specs=[pl.BlockSpec((1, tm, D_MODEL), row),
                  pl.BlockSpec((1, D_MODEL), lambda i, e, cnt: (0, 0)),
                  pl.BlockSpec((1, 1, D_MODEL), per_b),
                  pl.BlockSpec((1, 1, D_MODEL), per_b),
                  pl.BlockSpec((1, 1, D_MODEL), per_b),
                  pl.BlockSpec((1, tm, LANE), row),
                  pl.BlockSpec((1, tm, LANE), row),
                  pl.BlockSpec((1, SUBLANE, tm), row),
                  pl.BlockSpec((1, 1, D_MODEL, D_FF_EXPERT), expert),
                  pl.BlockSpec((1, 1, D_MODEL, D_FF_EXPERT), expert),
                  pl.BlockSpec((1, 1, D_FF_EXPERT, D_MODEL), expert)],
        out_specs=pl.BlockSpec((1, tm, D_MODEL), row),
        scratch_shapes=[pltpu.VMEM((tm, D_MODEL), BF16)],
    )
    out = pl.pallas_call(
        _moe_kernel,
        grid_spec=grid_spec,
        out_shape=jax.ShapeDtypeStruct((n_tiles, tm, D_MODEL), F32),
        compiler_params=_params(("arbitrary", "arbitrary")),
        name="moe_experts",
    )(counts, as_tiles(x), g, sh, sc, g2, as_tiles(comb), as_tiles(rank), rankt, wg, wu, wd)
    return out.reshape(BATCH, SEQ, D_MODEL)


def kernel(x, c, positions, norm_mix, norm_ffn, w_ada, b_ada, w_in, ssm_lam_re, ssm_lam_im, ssm_log_dt, ssm_b_re, ssm_b_im, ssm_c_re, ssm_c_im, ssm_d, ssm_w_glu, ssm_b_glu, mla_q_norm, mla_kv_norm, mla_w_uq, mla_w_ukv, mla_qk_gq, mla_qk_gk, fox_b_f, fox_qk_gq, fox_qk_gk, out_norm, w_out, ffn_w_gate, ffn_w_up, ffn_w_down, moe_w_router, moe_b_router, moe_w_gate, moe_w_up, moe_w_down):
    tabs = _rope_tables(positions)
    w_in_packed = _pack_w_in(w_in)
    moe_wg, moe_wu, moe_wd = (w.astype(BF16) for w in (moe_w_gate, moe_w_up, moe_w_down))
    ada = _ada_call(c, w_ada, b_ada)
    ada = ada.reshape(DEPTH, BATCH, 6, 1, D_MODEL)
    row2 = lambda a: a[None, :]

    for i in range(DEPTH):
        sh1, sc1, g1, sh2, sc2, g2 = (ada[i, :, n] for n in range(6))

        u, mla_qkv, fox_qkv = _front_call(
            x, row2(norm_mix[i]), sh1, sc1, w_in_packed, i,
            (*tabs, row2(mla_q_norm[i]), row2(mla_kv_norm[i]),
             *_mla_weights(mla_w_uq[i], mla_w_ukv[i], mla_qk_gq[i], mla_qk_gk[i])),
            _fox_operands(fox_b_f[i], fox_qk_gq[i], fox_qk_gk[i]))

        bmat, lam, cmat = _s5_operands(ssm_lam_re[i], ssm_lam_im[i], ssm_log_dt[i],
                                       ssm_b_re[i], ssm_b_im[i], ssm_c_re[i], ssm_c_im[i])
        o_ssm = _s5_call(u, bmat, lam, cmat, row2(ssm_d[i]), ssm_w_glu[i].astype(BF16),
                         row2(ssm_b_glu[i]), row2(out_norm[i, :SSM_WIDTH]))

        o_mla = _flash_call(*mla_qkv, CHUNK)
        o_fox = _flash_call(*fox_qkv, 1)

        e1, e2 = SSM_WIDTH, SSM_WIDTH + ATT_WIDTH
        x = _merge_call(o_ssm, o_mla, o_fox, x, g1, row2(out_norm[i, e1:e2]), row2(out_norm[i, e2:]),
                        w_out[i].astype(BF16))

        j = i // 2
        if i % 2 == 0:
            split = lambda w: w.reshape(D_MODEL, D_FF // FF_CHUNK, FF_CHUNK).transpose(1, 0, 2).astype(BF16)
            wd = ffn_w_down[j].reshape(D_FF // FF_CHUNK, FF_CHUNK, D_MODEL).astype(BF16)
            x = _ffn_call(x, row2(norm_ffn[i]), sh2, sc2, g2, split(ffn_w_gate[j]), split(ffn_w_up[j]), wd)
        else:
            wr = _pad_lanes(moe_w_router[j], LANE)
            wr_hi = wr.astype(BF16)
            wr_lo = (wr - wr_hi.astype(F32)).astype(BF16)
            comb, rank, rankt, counts = _router_call(x, row2(norm_ffn[i]), sh2, sc2, jnp.stack([wr_hi, wr_lo]),
                                                     _pad_lanes(row2(moe_b_router[j]), LANE))
            counts = counts[:, 0, :N_EXPERTS].astype(jnp.int32).reshape(-1)
            x = _moe_call(x, row2(norm_ffn[i]), sh2, sc2, g2, comb, rank, rankt, counts,
                          moe_wg, moe_wu, moe_wd, layer=j)
    return x
```

```python
import functools
import math

import jax
import jax.numpy as jnp
import numpy as np
from jax import lax
from jax.experimental import pallas as pl
from jax.experimental.pallas import tpu as pltpu

F32 = jnp.float32
BF16 = jnp.bfloat16

D_MODEL = 1024
BATCH = 8
SEQ = 4096
DEPTH = 4
CHUNK = 64
EPS = 1e-6

SSM_WIDTH = 256
SSM_GROUP = 16
N_SSM_GROUPS = 16
SSM_STATE = 64
N_STATE = N_SSM_GROUPS * SSM_STATE

MLA_HEADS = 6
MLA_Q_RANK = 256
MLA_KV_RANK = 128
MLA_NOPE = 64
MLA_ROPE = 32
MLA_V = 64
MLA_QK = 96
ROPE_BASE = 10000.0

FOX_HEADS = 6
FOX_HEAD_DIM = 64
ATT_WIDTH = 384

D_FF = 2816
N_EXPERTS = 8
D_FF_EXPERT = 1408

LANE = 128
SUBLANE = 8
HEAD_PAD = LANE
ONES_LANE = 64
NEG = -1e30

IN_PAD = 1920
KR_LANE = 64

ROW_TILE = 512
S5_STEPS = 64
ATT_TILE = 512
LOG2E = math.log2(math.e)
FF_PARTS = ((0, 1536), (1536, D_FF))
MOE_TILE = 1024
MOE_ROWS = 256
VMEM_LIMIT = 56 * 1024 * 1024


def _params(sem):
    return pltpu.CompilerParams(dimension_semantics=sem, vmem_limit_bytes=VMEM_LIMIT)


def _rms_mod(x, g, sc, sh):
    ms = jnp.mean(x * x, axis=-1, keepdims=True)
    h = x * lax.rsqrt(ms + EPS) * g
    return h * (1.0 + sc) + sh


def _split3(x):
    hi = x.astype(BF16).astype(F32)
    r = x - hi
    mid = r.astype(BF16).astype(F32)
    lo = (r - mid).astype(BF16).astype(F32)
    return hi, mid, lo


def _ada_kernel(c_ref, w_ref, b_ref, o_ref):
    c = c_ref[...]
    ca = (c * jax.nn.sigmoid(c)).astype(BF16)
    o_ref[0] = jnp.dot(ca, w_ref[0].astype(BF16), preferred_element_type=F32) + b_ref[0]


def _ada_call(c, w_ada, b_ada):
    tn = 1536
    return pl.pallas_call(
        _ada_kernel,
        grid=(DEPTH, 6 * D_MODEL // tn),
        in_specs=[pl.BlockSpec((BATCH, D_MODEL), lambda i, j: (0, 0)),
                  pl.BlockSpec((1, D_MODEL, tn), lambda i, j: (i, 0, j)),
                  pl.BlockSpec((1, 1, tn), lambda i, j: (i, 0, j))],
        out_specs=pl.BlockSpec((1, BATCH, tn), lambda i, j: (i, 0, j)),
        out_shape=jax.ShapeDtypeStruct((DEPTH, BATCH, 6 * D_MODEL), F32),
        compiler_params=_params(("arbitrary", "arbitrary")),
        name="ada",
    )(c, w_ada, b_ada.reshape(DEPTH, 1, 6 * D_MODEL))


_IN_GROUPS = ((0, 256), (256, 512), (512, 640), (640, 768), (768, 1152), (1152, 1536), (1536, 1920))


def _inproj_into(x_ref, g_ref, sh_ref, sc_ref, w_ref, proj_ref, u_ref):
    parts = 4
    step = x_ref.shape[1] // parts
    rows = [slice(r * step, (r + 1) * step) for r in range(parts)]
    normed = lambda r: _rms_mod(x_ref[0, rows[r]], g_ref[...], sc_ref[0], sh_ref[0]).astype(BF16)
    h_next = normed(0)
    for r in range(parts):
        h = h_next
        if r + 1 < parts:
            h_next = normed(r + 1)
        proj = jnp.dot(h, w_ref[0], preferred_element_type=F32)
        proj_ref[0, rows[r]] = proj
        u_ref[rows[r]] = proj[:, _IN_GROUPS[0][0]:_IN_GROUPS[0][1]]


def _pack_w_in(w):
    u, cq, ckv, kr, fq, fk, fv, fg = jnp.split(
        w.astype(BF16), (256, 512, 640, 672, 1056, 1440, 1824), axis=2)
    z = lambda n: jnp.zeros(w.shape[:2] + (n,), BF16)
    krfg = jnp.concatenate([fg, z(KR_LANE - FOX_HEADS), kr, z(LANE - KR_LANE - MLA_ROPE)], axis=2)
    return jnp.concatenate([u, cq, ckv, krfg, fq, fk, fv], axis=2)


def _s5_kernel(u2_ref, bmat_ref, lam_ref, cmat_ref, d_ref, wglu_ref, bglu_ref, gn_ref,
               o2_ref, u_ref, o_ref, bu0_ref, bu1_ref, state_ref, *, steps):
    rows = steps * BATCH
    lane_tiles = SSM_WIDTH // LANE
    for b in range(BATCH):
        for c in range(lane_tiles):
            lanes = slice(b * SSM_WIDTH + c * LANE, b * SSM_WIDTH + (c + 1) * LANE)
            u_ref.at[c][pl.ds(b, 2 * steps, stride=BATCH), :] = u2_ref[:, lanes]
    u_rows = lambda rs: jnp.concatenate([u_ref[c, rs, :] for c in range(lane_tiles)], axis=1)

    @pl.when(pl.program_id(0) == 0)
    def _():
        state_ref[...] = jnp.zeros_like(state_ref)

    halves = ((bu0_ref, slice(0, rows)), (bu1_ref, slice(rows, 2 * rows)))
    for bu_ref, rs in halves:
        bu_ref[...] = jnp.dot(u_rows(rs).astype(BF16), bmat_ref[...], preferred_element_type=F32)
    lr = jnp.broadcast_to(lam_ref[0:1, :], (SUBLANE, N_STATE))
    li = jnp.broadcast_to(lam_ref[1:2, :], (SUBLANE, N_STATE))
    sr, si = state_ref[:, 0:N_STATE], state_ref[:, N_STATE:2 * N_STATE]

    for bu_ref, rs in halves:
        for t in range(steps):
            r = slice(t * SUBLANE, (t + 1) * SUBLANE)
            nr = lr * sr - li * si + bu_ref[r, 0:N_STATE]
            ni = lr * si + li * sr + bu_ref[r, N_STATE:2 * N_STATE]
            bu_ref[r, 0:N_STATE] = nr
            bu_ref[r, N_STATE:2 * N_STATE] = ni
            sr, si = nr, ni
        y = jnp.dot(bu_ref[...].astype(BF16), cmat_ref[...], preferred_element_type=F32)
        y = jax.nn.gelu(y + d_ref[...] * u_rows(rs))
        gate = jnp.dot(y.astype(BF16), wglu_ref[...], preferred_element_type=F32) + bglu_ref[...]
        o = y * jax.nn.sigmoid(gate)
        ms = jnp.mean(o * o, axis=-1, keepdims=True)
        o = o * lax.rsqrt(ms + EPS) * gn_ref[...]
        for c in range(lane_tiles):
            o_ref[c, rs, :] = o[:, c * LANE:(c + 1) * LANE]

    state_ref[:, 0:N_STATE] = sr
    state_ref[:, N_STATE:2 * N_STATE] = si
    for b in range(BATCH):
        for c in range(lane_tiles):
            lanes = slice(b * SSM_WIDTH + c * LANE, b * SSM_WIDTH + (c + 1) * LANE)
            o2_ref[:, lanes] = o_ref.at[c][pl.ds(b, 2 * steps, stride=BATCH), :].astype(BF16)


def _s5_call(u2, bmat, lam, cmat, d_skip, wglu, bglu, gn):
    rows = S5_STEPS * BATCH
    const = lambda i: (0, 0)
    return pl.pallas_call(
        functools.partial(_s5_kernel, steps=S5_STEPS),
        grid=(SEQ // (2 * S5_STEPS),),
        in_specs=[pl.BlockSpec((2 * S5_STEPS, BATCH * SSM_WIDTH), lambda i: (i, 0)),
                  pl.BlockSpec((SSM_WIDTH, 2 * N_STATE), const),
                  pl.BlockSpec((2, N_STATE), const),
                  pl.BlockSpec((2 * N_STATE, SSM_WIDTH), const),
                  pl.BlockSpec((1, SSM_WIDTH), const),
                  pl.BlockSpec((SSM_WIDTH, SSM_WIDTH), const),
                  pl.BlockSpec((1, SSM_WIDTH), const),
                  pl.BlockSpec((1, SSM_WIDTH), const)],
        out_specs=pl.BlockSpec((2 * S5_STEPS, BATCH * SSM_WIDTH), lambda i: (i, 0)),
        out_shape=jax.ShapeDtypeStruct((SEQ, BATCH * SSM_WIDTH), BF16),
        scratch_shapes=[pltpu.VMEM((SSM_WIDTH // LANE, 2 * rows, LANE), F32),
                        pltpu.VMEM((SSM_WIDTH // LANE, 2 * rows, LANE), F32),
                        pltpu.VMEM((rows, 2 * N_STATE), F32),
                        pltpu.VMEM((rows, 2 * N_STATE), F32),
                        pltpu.VMEM((SUBLANE, 2 * N_STATE), F32)],
        compiler_params=_params(("arbitrary",)),
        name="s5",
    )(u2, bmat, lam, cmat, d_skip, wglu, bglu, gn)


def _s5_operands(lam_re, lam_im, log_dt, b_re, b_im, c_re, c_im):
    dt = jnp.exp(log_dt)[:, None]
    mag = jnp.exp(lam_re * dt)
    lb_re = mag * jnp.cos(lam_im * dt)
    lb_im = mag * jnp.sin(lam_im * dt)
    den = lam_re * lam_re + lam_im * lam_im
    co_re = ((lb_re - 1.0) * lam_re + lb_im * lam_im) / den
    co_im = (lb_im * lam_re - (lb_re - 1.0) * lam_im) / den
    bb_re = co_re[..., None] * b_re - co_im[..., None] * b_im
    bb_im = co_re[..., None] * b_im + co_im[..., None] * b_re
    eye = jnp.eye(N_SSM_GROUPS, dtype=F32)
    blk_b = lambda m: jnp.einsum("gpc,gh->gchp", m, eye).reshape(SSM_WIDTH, N_STATE)
    bmat = jnp.concatenate([blk_b(bb_re), blk_b(bb_im)], axis=1).astype(BF16)
    blk_c = lambda m: jnp.einsum("gcp,gh->gphc", m, eye).reshape(N_STATE, SSM_WIDTH)
    cmat = jnp.concatenate([blk_c(c_re), -blk_c(c_im)], axis=0).astype(BF16)
    lam = jnp.stack([lb_re.reshape(N_STATE), lb_im.reshape(N_STATE)], axis=0)
    return bmat, lam, cmat


def _rope_tables(positions):
    half = MLA_ROPE // 2
    inv = ROPE_BASE ** (-jnp.arange(half, dtype=F32) / half)
    ang = inv[:, None] * positions.astype(F32).reshape(1, -1)
    shp = positions.shape
    cos, sin = (lax.optimization_barrier(f(ang)).T.reshape(shp + (half,)) for f in (jnp.cos, jnp.sin))
    one = lambda n: jnp.ones(shp + (n,), F32)
    zero = lambda n: jnp.zeros(shp + (n,), F32)
    cos_t = jnp.concatenate([one(MLA_NOPE), cos, cos, zero(LANE - MLA_QK)], axis=-1)
    sin_t = jnp.concatenate([zero(MLA_NOPE), -sin, sin, zero(LANE - MLA_QK)], axis=-1)
    return cos_t, sin_t


def _swap_rope_halves(a):
    half = MLA_ROPE // 2
    lo, hi = a[..., MLA_NOPE:MLA_NOPE + half], a[..., MLA_NOPE + half:MLA_QK]
    return jnp.concatenate([jnp.zeros_like(a[..., :MLA_NOPE]), hi, lo, jnp.zeros_like(a[..., MLA_QK:])], axis=-1)


def _store_key_blocks(kt_ref, h, k):
    kt = k.T
    for s in range(k.shape[0] // ATT_TILE):
        kt_ref[0, h, s] = kt[:, s * ATT_TILE:(s + 1) * ATT_TILE].astype(BF16)


_KT_SPEC = lambda heads, tl: pl.BlockSpec((1, heads, tl // ATT_TILE, HEAD_PAD, ATT_TILE),
                                          lambda b, i: (b, 0, i, 0, 0))
_KT_SHAPE = lambda heads: jax.ShapeDtypeStruct((BATCH, heads, SEQ // ATT_TILE, HEAD_PAD, ATT_TILE), BF16)


def _mla_prep_kernel(cq_ref, ckv_ref, krfg_ref, cos_ref, sin_ref, qn_ref, kvn_ref, wq_ref, wk_ref, wv_ref,
                     gq_ref, gqs_ref, gk_ref, gks_ref, q_ref, k_ref, v_ref):
    tl = cq_ref.shape[1]
    lane = lax.broadcasted_iota(jnp.int32, (tl, LANE), 1)
    cos, sin = cos_ref[0], sin_ref[0]
    q_scale = LOG2E / math.sqrt(MLA_QK)
    q_cos, q_sin = gq_ref[...] * cos * q_scale, gqs_ref[...] * sin * q_scale
    k_cos, k_sin = gk_ref[...] * cos, gks_ref[...] * sin
    ones = jnp.ones((LANE, LANE), BF16)

    def inv_rms(x):
        ss = jnp.dot((x * x).astype(BF16), ones, preferred_element_type=F32)
        return lax.rsqrt(ss / MLA_QK + EPS)

    cq = cq_ref[0]
    cqn = (cq * lax.rsqrt(jnp.mean(cq * cq, axis=-1, keepdims=True) + EPS) * qn_ref[...]).astype(BF16)
    ckv = ckv_ref[0]
    ckvn = (ckv * lax.rsqrt(jnp.mean(ckv * ckv, axis=-1, keepdims=True) + EPS) * kvn_ref[...]).astype(BF16)
    kr = jnp.where((lane >= KR_LANE) & (lane < KR_LANE + MLA_ROPE), krfg_ref[0], 0.0)
    kr_swapped = jnp.where(lane < KR_LANE + MLA_ROPE // 2, pltpu.roll(kr, LANE - 16, 1), pltpu.roll(kr, 16, 1))
    k_rotary = kr_swapped * k_sin

    heads = range(MLA_HEADS)
    qqs = [jnp.dot(cqn, wq_ref[h], preferred_element_type=F32) for h in heads]
    ks = [jnp.dot(ckvn, wk_ref[h], preferred_element_type=F32) + kr for h in heads]
    q_inv = [inv_rms(qq[:, :LANE]) for qq in qqs]
    k_inv = [inv_rms(k) for k in ks]
    for h in heads:
        q, q_swapped = qqs[h][:, :LANE], qqs[h][:, LANE:]
        q_ref[0, h] = (q_inv[h] * (q * q_cos + q_swapped * q_sin)).astype(BF16)
        _store_key_blocks(k_ref, h, k_inv[h] * (ks[h] * k_cos + k_rotary))
        v = jnp.dot(ckvn, wv_ref[h], preferred_element_type=F32)
        v_ref[0, h] = jnp.where(lane == ONES_LANE, 1.0, v).astype(BF16)


def _mla_prep_specs(tl):
    row = lambda b, i: (b, i, 0)
    c2 = lambda b, i: (0, 0)
    c3 = lambda b, i: (0, 0, 0)
    head_out = pl.BlockSpec((1, MLA_HEADS, tl, HEAD_PAD), lambda b, i: (b, 0, i, 0))
    head_shape = jax.ShapeDtypeStruct((BATCH, MLA_HEADS, SEQ, HEAD_PAD), BF16)
    gain = pl.BlockSpec((1, HEAD_PAD), c2)
    in_specs = [pl.BlockSpec((1, tl, MLA_Q_RANK), row),
                pl.BlockSpec((1, tl, MLA_KV_RANK), row),
                pl.BlockSpec((1, tl, LANE), row),
                pl.BlockSpec((1, tl, LANE), row),
                pl.BlockSpec((1, tl, LANE), row),
                pl.BlockSpec((1, MLA_Q_RANK), c2),
                pl.BlockSpec((1, MLA_KV_RANK), c2),
                pl.BlockSpec((MLA_HEADS, MLA_Q_RANK, 2 * HEAD_PAD), c3),
                pl.BlockSpec((MLA_HEADS, MLA_KV_RANK, HEAD_PAD), c3),
                pl.BlockSpec((MLA_HEADS, MLA_KV_RANK, HEAD_PAD), c3),
                gain, gain, gain, gain]
    return (in_specs, [head_out, _KT_SPEC(MLA_HEADS, tl), head_out],
            [head_shape, _KT_SHAPE(MLA_HEADS), head_shape])


def _pad_lanes(a, n=HEAD_PAD):
    return jnp.pad(a, [(0, 0)] * (a.ndim - 1) + [(0, n - a.shape[-1])])


def _mla_weights(w_uq, w_ukv, gq, gk):
    wq = _pad_lanes(w_uq.reshape(MLA_Q_RANK, MLA_HEADS, MLA_QK).transpose(1, 0, 2))
    wq = jnp.concatenate([wq, _swap_rope_halves(wq)], axis=-1).astype(BF16)
    wkv = w_ukv.reshape(MLA_KV_RANK, MLA_HEADS, MLA_NOPE + MLA_V).transpose(1, 0, 2)
    wk = _pad_lanes(wkv[..., :MLA_NOPE]).astype(BF16)
    wv = _pad_lanes(wkv[..., MLA_NOPE:]).astype(BF16)
    gq, gk = _pad_lanes(gq[None, :]), _pad_lanes(gk[None, :])
    return wq, wk, wv, gq, _swap_rope_halves(gq), gk, _swap_rope_halves(gk)


GATE_MID_LANE = 8
GATE_LO_LANE = 16
GATE_ONE_LANE = LANE - 1
Q_GATE_LANE = FOX_HEAD_DIM
K_GATE_LANE = FOX_HEAD_DIM + 3


def _fox_prep_kernel(fq_ref, fk_ref, fv_ref, krfg_ref, bf_ref, gq_ref, gk_ref, pq_ref, pk_ref, pv_ref,
                     q_ref, k_ref, v_ref, carry_ref):
    tl = fq_ref.shape[1]
    lane = lax.broadcasted_iota(jnp.int32, (tl, LANE), 1)

    @pl.when(pl.program_id(1) == 0)
    def _():
        carry_ref[...] = jnp.zeros_like(carry_ref)

    logf = jax.nn.log_sigmoid(krfg_ref[0] + bf_ref[...])
    logf = jnp.where(lane < FOX_HEADS, logf, 0.0)
    r_i = lax.broadcasted_iota(jnp.int32, (tl, tl), 0)
    c_i = lax.broadcasted_iota(jnp.int32, (tl, tl), 1)
    tri = jnp.where(c_i <= r_i, 1.0, 0.0).astype(BF16)
    cum = carry_ref[0:1, :]
    for piece in _split3(logf):
        cum = cum + jnp.dot(tri, piece.astype(BF16), preferred_element_type=F32)
    carry_ref[0:1, :] = cum[tl - 1:tl, :]

    c_hi, c_mid, c_lo = _split3(cum * LOG2E)
    gate_row = (c_hi + pltpu.roll(c_mid, GATE_MID_LANE, 1) + pltpu.roll(c_lo, GATE_LO_LANE, 1)
                + jnp.where(lane == GATE_ONE_LANE, 1.0, 0.0)).astype(BF16)

    p_r = lax.broadcasted_iota(jnp.int32, (LANE, LANE), 0)
    p_c = lax.broadcasted_iota(jnp.int32, (LANE, LANE), 1)
    head_mean = jnp.where(p_r // FOX_HEAD_DIM == p_c // FOX_HEAD_DIM, 1.0 / FOX_HEAD_DIM, 0.0).astype(BF16)

    def mean_sq(ref, j):
        x = ref[0, :, j * LANE:(j + 1) * LANE]
        return jnp.dot((x * x).astype(BF16), head_mean, preferred_element_type=F32)

    def normed(ref, g_ref, j, ms):
        lanes = slice(j * LANE, (j + 1) * LANE)
        return (ref[0, :, lanes] * lax.rsqrt(ms + EPS) * g_ref[:, lanes]).astype(BF16)

    def placed(x, p_ref, j):
        return jnp.dot(jnp.concatenate([x, gate_row], axis=1), p_ref[j], preferred_element_type=F32)

    pairs = range(FOX_HEADS // 2)
    q_ms = [mean_sq(fq_ref, j) for j in pairs]
    k_ms = [mean_sq(fk_ref, j) for j in pairs]
    q_n = [normed(fq_ref, gq_ref, j, q_ms[j]) for j in pairs]
    k_n = [normed(fk_ref, gk_ref, j, k_ms[j]) for j in pairs]
    for j in pairs:
        q = placed(q_n[j], pq_ref, j)
        k = placed(k_n[j], pk_ref, j)
        v = placed(fv_ref[0, :, j * LANE:(j + 1) * LANE].astype(BF16), pv_ref, j)
        for hh in range(2):
            head = slice(hh * HEAD_PAD, (hh + 1) * HEAD_PAD)
            q_ref[0, 2 * j + hh] = q[:, head].astype(BF16)
            _store_key_blocks(k_ref, 2 * j + hh, k[:, head])
            v_ref[0, 2 * j + hh] = v[:, head].astype(BF16)


def _fox_prep_specs(tl):
    row = lambda b, i: (b, i, 0)
    c2 = lambda b, i: (0, 0)
    c3 = lambda b, i: (0, 0, 0)
    head_out = pl.BlockSpec((1, FOX_HEADS, tl, HEAD_PAD), lambda b, i: (b, 0, i, 0))
    head_shape = jax.ShapeDtypeStruct((BATCH, FOX_HEADS, SEQ, HEAD_PAD), BF16)
    place = pl.BlockSpec((FOX_HEADS // 2, 2 * LANE, 2 * HEAD_PAD), c3)
    in_specs = [pl.BlockSpec((1, tl, ATT_WIDTH), row),
                pl.BlockSpec((1, tl, ATT_WIDTH), row),
                pl.BlockSpec((1, tl, ATT_WIDTH), row),
                pl.BlockSpec((1, tl, LANE), row),
                pl.BlockSpec((1, LANE), c2),
                pl.BlockSpec((1, ATT_WIDTH), c2),
                pl.BlockSpec((1, ATT_WIDTH), c2),
                place, place, place]
    return (in_specs, [head_out, _KT_SPEC(FOX_HEADS, tl), head_out],
            [head_shape, _KT_SHAPE(FOX_HEADS), head_shape])


def _front_kernel(*refs, n_mla, n_fox):
    x_ref, g_ref, sh_ref, sc_ref, w_ref = refs[:5]
    mla_rest, fox_rest = refs[5:5 + n_mla], refs[5 + n_mla:5 + n_mla + n_fox]
    u_ref = refs[5 + n_mla + n_fox]
    outs = refs[6 + n_mla + n_fox:12 + n_mla + n_fox]
    proj_ref, carry_ref = refs[-2:]
    _inproj_into(x_ref, g_ref, sh_ref, sc_ref, w_ref, proj_ref, u_ref)
    cols = [proj_ref.at[:, :, c0:c1] for c0, c1 in _IN_GROUPS]
    _mla_prep_kernel(cols[1], cols[2], cols[3], *mla_rest, *outs[:3])
    _fox_prep_kernel(cols[4], cols[5], cols[6], cols[3], *fox_rest, *outs[3:], carry_ref)


def _front_call(x, g, sh, sc, w, layer, mla_rest, fox_rest):
    tl = ROW_TILE
    row = lambda b, i: (b, i, 0)
    per_b = lambda b, i: (b, 0, 0)
    const = lambda b, i: (0, 0)
    mla_in, mla_out, mla_shape = _mla_prep_specs(tl)
    fox_in, fox_out, fox_shape = _fox_prep_specs(tl)
    mla_in, fox_in = mla_in[3:], fox_in[4:]
    outs = pl.pallas_call(
        functools.partial(_front_kernel, n_mla=len(mla_in), n_fox=len(fox_in)),
        grid=(BATCH, SEQ // tl),
        in_specs=[pl.BlockSpec((1, tl, D_MODEL), row),
                  pl.BlockSpec((1, D_MODEL), const),
                  pl.BlockSpec((1, 1, D_MODEL), per_b),
                  pl.BlockSpec((1, 1, D_MODEL), per_b),
                  pl.BlockSpec((1, D_MODEL, IN_PAD), lambda b, i: (layer, 0, 0))] + mla_in + fox_in,
        out_specs=[pl.BlockSpec((tl, SSM_WIDTH), lambda b, i: (i, b))] + mla_out + fox_out,
        out_shape=[jax.ShapeDtypeStruct((SEQ, BATCH * SSM_WIDTH), F32)] + mla_shape + fox_shape,
        scratch_shapes=[pltpu.VMEM((1, tl, IN_PAD), F32), pltpu.VMEM((SUBLANE, LANE), F32)],
        compiler_params=_params(("arbitrary", "arbitrary")),
        name="front",
    )(x, g, sh, sc, w, *mla_rest, *fox_rest)
    return outs[0], outs[1:4], outs[4:]


def _fox_placements():
    pq = np.zeros((FOX_HEADS // 2, 2 * LANE, 2 * HEAD_PAD), np.float32)
    pk = np.zeros_like(pq)
    pv = np.zeros_like(pq)
    one_row = LANE + GATE_ONE_LANE
    for j in range(FOX_HEADS // 2):
        for hh in range(2):
            h, col0 = 2 * j + hh, hh * HEAD_PAD
            for d in range(FOX_HEAD_DIM):
                for p in (pq, pk, pv):
                    p[j, hh * FOX_HEAD_DIM + d, col0 + d] = 1.0
            pv[j, one_row, col0 + ONES_LANE] = 1.0
            for n, piece_lane in enumerate((0, GATE_MID_LANE, GATE_LO_LANE)):
                pq[j, LANE + piece_lane + h, col0 + Q_GATE_LANE + n] = 1.0
                pq[j, one_row, col0 + K_GATE_LANE + n] = 1.0
                pk[j, one_row, col0 + Q_GATE_LANE + n] = 1.0
                pk[j, LANE + piece_lane + h, col0 + K_GATE_LANE + n] = -1.0
    return tuple(jnp.asarray(p, BF16) for p in (pq, pk, pv))


def _fox_operands(bf, gq, gk):
    q_scale = LOG2E / math.sqrt(FOX_HEAD_DIM)
    return (_pad_lanes(bf[None, :], LANE), jnp.tile(gq * q_scale, FOX_HEADS)[None, :],
            jnp.tile(gk, FOX_HEADS)[None, :]) + _fox_placements()


def _flash_kernel(qa_ref, qb_ref, kt_ref, v_ref, gap_ref, o_ref, q_scr, s_ref, m_ref, acc_ref,
                  *, tile, chunk, n_tiles):
    p = pl.program_id(2)
    tiles = (p, n_tiles - 1 - p)
    n_tasks = n_tiles + 1
    half = tile // 2
    top, bottom = slice(0, half), slice(half, tile)
    lane = lax.broadcasted_iota(jnp.int32, (tile, HEAD_PAD), 1)
    q_scr[0] = qa_ref[0]
    q_scr[1] = qb_ref[0]

    def plain_task(t):
        second = t - 2 >= p
        return second, second.astype(jnp.int32), jnp.where(second, t - 2 - p, t - 2)

    def row_max_update(w, hh, rows, s):
        mr = m_ref[w, hh, rows]
        for c in range(s.shape[1] // LANE):
            mr = jnp.maximum(mr, s[:, c * LANE:(c + 1) * LANE])
        m_ref[w, hh, rows] = mr

    m_ref[...] = jnp.full(m_ref.shape, NEG, F32)
    for w in range(2):
        for hh in range(2):
            kt = kt_ref[0, hh, tiles[w]]
            s_top = jnp.dot(q_scr[w, hh, top], kt[:, top], preferred_element_type=F32)
            s_top = jnp.where(gap_ref[top, top] <= 0, s_top, NEG)
            s_ref[hh, w, top, top] = s_top
            row_max_update(w, hh, top, s_top)
            s_bot = jnp.dot(q_scr[w, hh, bottom], kt, preferred_element_type=F32)
            s_bot = jnp.where(gap_ref[bottom, :] <= 0, s_bot, NEG)
            s_ref[hh, w, bottom] = s_bot
            row_max_update(w, hh, bottom, s_bot)
    for t in range(2, n_tasks):
        _, which, j = plain_task(jnp.int32(t))
        for hh in range(2):
            s = jnp.dot(q_scr[which, hh], kt_ref[0, hh, j], preferred_element_type=F32)
            s_ref[hh, t] = s
            row_max_update(which, hh, slice(None), s)

    ms = [[jnp.max(m_ref[w, hh], axis=1, keepdims=True) for hh in range(2)] for w in range(2)]

    acc_ref[...] = jnp.zeros(acc_ref.shape, F32)
    for w in range(2):
        k0 = pl.multiple_of(tiles[w] * tile, tile)
        for hh in range(2):
            pr = jnp.exp2(s_ref[hh, w, top, top] - ms[w][hh][top]).astype(BF16)
            acc_ref[w, hh, top] += jnp.dot(pr, v_ref[0, hh, pl.ds(k0, half), :], preferred_element_type=F32)
            pr = jnp.exp2(s_ref[hh, w, bottom] - ms[w][hh][bottom]).astype(BF16)
            acc_ref[w, hh, bottom] += jnp.dot(pr, v_ref[0, hh, pl.ds(k0, tile), :], preferred_element_type=F32)
    for t in range(2, n_tasks):
        second, which, j = plain_task(jnp.int32(t))
        k0 = pl.multiple_of(j * tile, tile)
        for hh in range(2):
            row_max = jnp.where(second, ms[1][hh], ms[0][hh])
            pr = jnp.exp2(s_ref[hh, t] - row_max).astype(BF16)
            acc_ref[which, hh] += jnp.dot(pr, v_ref[0, hh, pl.ds(k0, tile), :], preferred_element_type=F32)

    for w in range(2):
        outs = [acc_ref[w, hh] / acc_ref[w, hh][:, ONES_LANE:ONES_LANE + 1] for hh in range(2)]
        o_ref[0, w, 0] = jnp.where(lane < 64, outs[0], pltpu.roll(outs[1], 64, 1)).astype(BF16)


def _flash_call(q, kt, v, chunk):
    tile = ATT_TILE
    heads = q.shape[1]
    n_tiles = SEQ // tile
    pos = np.arange(tile, dtype=np.int32) // chunk
    gap = jnp.asarray(pos[None, :] - pos[:, None])
    return pl.pallas_call(
        functools.partial(_flash_kernel, tile=tile, chunk=chunk, n_tiles=n_tiles),
        grid=(BATCH, heads // 2, n_tiles // 2),
        in_specs=[pl.BlockSpec((1, 2, tile, HEAD_PAD), lambda b, hp, p: (b, hp, p, 0)),
                  pl.BlockSpec((1, 2, tile, HEAD_PAD), lambda b, hp, p: (b, hp, n_tiles - 1 - p, 0)),
                  pl.BlockSpec((1, 2, n_tiles, HEAD_PAD, tile), lambda b, hp, p: (b, hp, 0, 0, 0)),
                  pl.BlockSpec((1, 2, SEQ, HEAD_PAD), lambda b, hp, p: (b, hp, 0, 0)),
                  pl.BlockSpec((tile, tile), lambda b, hp, p: (0, 0))],
        out_specs=pl.BlockSpec((1, 2, 1, tile, LANE), lambda b, hp, p: (b, 0, p, 0, hp)),
        out_shape=jax.ShapeDtypeStruct((BATCH, 2, n_tiles // 2, tile, ATT_WIDTH), BF16),
        scratch_shapes=[pltpu.VMEM((2, 2, tile, HEAD_PAD), BF16),
                        pltpu.VMEM((2, n_tiles + 1, tile, tile), F32),
                        pltpu.VMEM((2, 2, tile, LANE), F32),
                        pltpu.VMEM((2, 2, tile, HEAD_PAD), F32)],
        compiler_params=_params(("arbitrary", "arbitrary", "arbitrary")),
        name="flash_chunk%d" % chunk,
    )(q, q, kt, v, gap)


def _merge_kernel(ssm_ref, mla_ref, fox_ref, x_ref, g1_ref, gm_ref, gf_ref, w_ref, o_ref):
    def normed(ref, g_ref):
        a = ref[0, 0, 0].astype(F32)
        return (a * lax.rsqrt(jnp.mean(a * a, axis=-1, keepdims=True) + EPS) * g_ref[...]).astype(BF16)

    merged = jnp.concatenate([ssm_ref[...], normed(mla_ref, gm_ref), normed(fox_ref, gf_ref)], axis=1)
    mix = jnp.dot(merged, w_ref[...], preferred_element_type=F32)
    o_ref[0] = x_ref[0] + g1_ref[0] * mix


def _merge_call(o_ssm, o_mla, o_fox, x, g1, gm, gf, w):
    tm = ATT_TILE
    half = SEQ // tm // 2
    row = lambda b, i: (b, i, 0)
    c2 = lambda b, i: (0, 0)
    att = pl.BlockSpec((1, 1, 1, tm, ATT_WIDTH),
                       lambda b, i: (b, i // half, jnp.where(i < half, i, 2 * half - 1 - i), 0, 0))
    return pl.pallas_call(
        _merge_kernel,
        grid=(BATCH, SEQ // tm),
        in_specs=[pl.BlockSpec((tm, SSM_WIDTH), lambda b, i: (i, b)),
                  att, att,
                  pl.BlockSpec((1, tm, D_MODEL), row),
                  pl.BlockSpec((1, 1, D_MODEL), lambda b, i: (b, 0, 0)),
                  pl.BlockSpec((1, ATT_WIDTH), c2),
                  pl.BlockSpec((1, ATT_WIDTH), c2),
                  pl.BlockSpec((D_MODEL, D_MODEL), c2)],
        out_specs=pl.BlockSpec((1, tm, D_MODEL), row),
        out_shape=jax.ShapeDtypeStruct((BATCH, SEQ, D_MODEL), F32),
        compiler_params=_params(("arbitrary", "arbitrary")),
        name="merge",
    )(o_ssm, o_mla, o_fox, x, g1, gm, gf, w)


def _ffn_kernel(x_ref, g_ref, sh_ref, sc_ref, g2_ref, wg_ref, wu_ref, wd_ref, o_ref):
    half = x_ref.shape[1] // 2
    for r in (slice(0, half), slice(half, 2 * half)):
        x = x_ref[0, r]
        h = _rms_mod(x, g_ref[...], sc_ref[0], sh_ref[0]).astype(BF16)
        gate_up = [(jnp.dot(h, wg_ref[0, :, c0:c1], preferred_element_type=F32),
                    jnp.dot(h, wu_ref[0, :, c0:c1], preferred_element_type=F32)) for c0, c1 in FF_PARTS]
        acc = jnp.zeros((half, D_MODEL), F32)
        for (c0, c1), (gate, up) in zip(FF_PARTS, gate_up):
            a = (gate * jax.nn.sigmoid(gate) * up).astype(BF16)
            acc = acc + jnp.dot(a, wd_ref[0, c0:c1, :], preferred_element_type=F32)
        o_ref[0, r] = x + g2_ref[0] * acc


def _ffn_call(x, g, sh, sc, g2, wg, wu, wd, layer):
    tm = MOE_TILE
    row = lambda b, i: (b, i, 0)
    per_b = lambda b, i: (b, 0, 0)
    resident = lambda shape: pl.BlockSpec(shape, lambda b, i: (layer, 0, 0), pipeline_mode=pl.Buffered(1))
    return pl.pallas_call(
        _ffn_kernel,
        grid=(BATCH, SEQ // tm),
        in_specs=[pl.BlockSpec((1, tm, D_MODEL), row),
                  pl.BlockSpec((1, D_MODEL), lambda b, i: (0, 0)),
                  pl.BlockSpec((1, 1, D_MODEL), per_b),
                  pl.BlockSpec((1, 1, D_MODEL), per_b),
                  pl.BlockSpec((1, 1, D_MODEL), per_b),
                  resident((1, D_MODEL, D_FF)),
                  resident((1, D_MODEL, D_FF)),
                  resident((1, D_FF, D_MODEL))],
        out_specs=pl.BlockSpec((1, tm, D_MODEL), row),
        out_shape=jax.ShapeDtypeStruct((BATCH, SEQ, D_MODEL), F32),
        compiler_params=_params(("arbitrary", "arbitrary")),
        name="ffn_dense",
    )(x, g, sh, sc, g2, wg, wu, wd)


def _router_kernel(x_ref, g_ref, sh_ref, sc_ref, w_ref, b_ref, comb_ref, rank_ref, rankt_ref, count_ref):
    tm = x_ref.shape[1]
    h = _rms_mod(x_ref[0], g_ref[...], sc_ref[0], sh_ref[0])
    h_hi = h.astype(BF16)
    h_lo = (h - h_hi.astype(F32)).astype(BF16)
    w_hi, w_lo = w_ref[0], w_ref[1]
    logits = (jnp.dot(h_hi, w_hi, preferred_element_type=F32)
              + jnp.dot(h_lo, w_hi, preferred_element_type=F32)
              + jnp.dot(h_hi, w_lo, preferred_element_type=F32)) + b_ref[...]
    lane = lax.broadcasted_iota(jnp.int32, logits.shape, 1)
    logits = jnp.where(lane < N_EXPERTS, logits, -jnp.inf)
    m1 = jnp.max(logits, axis=-1, keepdims=True)
    i1 = jnp.min(jnp.where(logits == m1, lane, LANE), axis=-1, keepdims=True)
    rest = jnp.where(lane == i1, -jnp.inf, logits)
    m2 = jnp.max(rest, axis=-1, keepdims=True)
    i2 = jnp.min(jnp.where(rest == m2, lane, LANE), axis=-1, keepdims=True)
    e = jnp.exp(m2 - m1)
    p1 = 1.0 / (1.0 + e)
    comb_ref[0] = jnp.where(lane == i1, p1, 0.0) + jnp.where(lane == i2, e * p1, 0.0)

    chosen = (lane == i1) | (lane == i2)
    chosen_f = jnp.where(chosen, 1.0, 0.0)
    r_i = lax.broadcasted_iota(jnp.int32, (tm, tm), 0)
    c_i = lax.broadcasted_iota(jnp.int32, (tm, tm), 1)
    earlier = jnp.where(c_i < r_i, 1.0, 0.0).astype(BF16)
    rank = jnp.dot(earlier, chosen_f.astype(BF16), preferred_element_type=F32)
    rank = jnp.where(chosen, rank, -1.0)
    rank_ref[0] = rank
    rankt_ref[0] = rank.T[0:SUBLANE, :]
    count_ref[0] = jnp.sum(chosen_f, axis=0, keepdims=True)


def _router_call(x, g, sh, sc, w, b):
    tm = MOE_TILE
    tiles = SEQ // tm
    row = lambda b_, i: (b_, i, 0)
    per_b = lambda b_, i: (b_, 0, 0)
    per_tile = lambda b_, i: (b_ * tiles + i, 0, 0)
    return pl.pallas_call(
        _router_kernel,
        grid=(BATCH, tiles),
        in_specs=[pl.BlockSpec((1, tm, D_MODEL), row),
                  pl.BlockSpec((1, D_MODEL), lambda b_, i: (0, 0)),
                  pl.BlockSpec((1, 1, D_MODEL), per_b),
                  pl.BlockSpec((1, 1, D_MODEL), per_b),
                  pl.BlockSpec((2, D_MODEL, LANE), lambda b_, i: (0, 0, 0)),
                  pl.BlockSpec((1, LANE), lambda b_, i: (0, 0))],
        out_specs=[pl.BlockSpec((1, tm, LANE), row),
                   pl.BlockSpec((1, tm, LANE), row),
                   pl.BlockSpec((1, SUBLANE, tm), per_tile),
                   pl.BlockSpec((1, 1, LANE), per_tile)],
        out_shape=[jax.ShapeDtypeStruct((BATCH, SEQ, LANE), F32),
                   jax.ShapeDtypeStruct((BATCH, SEQ, LANE), F32),
                   jax.ShapeDtypeStruct((BATCH * tiles, SUBLANE, tm), F32),
                   jax.ShapeDtypeStruct((BATCH * tiles, 1, LANE), F32)],
        compiler_params=_params(("arbitrary", "arbitrary")),
        name="router",
    )(x, g, sh, sc, w, b)


def _moe_kernel(count_ref, x_ref, g_ref, sh_ref, sc_ref, g2_ref, comb_ref, rank_ref, rankt_ref,
                wg_ref, wu_ref, wd_ref, o_ref, h_ref):
    tm = x_ref.shape[1]
    e = pl.program_id(1)

    @pl.when(e == 0)
    def _():
        x = x_ref[0]
        h_ref[...] = _rms_mod(x, g_ref[...], sc_ref[0], sh_ref[0]).astype(BF16)
        o_ref[0] = x

    lane = lax.broadcasted_iota(jnp.int32, (tm, LANE), 1)
    mine = lane == e
    rank_col = jnp.sum(jnp.where(mine, rank_ref[0], 0.0), axis=-1, keepdims=True)
    gate_col = jnp.sum(jnp.where(mine, comb_ref[0], 0.0), axis=-1, keepdims=True)
    rank_row = rankt_ref[0, pl.ds(e, 1), :]
    count = count_ref[pl.program_id(0) * N_EXPERTS + e]

    def expert_pass(first, n_rows):
        base = first.astype(F32)
        slot_sub = lax.broadcasted_iota(jnp.int32, (n_rows, tm), 0).astype(F32)
        slot_lane = lax.broadcasted_iota(jnp.int32, (tm, n_rows), 1).astype(F32)
        pick = jnp.where(rank_row - base == slot_sub, 1.0, 0.0).astype(BF16)
        rows = jnp.dot(pick, h_ref[...], preferred_element_type=F32).astype(BF16)
        gate = jnp.dot(rows, wg_ref[0, 0], preferred_element_type=F32)
        up = jnp.dot(rows, wu_ref[0, 0], preferred_element_type=F32)
        a = (gate * jax.nn.sigmoid(gate) * up).astype(BF16)
        y = jnp.dot(a, wd_ref[0, 0], preferred_element_type=F32).astype(BF16)
        place = jnp.where(rank_col - base == slot_lane, 1.0, 0.0).astype(BF16)
        back = jnp.dot(place, y, preferred_element_type=F32)
        o_ref[0] += g2_ref[0] * (gate_col * back)

    def full_pass(sb, carry):
        expert_pass(sb * MOE_ROWS, MOE_ROWS)
        return carry

    n_full = count // MOE_ROWS
    lax.fori_loop(0, n_full, full_pass, 0)
    left = count - n_full * MOE_ROWS

    @pl.when(left > MOE_ROWS // 2)
    def _():
        expert_pass(n_full * MOE_ROWS, MOE_ROWS)

    @pl.when((left > 0) & (left <= MOE_ROWS // 2))
    def _():
        expert_pass(n_full * MOE_ROWS, MOE_ROWS // 2)


def _moe_call(x, g, sh, sc, g2, comb, rank, rankt, counts, wg, wu, wd, layer):
    tm = MOE_TILE
    tiles = SEQ // tm
    n_tiles = BATCH * tiles
    row = lambda i, e, cnt: (i, 0, 0)
    per_b = lambda i, e, cnt: (i // tiles, 0, 0)
    expert = lambda i, e, cnt: (layer, e, 0, 0)
    as_tiles = lambda a: a.reshape(n_tiles, tm, a.shape[-1])
    grid_spec = pltpu.PrefetchScalarGridSpec(
        num_scalar_prefetch=1,
        grid=(n_tiles, N_EXPERTS),
        in_specs=[pl.BlockSpec((1, tm, D_MODEL), row),
                  pl.BlockSpec((1, D_MODEL), lambda i, e, cnt: (0, 0)),
                  pl.BlockSpec((1, 1, D_MODEL), per_b),
                  pl.BlockSpec((1, 1, D_MODEL), per_b),
                  pl.BlockSpec((1, 1, D_MODEL), per_b),
                  pl.BlockSpec((1, tm, LANE), row),
                  pl.BlockSpec((1, tm, LANE), row),
                  pl.BlockSpec((1, SUBLANE, tm), row),
                  pl.BlockSpec((1, 1, D_MODEL, D_FF_EXPERT), expert),
                  pl.BlockSpec((1, 1, D_MODEL, D_FF_EXPERT), expert),
                  pl.BlockSpec((1, 1, D_FF_EXPERT, D_MODEL), expert)],
        out_specs=pl.BlockSpec((1, tm, D_MODEL), row),
        scratch_shapes=[pltpu.VMEM((tm, D_MODEL), BF16)],
    )
    out = pl.pallas_call(
        _moe_kernel,
        grid_spec=grid_spec,
        out_shape=jax.ShapeDtypeStruct((n_tiles, tm, D_MODEL), F32),
        compiler_params=_params(("arbitrary", "arbitrary")),
        name="moe_experts",
    )(counts, as_tiles(x), g, sh, sc, g2, as_tiles(comb), as_tiles(rank), rankt, wg, wu, wd)
    return out.reshape(BATCH, SEQ, D_MODEL)


def kernel(x, c, positions, norm_mix, norm_ffn, w_ada, b_ada, w_in, ssm_lam_re, ssm_lam_im, ssm_log_dt, ssm_b_re, ssm_b_im, ssm_c_re, ssm_c_im, ssm_d, ssm_w_glu, ssm_b_glu, mla_q_norm, mla_kv_norm, mla_w_uq, mla_w_ukv, mla_qk_gq, mla_qk_gk, fox_b_f, fox_qk_gq, fox_qk_gk, out_norm, w_out, ffn_w_gate, ffn_w_up, ffn_w_down, moe_w_router, moe_b_router, moe_w_gate, moe_w_up, moe_w_down):
    tabs = _rope_tables(positions)
    w_in_packed = _pack_w_in(w_in)
    moe_wg, moe_wu, moe_wd = (w.astype(BF16) for w in (moe_w_gate, moe_w_up, moe_w_down))
    ffn_wg, ffn_wu, ffn_wd = (w.astype(BF16) for w in (ffn_w_gate, ffn_w_up, ffn_w_down))
    ada = _ada_call(c, w_ada, b_ada)
    ada = ada.reshape(DEPTH, BATCH, 6, 1, D_MODEL)
    row2 = lambda a: a[None, :]

    for i in range(DEPTH):
        sh1, sc1, g1, sh2, sc2, g2 = (ada[i, :, n] for n in range(6))

        u, mla_qkv, fox_qkv = _front_call(
            x, row2(norm_mix[i]), sh1, sc1, w_in_packed, i,
            (*tabs, row2(mla_q_norm[i]), row2(mla_kv_norm[i]),
             *_mla_weights(mla_w_uq[i], mla_w_ukv[i], mla_qk_gq[i], mla_qk_gk[i])),
            _fox_operands(fox_b_f[i], fox_qk_gq[i], fox_qk_gk[i]))

        bmat, lam, cmat = _s5_operands(ssm_lam_re[i], ssm_lam_im[i], ssm_log_dt[i],
                                       ssm_b_re[i], ssm_b_im[i], ssm_c_re[i], ssm_c_im[i])
        o_ssm = _s5_call(u, bmat, lam, cmat, row2(ssm_d[i]), ssm_w_glu[i].astype(BF16),
                         row2(ssm_b_glu[i]), row2(out_norm[i, :SSM_WIDTH]))

        o_mla = _flash_call(*mla_qkv, CHUNK)
        o_fox = _flash_call(*fox_qkv, 1)

        e1, e2 = SSM_WIDTH, SSM_WIDTH + ATT_WIDTH
        x = _merge_call(o_ssm, o_mla, o_fox, x, g1, row2(out_norm[i, e1:e2]), row2(out_norm[i, e2:]),
                        w_out[i].astype(BF16))

        j = i // 2
        if i % 2 == 0:
            x = _ffn_call(x, row2(norm_ffn[i]), sh2, sc2, g2, ffn_wg, ffn_wu, ffn_wd, layer=j)
        else:
            wr = _pad_lanes(moe_w_router[j], LANE)
            wr_hi = wr.astype(BF16)
            wr_lo = (wr - wr_hi.astype(F32)).astype(BF16)
            comb, rank, rankt, counts = _router_call(x, row2(norm_ffn[i]), sh2, sc2, jnp.stack([wr_hi, wr_lo]),
                                                     _pad_lanes(row2(moe_b_router[j]), LANE))
            counts = counts[:, 0, :N_EXPERTS].astype(jnp.int32).reshape(-1)
            x = _moe_call(x, row2(norm_ffn[i]), sh2, sc2, g2, comb, rank, rankt, counts,
                          moe_wg, moe_wu, moe_wd, layer=j)
    return x
```

```python
import functools
import math

import jax
import jax.numpy as jnp
import numpy as np
from jax import lax
from jax.experimental import pallas as pl
from jax.experimental.pallas import tpu as pltpu

F32 = jnp.float32
BF16 = jnp.bfloat16

D_MODEL = 1024
BATCH = 8
SEQ = 4096
DEPTH = 4
CHUNK = 64
EPS = 1e-6

SSM_WIDTH = 256
SSM_GROUP = 16
N_SSM_GROUPS = 16
SSM_STATE = 64
N_STATE = N_SSM_GROUPS * SSM_STATE

MLA_HEADS = 6
MLA_Q_RANK = 256
MLA_KV_RANK = 128
MLA_NOPE = 64
MLA_ROPE = 32
MLA_V = 64
MLA_QK = 96
ROPE_BASE = 10000.0

FOX_HEADS = 6
FOX_HEAD_DIM = 64
ATT_WIDTH = 384

D_FF = 2816
N_EXPERTS = 8
D_FF_EXPERT = 1408

LANE = 128
SUBLANE = 8
HEAD_PAD = LANE
ONES_LANE = 64
NEG = -1e30

IN_PAD = 1920
KR_LANE = 64

ROW_TILE = 512
S5_STEPS = 64
ATT_TILE = 512
LOG2E = math.log2(math.e)
FF_PARTS = ((0, 1536), (1536, D_FF))
MOE_TILE = 1024
MOE_ROWS = 256
VMEM_LIMIT = 56 * 1024 * 1024


def _params(sem):
    return pltpu.CompilerParams(dimension_semantics=sem, vmem_limit_bytes=VMEM_LIMIT)


def _rms_mod(x, g, sc, sh):
    ms = jnp.mean(x * x, axis=-1, keepdims=True)
    h = x * lax.rsqrt(ms + EPS) * g
    return h * (1.0 + sc) + sh


def _split3(x):
    hi = x.astype(BF16).astype(F32)
    r = x - hi
    mid = r.astype(BF16).astype(F32)
    lo = (r - mid).astype(BF16).astype(F32)
    return hi, mid, lo


def _ada_kernel(c_ref, w_ref, b_ref, o_ref):
    c = c_ref[...]
    ca = (c * jax.nn.sigmoid(c)).astype(BF16)
    o_ref[0] = jnp.dot(ca, w_ref[0].astype(BF16), preferred_element_type=F32) + b_ref[0]


def _ada_call(c, w_ada, b_ada):
    tn = 1536
    return pl.pallas_call(
        _ada_kernel,
        grid=(DEPTH, 6 * D_MODEL // tn),
        in_specs=[pl.BlockSpec((BATCH, D_MODEL), lambda i, j: (0, 0)),
                  pl.BlockSpec((1, D_MODEL, tn), lambda i, j: (i, 0, j)),
                  pl.BlockSpec((1, 1, tn), lambda i, j: (i, 0, j))],
        out_specs=pl.BlockSpec((1, BATCH, tn), lambda i, j: (i, 0, j)),
        out_shape=jax.ShapeDtypeStruct((DEPTH, BATCH, 6 * D_MODEL), F32),
        compiler_params=_params(("arbitrary", "arbitrary")),
        name="ada",
    )(c, w_ada, b_ada.reshape(DEPTH, 1, 6 * D_MODEL))


_IN_GROUPS = ((0, 256), (256, 512), (512, 640), (640, 768), (768, 1152), (1152, 1536), (1536, 1920))


def _inproj_into(x_ref, g_ref, sh_ref, sc_ref, w_ref, proj_ref, u_ref):
    parts = 4
    step = x_ref.shape[1] // parts
    rows = [slice(r * step, (r + 1) * step) for r in range(parts)]
    normed = lambda r: _rms_mod(x_ref[0, rows[r]], g_ref[...], sc_ref[0], sh_ref[0]).astype(BF16)
    h_next = normed(0)
    for r in range(parts):
        h = h_next
        if r + 1 < parts:
            h_next = normed(r + 1)
        proj = jnp.dot(h, w_ref[0], preferred_element_type=F32)
        proj_ref[0, rows[r]] = proj
        u_ref[rows[r]] = proj[:, _IN_GROUPS[0][0]:_IN_GROUPS[0][1]]


def _pack_w_in(w):
    u, cq, ckv, kr, fq, fk, fv, fg = jnp.split(
        w.astype(BF16), (256, 512, 640, 672, 1056, 1440, 1824), axis=2)
    z = lambda n: jnp.zeros(w.shape[:2] + (n,), BF16)
    krfg = jnp.concatenate([fg, z(KR_LANE - FOX_HEADS), kr, z(LANE - KR_LANE - MLA_ROPE)], axis=2)
    return jnp.concatenate([u, cq, ckv, krfg, fq, fk, fv], axis=2)


def _s5_kernel(u2_ref, bmat_ref, lam_ref, cmat_ref, d_ref, wglu_ref, bglu_ref, gn_ref,
               o2_ref, u_ref, o_ref, bu0_ref, bu1_ref, state_ref, *, steps):
    rows = steps * BATCH
    lane_tiles = SSM_WIDTH // LANE
    for b in range(BATCH):
        for c in range(lane_tiles):
            lanes = slice(b * SSM_WIDTH + c * LANE, b * SSM_WIDTH + (c + 1) * LANE)
            u_ref.at[c][pl.ds(b, 2 * steps, stride=BATCH), :] = u2_ref[:, lanes]
    u_rows = lambda rs: jnp.concatenate([u_ref[c, rs, :] for c in range(lane_tiles)], axis=1)

    @pl.when(pl.program_id(0) == 0)
    def _():
        state_ref[...] = jnp.zeros_like(state_ref)

    halves = ((bu0_ref, slice(0, rows)), (bu1_ref, slice(rows, 2 * rows)))
    for bu_ref, rs in halves:
        bu_ref[...] = jnp.dot(u_rows(rs).astype(BF16), bmat_ref[...], preferred_element_type=F32)
    lr = jnp.broadcast_to(lam_ref[0:1, :], (SUBLANE, N_STATE))
    li = jnp.broadcast_to(lam_ref[1:2, :], (SUBLANE, N_STATE))
    sr, si = state_ref[:, 0:N_STATE], state_ref[:, N_STATE:2 * N_STATE]

    for bu_ref, rs in halves:
        for t in range(steps):
            r = slice(t * SUBLANE, (t + 1) * SUBLANE)
            nr = lr * sr - li * si + bu_ref[r, 0:N_STATE]
            ni = lr * si + li * sr + bu_ref[r, N_STATE:2 * N_STATE]
            bu_ref[r, 0:N_STATE] = nr
            bu_ref[r, N_STATE:2 * N_STATE] = ni
            sr, si = nr, ni
        y = jnp.dot(bu_ref[...].astype(BF16), cmat_ref[...], preferred_element_type=F32)
        y = jax.nn.gelu(y + d_ref[...] * u_rows(rs))
        gate = jnp.dot(y.astype(BF16), wglu_ref[...], preferred_element_type=F32) + bglu_ref[...]
        o = y * jax.nn.sigmoid(gate)
        ms = jnp.mean(o * o, axis=-1, keepdims=True)
        o = o * lax.rsqrt(ms + EPS) * gn_ref[...]
        for c in range(lane_tiles):
            o_ref[c, rs, :] = o[:, c * LANE:(c + 1) * LANE]

    state_ref[:, 0:N_STATE] = sr
    state_ref[:, N_STATE:2 * N_STATE] = si
    for b in range(BATCH):
        for c in range(lane_tiles):
            lanes = slice(b * SSM_WIDTH + c * LANE, b * SSM_WIDTH + (c + 1) * LANE)
            o2_ref[:, lanes] = o_ref.at[c][pl.ds(b, 2 * steps, stride=BATCH), :].astype(BF16)


def _s5_call(u2, bmat, lam, cmat, d_skip, wglu, bglu, gn):
    rows = S5_STEPS * BATCH
    const = lambda i: (0, 0)
    return pl.pallas_call(
        functools.partial(_s5_kernel, steps=S5_STEPS),
        grid=(SEQ // (2 * S5_STEPS),),
        in_specs=[pl.BlockSpec((2 * S5_STEPS, BATCH * SSM_WIDTH), lambda i: (i, 0)),
                  pl.BlockSpec((SSM_WIDTH, 2 * N_STATE), const),
                  pl.BlockSpec((2, N_STATE), const),
                  pl.BlockSpec((2 * N_STATE, SSM_WIDTH), const),
                  pl.BlockSpec((1, SSM_WIDTH), const),
                  pl.BlockSpec((SSM_WIDTH, SSM_WIDTH), const),
                  pl.BlockSpec((1, SSM_WIDTH), const),
                  pl.BlockSpec((1, SSM_WIDTH), const)],
        out_specs=pl.BlockSpec((2 * S5_STEPS, BATCH * SSM_WIDTH), lambda i: (i, 0)),
        out_shape=jax.ShapeDtypeStruct((SEQ, BATCH * SSM_WIDTH), BF16),
        scratch_shapes=[pltpu.VMEM((SSM_WIDTH // LANE, 2 * rows, LANE), F32),
                        pltpu.VMEM((SSM_WIDTH // LANE, 2 * rows, LANE), F32),
                        pltpu.VMEM((rows, 2 * N_STATE), F32),
                        pltpu.VMEM((rows, 2 * N_STATE), F32),
                        pltpu.VMEM((SUBLANE, 2 * N_STATE), F32)],
        compiler_params=_params(("arbitrary",)),
        name="s5",
    )(u2, bmat, lam, cmat, d_skip, wglu, bglu, gn)


def _s5_operands(lam_re, lam_im, log_dt, b_re, b_im, c_re, c_im):
    dt = jnp.exp(log_dt)[:, None]
    mag = jnp.exp(lam_re * dt)
    lb_re = mag * jnp.cos(lam_im * dt)
    lb_im = mag * jnp.sin(lam_im * dt)
    den = lam_re * lam_re + lam_im * lam_im
    co_re = ((lb_re - 1.0) * lam_re + lb_im * lam_im) / den
    co_im = (lb_im * lam_re - (lb_re - 1.0) * lam_im) / den
    bb_re = co_re[..., None] * b_re - co_im[..., None] * b_im
    bb_im = co_re[..., None] * b_im + co_im[..., None] * b_re
    eye = jnp.eye(N_SSM_GROUPS, dtype=F32)
    blk_b = lambda m: jnp.einsum("gpc,gh->gchp", m, eye).reshape(SSM_WIDTH, N_STATE)
    bmat = jnp.concatenate([blk_b(bb_re), blk_b(bb_im)], axis=1).astype(BF16)
    blk_c = lambda m: jnp.einsum("gcp,gh->gphc", m, eye).reshape(N_STATE, SSM_WIDTH)
    cmat = jnp.concatenate([blk_c(c_re), -blk_c(c_im)], axis=0).astype(BF16)
    lam = jnp.stack([lb_re.reshape(N_STATE), lb_im.reshape(N_STATE)], axis=0)
    return bmat, lam, cmat


def _rope_tables(positions):
    half = MLA_ROPE // 2
    inv = ROPE_BASE ** (-jnp.arange(half, dtype=F32) / half)
    ang = inv[:, None] * positions.astype(F32).reshape(1, -1)
    shp = positions.shape
    cos, sin = (lax.optimization_barrier(f(ang)).T.reshape(shp + (half,)) for f in (jnp.cos, jnp.sin))
    one = lambda n: jnp.ones(shp + (n,), F32)
    zero = lambda n: jnp.zeros(shp + (n,), F32)
    cos_t = jnp.concatenate([one(MLA_NOPE), cos, cos, zero(LANE - MLA_QK)], axis=-1)
    sin_t = jnp.concatenate([zero(MLA_NOPE), -sin, sin, zero(LANE - MLA_QK)], axis=-1)
    return cos_t, sin_t


def _swap_rope_halves(a):
    half = MLA_ROPE // 2
    lo, hi = a[..., MLA_NOPE:MLA_NOPE + half], a[..., MLA_NOPE + half:MLA_QK]
    return jnp.concatenate([jnp.zeros_like(a[..., :MLA_NOPE]), hi, lo, jnp.zeros_like(a[..., MLA_QK:])], axis=-1)


def _store_key_blocks(kt_ref, h, k):
    kt = k.T
    for s in range(k.shape[0] // ATT_TILE):
        kt_ref[0, h, s] = kt[:, s * ATT_TILE:(s + 1) * ATT_TILE].astype(BF16)


_KT_SPEC = lambda heads, tl: pl.BlockSpec((1, heads, tl // ATT_TILE, HEAD_PAD, ATT_TILE),
                                          lambda b, i: (b, 0, i, 0, 0))
_KT_SHAPE = lambda heads: jax.ShapeDtypeStruct((BATCH, heads, SEQ // ATT_TILE, HEAD_PAD, ATT_TILE), BF16)


def _mla_prep_kernel(cq_ref, ckv_ref, krfg_ref, cos_ref, sin_ref, qn_ref, kvn_ref, wq_ref, wk_ref, wv_ref,
                     gq_ref, gqs_ref, gk_ref, gks_ref, q_ref, k_ref, v_ref):
    tl = cq_ref.shape[1]
    lane = lax.broadcasted_iota(jnp.int32, (tl, LANE), 1)
    cos, sin = cos_ref[0], sin_ref[0]
    q_scale = LOG2E / math.sqrt(MLA_QK)
    q_cos, q_sin = gq_ref[...] * cos * q_scale, gqs_ref[...] * sin * q_scale
    k_cos, k_sin = gk_ref[...] * cos, gks_ref[...] * sin
    ones = jnp.ones((LANE, LANE), BF16)

    def inv_rms(x):
        ss = jnp.dot((x * x).astype(BF16), ones, preferred_element_type=F32)
        return lax.rsqrt(ss / MLA_QK + EPS)

    cq = cq_ref[0]
    cqn = (cq * lax.rsqrt(jnp.mean(cq * cq, axis=-1, keepdims=True) + EPS) * qn_ref[...]).astype(BF16)
    ckv = ckv_ref[0]
    ckvn = (ckv * lax.rsqrt(jnp.mean(ckv * ckv, axis=-1, keepdims=True) + EPS) * kvn_ref[...]).astype(BF16)
    kr = jnp.where((lane >= KR_LANE) & (lane < KR_LANE + MLA_ROPE), krfg_ref[0], 0.0)
    kr_swapped = jnp.where(lane < KR_LANE + MLA_ROPE // 2, pltpu.roll(kr, LANE - 16, 1), pltpu.roll(kr, 16, 1))
    k_rotary = kr_swapped * k_sin

    heads = range(MLA_HEADS)
    qqs = [jnp.dot(cqn, wq_ref[h], preferred_element_type=F32) for h in heads]
    ks = [jnp.dot(ckvn, wk_ref[h], preferred_element_type=F32) + kr for h in heads]
    q_inv = [inv_rms(qq[:, :LANE]) for qq in qqs]
    k_inv = [inv_rms(k) for k in ks]
    for h in heads:
        q, q_swapped = qqs[h][:, :LANE], qqs[h][:, LANE:]
        q_ref[0, h] = (q_inv[h] * (q * q_cos + q_swapped * q_sin)).astype(BF16)
        _store_key_blocks(k_ref, h, k_inv[h] * (ks[h] * k_cos + k_rotary))
        v = jnp.dot(ckvn, wv_ref[h], preferred_element_type=F32)
        v_ref[0, h] = jnp.where(lane == ONES_LANE, 1.0, v).astype(BF16)


def _mla_prep_specs(tl):
    row = lambda b, i: (b, i, 0)
    c2 = lambda b, i: (0, 0)
    c3 = lambda b, i: (0, 0, 0)
    head_out = pl.BlockSpec((1, MLA_HEADS, tl, HEAD_PAD), lambda b, i: (b, 0, i, 0))
    head_shape = jax.ShapeDtypeStruct((BATCH, MLA_HEADS, SEQ, HEAD_PAD), BF16)
    gain = pl.BlockSpec((1, HEAD_PAD), c2)
    in_specs = [pl.BlockSpec((1, tl, MLA_Q_RANK), row),
                pl.BlockSpec((1, tl, MLA_KV_RANK), row),
                pl.BlockSpec((1, tl, LANE), row),
                pl.BlockSpec((1, tl, LANE), row),
                pl.BlockSpec((1, tl, LANE), row),
                pl.BlockSpec((1, MLA_Q_RANK), c2),
                pl.BlockSpec((1, MLA_KV_RANK), c2),
                pl.BlockSpec((MLA_HEADS, MLA_Q_RANK, 2 * HEAD_PAD), c3),
                pl.BlockSpec((MLA_HEADS, MLA_KV_RANK, HEAD_PAD), c3),
                pl.BlockSpec((MLA_HEADS, MLA_KV_RANK, HEAD_PAD), c3),
                gain, gain, gain, gain]
    return (in_specs, [head_out, _KT_SPEC(MLA_HEADS, tl), head_out],
            [head_shape, _KT_SHAPE(MLA_HEADS), head_shape])


def _pad_lanes(a, n=HEAD_PAD):
    return jnp.pad(a, [(0, 0)] * (a.ndim - 1) + [(0, n - a.shape[-1])])


def _mla_weights(w_uq, w_ukv, gq, gk):
    wq = _pad_lanes(w_uq.reshape(MLA_Q_RANK, MLA_HEADS, MLA_QK).transpose(1, 0, 2))
    wq = jnp.concatenate([wq, _swap_rope_halves(wq)], axis=-1).astype(BF16)
    wkv = w_ukv.reshape(MLA_KV_RANK, MLA_HEADS, MLA_NOPE + MLA_V).transpose(1, 0, 2)
    wk = _pad_lanes(wkv[..., :MLA_NOPE]).astype(BF16)
    wv = _pad_lanes(wkv[..., MLA_NOPE:]).astype(BF16)
    gq, gk = _pad_lanes(gq[None, :]), _pad_lanes(gk[None, :])
    return wq, wk, wv, gq, _swap_rope_halves(gq), gk, _swap_rope_halves(gk)


GATE_MID_LANE = 8
GATE_LO_LANE = 16
GATE_ONE_LANE = LANE - 1
Q_GATE_LANE = FOX_HEAD_DIM
K_GATE_LANE = FOX_HEAD_DIM + 3


def _fox_prep_kernel(fq_ref, fk_ref, fv_ref, krfg_ref, bf_ref, gq_ref, gk_ref, pq_ref, pk_ref, pv_ref,
                     q_ref, k_ref, v_ref, carry_ref):
    tl = fq_ref.shape[1]
    lane = lax.broadcasted_iota(jnp.int32, (tl, LANE), 1)

    @pl.when(pl.program_id(1) == 0)
    def _():
        carry_ref[...] = jnp.zeros_like(carry_ref)

    logf = jax.nn.log_sigmoid(krfg_ref[0] + bf_ref[...])
    logf = jnp.where(lane < FOX_HEADS, logf, 0.0)
    r_i = lax.broadcasted_iota(jnp.int32, (tl, tl), 0)
    c_i = lax.broadcasted_iota(jnp.int32, (tl, tl), 1)
    tri = jnp.where(c_i <= r_i, 1.0, 0.0).astype(BF16)
    cum = carry_ref[0:1, :]
    for piece in _split3(logf):
        cum = cum + jnp.dot(tri, piece.astype(BF16), preferred_element_type=F32)
    carry_ref[0:1, :] = cum[tl - 1:tl, :]

    c_hi, c_mid, c_lo = _split3(cum * LOG2E)
    gate_row = (c_hi + pltpu.roll(c_mid, GATE_MID_LANE, 1) + pltpu.roll(c_lo, GATE_LO_LANE, 1)
                + jnp.where(lane == GATE_ONE_LANE, 1.0, 0.0)).astype(BF16)

    p_r = lax.broadcasted_iota(jnp.int32, (LANE, LANE), 0)
    p_c = lax.broadcasted_iota(jnp.int32, (LANE, LANE), 1)
    head_mean = jnp.where(p_r // FOX_HEAD_DIM == p_c // FOX_HEAD_DIM, 1.0 / FOX_HEAD_DIM, 0.0).astype(BF16)

    def mean_sq(ref, j):
        x = ref[0, :, j * LANE:(j + 1) * LANE]
        return jnp.dot((x * x).astype(BF16), head_mean, preferred_element_type=F32)

    def normed(ref, g_ref, j, ms):
        lanes = slice(j * LANE, (j + 1) * LANE)
        return (ref[0, :, lanes] * lax.rsqrt(ms + EPS) * g_ref[:, lanes]).astype(BF16)

    def placed(x, p_ref, j):
        return jnp.dot(jnp.concatenate([x, gate_row], axis=1), p_ref[j], preferred_element_type=F32)

    pairs = range(FOX_HEADS // 2)
    q_ms = [mean_sq(fq_ref, j) for j in pairs]
    k_ms = [mean_sq(fk_ref, j) for j in pairs]
    q_n = [normed(fq_ref, gq_ref, j, q_ms[j]) for j in pairs]
    k_n = [normed(fk_ref, gk_ref, j, k_ms[j]) for j in pairs]
    for j in pairs:
        q = placed(q_n[j], pq_ref, j)
        k = placed(k_n[j], pk_ref, j)
        v = placed(fv_ref[0, :, j * LANE:(j + 1) * LANE].astype(BF16), pv_ref, j)
        for hh in range(2):
            head = slice(hh * HEAD_PAD, (hh + 1) * HEAD_PAD)
            q_ref[0, 2 * j + hh] = q[:, head].astype(BF16)
            _store_key_blocks(k_ref, 2 * j + hh, k[:, head])
            v_ref[0, 2 * j + hh] = v[:, head].astype(BF16)


def _fox_prep_specs(tl):
    row = lambda b, i: (b, i, 0)
    c2 = lambda b, i: (0, 0)
    c3 = lambda b, i: (0, 0, 0)
    head_out = pl.BlockSpec((1, FOX_HEADS, tl, HEAD_PAD), lambda b, i: (b, 0, i, 0))
    head_shape = jax.ShapeDtypeStruct((BATCH, FOX_HEADS, SEQ, HEAD_PAD), BF16)
    place = pl.BlockSpec((FOX_HEADS // 2, 2 * LANE, 2 * HEAD_PAD), c3)
    in_specs = [pl.BlockSpec((1, tl, ATT_WIDTH), row),
                pl.BlockSpec((1, tl, ATT_WIDTH), row),
                pl.BlockSpec((1, tl, ATT_WIDTH), row),
                pl.BlockSpec((1, tl, LANE), row),
                pl.BlockSpec((1, LANE), c2),
                pl.BlockSpec((1, ATT_WIDTH), c2),
                pl.BlockSpec((1, ATT_WIDTH), c2),
                place, place, place]
    return (in_specs, [head_out, _KT_SPEC(FOX_HEADS, tl), head_out],
            [head_shape, _KT_SHAPE(FOX_HEADS), head_shape])


def _front_kernel(*refs, n_mla, n_fox):
    x_ref, g_ref, sh_ref, sc_ref, w_ref = refs[:5]
    mla_rest, fox_rest = refs[5:5 + n_mla], refs[5 + n_mla:5 + n_mla + n_fox]
    u_ref = refs[5 + n_mla + n_fox]
    outs = refs[6 + n_mla + n_fox:12 + n_mla + n_fox]
    proj_ref, carry_ref = refs[-2:]
    _inproj_into(x_ref, g_ref, sh_ref, sc_ref, w_ref, proj_ref, u_ref)
    cols = [proj_ref.at[:, :, c0:c1] for c0, c1 in _IN_GROUPS]
    _mla_prep_kernel(cols[1], cols[2], cols[3], *mla_rest, *outs[:3])
    _fox_prep_kernel(cols[4], cols[5], cols[6], cols[3], *fox_rest, *outs[3:], carry_ref)


def _front_call(x, g, sh, sc, w, layer, mla_rest, fox_rest):
    tl = ROW_TILE
    row = lambda b, i: (b, i, 0)
    per_b = lambda b, i: (b, 0, 0)
    const = lambda b, i: (0, 0)
    mla_in, mla_out, mla_shape = _mla_prep_specs(tl)
    fox_in, fox_out, fox_shape = _fox_prep_specs(tl)
    mla_in, fox_in = mla_in[3:], fox_in[4:]
    outs = pl.pallas_call(
        functools.partial(_front_kernel, n_mla=len(mla_in), n_fox=len(fox_in)),
        grid=(BATCH, SEQ // tl),
        in_specs=[pl.BlockSpec((1, tl, D_MODEL), row),
                  pl.BlockSpec((1, D_MODEL), const),
                  pl.BlockSpec((1, 1, D_MODEL), per_b),
                  pl.BlockSpec((1, 1, D_MODEL), per_b),
                  pl.BlockSpec((1, D_MODEL, IN_PAD), lambda b, i: (layer, 0, 0))] + mla_in + fox_in,
        out_specs=[pl.BlockSpec((tl, SSM_WIDTH), lambda b, i: (i, b))] + mla_out + fox_out,
        out_shape=[jax.ShapeDtypeStruct((SEQ, BATCH * SSM_WIDTH), F32)] + mla_shape + fox_shape,
        scratch_shapes=[pltpu.VMEM((1, tl, IN_PAD), F32), pltpu.VMEM((SUBLANE, LANE), F32)],
        compiler_params=_params(("arbitrary", "arbitrary")),
        name="front",
    )(x, g, sh, sc, w, *mla_rest, *fox_rest)
    return outs[0], outs[1:4], outs[4:]


def _fox_placements():
    pq = np.zeros((FOX_HEADS // 2, 2 * LANE, 2 * HEAD_PAD), np.float32)
    pk = np.zeros_like(pq)
    pv = np.zeros_like(pq)
    one_row = LANE + GATE_ONE_LANE
    for j in range(FOX_HEADS // 2):
        for hh in range(2):
            h, col0 = 2 * j + hh, hh * HEAD_PAD
            for d in range(FOX_HEAD_DIM):
                for p in (pq, pk, pv):
                    p[j, hh * FOX_HEAD_DIM + d, col0 + d] = 1.0
            pv[j, one_row, col0 + ONES_LANE] = 1.0
            for n, piece_lane in enumerate((0, GATE_MID_LANE, GATE_LO_LANE)):
                pq[j, LANE + piece_lane + h, col0 + Q_GATE_LANE + n] = 1.0
                pq[j, one_row, col0 + K_GATE_LANE + n] = 1.0
                pk[j, one_row, col0 + Q_GATE_LANE + n] = 1.0
                pk[j, LANE + piece_lane + h, col0 + K_GATE_LANE + n] = -1.0
    return tuple(jnp.asarray(p, BF16) for p in (pq, pk, pv))


def _fox_operands(bf, gq, gk):
    q_scale = LOG2E / math.sqrt(FOX_HEAD_DIM)
    return (_pad_lanes(bf[None, :], LANE), jnp.tile(gq * q_scale, FOX_HEADS)[None, :],
            jnp.tile(gk, FOX_HEADS)[None, :]) + _fox_placements()


def _flash_kernel(qa_ref, qb_ref, kt_ref, v_ref, gap_ref, o_ref, q_scr, s_ref, m_ref, acc_ref,
                  *, tile, chunk, n_tiles):
    p = pl.program_id(2)
    tiles = (p, n_tiles - 1 - p)
    n_tasks = n_tiles + 1
    half = tile // 2
    top, bottom = slice(0, half), slice(half, tile)
    lane = lax.broadcasted_iota(jnp.int32, (tile, HEAD_PAD), 1)
    q_scr[0] = qa_ref[0]
    q_scr[1] = qb_ref[0]

    def plain_task(t):
        second = t - 2 >= p
        return second, second.astype(jnp.int32), jnp.where(second, t - 2 - p, t - 2)

    def row_max_update(w, hh, rows, s):
        mr = m_ref[w, hh, rows]
        for c in range(s.shape[1] // LANE):
            mr = jnp.maximum(mr, s[:, c * LANE:(c + 1) * LANE])
        m_ref[w, hh, rows] = mr

    m_ref[...] = jnp.full(m_ref.shape, NEG, F32)
    for w in range(2):
        for hh in range(2):
            kt = kt_ref[0, hh, tiles[w]]
            s_top = jnp.dot(q_scr[w, hh, top], kt[:, top], preferred_element_type=F32)
            s_top = jnp.where(gap_ref[top, top] <= 0, s_top, NEG)
            s_ref[hh, w, top, top] = s_top
            row_max_update(w, hh, top, s_top)
            s_bot = jnp.dot(q_scr[w, hh, bottom], kt, preferred_element_type=F32)
            s_bot = jnp.where(gap_ref[bottom, :] <= 0, s_bot, NEG)
            s_ref[hh, w, bottom] = s_bot
            row_max_update(w, hh, bottom, s_bot)
    for t in range(2, n_tasks):
        _, which, j = plain_task(jnp.int32(t))
        for hh in range(2):
            s = jnp.dot(q_scr[which, hh], kt_ref[0, hh, j], preferred_element_type=F32)
            s_ref[hh, t] = s
            row_max_update(which, hh, slice(None), s)

    ms = [[jnp.max(m_ref[w, hh], axis=1, keepdims=True) for hh in range(2)] for w in range(2)]

    acc_ref[...] = jnp.zeros(acc_ref.shape, F32)
    for w in range(2):
        k0 = pl.multiple_of(tiles[w] * tile, tile)
        for hh in range(2):
            pr = jnp.exp2(s_ref[hh, w, top, top] - ms[w][hh][top]).astype(BF16)
            acc_ref[w, hh, top] += jnp.dot(pr, v_ref[0, hh, pl.ds(k0, half), :], preferred_element_type=F32)
            pr = jnp.exp2(s_ref[hh, w, bottom] - ms[w][hh][bottom]).astype(BF16)
            acc_ref[w, hh, bottom] += jnp.dot(pr, v_ref[0, hh, pl.ds(k0, tile), :], preferred_element_type=F32)
    for t in range(2, n_tasks):
        second, which, j = plain_task(jnp.int32(t))
        k0 = pl.multiple_of(j * tile, tile)
        for hh in range(2):
            row_max = jnp.where(second, ms[1][hh], ms[0][hh])
            pr = jnp.exp2(s_ref[hh, t] - row_max).astype(BF16)
            acc_ref[which, hh] += jnp.dot(pr, v_ref[0, hh, pl.ds(k0, tile), :], preferred_element_type=F32)

    for w in range(2):
        outs = [acc_ref[w, hh] / acc_ref[w, hh][:, ONES_LANE:ONES_LANE + 1] for hh in range(2)]
        o_ref[0, w, 0] = jnp.where(lane < 64, outs[0], pltpu.roll(outs[1], 64, 1)).astype(BF16)


def _flash_call(q, kt, v, chunk):
    tile = ATT_TILE
    heads = q.shape[1]
    n_tiles = SEQ // tile
    pos = np.arange(tile, dtype=np.int32) // chunk
    gap = jnp.asarray(pos[None, :] - pos[:, None])
    return pl.pallas_call(
        functools.partial(_flash_kernel, tile=tile, chunk=chunk, n_tiles=n_tiles),
        grid=(BATCH, heads // 2, n_tiles // 2),
        in_specs=[pl.BlockSpec((1, 2, tile, HEAD_PAD), lambda b, hp, p: (b, hp, p, 0)),
                  pl.BlockSpec((1, 2, tile, HEAD_PAD), lambda b, hp, p: (b, hp, n_tiles - 1 - p, 0)),
                  pl.BlockSpec((1, 2, n_tiles, HEAD_PAD, tile), lambda b, hp, p: (b, hp, 0, 0, 0)),
                  pl.BlockSpec((1, 2, SEQ, HEAD_PAD), lambda b, hp, p: (b, hp, 0, 0)),
                  pl.BlockSpec((tile, tile), lambda b, hp, p: (0, 0))],
        out_specs=pl.BlockSpec((1, 2, 1, tile, LANE), lambda b, hp, p: (b, 0, p, 0, hp)),
        out_shape=jax.ShapeDtypeStruct((BATCH, 2, n_tiles // 2, tile, ATT_WIDTH), BF16),
        scratch_shapes=[pltpu.VMEM((2, 2, tile, HEAD_PAD), BF16),
                        pltpu.VMEM((2, n_tiles + 1, tile, tile), F32),
                        pltpu.VMEM((2, 2, tile, LANE), F32),
                        pltpu.VMEM((2, 2, tile, HEAD_PAD), F32)],
        compiler_params=_params(("arbitrary", "arbitrary", "arbitrary")),
        name="flash_chunk%d" % chunk,
    )(q, q, kt, v, gap)


def _merge_kernel(ssm_ref, mla_ref, fox_ref, x_ref, g1_ref, gm_ref, gf_ref, w_ref, o_ref):
    def normed(ref, g_ref):
        a = ref[0, 0, 0].astype(F32)
        return (a * lax.rsqrt(jnp.mean(a * a, axis=-1, keepdims=True) + EPS) * g_ref[...]).astype(BF16)

    merged = jnp.concatenate([ssm_ref[...], normed(mla_ref, gm_ref), normed(fox_ref, gf_ref)], axis=1)
    mix = jnp.dot(merged, w_ref[...], preferred_element_type=F32)
    o_ref[0] = x_ref[0] + g1_ref[0] * mix


def _merge_call(o_ssm, o_mla, o_fox, x, g1, gm, gf, w):
    tm = ATT_TILE
    half = SEQ // tm // 2
    row = lambda b, i: (b, i, 0)
    c2 = lambda b, i: (0, 0)
    att = pl.BlockSpec((1, 1, 1, tm, ATT_WIDTH),
                       lambda b, i: (b, i // half, jnp.where(i < half, i, 2 * half - 1 - i), 0, 0))
    return pl.pallas_call(
        _merge_kernel,
        grid=(BATCH, SEQ // tm),
        in_specs=[pl.BlockSpec((tm, SSM_WIDTH), lambda b, i: (i, b)),
                  att, att,
                  pl.BlockSpec((1, tm, D_MODEL), row),
                  pl.BlockSpec((1, 1, D_MODEL), lambda b, i: (b, 0, 0)),
                  pl.BlockSpec((1, ATT_WIDTH), c2),
                  pl.BlockSpec((1, ATT_WIDTH), c2),
                  pl.BlockSpec((D_MODEL, D_MODEL), c2)],
        out_specs=pl.BlockSpec((1, tm, D_MODEL), row),
        out_shape=jax.ShapeDtypeStruct((BATCH, SEQ, D_MODEL), F32),
        compiler_params=_params(("arbitrary", "arbitrary")),
        name="merge",
    )(o_ssm, o_mla, o_fox, x, g1, gm, gf, w)


def _ffn_kernel(x_ref, g_ref, sh_ref, sc_ref, g2_ref, wg_ref, wu_ref, wd_ref, o_ref):
    half = x_ref.shape[1] // 2
    for r in (slice(0, half), slice(half, 2 * half)):
        x = x_ref[0, r]
        h = _rms_mod(x, g_ref[...], sc_ref[0], sh_ref[0]).astype(BF16)
        gate_up = [(jnp.dot(h, wg_ref[0, :, c0:c1], preferred_element_type=F32),
                    jnp.dot(h, wu_ref[0, :, c0:c1], preferred_element_type=F32)) for c0, c1 in FF_PARTS]
        acc = jnp.zeros((half, D_MODEL), F32)
        for (c0, c1), (gate, up) in zip(FF_PARTS, gate_up):
            a = (gate * jax.nn.sigmoid(gate) * up).astype(BF16)
            acc = acc + jnp.dot(a, wd_ref[0, c0:c1, :], preferred_element_type=F32)
        o_ref[0, r] = x + g2_ref[0] * acc


def _ffn_call(x, g, sh, sc, g2, wg, wu, wd, layer):
    tm = MOE_TILE
    row = lambda b, i: (b, i, 0)
    per_b = lambda b, i: (b, 0, 0)
    resident = lambda shape: pl.BlockSpec(shape, lambda b, i: (layer, 0, 0), pipeline_mode=pl.Buffered(1))
    return pl.pallas_call(
        _ffn_kernel,
        grid=(BATCH, SEQ // tm),
        in_specs=[pl.BlockSpec((1, tm, D_MODEL), row),
                  pl.BlockSpec((1, D_MODEL), lambda b, i: (0, 0)),
                  pl.BlockSpec((1, 1, D_MODEL), per_b),
                  pl.BlockSpec((1, 1, D_MODEL), per_b),
                  pl.BlockSpec((1, 1, D_MODEL), per_b),
                  resident((1, D_MODEL, D_FF)),
                  resident((1, D_MODEL, D_FF)),
                  resident((1, D_FF, D_MODEL))],
        out_specs=pl.BlockSpec((1, tm, D_MODEL), row),
        out_shape=jax.ShapeDtypeStruct((BATCH, SEQ, D_MODEL), F32),
        compiler_params=_params(("arbitrary", "arbitrary")),
        name="ffn_dense",
    )(x, g, sh, sc, g2, wg, wu, wd)


def _router_kernel(x_ref, g_ref, sh_ref, sc_ref, w_ref, b_ref, comb_ref, rank_ref, rankt_ref, count_ref):
    tm = x_ref.shape[1]
    h = _rms_mod(x_ref[0], g_ref[...], sc_ref[0], sh_ref[0])
    h_hi = h.astype(BF16)
    h_lo = (h - h_hi.astype(F32)).astype(BF16)
    parts = jnp.dot(jnp.concatenate([h_hi, h_lo], axis=1), w_ref[...], preferred_element_type=F32)
    logits = parts + pltpu.roll(parts, LANE - N_EXPERTS, 1) + b_ref[...]
    lane = lax.broadcasted_iota(jnp.int32, logits.shape, 1)
    logits = jnp.where(lane < N_EXPERTS, logits, -jnp.inf)
    m1 = jnp.max(logits, axis=-1, keepdims=True)
    i1 = jnp.min(jnp.where(logits == m1, lane, LANE), axis=-1, keepdims=True)
    rest = jnp.where(lane == i1, -jnp.inf, logits)
    m2 = jnp.max(rest, axis=-1, keepdims=True)
    i2 = jnp.min(jnp.where(rest == m2, lane, LANE), axis=-1, keepdims=True)
    e = jnp.exp(m2 - m1)
    p1 = 1.0 / (1.0 + e)
    comb_ref[0] = jnp.where(lane == i1, p1, 0.0) + jnp.where(lane == i2, e * p1, 0.0)

    chosen = (lane == i1) | (lane == i2)
    chosen_f = jnp.where(chosen, 1.0, 0.0)
    r_i = lax.broadcasted_iota(jnp.int32, (tm, tm), 0)
    c_i = lax.broadcasted_iota(jnp.int32, (tm, tm), 1)
    earlier = jnp.where(c_i < r_i, 1.0, 0.0).astype(BF16)
    rank = jnp.dot(earlier, chosen_f.astype(BF16), preferred_element_type=F32)
    rank = jnp.where(chosen, rank, -1.0)
    rank_ref[0] = rank
    rankt_ref[0] = rank.T[0:SUBLANE, :]
    count_ref[0] = jnp.sum(chosen_f, axis=0, keepdims=True)


def _router_call(x, g, sh, sc, w, b):
    tm = MOE_TILE
    tiles = SEQ // tm
    row = lambda b_, i: (b_, i, 0)
    per_b = lambda b_, i: (b_, 0, 0)
    per_tile = lambda b_, i: (b_ * tiles + i, 0, 0)
    return pl.pallas_call(
        _router_kernel,
        grid=(BATCH, tiles),
        in_specs=[pl.BlockSpec((1, tm, D_MODEL), row),
                  pl.BlockSpec((1, D_MODEL), lambda b_, i: (0, 0)),
                  pl.BlockSpec((1, 1, D_MODEL), per_b),
                  pl.BlockSpec((1, 1, D_MODEL), per_b),
                  pl.BlockSpec((2 * D_MODEL, LANE), lambda b_, i: (0, 0)),
                  pl.BlockSpec((1, LANE), lambda b_, i: (0, 0))],
        out_specs=[pl.BlockSpec((1, tm, LANE), row),
                   pl.BlockSpec((1, tm, LANE), row),
                   pl.BlockSpec((1, SUBLANE, tm), per_tile),
                   pl.BlockSpec((1, 1, LANE), per_tile)],
        out_shape=[jax.ShapeDtypeStruct((BATCH, SEQ, LANE), F32),
                   jax.ShapeDtypeStruct((BATCH, SEQ, LANE), F32),
                   jax.ShapeDtypeStruct((BATCH * tiles, SUBLANE, tm), F32),
                   jax.ShapeDtypeStruct((BATCH * tiles, 1, LANE), F32)],
        compiler_params=_params(("arbitrary", "arbitrary")),
        name="router",
    )(x, g, sh, sc, w, b)


def _moe_kernel(count_ref, x_ref, g_ref, sh_ref, sc_ref, g2_ref, comb_ref, rank_ref, rankt_ref,
                wgu_ref, wd_ref, o_ref, h_ref):
    tm = x_ref.shape[1]
    e = pl.program_id(1)

    @pl.when(e == 0)
    def _():
        x = x_ref[0]
        h_ref[...] = _rms_mod(x, g_ref[...], sc_ref[0], sh_ref[0]).astype(BF16)
        o_ref[0] = x

    lane = lax.broadcasted_iota(jnp.int32, (tm, LANE), 1)
    mine = lane == e
    rank_col = jnp.sum(jnp.where(mine, rank_ref[0], 0.0), axis=-1, keepdims=True)
    gate_col = jnp.sum(jnp.where(mine, comb_ref[0], 0.0), axis=-1, keepdims=True)
    rank_row = rankt_ref[0, pl.ds(e, 1), :]
    count = count_ref[pl.program_id(0) * N_EXPERTS + e]

    def expert_pass(first, n_rows):
        base = first.astype(F32)
        slot_sub = lax.broadcasted_iota(jnp.int32, (n_rows, tm), 0).astype(F32)
        slot_lane = lax.broadcasted_iota(jnp.int32, (tm, n_rows), 1).astype(F32)
        pick = jnp.where(rank_row - base == slot_sub, 1.0, 0.0).astype(BF16)
        rows = jnp.dot(pick, h_ref[...], preferred_element_type=F32).astype(BF16)
        gate_up = jnp.dot(rows, wgu_ref[0, 0], preferred_element_type=F32)
        gate, up = gate_up[:, :D_FF_EXPERT], gate_up[:, D_FF_EXPERT:]
        a = (gate * jax.nn.sigmoid(gate) * up).astype(BF16)
        y = jnp.dot(a, wd_ref[0, 0], preferred_element_type=F32).astype(BF16)
        place = jnp.where(rank_col - base == slot_lane, 1.0, 0.0).astype(BF16)
        back = jnp.dot(place, y, preferred_element_type=F32)
        o_ref[0] += g2_ref[0] * (gate_col * back)

    def full_pass(sb, carry):
        expert_pass(sb * MOE_ROWS, MOE_ROWS)
        return carry

    n_full = count // MOE_ROWS
    lax.fori_loop(0, n_full, full_pass, 0)
    left = count - n_full * MOE_ROWS

    @pl.when(left > MOE_ROWS // 2)
    def _():
        expert_pass(n_full * MOE_ROWS, MOE_ROWS)

    @pl.when((left > 0) & (left <= MOE_ROWS // 2))
    def _():
        expert_pass(n_full * MOE_ROWS, MOE_ROWS // 2)


def _moe_call(x, g, sh, sc, g2, comb, rank, rankt, counts, wgu, wd, layer):
    tm = MOE_TILE
    tiles = SEQ // tm
    n_tiles = BATCH * tiles
    row = lambda i, e, cnt: (i, 0, 0)
    per_b = lambda i, e, cnt: (i // tiles, 0, 0)
    expert = lambda i, e, cnt: (layer, e, 0, 0)
    as_tiles = lambda a: a.reshape(n_tiles, tm, a.shape[-1])
    grid_spec = pltpu.PrefetchScalarGridSpec(
        num_scalar_prefetch=1,
        grid=(n_tiles, N_EXPERTS),
        in_specs=[pl.BlockSpec((1, tm, D_MODEL), row),
                  pl.BlockSpec((1, D_MODEL), lambda i, e, cnt: (0, 0)),
                  pl.BlockSpec((1, 1, D_MODEL), per_b),
                  pl.BlockSpec((1, 1, D_MODEL), per_b),
                  pl.BlockSpec((1, 1, D_MODEL), per_b),
                  pl.BlockSpec((1, tm, LANE), row),
                  pl.BlockSpec((1, tm, LANE), row),
                  pl.BlockSpec((1, SUBLANE, tm), row),
                  pl.BlockSpec((1, 1, D_MODEL, 2 * D_FF_EXPERT), expert),
                  pl.BlockSpec((1, 1, D_FF_EXPERT, D_MODEL), expert)],
        out_specs=pl.BlockSpec((1, tm, D_MODEL), row),
        scratch_shapes=[pltpu.VMEM((tm, D_MODEL), BF16)],
    )
    out = pl.pallas_call(
        _moe_kernel,
        grid_spec=grid_spec,
        out_shape=jax.ShapeDtypeStruct((n_tiles, tm, D_MODEL), F32),
        compiler_params=_params(("arbitrary", "arbitrary")),
        name="moe_experts",
    )(counts, as_tiles(x), g, sh, sc, g2, as_tiles(comb), as_tiles(rank), rankt, wgu, wd)
    return out.reshape(BATCH, SEQ, D_MODEL)


def kernel(x, c, positions, norm_mix, norm_ffn, w_ada, b_ada, w_in, ssm_lam_re, ssm_lam_im, ssm_log_dt, ssm_b_re, ssm_b_im, ssm_c_re, ssm_c_im, ssm_d, ssm_w_glu, ssm_b_glu, mla_q_norm, mla_kv_norm, mla_w_uq, mla_w_ukv, mla_qk_gq, mla_qk_gk, fox_b_f, fox_qk_gq, fox_qk_gk, out_norm, w_out, ffn_w_gate, ffn_w_up, ffn_w_down, moe_w_router, moe_b_router, moe_w_gate, moe_w_up, moe_w_down):
    tabs = _rope_tables(positions)
    w_in_packed = _pack_w_in(w_in)
    moe_wgu = jnp.concatenate([moe_w_gate, moe_w_up], axis=-1).astype(BF16)
    moe_wd = moe_w_down.astype(BF16)
    ffn_wg, ffn_wu, ffn_wd = (w.astype(BF16) for w in (ffn_w_gate, ffn_w_up, ffn_w_down))
    ada = _ada_call(c, w_ada, b_ada)
    ada = ada.reshape(DEPTH, BATCH, 6, 1, D_MODEL)
    row2 = lambda a: a[None, :]

    for i in range(DEPTH):
        sh1, sc1, g1, sh2, sc2, g2 = (ada[i, :, n] for n in range(6))

        u, mla_qkv, fox_qkv = _front_call(
            x, row2(norm_mix[i]), sh1, sc1, w_in_packed, i,
            (*tabs, row2(mla_q_norm[i]), row2(mla_kv_norm[i]),
             *_mla_weights(mla_w_uq[i], mla_w_ukv[i], mla_qk_gq[i], mla_qk_gk[i])),
            _fox_operands(fox_b_f[i], fox_qk_gq[i], fox_qk_gk[i]))

        bmat, lam, cmat = _s5_operands(ssm_lam_re[i], ssm_lam_im[i], ssm_log_dt[i],
                                       ssm_b_re[i], ssm_b_im[i], ssm_c_re[i], ssm_c_im[i])
        o_ssm = _s5_call(u, bmat, lam, cmat, row2(ssm_d[i]), ssm_w_glu[i].astype(BF16),
                         row2(ssm_b_glu[i]), row2(out_norm[i, :SSM_WIDTH]))

        o_mla = _flash_call(*mla_qkv, CHUNK)
        o_fox = _flash_call(*fox_qkv, 1)

        e1, e2 = SSM_WIDTH, SSM_WIDTH + ATT_WIDTH
        x = _merge_call(o_ssm, o_mla, o_fox, x, g1, row2(out_norm[i, e1:e2]), row2(out_norm[i, e2:]),
                        w_out[i].astype(BF16))

        j = i // 2
        if i % 2 == 0:
            x = _ffn_call(x, row2(norm_ffn[i]), sh2, sc2, g2, ffn_wg, ffn_wu, ffn_wd, layer=j)
        else:
            wr_hi = moe_w_router[j].astype(BF16)
            wr_lo = (moe_w_router[j] - wr_hi.astype(F32)).astype(BF16)
            wr = _pad_lanes(jnp.concatenate([wr_hi, wr_lo], axis=1), LANE)
            comb, rank, rankt, counts = _router_call(x, row2(norm_ffn[i]), sh2, sc2, jnp.concatenate([wr, wr], axis=0),
                                                     _pad_lanes(row2(moe_b_router[j]), LANE))
            counts = counts[:, 0, :N_EXPERTS].astype(jnp.int32).reshape(-1)
            x = _moe_call(x, row2(norm_ffn[i]), sh2, sc2, g2, comb, rank, rankt, counts,
                          moe_wgu, moe_wd, layer=j)
    return x
```

```python
import functools
import math

import jax
import jax.numpy as jnp
import numpy as np
from jax import lax
from jax.experimental import pallas as pl
from jax.experimental.pallas import tpu as pltpu

F32 = jnp.float32
BF16 = jnp.bfloat16

D_MODEL = 1024
BATCH = 8
SEQ = 4096
DEPTH = 4
CHUNK = 64
EPS = 1e-6

SSM_WIDTH = 256
SSM_GROUP = 16
N_SSM_GROUPS = 16
SSM_STATE = 64
N_STATE = N_SSM_GROUPS * SSM_STATE

MLA_HEADS = 6
MLA_Q_RANK = 256
MLA_KV_RANK = 128
MLA_NOPE = 64
MLA_ROPE = 32
MLA_V = 64
MLA_QK = 96
ROPE_BASE = 10000.0

FOX_HEADS = 6
FOX_HEAD_DIM = 64
ATT_WIDTH = 384

D_FF = 2816
N_EXPERTS = 8
D_FF_EXPERT = 1408
EXPERT_MAIN = 1280

LANE = 128
SUBLANE = 8
HEAD_PAD = LANE
ONES_LANE = 64
NEG = -1e30

IN_PAD = 1920
KR_LANE = 64

ROW_TILE = 512
S5_STEPS = 64
ATT_TILE = 512
LOG2E = math.log2(math.e)
FF_PARTS = ((0, 1536), (1536, D_FF))
MOE_TILE = 1024
MOE_ROWS = 256
VMEM_LIMIT = 56 * 1024 * 1024


def _params(sem):
    return pltpu.CompilerParams(dimension_semantics=sem, vmem_limit_bytes=VMEM_LIMIT)


def _rms_mod(x, g, sc, sh):
    ms = jnp.mean(x * x, axis=-1, keepdims=True)
    h = x * lax.rsqrt(ms + EPS) * g
    return h * (1.0 + sc) + sh


def _split3(x):
    hi = x.astype(BF16).astype(F32)
    r = x - hi
    mid = r.astype(BF16).astype(F32)
    lo = (r - mid).astype(BF16).astype(F32)
    return hi, mid, lo


def _ada_kernel(c_ref, w_ref, b_ref, o_ref):
    c = c_ref[...]
    ca = (c * jax.nn.sigmoid(c)).astype(BF16)
    o_ref[0] = jnp.dot(ca, w_ref[0].astype(BF16), preferred_element_type=F32) + b_ref[0]


def _ada_call(c, w_ada, b_ada):
    tn = 1536
    return pl.pallas_call(
        _ada_kernel,
        grid=(DEPTH, 6 * D_MODEL // tn),
        in_specs=[pl.BlockSpec((BATCH, D_MODEL), lambda i, j: (0, 0)),
                  pl.BlockSpec((1, D_MODEL, tn), lambda i, j: (i, 0, j)),
                  pl.BlockSpec((1, 1, tn), lambda i, j: (i, 0, j))],
        out_specs=pl.BlockSpec((1, BATCH, tn), lambda i, j: (i, 0, j)),
        out_shape=jax.ShapeDtypeStruct((DEPTH, BATCH, 6 * D_MODEL), F32),
        compiler_params=_params(("arbitrary", "arbitrary")),
        name="ada",
    )(c, w_ada, b_ada.reshape(DEPTH, 1, 6 * D_MODEL))


_IN_GROUPS = ((0, 256), (256, 512), (512, 640), (640, 768), (768, 1152), (1152, 1536), (1536, 1920))


def _inproj_into(x_ref, g_ref, sh_ref, sc_ref, w_ref, proj_ref, u_ref):
    parts = 4
    step = x_ref.shape[1] // parts
    rows = [slice(r * step, (r + 1) * step) for r in range(parts)]
    normed = lambda r: _rms_mod(x_ref[0, rows[r]], g_ref[...], sc_ref[0], sh_ref[0]).astype(BF16)
    h_next = normed(0)
    for r in range(parts):
        h = h_next
        if r + 1 < parts:
            h_next = normed(r + 1)
        proj = jnp.dot(h, w_ref[0], preferred_element_type=F32)
        proj_ref[0, rows[r]] = proj
        u_ref[rows[r]] = proj[:, _IN_GROUPS[0][0]:_IN_GROUPS[0][1]]


def _pack_w_in(w):
    u, cq, ckv, kr, fq, fk, fv, fg = jnp.split(
        w.astype(BF16), (256, 512, 640, 672, 1056, 1440, 1824), axis=2)
    z = lambda n: jnp.zeros(w.shape[:2] + (n,), BF16)
    krfg = jnp.concatenate([fg, z(KR_LANE - FOX_HEADS), kr, z(LANE - KR_LANE - MLA_ROPE)], axis=2)
    return jnp.concatenate([u, cq, ckv, krfg, fq, fk, fv], axis=2)


def _s5_kernel(u2_ref, bmat_ref, lam_ref, cmat_ref, d_ref, wglu_ref, bglu_ref, gn_ref,
               o2_ref, u_ref, o_ref, bu0_ref, bu1_ref, state_ref, *, steps):
    rows = steps * BATCH
    lane_tiles = SSM_WIDTH // LANE
    for b in range(BATCH):
        for c in range(lane_tiles):
            lanes = slice(b * SSM_WIDTH + c * LANE, b * SSM_WIDTH + (c + 1) * LANE)
            u_ref.at[c][pl.ds(b, 2 * steps, stride=BATCH), :] = u2_ref[:, lanes]
    u_rows = lambda rs: jnp.concatenate([u_ref[c, rs, :] for c in range(lane_tiles)], axis=1)

    @pl.when(pl.program_id(0) == 0)
    def _():
        state_ref[...] = jnp.zeros_like(state_ref)

    halves = ((bu0_ref, slice(0, rows)), (bu1_ref, slice(rows, 2 * rows)))
    for bu_ref, rs in halves:
        bu_ref[...] = jnp.dot(u_rows(rs).astype(BF16), bmat_ref[...], preferred_element_type=F32)
    lr = jnp.broadcast_to(lam_ref[0:1, :], (SUBLANE, N_STATE))
    li = jnp.broadcast_to(lam_ref[1:2, :], (SUBLANE, N_STATE))
    sr, si = state_ref[:, 0:N_STATE], state_ref[:, N_STATE:2 * N_STATE]

    for bu_ref, rs in halves:
        for t in range(steps):
            r = slice(t * SUBLANE, (t + 1) * SUBLANE)
            nr = lr * sr - li * si + bu_ref[r, 0:N_STATE]
            ni = lr * si + li * sr + bu_ref[r, N_STATE:2 * N_STATE]
            bu_ref[r, 0:N_STATE] = nr
            bu_ref[r, N_STATE:2 * N_STATE] = ni
            sr, si = nr, ni
        y = jnp.dot(bu_ref[...].astype(BF16), cmat_ref[...], preferred_element_type=F32)
        y = jax.nn.gelu(y + d_ref[...] * u_rows(rs))
        gate = jnp.dot(y.astype(BF16), wglu_ref[...], preferred_element_type=F32) + bglu_ref[...]
        o = y * jax.nn.sigmoid(gate)
        ms = jnp.mean(o * o, axis=-1, keepdims=True)
        o = o * lax.rsqrt(ms + EPS) * gn_ref[...]
        for c in range(lane_tiles):
            o_ref[c, rs, :] = o[:, c * LANE:(c + 1) * LANE]

    state_ref[:, 0:N_STATE] = sr
    state_ref[:, N_STATE:2 * N_STATE] = si
    for b in range(BATCH):
        for c in range(lane_tiles):
            lanes = slice(b * SSM_WIDTH + c * LANE, b * SSM_WIDTH + (c + 1) * LANE)
            o2_ref[:, lanes] = o_ref.at[c][pl.ds(b, 2 * steps, stride=BATCH), :].astype(BF16)


def _s5_call(u2, bmat, lam, cmat, d_skip, wglu, bglu, gn):
    rows = S5_STEPS * BATCH
    const = lambda i: (0, 0)
    return pl.pallas_call(
        functools.partial(_s5_kernel, steps=S5_STEPS),
        grid=(SEQ // (2 * S5_STEPS),),
        in_specs=[pl.BlockSpec((2 * S5_STEPS, BATCH * SSM_WIDTH), lambda i: (i, 0)),
                  pl.BlockSpec((SSM_WIDTH, 2 * N_STATE), const),
                  pl.BlockSpec((2, N_STATE), const),
                  pl.BlockSpec((2 * N_STATE, SSM_WIDTH), const),
                  pl.BlockSpec((1, SSM_WIDTH), const),
                  pl.BlockSpec((SSM_WIDTH, SSM_WIDTH), const),
                  pl.BlockSpec((1, SSM_WIDTH), const),
                  pl.BlockSpec((1, SSM_WIDTH), const)],
        out_specs=pl.BlockSpec((2 * S5_STEPS, BATCH * SSM_WIDTH), lambda i: (i, 0)),
        out_shape=jax.ShapeDtypeStruct((SEQ, BATCH * SSM_WIDTH), BF16),
        scratch_shapes=[pltpu.VMEM((SSM_WIDTH // LANE, 2 * rows, LANE), F32),
                        pltpu.VMEM((SSM_WIDTH // LANE, 2 * rows, LANE), F32),
                        pltpu.VMEM((rows, 2 * N_STATE), F32),
                        pltpu.VMEM((rows, 2 * N_STATE), F32),
                        pltpu.VMEM((SUBLANE, 2 * N_STATE), F32)],
        compiler_params=_params(("arbitrary",)),
        name="s5",
    )(u2, bmat, lam, cmat, d_skip, wglu, bglu, gn)


def _s5_operands(lam_re, lam_im, log_dt, b_re, b_im, c_re, c_im):
    dt = jnp.exp(log_dt)[:, None]
    mag = jnp.exp(lam_re * dt)
    lb_re = mag * jnp.cos(lam_im * dt)
    lb_im = mag * jnp.sin(lam_im * dt)
    den = lam_re * lam_re + lam_im * lam_im
    co_re = ((lb_re - 1.0) * lam_re + lb_im * lam_im) / den
    co_im = (lb_im * lam_re - (lb_re - 1.0) * lam_im) / den
    bb_re = co_re[..., None] * b_re - co_im[..., None] * b_im
    bb_im = co_re[..., None] * b_im + co_im[..., None] * b_re
    eye = jnp.eye(N_SSM_GROUPS, dtype=F32)
    blk_b = lambda m: jnp.einsum("gpc,gh->gchp", m, eye).reshape(SSM_WIDTH, N_STATE)
    bmat = jnp.concatenate([blk_b(bb_re), blk_b(bb_im)], axis=1).astype(BF16)
    blk_c = lambda m: jnp.einsum("gcp,gh->gphc", m, eye).reshape(N_STATE, SSM_WIDTH)
    cmat = jnp.concatenate([blk_c(c_re), -blk_c(c_im)], axis=0).astype(BF16)
    lam = jnp.stack([lb_re.reshape(N_STATE), lb_im.reshape(N_STATE)], axis=0)
    return bmat, lam, cmat


def _rope_tables(positions):
    half = MLA_ROPE // 2
    inv = ROPE_BASE ** (-jnp.arange(half, dtype=F32) / half)
    ang = inv[:, None] * positions.astype(F32).reshape(1, -1)
    shp = positions.shape
    cos, sin = (lax.optimization_barrier(f(ang)).T.reshape(shp + (half,)) for f in (jnp.cos, jnp.sin))
    one = lambda n: jnp.ones(shp + (n,), F32)
    zero = lambda n: jnp.zeros(shp + (n,), F32)
    cos_t = jnp.concatenate([one(MLA_NOPE), cos, cos, zero(LANE - MLA_QK)], axis=-1)
    sin_t = jnp.concatenate([zero(MLA_NOPE), -sin, sin, zero(LANE - MLA_QK)], axis=-1)
    return cos_t, sin_t


def _swap_rope_halves(a):
    half = MLA_ROPE // 2
    lo, hi = a[..., MLA_NOPE:MLA_NOPE + half], a[..., MLA_NOPE + half:MLA_QK]
    return jnp.concatenate([jnp.zeros_like(a[..., :MLA_NOPE]), hi, lo, jnp.zeros_like(a[..., MLA_QK:])], axis=-1)


def _store_key_blocks(kt_ref, h, k):
    kt = k.T
    for s in range(k.shape[0] // ATT_TILE):
        kt_ref[0, h, s] = kt[:, s * ATT_TILE:(s + 1) * ATT_TILE].astype(BF16)


_KT_SPEC = lambda heads, tl: pl.BlockSpec((1, heads, tl // ATT_TILE, HEAD_PAD, ATT_TILE),
                                          lambda b, i: (b, 0, i, 0, 0))
_KT_SHAPE = lambda heads: jax.ShapeDtypeStruct((BATCH, heads, SEQ // ATT_TILE, HEAD_PAD, ATT_TILE), BF16)


def _mla_prep_kernel(cq_ref, ckv_ref, krfg_ref, cos_ref, sin_ref, qn_ref, kvn_ref, wq_ref, wk_ref, wv_ref,
                     gq_ref, gqs_ref, gk_ref, gks_ref, q_ref, k_ref, v_ref):
    tl = cq_ref.shape[1]
    lane = lax.broadcasted_iota(jnp.int32, (tl, LANE), 1)
    cos, sin = cos_ref[0], sin_ref[0]
    q_scale = LOG2E / math.sqrt(MLA_QK)
    q_cos, q_sin = gq_ref[...] * cos * q_scale, gqs_ref[...] * sin * q_scale
    k_cos, k_sin = gk_ref[...] * cos, gks_ref[...] * sin
    ones = jnp.ones((LANE, LANE), BF16)

    def inv_rms(x):
        ss = jnp.dot((x * x).astype(BF16), ones, preferred_element_type=F32)
        return lax.rsqrt(ss / MLA_QK + EPS)

    cq = cq_ref[0]
    cqn = (cq * lax.rsqrt(jnp.mean(cq * cq, axis=-1, keepdims=True) + EPS) * qn_ref[...]).astype(BF16)
    ckv = ckv_ref[0]
    ckvn = (ckv * lax.rsqrt(jnp.mean(ckv * ckv, axis=-1, keepdims=True) + EPS) * kvn_ref[...]).astype(BF16)
    kr = jnp.where((lane >= KR_LANE) & (lane < KR_LANE + MLA_ROPE), krfg_ref[0], 0.0)
    kr_swapped = jnp.where(lane < KR_LANE + MLA_ROPE // 2, pltpu.roll(kr, LANE - 16, 1), pltpu.roll(kr, 16, 1))
    k_rotary = kr_swapped * k_sin

    heads = range(MLA_HEADS)
    qqs = [jnp.dot(cqn, wq_ref[h], preferred_element_type=F32) for h in heads]
    ks = [jnp.dot(ckvn, wk_ref[h], preferred_element_type=F32) + kr for h in heads]
    q_inv = [inv_rms(qq[:, :LANE]) for qq in qqs]
    k_inv = [inv_rms(k) for k in ks]
    for h in heads:
        q, q_swapped = qqs[h][:, :LANE], qqs[h][:, LANE:]
        q_ref[0, h] = (q_inv[h] * (q * q_cos + q_swapped * q_sin)).astype(BF16)
        _store_key_blocks(k_ref, h, k_inv[h] * (ks[h] * k_cos + k_rotary))
        v = jnp.dot(ckvn, wv_ref[h], preferred_element_type=F32)
        v_ref[0, h] = jnp.where(lane == ONES_LANE, 1.0, v).astype(BF16)


def _mla_prep_specs(tl):
    row = lambda b, i: (b, i, 0)
    c2 = lambda b, i: (0, 0)
    c3 = lambda b, i: (0, 0, 0)
    head_out = pl.BlockSpec((1, MLA_HEADS, tl, HEAD_PAD), lambda b, i: (b, 0, i, 0))
    head_shape = jax.ShapeDtypeStruct((BATCH, MLA_HEADS, SEQ, HEAD_PAD), BF16)
    gain = pl.BlockSpec((1, HEAD_PAD), c2)
    in_specs = [pl.BlockSpec((1, tl, MLA_Q_RANK), row),
                pl.BlockSpec((1, tl, MLA_KV_RANK), row),
                pl.BlockSpec((1, tl, LANE), row),
                pl.BlockSpec((1, tl, LANE), row),
                pl.BlockSpec((1, tl, LANE), row),
                pl.BlockSpec((1, MLA_Q_RANK), c2),
                pl.BlockSpec((1, MLA_KV_RANK), c2),
                pl.BlockSpec((MLA_HEADS, MLA_Q_RANK, 2 * HEAD_PAD), c3),
                pl.BlockSpec((MLA_HEADS, MLA_KV_RANK, HEAD_PAD), c3),
                pl.BlockSpec((MLA_HEADS, MLA_KV_RANK, HEAD_PAD), c3),
                gain, gain, gain, gain]
    return (in_specs, [head_out, _KT_SPEC(MLA_HEADS, tl), head_out],
            [head_shape, _KT_SHAPE(MLA_HEADS), head_shape])


def _pad_lanes(a, n=HEAD_PAD):
    return jnp.pad(a, [(0, 0)] * (a.ndim - 1) + [(0, n - a.shape[-1])])


def _mla_weights(w_uq, w_ukv, gq, gk):
    wq = _pad_lanes(w_uq.reshape(MLA_Q_RANK, MLA_HEADS, MLA_QK).transpose(1, 0, 2))
    wq = jnp.concatenate([wq, _swap_rope_halves(wq)], axis=-1).astype(BF16)
    wkv = w_ukv.reshape(MLA_KV_RANK, MLA_HEADS, MLA_NOPE + MLA_V).transpose(1, 0, 2)
    wk = _pad_lanes(wkv[..., :MLA_NOPE]).astype(BF16)
    wv = _pad_lanes(wkv[..., MLA_NOPE:]).astype(BF16)
    gq, gk = _pad_lanes(gq[None, :]), _pad_lanes(gk[None, :])
    return wq, wk, wv, gq, _swap_rope_halves(gq), gk, _swap_rope_halves(gk)


GATE_MID_LANE = 8
GATE_LO_LANE = 16
GATE_ONE_LANE = LANE - 1
Q_GATE_LANE = FOX_HEAD_DIM
K_GATE_LANE = FOX_HEAD_DIM + 3


def _fox_prep_kernel(fq_ref, fk_ref, fv_ref, krfg_ref, bf_ref, gq_ref, gk_ref, pq_ref, pk_ref, pv_ref,
                     q_ref, k_ref, v_ref, carry_ref):
    tl = fq_ref.shape[1]
    lane = lax.broadcasted_iota(jnp.int32, (tl, LANE), 1)

    @pl.when(pl.program_id(1) == 0)
    def _():
        carry_ref[...] = jnp.zeros_like(carry_ref)

    logf = jax.nn.log_sigmoid(krfg_ref[0] + bf_ref[...])
    logf = jnp.where(lane < FOX_HEADS, logf, 0.0)
    r_i = lax.broadcasted_iota(jnp.int32, (tl, tl), 0)
    c_i = lax.broadcasted_iota(jnp.int32, (tl, tl), 1)
    tri = jnp.where(c_i <= r_i, 1.0, 0.0).astype(BF16)
    cum = carry_ref[0:1, :]
    for piece in _split3(logf):
        cum = cum + jnp.dot(tri, piece.astype(BF16), preferred_element_type=F32)
    carry_ref[0:1, :] = cum[tl - 1:tl, :]

    c_hi, c_mid, c_lo = _split3(cum * LOG2E)
    gate_row = (c_hi + pltpu.roll(c_mid, GATE_MID_LANE, 1) + pltpu.roll(c_lo, GATE_LO_LANE, 1)
                + jnp.where(lane == GATE_ONE_LANE, 1.0, 0.0)).astype(BF16)

    p_r = lax.broadcasted_iota(jnp.int32, (LANE, LANE), 0)
    p_c = lax.broadcasted_iota(jnp.int32, (LANE, LANE), 1)
    head_mean = jnp.where(p_r // FOX_HEAD_DIM == p_c // FOX_HEAD_DIM, 1.0 / FOX_HEAD_DIM, 0.0).astype(BF16)

    def mean_sq(ref, j):
        x = ref[0, :, j * LANE:(j + 1) * LANE]
        return jnp.dot((x * x).astype(BF16), head_mean, preferred_element_type=F32)

    def normed(ref, g_ref, j, ms):
        lanes = slice(j * LANE, (j + 1) * LANE)
        return (ref[0, :, lanes] * lax.rsqrt(ms + EPS) * g_ref[:, lanes]).astype(BF16)

    def placed(x, p_ref, j):
        return jnp.dot(jnp.concatenate([x, gate_row], axis=1), p_ref[j], preferred_element_type=F32)

    pairs = range(FOX_HEADS // 2)
    q_ms = [mean_sq(fq_ref, j) for j in pairs]
    k_ms = [mean_sq(fk_ref, j) for j in pairs]
    q_n = [normed(fq_ref, gq_ref, j, q_ms[j]) for j in pairs]
    k_n = [normed(fk_ref, gk_ref, j, k_ms[j]) for j in pairs]
    for j in pairs:
        q = placed(q_n[j], pq_ref, j)
        k = placed(k_n[j], pk_ref, j)
        v = placed(fv_ref[0, :, j * LANE:(j + 1) * LANE].astype(BF16), pv_ref, j)
        for hh in range(2):
            head = slice(hh * HEAD_PAD, (hh + 1) * HEAD_PAD)
            q_ref[0, 2 * j + hh] = q[:, head].astype(BF16)
            _store_key_blocks(k_ref, 2 * j + hh, k[:, head])
            v_ref[0, 2 * j + hh] = v[:, head].astype(BF16)


def _fox_prep_specs(tl):
    row = lambda b, i: (b, i, 0)
    c2 = lambda b, i: (0, 0)
    c3 = lambda b, i: (0, 0, 0)
    head_out = pl.BlockSpec((1, FOX_HEADS, tl, HEAD_PAD), lambda b, i: (b, 0, i, 0))
    head_shape = jax.ShapeDtypeStruct((BATCH, FOX_HEADS, SEQ, HEAD_PAD), BF16)
    place = pl.BlockSpec((FOX_HEADS // 2, 2 * LANE, 2 * HEAD_PAD), c3)
    in_specs = [pl.BlockSpec((1, tl, ATT_WIDTH), row),
                pl.BlockSpec((1, tl, ATT_WIDTH), row),
                pl.BlockSpec((1, tl, ATT_WIDTH), row),
                pl.BlockSpec((1, tl, LANE), row),
                pl.BlockSpec((1, LANE), c2),
                pl.BlockSpec((1, ATT_WIDTH), c2),
                pl.BlockSpec((1, ATT_WIDTH), c2),
                place, place, place]
    return (in_specs, [head_out, _KT_SPEC(FOX_HEADS, tl), head_out],
            [head_shape, _KT_SHAPE(FOX_HEADS), head_shape])


def _front_kernel(*refs, n_mla, n_fox):
    x_ref, g_ref, sh_ref, sc_ref, w_ref = refs[:5]
    mla_rest, fox_rest = refs[5:5 + n_mla], refs[5 + n_mla:5 + n_mla + n_fox]
    u_ref = refs[5 + n_mla + n_fox]
    outs = refs[6 + n_mla + n_fox:12 + n_mla + n_fox]
    proj_ref, carry_ref = refs[-2:]
    _inproj_into(x_ref, g_ref, sh_ref, sc_ref, w_ref, proj_ref, u_ref)
    cols = [proj_ref.at[:, :, c0:c1] for c0, c1 in _IN_GROUPS]
    _mla_prep_kernel(cols[1], cols[2], cols[3], *mla_rest, *outs[:3])
    _fox_prep_kernel(cols[4], cols[5], cols[6], cols[3], *fox_rest, *outs[3:], carry_ref)


def _front_call(x, g, sh, sc, w, layer, mla_rest, fox_rest):
    tl = ROW_TILE
    row = lambda b, i: (b, i, 0)
    per_b = lambda b, i: (b, 0, 0)
    const = lambda b, i: (0, 0)
    mla_in, mla_out, mla_shape = _mla_prep_specs(tl)
    fox_in, fox_out, fox_shape = _fox_prep_specs(tl)
    mla_in, fox_in = mla_in[3:], fox_in[4:]
    outs = pl.pallas_call(
        functools.partial(_front_kernel, n_mla=len(mla_in), n_fox=len(fox_in)),
        grid=(BATCH, SEQ // tl),
        in_specs=[pl.BlockSpec((1, tl, D_MODEL), row),
                  pl.BlockSpec((1, D_MODEL), const),
                  pl.BlockSpec((1, 1, D_MODEL), per_b),
                  pl.BlockSpec((1, 1, D_MODEL), per_b),
                  pl.BlockSpec((1, D_MODEL, IN_PAD), lambda b, i: (layer, 0, 0))] + mla_in + fox_in,
        out_specs=[pl.BlockSpec((tl, SSM_WIDTH), lambda b, i: (i, b))] + mla_out + fox_out,
        out_shape=[jax.ShapeDtypeStruct((SEQ, BATCH * SSM_WIDTH), F32)] + mla_shape + fox_shape,
        scratch_shapes=[pltpu.VMEM((1, tl, IN_PAD), F32), pltpu.VMEM((SUBLANE, LANE), F32)],
        compiler_params=_params(("arbitrary", "arbitrary")),
        name="front",
    )(x, g, sh, sc, w, *mla_rest, *fox_rest)
    return outs[0], outs[1:4], outs[4:]


def _fox_placements():
    pq = np.zeros((FOX_HEADS // 2, 2 * LANE, 2 * HEAD_PAD), np.float32)
    pk = np.zeros_like(pq)
    pv = np.zeros_like(pq)
    one_row = LANE + GATE_ONE_LANE
    for j in range(FOX_HEADS // 2):
        for hh in range(2):
            h, col0 = 2 * j + hh, hh * HEAD_PAD
            for d in range(FOX_HEAD_DIM):
                for p in (pq, pk, pv):
                    p[j, hh * FOX_HEAD_DIM + d, col0 + d] = 1.0
            pv[j, one_row, col0 + ONES_LANE] = 1.0
            for n, piece_lane in enumerate((0, GATE_MID_LANE, GATE_LO_LANE)):
                pq[j, LANE + piece_lane + h, col0 + Q_GATE_LANE + n] = 1.0
                pq[j, one_row, col0 + K_GATE_LANE + n] = 1.0
                pk[j, one_row, col0 + Q_GATE_LANE + n] = 1.0
                pk[j, LANE + piece_lane + h, col0 + K_GATE_LANE + n] = -1.0
    return tuple(jnp.asarray(p, BF16) for p in (pq, pk, pv))


def _fox_operands(bf, gq, gk):
    q_scale = LOG2E / math.sqrt(FOX_HEAD_DIM)
    return (_pad_lanes(bf[None, :], LANE), jnp.tile(gq * q_scale, FOX_HEADS)[None, :],
            jnp.tile(gk, FOX_HEADS)[None, :]) + _fox_placements()


def _flash_kernel(qa_ref, qb_ref, kt_ref, v_ref, gap_ref, o_ref, q_scr, s_ref, m_ref, acc_ref,
                  *, tile, chunk, n_tiles):
    p = pl.program_id(2)
    tiles = (p, n_tiles - 1 - p)
    n_tasks = n_tiles + 1
    half = tile // 2
    top, bottom = slice(0, half), slice(half, tile)
    lane = lax.broadcasted_iota(jnp.int32, (tile, HEAD_PAD), 1)
    q_scr[0] = qa_ref[0]
    q_scr[1] = qb_ref[0]

    def plain_task(t):
        second = t - 2 >= p
        return second, second.astype(jnp.int32), jnp.where(second, t - 2 - p, t - 2)

    def row_max_update(w, hh, rows, s):
        mr = m_ref[w, hh, rows]
        for c in range(s.shape[1] // LANE):
            mr = jnp.maximum(mr, s[:, c * LANE:(c + 1) * LANE])
        m_ref[w, hh, rows] = mr

    m_ref[...] = jnp.full(m_ref.shape, NEG, F32)
    for w in range(2):
        for hh in range(2):
            kt = kt_ref[0, hh, tiles[w]]
            s_top = jnp.dot(q_scr[w, hh, top], kt[:, top], preferred_element_type=F32)
            s_top = jnp.where(gap_ref[top, top] <= 0, s_top, NEG)
            s_ref[hh, w, top, top] = s_top
            row_max_update(w, hh, top, s_top)
            s_bot = jnp.dot(q_scr[w, hh, bottom], kt, preferred_element_type=F32)
            s_bot = jnp.where(gap_ref[bottom, :] <= 0, s_bot, NEG)
            s_ref[hh, w, bottom] = s_bot
            row_max_update(w, hh, bottom, s_bot)
    for t in range(2, n_tasks):
        _, which, j = plain_task(jnp.int32(t))
        for hh in range(2):
            s = jnp.dot(q_scr[which, hh], kt_ref[0, hh, j], preferred_element_type=F32)
            s_ref[hh, t] = s
            row_max_update(which, hh, slice(None), s)

    ms = [[jnp.max(m_ref[w, hh], axis=1, keepdims=True) for hh in range(2)] for w in range(2)]

    acc_ref[...] = jnp.zeros(acc_ref.shape, F32)
    for w in range(2):
        k0 = pl.multiple_of(tiles[w] * tile, tile)
        for hh in range(2):
            pr = jnp.exp2(s_ref[hh, w, top, top] - ms[w][hh][top]).astype(BF16)
            acc_ref[w, hh, top] += jnp.dot(pr, v_ref[0, hh, pl.ds(k0, half), :], preferred_element_type=F32)
            pr = jnp.exp2(s_ref[hh, w, bottom] - ms[w][hh][bottom]).astype(BF16)
            acc_ref[w, hh, bottom] += jnp.dot(pr, v_ref[0, hh, pl.ds(k0, tile), :], preferred_element_type=F32)
    for t in range(2, n_tasks):
        second, which, j = plain_task(jnp.int32(t))
        k0 = pl.multiple_of(j * tile, tile)
        for hh in range(2):
            row_max = jnp.where(second, ms[1][hh], ms[0][hh])
            pr = jnp.exp2(s_ref[hh, t] - row_max).astype(BF16)
            acc_ref[which, hh] += jnp.dot(pr, v_ref[0, hh, pl.ds(k0, tile), :], preferred_element_type=F32)

    for w in range(2):
        outs = [acc_ref[w, hh] / acc_ref[w, hh][:, ONES_LANE:ONES_LANE + 1] for hh in range(2)]
        o_ref[0, w, 0] = jnp.where(lane < 64, outs[0], pltpu.roll(outs[1], 64, 1)).astype(BF16)


def _flash_call(q, kt, v, chunk):
    tile = ATT_TILE
    heads = q.shape[1]
    n_tiles = SEQ // tile
    pos = np.arange(tile, dtype=np.int32) // chunk
    gap = jnp.asarray(pos[None, :] - pos[:, None])
    return pl.pallas_call(
        functools.partial(_flash_kernel, tile=tile, chunk=chunk, n_tiles=n_tiles),
        grid=(BATCH, heads // 2, n_tiles // 2),
        in_specs=[pl.BlockSpec((1, 2, tile, HEAD_PAD), lambda b, hp, p: (b, hp, p, 0)),
                  pl.BlockSpec((1, 2, tile, HEAD_PAD), lambda b, hp, p: (b, hp, n_tiles - 1 - p, 0)),
                  pl.BlockSpec((1, 2, n_tiles, HEAD_PAD, tile), lambda b, hp, p: (b, hp, 0, 0, 0)),
                  pl.BlockSpec((1, 2, SEQ, HEAD_PAD), lambda b, hp, p: (b, hp, 0, 0)),
                  pl.BlockSpec((tile, tile), lambda b, hp, p: (0, 0))],
        out_specs=pl.BlockSpec((1, 2, 1, tile, LANE), lambda b, hp, p: (b, 0, p, 0, hp)),
        out_shape=jax.ShapeDtypeStruct((BATCH, 2, n_tiles // 2, tile, ATT_WIDTH), BF16),
        scratch_shapes=[pltpu.VMEM((2, 2, tile, HEAD_PAD), BF16),
                        pltpu.VMEM((2, n_tiles + 1, tile, tile), F32),
                        pltpu.VMEM((2, 2, tile, LANE), F32),
                        pltpu.VMEM((2, 2, tile, HEAD_PAD), F32)],
        compiler_params=_params(("arbitrary", "arbitrary", "arbitrary")),
        name="flash_chunk%d" % chunk,
    )(q, q, kt, v, gap)


def _merge_kernel(ssm_ref, mla_ref, fox_ref, x_ref, g1_ref, gm_ref, gf_ref, w_ref, o_ref):
    def normed(ref, g_ref):
        a = ref[0, 0, 0].astype(F32)
        return (a * lax.rsqrt(jnp.mean(a * a, axis=-1, keepdims=True) + EPS) * g_ref[...]).astype(BF16)

    merged = jnp.concatenate([ssm_ref[...], normed(mla_ref, gm_ref), normed(fox_ref, gf_ref)], axis=1)
    mix = jnp.dot(merged, w_ref[...], preferred_element_type=F32)
    o_ref[0] = x_ref[0] + g1_ref[0] * mix


def _merge_call(o_ssm, o_mla, o_fox, x, g1, gm, gf, w):
    tm = ATT_TILE
    half = SEQ // tm // 2
    row = lambda b, i: (b, i, 0)
    c2 = lambda b, i: (0, 0)
    att = pl.BlockSpec((1, 1, 1, tm, ATT_WIDTH),
                       lambda b, i: (b, i // half, jnp.where(i < half, i, 2 * half - 1 - i), 0, 0))
    return pl.pallas_call(
        _merge_kernel,
        grid=(BATCH, SEQ // tm),
        in_specs=[pl.BlockSpec((tm, SSM_WIDTH), lambda b, i: (i, b)),
                  att, att,
                  pl.BlockSpec((1, tm, D_MODEL), row),
                  pl.BlockSpec((1, 1, D_MODEL), lambda b, i: (b, 0, 0)),
                  pl.BlockSpec((1, ATT_WIDTH), c2),
                  pl.BlockSpec((1, ATT_WIDTH), c2),
                  pl.BlockSpec((D_MODEL, D_MODEL), c2)],
        out_specs=pl.BlockSpec((1, tm, D_MODEL), row),
        out_shape=jax.ShapeDtypeStruct((BATCH, SEQ, D_MODEL), F32),
        compiler_params=_params(("arbitrary", "arbitrary")),
        name="merge",
    )(o_ssm, o_mla, o_fox, x, g1, gm, gf, w)


def _ffn_kernel(x_ref, g_ref, sh_ref, sc_ref, g2_ref, wg_ref, wu_ref, wd_ref, o_ref):
    half = x_ref.shape[1] // 2
    for r in (slice(0, half), slice(half, 2 * half)):
        x = x_ref[0, r]
        h = _rms_mod(x, g_ref[...], sc_ref[0], sh_ref[0]).astype(BF16)
        gate_up = [(jnp.dot(h, wg_ref[0, :, c0:c1], preferred_element_type=F32),
                    jnp.dot(h, wu_ref[0, :, c0:c1], preferred_element_type=F32)) for c0, c1 in FF_PARTS]
        acc = jnp.zeros((half, D_MODEL), F32)
        for (c0, c1), (gate, up) in zip(FF_PARTS, gate_up):
            a = (gate * jax.nn.sigmoid(gate) * up).astype(BF16)
            acc = acc + jnp.dot(a, wd_ref[0, c0:c1, :], preferred_element_type=F32)
        o_ref[0, r] = x + g2_ref[0] * acc


def _ffn_call(x, g, sh, sc, g2, wg, wu, wd, layer):
    tm = MOE_TILE
    row = lambda b, i: (b, i, 0)
    per_b = lambda b, i: (b, 0, 0)
    resident = lambda shape: pl.BlockSpec(shape, lambda b, i: (layer, 0, 0), pipeline_mode=pl.Buffered(1))
    return pl.pallas_call(
        _ffn_kernel,
        grid=(BATCH, SEQ // tm),
        in_specs=[pl.BlockSpec((1, tm, D_MODEL), row),
                  pl.BlockSpec((1, D_MODEL), lambda b, i: (0, 0)),
                  pl.BlockSpec((1, 1, D_MODEL), per_b),
                  pl.BlockSpec((1, 1, D_MODEL), per_b),
                  pl.BlockSpec((1, 1, D_MODEL), per_b),
                  resident((1, D_MODEL, D_FF)),
                  resident((1, D_MODEL, D_FF)),
                  resident((1, D_FF, D_MODEL))],
        out_specs=pl.BlockSpec((1, tm, D_MODEL), row),
        out_shape=jax.ShapeDtypeStruct((BATCH, SEQ, D_MODEL), F32),
        compiler_params=_params(("arbitrary", "arbitrary")),
        name="ffn_dense",
    )(x, g, sh, sc, g2, wg, wu, wd)


def _router_kernel(x_ref, g_ref, sh_ref, sc_ref, w_ref, b_ref, comb_ref, rank_ref, rankt_ref, count_ref):
    tm = x_ref.shape[1]
    h = _rms_mod(x_ref[0], g_ref[...], sc_ref[0], sh_ref[0])
    h_hi = h.astype(BF16)
    h_lo = (h - h_hi.astype(F32)).astype(BF16)
    parts = jnp.dot(jnp.concatenate([h_hi, h_lo], axis=1), w_ref[...], preferred_element_type=F32)
    logits = parts + pltpu.roll(parts, LANE - N_EXPERTS, 1) + b_ref[...]
    lane = lax.broadcasted_iota(jnp.int32, logits.shape, 1)
    logits = jnp.where(lane < N_EXPERTS, logits, -jnp.inf)
    m1 = jnp.max(logits, axis=-1, keepdims=True)
    i1 = jnp.min(jnp.where(logits == m1, lane, LANE), axis=-1, keepdims=True)
    rest = jnp.where(lane == i1, -jnp.inf, logits)
    m2 = jnp.max(rest, axis=-1, keepdims=True)
    i2 = jnp.min(jnp.where(rest == m2, lane, LANE), axis=-1, keepdims=True)
    e = jnp.exp(m2 - m1)
    p1 = 1.0 / (1.0 + e)
    comb_ref[0] = jnp.where(lane == i1, p1, 0.0) + jnp.where(lane == i2, e * p1, 0.0)

    chosen = (lane == i1) | (lane == i2)
    chosen_f = jnp.where(chosen, 1.0, 0.0)
    r_i = lax.broadcasted_iota(jnp.int32, (tm, tm), 0)
    c_i = lax.broadcasted_iota(jnp.int32, (tm, tm), 1)
    earlier = jnp.where(c_i < r_i, 1.0, 0.0).astype(BF16)
    rank = jnp.dot(earlier, chosen_f.astype(BF16), preferred_element_type=F32)
    rank = jnp.where(chosen, rank, -1.0)
    rank_ref[0] = rank
    rankt_ref[0] = rank.T[0:SUBLANE, :]
    count_ref[0] = jnp.sum(chosen_f, axis=0, keepdims=True)


def _router_call(x, g, sh, sc, w, b):
    tm = MOE_TILE
    tiles = SEQ // tm
    row = lambda b_, i: (b_, i, 0)
    per_b = lambda b_, i: (b_, 0, 0)
    per_tile = lambda b_, i: (b_ * tiles + i, 0, 0)
    return pl.pallas_call(
        _router_kernel,
        grid=(BATCH, tiles),
        in_specs=[pl.BlockSpec((1, tm, D_MODEL), row),
                  pl.BlockSpec((1, D_MODEL), lambda b_, i: (0, 0)),
                  pl.BlockSpec((1, 1, D_MODEL), per_b),
                  pl.BlockSpec((1, 1, D_MODEL), per_b),
                  pl.BlockSpec((2 * D_MODEL, LANE), lambda b_, i: (0, 0)),
                  pl.BlockSpec((1, LANE), lambda b_, i: (0, 0))],
        out_specs=[pl.BlockSpec((1, tm, LANE), row),
                   pl.BlockSpec((1, tm, LANE), row),
                   pl.BlockSpec((1, SUBLANE, tm), per_tile),
                   pl.BlockSpec((1, 1, LANE), per_tile)],
        out_shape=[jax.ShapeDtypeStruct((BATCH, SEQ, LANE), F32),
                   jax.ShapeDtypeStruct((BATCH, SEQ, LANE), F32),
                   jax.ShapeDtypeStruct((BATCH * tiles, SUBLANE, tm), F32),
                   jax.ShapeDtypeStruct((BATCH * tiles, 1, LANE), F32)],
        compiler_params=_params(("arbitrary", "arbitrary")),
        name="router",
    )(x, g, sh, sc, w, b)


def _moe_kernel(count_ref, x_ref, g_ref, sh_ref, sc_ref, g2_ref, comb_ref, rank_ref, rankt_ref,
                wg_ref, wu_ref, wt_ref, wd_ref, o_ref, h_ref):
    tm = x_ref.shape[1]
    e = pl.program_id(1)

    @pl.when(e == 0)
    def _():
        x = x_ref[0]
        h_ref[...] = _rms_mod(x, g_ref[...], sc_ref[0], sh_ref[0]).astype(BF16)
        o_ref[0] = x

    lane = lax.broadcasted_iota(jnp.int32, (tm, LANE), 1)
    mine = lane == e
    rank_col = jnp.sum(jnp.where(mine, rank_ref[0], 0.0), axis=-1, keepdims=True)
    gate_col = jnp.sum(jnp.where(mine, comb_ref[0], 0.0), axis=-1, keepdims=True)
    rank_row = rankt_ref[0, pl.ds(e, 1), :]
    count = count_ref[pl.program_id(0) * N_EXPERTS + e]

    def expert_pass(first, n_rows):
        base = first.astype(F32)
        slot_sub = lax.broadcasted_iota(jnp.int32, (n_rows, tm), 0).astype(F32)
        slot_lane = lax.broadcasted_iota(jnp.int32, (tm, n_rows), 1).astype(F32)
        pick = jnp.where(rank_row - base == slot_sub, 1.0, 0.0).astype(BF16)
        rows = jnp.dot(pick, h_ref[...], preferred_element_type=F32).astype(BF16)
        gate = jnp.dot(rows, wg_ref[0, 0, :, :EXPERT_MAIN], preferred_element_type=F32)
        up = jnp.dot(rows, wu_ref[0, 0, :, :EXPERT_MAIN], preferred_element_type=F32)
        tail = jnp.dot(rows, wt_ref[0, 0], preferred_element_type=F32)
        gate_t, up_t = tail[:, :LANE], tail[:, LANE:]
        a = jnp.concatenate([gate * jax.nn.sigmoid(gate) * up, gate_t * jax.nn.sigmoid(gate_t) * up_t],
                            axis=1).astype(BF16)
        y = jnp.dot(a, wd_ref[0, 0], preferred_element_type=F32).astype(BF16)
        place = jnp.where(rank_col - base == slot_lane, 1.0, 0.0).astype(BF16)
        back = jnp.dot(place, y, preferred_element_type=F32)
        o_ref[0] += g2_ref[0] * (gate_col * back)

    def full_pass(sb, carry):
        expert_pass(sb * MOE_ROWS, MOE_ROWS)
        return carry

    n_full = count // MOE_ROWS
    lax.fori_loop(0, n_full, full_pass, 0)
    left = count - n_full * MOE_ROWS

    @pl.when(left > MOE_ROWS // 2)
    def _():
        expert_pass(n_full * MOE_ROWS, MOE_ROWS)

    @pl.when((left > 0) & (left <= MOE_ROWS // 2))
    def _():
        expert_pass(n_full * MOE_ROWS, MOE_ROWS // 2)


def _moe_call(x, g, sh, sc, g2, comb, rank, rankt, counts, wg, wu, wt, wd, layer):
    tm = MOE_TILE
    tiles = SEQ // tm
    n_tiles = BATCH * tiles
    row = lambda i, e, cnt: (i, 0, 0)
    per_b = lambda i, e, cnt: (i // tiles, 0, 0)
    expert = lambda i, e, cnt: (layer, e, 0, 0)
    as_tiles = lambda a: a.reshape(n_tiles, tm, a.shape[-1])
    grid_spec = pltpu.PrefetchScalarGridSpec(
        num_scalar_prefetch=1,
        grid=(n_tiles, N_EXPERTS),
        in_specs=[pl.BlockSpec((1, tm, D_MODEL), row),
                  pl.BlockSpec((1, D_MODEL), lambda i, e, cnt: (0, 0)),
                  pl.BlockSpec((1, 1, D_MODEL), per_b),
                  pl.BlockSpec((1, 1, D_MODEL), per_b),
                  pl.BlockSpec((1, 1, D_MODEL), per_b),
                  pl.BlockSpec((1, tm, LANE), row),
                  pl.BlockSpec((1, tm, LANE), row),
                  pl.BlockSpec((1, SUBLANE, tm), row),
                  pl.BlockSpec((1, 1, D_MODEL, D_FF_EXPERT), expert),
                  pl.BlockSpec((1, 1, D_MODEL, D_FF_EXPERT), expert),
                  pl.BlockSpec((1, 1, D_MODEL, 2 * LANE), expert),
                  pl.BlockSpec((1, 1, D_FF_EXPERT, D_MODEL), expert)],
        out_specs=pl.BlockSpec((1, tm, D_MODEL), row),
        scratch_shapes=[pltpu.VMEM((tm, D_MODEL), BF16)],
    )
    out = pl.pallas_call(
        _moe_kernel,
        grid_spec=grid_spec,
        out_shape=jax.ShapeDtypeStruct((n_tiles, tm, D_MODEL), F32),
        compiler_params=_params(("arbitrary", "arbitrary")),
        name="moe_experts",
    )(counts, as_tiles(x), g, sh, sc, g2, as_tiles(comb), as_tiles(rank), rankt, wg, wu, wt, wd)
    return out.reshape(BATCH, SEQ, D_MODEL)


def kernel(x, c, positions, norm_mix, norm_ffn, w_ada, b_ada, w_in, ssm_lam_re, ssm_lam_im, ssm_log_dt, ssm_b_re, ssm_b_im, ssm_c_re, ssm_c_im, ssm_d, ssm_w_glu, ssm_b_glu, mla_q_norm, mla_kv_norm, mla_w_uq, mla_w_ukv, mla_qk_gq, mla_qk_gk, fox_b_f, fox_qk_gq, fox_qk_gk, out_norm, w_out, ffn_w_gate, ffn_w_up, ffn_w_down, moe_w_router, moe_b_router, moe_w_gate, moe_w_up, moe_w_down):
    tabs = _rope_tables(positions)
    w_in_packed = _pack_w_in(w_in)
    moe_wg, moe_wu, moe_wd = (w.astype(BF16) for w in (moe_w_gate, moe_w_up, moe_w_down))
    moe_wt = jnp.concatenate([moe_wg[..., EXPERT_MAIN:], moe_wu[..., EXPERT_MAIN:]], axis=-1)
    ffn_wg, ffn_wu, ffn_wd = (w.astype(BF16) for w in (ffn_w_gate, ffn_w_up, ffn_w_down))
    ada = _ada_call(c, w_ada, b_ada)
    ada = ada.reshape(DEPTH, BATCH, 6, 1, D_MODEL)
    row2 = lambda a: a[None, :]

    for i in range(DEPTH):
        sh1, sc1, g1, sh2, sc2, g2 = (ada[i, :, n] for n in range(6))

        u, mla_qkv, fox_qkv = _front_call(
            x, row2(norm_mix[i]), sh1, sc1, w_in_packed, i,
            (*tabs, row2(mla_q_norm[i]), row2(mla_kv_norm[i]),
             *_mla_weights(mla_w_uq[i], mla_w_ukv[i], mla_qk_gq[i], mla_qk_gk[i])),
            _fox_operands(fox_b_f[i], fox_qk_gq[i], fox_qk_gk[i]))

        bmat, lam, cmat = _s5_operands(ssm_lam_re[i], ssm_lam_im[i], ssm_log_dt[i],
                                       ssm_b_re[i], ssm_b_im[i], ssm_c_re[i], ssm_c_im[i])
        o_ssm = _s5_call(u, bmat, lam, cmat, row2(ssm_d[i]), ssm_w_glu[i].astype(BF16),
                         row2(ssm_b_glu[i]), row2(out_norm[i, :SSM_WIDTH]))

        o_mla = _flash_call(*mla_qkv, CHUNK)
        o_fox = _flash_call(*fox_qkv, 1)

        e1, e2 = SSM_WIDTH, SSM_WIDTH + ATT_WIDTH
        x = _merge_call(o_ssm, o_mla, o_fox, x, g1, row2(out_norm[i, e1:e2]), row2(out_norm[i, e2:]),
                        w_out[i].astype(BF16))

        j = i // 2
        if i % 2 == 0:
            x = _ffn_call(x, row2(norm_ffn[i]), sh2, sc2, g2, ffn_wg, ffn_wu, ffn_wd, layer=j)
        else:
            wr_hi = moe_w_router[j].astype(BF16)
            wr_lo = (moe_w_router[j] - wr_hi.astype(F32)).astype(BF16)
            wr = _pad_lanes(jnp.concatenate([wr_hi, wr_lo], axis=1), LANE)
            comb, rank, rankt, counts = _router_call(x, row2(norm_ffn[i]), sh2, sc2, jnp.concatenate([wr, wr], axis=0),
                                                     _pad_lanes(row2(moe_b_router[j]), LANE))
            counts = counts[:, 0, :N_EXPERTS].astype(jnp.int32).reshape(-1)
            x = _moe_call(x, row2(norm_ffn[i]), sh2, sc2, g2, comb, rank, rankt, counts,
                          moe_wg, moe_wu, moe_wt, moe_wd, layer=j)
    return x
```

```python
import functools
import math

import jax
import jax.numpy as jnp
import numpy as np
from jax import lax
from jax.experimental import pallas as pl
from jax.experimental.pallas import tpu as pltpu

F32 = jnp.float32
BF16 = jnp.bfloat16

D_MODEL = 1024
BATCH = 8
SEQ = 4096
DEPTH = 4
CHUNK = 64
EPS = 1e-6

SSM_WIDTH = 256
SSM_GROUP = 16
N_SSM_GROUPS = 16
SSM_STATE = 64
N_STATE = N_SSM_GROUPS * SSM_STATE

MLA_HEADS = 6
MLA_Q_RANK = 256
MLA_KV_RANK = 128
MLA_NOPE = 64
MLA_ROPE = 32
MLA_V = 64
MLA_QK = 96
ROPE_BASE = 10000.0

FOX_HEADS = 6
FOX_HEAD_DIM = 64
ATT_WIDTH = 384

D_FF = 2816
N_EXPERTS = 8
D_FF_EXPERT = 1408
EXPERT_MAIN = 1280

LANE = 128
SUBLANE = 8
HEAD_PAD = LANE
ONES_LANE = 64
NEG = -1e30

IN_PAD = 1920
KR_LANE = 64

ROW_TILE = 512
S5_STEPS = 64
ATT_TILE = 512
LOG2E = math.log2(math.e)
FF_PARTS = ((0, 1536), (1536, D_FF))
MOE_TILE = 1024
MOE_ROWS = 256
VMEM_LIMIT = 56 * 1024 * 1024


def _params(sem):
    return pltpu.CompilerParams(dimension_semantics=sem, vmem_limit_bytes=VMEM_LIMIT)


def _rms_mod(x, g, sc, sh):
    ms = jnp.mean(x * x, axis=-1, keepdims=True)
    h = x * lax.rsqrt(ms + EPS) * g
    return h * (1.0 + sc) + sh


def _split3(x):
    hi = x.astype(BF16).astype(F32)
    r = x - hi
    mid = r.astype(BF16).astype(F32)
    lo = (r - mid).astype(BF16).astype(F32)
    return hi, mid, lo


def _ada_kernel(c_ref, w_ref, b_ref, o_ref):
    c = c_ref[...]
    ca = (c * jax.nn.sigmoid(c)).astype(BF16)
    o_ref[0] = jnp.dot(ca, w_ref[0].astype(BF16), preferred_element_type=F32) + b_ref[0]


def _ada_call(c, w_ada, b_ada):
    tn = 1536
    return pl.pallas_call(
        _ada_kernel,
        grid=(DEPTH, 6 * D_MODEL // tn),
        in_specs=[pl.BlockSpec((BATCH, D_MODEL), lambda i, j: (0, 0)),
                  pl.BlockSpec((1, D_MODEL, tn), lambda i, j: (i, 0, j)),
                  pl.BlockSpec((1, 1, tn), lambda i, j: (i, 0, j))],
        out_specs=pl.BlockSpec((1, BATCH, tn), lambda i, j: (i, 0, j)),
        out_shape=jax.ShapeDtypeStruct((DEPTH, BATCH, 6 * D_MODEL), F32),
        compiler_params=_params(("arbitrary", "arbitrary")),
        name="ada",
    )(c, w_ada, b_ada.reshape(DEPTH, 1, 6 * D_MODEL))


_IN_GROUPS = ((0, 256), (256, 512), (512, 640), (640, 768), (768, 1152), (1152, 1536), (1536, 1920))


def _inproj_into(x_ref, g_ref, sh_ref, sc_ref, w_ref, proj_ref, u_ref):
    parts = 4
    step = x_ref.shape[1] // parts
    rows = [slice(r * step, (r + 1) * step) for r in range(parts)]
    normed = lambda r: _rms_mod(x_ref[0, rows[r]], g_ref[...], sc_ref[0], sh_ref[0]).astype(BF16)
    h_next = normed(0)
    for r in range(parts):
        h = h_next
        if r + 1 < parts:
            h_next = normed(r + 1)
        proj = jnp.dot(h, w_ref[0], preferred_element_type=F32)
        proj_ref[0, rows[r]] = proj
        u_ref[rows[r]] = proj[:, _IN_GROUPS[0][0]:_IN_GROUPS[0][1]]


def _pack_w_in(w):
    u, cq, ckv, kr, fq, fk, fv, fg = jnp.split(
        w.astype(BF16), (256, 512, 640, 672, 1056, 1440, 1824), axis=2)
    z = lambda n: jnp.zeros(w.shape[:2] + (n,), BF16)
    krfg = jnp.concatenate([fg, z(KR_LANE - FOX_HEADS), kr, z(LANE - KR_LANE - MLA_ROPE)], axis=2)
    return jnp.concatenate([u, cq, ckv, krfg, fq, fk, fv], axis=2)


def _s5_kernel(u2_ref, bmat_ref, lam_ref, cmat_ref, d_ref, wglu_ref, bglu_ref, gn_ref,
               o2_ref, u_ref, o_ref, bu0_ref, bu1_ref, state_ref, *, steps):
    rows = steps * BATCH
    lane_tiles = SSM_WIDTH // LANE
    for b in range(BATCH):
        for c in range(lane_tiles):
            lanes = slice(b * SSM_WIDTH + c * LANE, b * SSM_WIDTH + (c + 1) * LANE)
            u_ref.at[c][pl.ds(b, 2 * steps, stride=BATCH), :] = u2_ref[:, lanes]
    u_rows = lambda rs: jnp.concatenate([u_ref[c, rs, :] for c in range(lane_tiles)], axis=1)

    @pl.when(pl.program_id(0) == 0)
    def _():
        state_ref[...] = jnp.zeros_like(state_ref)

    halves = ((bu0_ref, slice(0, rows)), (bu1_ref, slice(rows, 2 * rows)))
    for bu_ref, rs in halves:
        bu_ref[...] = jnp.dot(u_rows(rs).astype(BF16), bmat_ref[...], preferred_element_type=F32)
    lr = jnp.broadcast_to(lam_ref[0:1, :], (SUBLANE, N_STATE))
    li = jnp.broadcast_to(lam_ref[1:2, :], (SUBLANE, N_STATE))
    sr, si = state_ref[:, 0:N_STATE], state_ref[:, N_STATE:2 * N_STATE]

    for bu_ref, rs in halves:
        for t in range(steps):
            r = slice(t * SUBLANE, (t + 1) * SUBLANE)
            nr = lr * sr - li * si + bu_ref[r, 0:N_STATE]
            ni = lr * si + li * sr + bu_ref[r, N_STATE:2 * N_STATE]
            bu_ref[r, 0:N_STATE] = nr
            bu_ref[r, N_STATE:2 * N_STATE] = ni
            sr, si = nr, ni
        y = jnp.dot(bu_ref[...].astype(BF16), cmat_ref[...], preferred_element_type=F32)
        y = jax.nn.gelu(y + d_ref[...] * u_rows(rs))
        gate = jnp.dot(y.astype(BF16), wglu_ref[...], preferred_element_type=F32) + bglu_ref[...]
        o = y * jax.nn.sigmoid(gate)
        ms = jnp.mean(o * o, axis=-1, keepdims=True)
        o = o * lax.rsqrt(ms + EPS) * gn_ref[...]
        for c in range(lane_tiles):
            o_ref[c, rs, :] = o[:, c * LANE:(c + 1) * LANE]

    state_ref[:, 0:N_STATE] = sr
    state_ref[:, N_STATE:2 * N_STATE] = si
    for b in range(BATCH):
        for c in range(lane_tiles):
            lanes = slice(b * SSM_WIDTH + c * LANE, b * SSM_WIDTH + (c + 1) * LANE)
            o2_ref[:, lanes] = o_ref.at[c][pl.ds(b, 2 * steps, stride=BATCH), :].astype(BF16)


def _s5_call(u2, bmat, lam, cmat, d_skip, wglu, bglu, gn):
    rows = S5_STEPS * BATCH
    const = lambda i: (0, 0)
    return pl.pallas_call(
        functools.partial(_s5_kernel, steps=S5_STEPS),
        grid=(SEQ // (2 * S5_STEPS),),
        in_specs=[pl.BlockSpec((2 * S5_STEPS, BATCH * SSM_WIDTH), lambda i: (i, 0)),
                  pl.BlockSpec((SSM_WIDTH, 2 * N_STATE), const),
                  pl.BlockSpec((2, N_STATE), const),
                  pl.BlockSpec((2 * N_STATE, SSM_WIDTH), const),
                  pl.BlockSpec((1, SSM_WIDTH), const),
                  pl.BlockSpec((SSM_WIDTH, SSM_WIDTH), const),
                  pl.BlockSpec((1, SSM_WIDTH), const),
                  pl.BlockSpec((1, SSM_WIDTH), const)],
        out_specs=pl.BlockSpec((2 * S5_STEPS, BATCH * SSM_WIDTH), lambda i: (i, 0)),
        out_shape=jax.ShapeDtypeStruct((SEQ, BATCH * SSM_WIDTH), BF16),
        scratch_shapes=[pltpu.VMEM((SSM_WIDTH // LANE, 2 * rows, LANE), F32),
                        pltpu.VMEM((SSM_WIDTH // LANE, 2 * rows, LANE), F32),
                        pltpu.VMEM((rows, 2 * N_STATE), F32),
                        pltpu.VMEM((rows, 2 * N_STATE), F32),
                        pltpu.VMEM((SUBLANE, 2 * N_STATE), F32)],
        compiler_params=_params(("arbitrary",)),
        name="s5",
    )(u2, bmat, lam, cmat, d_skip, wglu, bglu, gn)


def _s5_operands(lam_re, lam_im, log_dt, b_re, b_im, c_re, c_im):
    dt = jnp.exp(log_dt)[:, None]
    mag = jnp.exp(lam_re * dt)
    lb_re = mag * jnp.cos(lam_im * dt)
    lb_im = mag * jnp.sin(lam_im * dt)
    den = lam_re * lam_re + lam_im * lam_im
    co_re = ((lb_re - 1.0) * lam_re + lb_im * lam_im) / den
    co_im = (lb_im * lam_re - (lb_re - 1.0) * lam_im) / den
    bb_re = co_re[..., None] * b_re - co_im[..., None] * b_im
    bb_im = co_re[..., None] * b_im + co_im[..., None] * b_re
    eye = jnp.eye(N_SSM_GROUPS, dtype=F32)
    blk_b = lambda m: jnp.einsum("gpc,gh->gchp", m, eye).reshape(SSM_WIDTH, N_STATE)
    bmat = jnp.concatenate([blk_b(bb_re), blk_b(bb_im)], axis=1).astype(BF16)
    blk_c = lambda m: jnp.einsum("gcp,gh->gphc", m, eye).reshape(N_STATE, SSM_WIDTH)
    cmat = jnp.concatenate([blk_c(c_re), -blk_c(c_im)], axis=0).astype(BF16)
    lam = jnp.stack([lb_re.reshape(N_STATE), lb_im.reshape(N_STATE)], axis=0)
    return bmat, lam, cmat


def _rope_tables(positions):
    half = MLA_ROPE // 2
    inv = ROPE_BASE ** (-jnp.arange(half, dtype=F32) / half)
    ang = inv[:, None] * positions.astype(F32).reshape(1, -1)
    shp = positions.shape
    cos, sin = (lax.optimization_barrier(f(ang)).T.reshape(shp + (half,)) for f in (jnp.cos, jnp.sin))
    one = lambda n: jnp.ones(shp + (n,), F32)
    zero = lambda n: jnp.zeros(shp + (n,), F32)
    cos_t = jnp.concatenate([one(MLA_NOPE), cos, cos, zero(LANE - MLA_QK)], axis=-1)
    sin_t = jnp.concatenate([zero(MLA_NOPE), -sin, sin, zero(LANE - MLA_QK)], axis=-1)
    return cos_t, sin_t


def _swap_rope_halves(a):
    half = MLA_ROPE // 2
    lo, hi = a[..., MLA_NOPE:MLA_NOPE + half], a[..., MLA_NOPE + half:MLA_QK]
    return jnp.concatenate([jnp.zeros_like(a[..., :MLA_NOPE]), hi, lo, jnp.zeros_like(a[..., MLA_QK:])], axis=-1)


def _store_key_blocks(kt_ref, h, k):
    kt = k.T
    for s in range(k.shape[0] // ATT_TILE):
        kt_ref[0, h, s] = kt[:, s * ATT_TILE:(s + 1) * ATT_TILE].astype(BF16)


_KT_SPEC = lambda heads, tl: pl.BlockSpec((1, heads, tl // ATT_TILE, HEAD_PAD, ATT_TILE),
                                          lambda b, i: (b, 0, i, 0, 0))
_KT_SHAPE = lambda heads: jax.ShapeDtypeStruct((BATCH, heads, SEQ // ATT_TILE, HEAD_PAD, ATT_TILE), BF16)


def _mla_prep_kernel(cq_ref, ckv_ref, krfg_ref, cos_ref, sin_ref, qn_ref, kvn_ref, wq_ref, wk_ref, wv_ref,
                     gq_ref, gqs_ref, gk_ref, gks_ref, q_ref, k_ref, v_ref):
    tl = cq_ref.shape[1]
    lane = lax.broadcasted_iota(jnp.int32, (tl, LANE), 1)
    cos, sin = cos_ref[0], sin_ref[0]
    q_scale = LOG2E / math.sqrt(MLA_QK)
    q_cos, q_sin = gq_ref[...] * cos * q_scale, gqs_ref[...] * sin * q_scale
    k_cos, k_sin = gk_ref[...] * cos, gks_ref[...] * sin
    ones = jnp.ones((LANE, LANE), BF16)

    def inv_rms(x):
        ss = jnp.dot((x * x).astype(BF16), ones, preferred_element_type=F32)
        return lax.rsqrt(ss / MLA_QK + EPS)

    cq = cq_ref[0]
    cqn = (cq * lax.rsqrt(jnp.mean(cq * cq, axis=-1, keepdims=True) + EPS) * qn_ref[...]).astype(BF16)
    ckv = ckv_ref[0]
    ckvn = (ckv * lax.rsqrt(jnp.mean(ckv * ckv, axis=-1, keepdims=True) + EPS) * kvn_ref[...]).astype(BF16)
    kr = jnp.where((lane >= KR_LANE) & (lane < KR_LANE + MLA_ROPE), krfg_ref[0], 0.0)
    kr_swapped = jnp.where(lane < KR_LANE + MLA_ROPE // 2, pltpu.roll(kr, LANE - 16, 1), pltpu.roll(kr, 16, 1))
    k_rotary = kr_swapped * k_sin

    heads = range(MLA_HEADS)
    qqs = [jnp.dot(cqn, wq_ref[h], preferred_element_type=F32) for h in heads]
    ks = [jnp.dot(ckvn, wk_ref[h], preferred_element_type=F32) + kr for h in heads]
    q_inv = [inv_rms(qq[:, :LANE]) for qq in qqs]
    k_inv = [inv_rms(k) for k in ks]
    for h in heads:
        q, q_swapped = qqs[h][:, :LANE], qqs[h][:, LANE:]
        q_ref[0, h] = (q_inv[h] * (q * q_cos + q_swapped * q_sin)).astype(BF16)
        _store_key_blocks(k_ref, h, k_inv[h] * (ks[h] * k_cos + k_rotary))
        v = jnp.dot(ckvn, wv_ref[h], preferred_element_type=F32)
        v_ref[0, h] = jnp.where(lane == ONES_LANE, 1.0, v).astype(BF16)


def _mla_prep_specs(tl):
    row = lambda b, i: (b, i, 0)
    c2 = lambda b, i: (0, 0)
    c3 = lambda b, i: (0, 0, 0)
    head_out = pl.BlockSpec((1, MLA_HEADS, tl, HEAD_PAD), lambda b, i: (b, 0, i, 0))
    head_shape = jax.ShapeDtypeStruct((BATCH, MLA_HEADS, SEQ, HEAD_PAD), BF16)
    gain = pl.BlockSpec((1, HEAD_PAD), c2)
    in_specs = [pl.BlockSpec((1, tl, MLA_Q_RANK), row),
                pl.BlockSpec((1, tl, MLA_KV_RANK), row),
                pl.BlockSpec((1, tl, LANE), row),
                pl.BlockSpec((1, tl, LANE), row),
                pl.BlockSpec((1, tl, LANE), row),
                pl.BlockSpec((1, MLA_Q_RANK), c2),
                pl.BlockSpec((1, MLA_KV_RANK), c2),
                pl.BlockSpec((MLA_HEADS, MLA_Q_RANK, 2 * HEAD_PAD), c3),
                pl.BlockSpec((MLA_HEADS, MLA_KV_RANK, HEAD_PAD), c3),
                pl.BlockSpec((MLA_HEADS, MLA_KV_RANK, HEAD_PAD), c3),
                gain, gain, gain, gain]
    return (in_specs, [head_out, _KT_SPEC(MLA_HEADS, tl), head_out],
            [head_shape, _KT_SHAPE(MLA_HEADS), head_shape])


def _pad_lanes(a, n=HEAD_PAD):
    return jnp.pad(a, [(0, 0)] * (a.ndim - 1) + [(0, n - a.shape[-1])])


def _mla_weights(w_uq, w_ukv, gq, gk):
    wq = _pad_lanes(w_uq.reshape(MLA_Q_RANK, MLA_HEADS, MLA_QK).transpose(1, 0, 2))
    wq = jnp.concatenate([wq, _swap_rope_halves(wq)], axis=-1).astype(BF16)
    wkv = w_ukv.reshape(MLA_KV_RANK, MLA_HEADS, MLA_NOPE + MLA_V).transpose(1, 0, 2)
    wk = _pad_lanes(wkv[..., :MLA_NOPE]).astype(BF16)
    wv = _pad_lanes(wkv[..., MLA_NOPE:]).astype(BF16)
    gq, gk = _pad_lanes(gq[None, :]), _pad_lanes(gk[None, :])
    return wq, wk, wv, gq, _swap_rope_halves(gq), gk, _swap_rope_halves(gk)


GATE_MID_LANE = 8
GATE_LO_LANE = 16
GATE_ONE_LANE = LANE - 1
Q_GATE_LANE = FOX_HEAD_DIM
K_GATE_LANE = FOX_HEAD_DIM + 3


def _fox_prep_kernel(fq_ref, fk_ref, fv_ref, krfg_ref, bf_ref, gq_ref, gk_ref, pq_ref, pk_ref, pv_ref,
                     q_ref, k_ref, v_ref, carry_ref):
    tl = fq_ref.shape[1]
    lane = lax.broadcasted_iota(jnp.int32, (tl, LANE), 1)

    @pl.when(pl.program_id(1) == 0)
    def _():
        carry_ref[...] = jnp.zeros_like(carry_ref)

    logf = jax.nn.log_sigmoid(krfg_ref[0] + bf_ref[...])
    logf = jnp.where(lane < FOX_HEADS, logf, 0.0)
    r_i = lax.broadcasted_iota(jnp.int32, (tl, tl), 0)
    c_i = lax.broadcasted_iota(jnp.int32, (tl, tl), 1)
    tri = jnp.where(c_i <= r_i, 1.0, 0.0).astype(BF16)
    cum = carry_ref[0:1, :]
    for piece in _split3(logf):
        cum = cum + jnp.dot(tri, piece.astype(BF16), preferred_element_type=F32)
    carry_ref[0:1, :] = cum[tl - 1:tl, :]

    c_hi, c_mid, c_lo = _split3(cum * LOG2E)
    gate_row = (c_hi + pltpu.roll(c_mid, GATE_MID_LANE, 1) + pltpu.roll(c_lo, GATE_LO_LANE, 1)
                + jnp.where(lane == GATE_ONE_LANE, 1.0, 0.0)).astype(BF16)

    p_r = lax.broadcasted_iota(jnp.int32, (LANE, LANE), 0)
    p_c = lax.broadcasted_iota(jnp.int32, (LANE, LANE), 1)
    head_mean = jnp.where(p_r // FOX_HEAD_DIM == p_c // FOX_HEAD_DIM, 1.0 / FOX_HEAD_DIM, 0.0).astype(BF16)

    def mean_sq(ref, j):
        x = ref[0, :, j * LANE:(j + 1) * LANE]
        return jnp.dot((x * x).astype(BF16), head_mean, preferred_element_type=F32)

    def normed(ref, g_ref, j, ms):
        lanes = slice(j * LANE, (j + 1) * LANE)
        return (ref[0, :, lanes] * lax.rsqrt(ms + EPS) * g_ref[:, lanes]).astype(BF16)

    def placed(x, p_ref, j):
        return jnp.dot(jnp.concatenate([x, gate_row], axis=1), p_ref[j], preferred_element_type=F32)

    pairs = range(FOX_HEADS // 2)
    q_ms = [mean_sq(fq_ref, j) for j in pairs]
    k_ms = [mean_sq(fk_ref, j) for j in pairs]
    q_n = [normed(fq_ref, gq_ref, j, q_ms[j]) for j in pairs]
    k_n = [normed(fk_ref, gk_ref, j, k_ms[j]) for j in pairs]
    for j in pairs:
        q = placed(q_n[j], pq_ref, j)
        k = placed(k_n[j], pk_ref, j)
        v = placed(fv_ref[0, :, j * LANE:(j + 1) * LANE].astype(BF16), pv_ref, j)
        for hh in range(2):
            head = slice(hh * HEAD_PAD, (hh + 1) * HEAD_PAD)
            q_ref[0, 2 * j + hh] = q[:, head].astype(BF16)
            _store_key_blocks(k_ref, 2 * j + hh, k[:, head])
            v_ref[0, 2 * j + hh] = v[:, head].astype(BF16)


def _fox_prep_specs(tl):
    row = lambda b, i: (b, i, 0)
    c2 = lambda b, i: (0, 0)
    c3 = lambda b, i: (0, 0, 0)
    head_out = pl.BlockSpec((1, FOX_HEADS, tl, HEAD_PAD), lambda b, i: (b, 0, i, 0))
    head_shape = jax.ShapeDtypeStruct((BATCH, FOX_HEADS, SEQ, HEAD_PAD), BF16)
    place = pl.BlockSpec((FOX_HEADS // 2, 2 * LANE, 2 * HEAD_PAD), c3)
    in_specs = [pl.BlockSpec((1, tl, ATT_WIDTH), row),
                pl.BlockSpec((1, tl, ATT_WIDTH), row),
                pl.BlockSpec((1, tl, ATT_WIDTH), row),
                pl.BlockSpec((1, tl, LANE), row),
                pl.BlockSpec((1, LANE), c2),
                pl.BlockSpec((1, ATT_WIDTH), c2),
                pl.BlockSpec((1, ATT_WIDTH), c2),
                place, place, place]
    return (in_specs, [head_out, _KT_SPEC(FOX_HEADS, tl), head_out],
            [head_shape, _KT_SHAPE(FOX_HEADS), head_shape])


def _front_kernel(*refs, n_mla, n_fox):
    x_ref, g_ref, sh_ref, sc_ref, w_ref = refs[:5]
    mla_rest, fox_rest = refs[5:5 + n_mla], refs[5 + n_mla:5 + n_mla + n_fox]
    u_ref = refs[5 + n_mla + n_fox]
    outs = refs[6 + n_mla + n_fox:12 + n_mla + n_fox]
    proj_ref, carry_ref = refs[-2:]
    _inproj_into(x_ref, g_ref, sh_ref, sc_ref, w_ref, proj_ref, u_ref)
    cols = [proj_ref.at[:, :, c0:c1] for c0, c1 in _IN_GROUPS]
    _mla_prep_kernel(cols[1], cols[2], cols[3], *mla_rest, *outs[:3])
    _fox_prep_kernel(cols[4], cols[5], cols[6], cols[3], *fox_rest, *outs[3:], carry_ref)


def _front_call(x, g, sh, sc, w, layer, mla_rest, fox_rest):
    tl = ROW_TILE
    row = lambda b, i: (b, i, 0)
    per_b = lambda b, i: (b, 0, 0)
    const = lambda b, i: (0, 0)
    mla_in, mla_out, mla_shape = _mla_prep_specs(tl)
    fox_in, fox_out, fox_shape = _fox_prep_specs(tl)
    mla_in, fox_in = mla_in[3:], fox_in[4:]
    outs = pl.pallas_call(
        functools.partial(_front_kernel, n_mla=len(mla_in), n_fox=len(fox_in)),
        grid=(BATCH, SEQ // tl),
        in_specs=[pl.BlockSpec((1, tl, D_MODEL), row),
                  pl.BlockSpec((1, D_MODEL), const),
                  pl.BlockSpec((1, 1, D_MODEL), per_b),
                  pl.BlockSpec((1, 1, D_MODEL), per_b),
                  pl.BlockSpec((1, D_MODEL, IN_PAD), lambda b, i: (layer, 0, 0))] + mla_in + fox_in,
        out_specs=[pl.BlockSpec((tl, SSM_WIDTH), lambda b, i: (i, b))] + mla_out + fox_out,
        out_shape=[jax.ShapeDtypeStruct((SEQ, BATCH * SSM_WIDTH), F32)] + mla_shape + fox_shape,
        scratch_shapes=[pltpu.VMEM((1, tl, IN_PAD), F32), pltpu.VMEM((SUBLANE, LANE), F32)],
        compiler_params=_params(("arbitrary", "arbitrary")),
        name="front",
    )(x, g, sh, sc, w, *mla_rest, *fox_rest)
    return outs[0], outs[1:4], outs[4:]


def _fox_placements():
    pq = np.zeros((FOX_HEADS // 2, 2 * LANE, 2 * HEAD_PAD), np.float32)
    pk = np.zeros_like(pq)
    pv = np.zeros_like(pq)
    one_row = LANE + GATE_ONE_LANE
    for j in range(FOX_HEADS // 2):
        for hh in range(2):
            h, col0 = 2 * j + hh, hh * HEAD_PAD
            for d in range(FOX_HEAD_DIM):
                for p in (pq, pk, pv):
                    p[j, hh * FOX_HEAD_DIM + d, col0 + d] = 1.0
            pv[j, one_row, col0 + ONES_LANE] = 1.0
            for n, piece_lane in enumerate((0, GATE_MID_LANE, GATE_LO_LANE)):
                pq[j, LANE + piece_lane + h, col0 + Q_GATE_LANE + n] = 1.0
                pq[j, one_row, col0 + K_GATE_LANE + n] = 1.0
                pk[j, one_row, col0 + Q_GATE_LANE + n] = 1.0
                pk[j, LANE + piece_lane + h, col0 + K_GATE_LANE + n] = -1.0
    return tuple(jnp.asarray(p, BF16) for p in (pq, pk, pv))


def _fox_operands(bf, gq, gk):
    q_scale = LOG2E / math.sqrt(FOX_HEAD_DIM)
    return (_pad_lanes(bf[None, :], LANE), jnp.tile(gq * q_scale, FOX_HEADS)[None, :],
            jnp.tile(gk, FOX_HEADS)[None, :]) + _fox_placements()


def _flash_kernel(qa_ref, qb_ref, kt_ref, v_ref, gap_ref, o_ref, q_scr, s_ref, m_ref, acc_ref,
                  *, tile, chunk, n_tiles):
    p = pl.program_id(2)
    tiles = (p, n_tiles - 1 - p)
    n_tasks = n_tiles + 1
    half = tile // 2
    top, bottom = slice(0, half), slice(half, tile)
    lane = lax.broadcasted_iota(jnp.int32, (tile, HEAD_PAD), 1)
    q_scr[0] = qa_ref[0]
    q_scr[1] = qb_ref[0]

    def plain_task(t):
        second = t - 2 >= p
        return second, second.astype(jnp.int32), jnp.where(second, t - 2 - p, t - 2)

    def row_max_update(w, hh, rows, s):
        mr = m_ref[w, hh, rows]
        for c in range(s.shape[1] // LANE):
            mr = jnp.maximum(mr, s[:, c * LANE:(c + 1) * LANE])
        m_ref[w, hh, rows] = mr

    m_ref[...] = jnp.full(m_ref.shape, NEG, F32)
    for w in range(2):
        for hh in range(2):
            kt = kt_ref[0, hh, tiles[w]]
            s_top = jnp.dot(q_scr[w, hh, top], kt[:, top], preferred_element_type=F32)
            s_top = jnp.where(gap_ref[top, top] <= 0, s_top, NEG)
            s_ref[hh, w, top, top] = s_top
            row_max_update(w, hh, top, s_top)
            s_bot = jnp.dot(q_scr[w, hh, bottom], kt, preferred_element_type=F32)
            s_bot = jnp.where(gap_ref[bottom, :] <= 0, s_bot, NEG)
            s_ref[hh, w, bottom] = s_bot
            row_max_update(w, hh, bottom, s_bot)
    for t in range(2, n_tasks):
        _, which, j = plain_task(jnp.int32(t))
        for hh in range(2):
            s = jnp.dot(q_scr[which, hh], kt_ref[0, hh, j], preferred_element_type=F32)
            s_ref[hh, t] = s
            row_max_update(which, hh, slice(None), s)

    ms = [[jnp.max(m_ref[w, hh], axis=1, keepdims=True) for hh in range(2)] for w in range(2)]

    acc_ref[...] = jnp.zeros(acc_ref.shape, F32)
    for w in range(2):
        k0 = pl.multiple_of(tiles[w] * tile, tile)
        for hh in range(2):
            pr = jnp.exp2(s_ref[hh, w, top, top] - ms[w][hh][top]).astype(BF16)
            acc_ref[w, hh, top] += jnp.dot(pr, v_ref[0, hh, pl.ds(k0, half), :], preferred_element_type=F32)
            pr = jnp.exp2(s_ref[hh, w, bottom] - ms[w][hh][bottom]).astype(BF16)
            acc_ref[w, hh, bottom] += jnp.dot(pr, v_ref[0, hh, pl.ds(k0, tile), :], preferred_element_type=F32)
    for t in range(2, n_tasks):
        second, which, j = plain_task(jnp.int32(t))
        k0 = pl.multiple_of(j * tile, tile)
        for hh in range(2):
            row_max = jnp.where(second, ms[1][hh], ms[0][hh])
            pr = jnp.exp2(s_ref[hh, t] - row_max).astype(BF16)
            acc_ref[which, hh] += jnp.dot(pr, v_ref[0, hh, pl.ds(k0, tile), :], preferred_element_type=F32)

    for w in range(2):
        outs = [acc_ref[w, hh] / acc_ref[w, hh][:, ONES_LANE:ONES_LANE + 1] for hh in range(2)]
        o_ref[0, w, 0] = jnp.where(lane < 64, outs[0], pltpu.roll(outs[1], 64, 1)).astype(BF16)


def _flash_call(q, kt, v, chunk):
    tile = ATT_TILE
    heads = q.shape[1]
    n_tiles = SEQ // tile
    pos = np.arange(tile, dtype=np.int32) // chunk
    gap = jnp.asarray(pos[None, :] - pos[:, None])
    return pl.pallas_call(
        functools.partial(_flash_kernel, tile=tile, chunk=chunk, n_tiles=n_tiles),
        grid=(BATCH, heads // 2, n_tiles // 2),
        in_specs=[pl.BlockSpec((1, 2, tile, HEAD_PAD), lambda b, hp, p: (b, hp, p, 0)),
                  pl.BlockSpec((1, 2, tile, HEAD_PAD), lambda b, hp, p: (b, hp, n_tiles - 1 - p, 0)),
                  pl.BlockSpec((1, 2, n_tiles, HEAD_PAD, tile), lambda b, hp, p: (b, hp, 0, 0, 0)),
                  pl.BlockSpec((1, 2, SEQ, HEAD_PAD), lambda b, hp, p: (b, hp, 0, 0)),
                  pl.BlockSpec((tile, tile), lambda b, hp, p: (0, 0))],
        out_specs=pl.BlockSpec((1, 2, 1, tile, LANE), lambda b, hp, p: (b, 0, p, 0, hp)),
        out_shape=jax.ShapeDtypeStruct((BATCH, 2, n_tiles // 2, tile, ATT_WIDTH), BF16),
        scratch_shapes=[pltpu.VMEM((2, 2, tile, HEAD_PAD), BF16),
                        pltpu.VMEM((2, n_tiles + 1, tile, tile), F32),
                        pltpu.VMEM((2, 2, tile, LANE), F32),
                        pltpu.VMEM((2, 2, tile, HEAD_PAD), F32)],
        compiler_params=_params(("arbitrary", "arbitrary", "arbitrary")),
        name="flash_chunk%d" % chunk,
    )(q, q, kt, v, gap)


def _mixed(ssm, mla_ref, fox_ref, gm_ref, gf_ref, w_ref):
    def normed(ref, g_ref):
        a = ref[0, 0, 0].astype(F32)
        return (a * lax.rsqrt(jnp.mean(a * a, axis=-1, keepdims=True) + EPS) * g_ref[...]).astype(BF16)

    merged = jnp.concatenate([ssm, normed(mla_ref, gm_ref), normed(fox_ref, gf_ref)], axis=1)
    return jnp.dot(merged, w_ref[...], preferred_element_type=F32)


def _merge_kernel(ssm_ref, mla_ref, fox_ref, x_ref, g1_ref, gm_ref, gf_ref, w_ref, o_ref):
    o_ref[0] = x_ref[0] + g1_ref[0] * _mixed(ssm_ref[...], mla_ref, fox_ref, gm_ref, gf_ref, w_ref)


def _att_tile_spec(tile_of):
    half = SEQ // ATT_TILE // 2

    def index(b, i):
        t = tile_of(i)
        return (b, t // half, jnp.where(t < half, t, 2 * half - 1 - t), 0, 0)
    return pl.BlockSpec((1, 1, 1, ATT_TILE, ATT_WIDTH), index)


def _merge_call(o_ssm, o_mla, o_fox, x, g1, gm, gf, w):
    tm = ATT_TILE
    row = lambda b, i: (b, i, 0)
    c2 = lambda b, i: (0, 0)
    att = _att_tile_spec(lambda i: i)
    return pl.pallas_call(
        _merge_kernel,
        grid=(BATCH, SEQ // tm),
        in_specs=[pl.BlockSpec((tm, SSM_WIDTH), lambda b, i: (i, b)),
                  att, att,
                  pl.BlockSpec((1, tm, D_MODEL), row),
                  pl.BlockSpec((1, 1, D_MODEL), lambda b, i: (b, 0, 0)),
                  pl.BlockSpec((1, ATT_WIDTH), c2),
                  pl.BlockSpec((1, ATT_WIDTH), c2),
                  pl.BlockSpec((D_MODEL, D_MODEL), c2)],
        out_specs=pl.BlockSpec((1, tm, D_MODEL), row),
        out_shape=jax.ShapeDtypeStruct((BATCH, SEQ, D_MODEL), F32),
        compiler_params=_params(("arbitrary", "arbitrary")),
        name="merge",
    )(o_ssm, o_mla, o_fox, x, g1, gm, gf, w)


def _merge_ffn_kernel(ssm_ref, mla0_ref, mla1_ref, fox0_ref, fox1_ref, x_ref, g1_ref, gm_ref, gf_ref, wo_ref,
                      g_ref, sh_ref, sc_ref, g2_ref, wg_ref, wu_ref, wd_ref, o_ref):
    half = x_ref.shape[1] // 2
    att = ((mla0_ref, fox0_ref), (mla1_ref, fox1_ref))
    for n, r in enumerate((slice(0, half), slice(half, 2 * half))):
        x = x_ref[0, r] + g1_ref[0] * _mixed(ssm_ref[r, :], *att[n], gm_ref, gf_ref, wo_ref)
        h = _rms_mod(x, g_ref[...], sc_ref[0], sh_ref[0]).astype(BF16)
        gate_up = [(jnp.dot(h, wg_ref[0, :, c0:c1], preferred_element_type=F32),
                    jnp.dot(h, wu_ref[0, :, c0:c1], preferred_element_type=F32)) for c0, c1 in FF_PARTS]
        acc = jnp.zeros((half, D_MODEL), F32)
        for (c0, c1), (gate, up) in zip(FF_PARTS, gate_up):
            a = (gate * jax.nn.sigmoid(gate) * up).astype(BF16)
            acc = acc + jnp.dot(a, wd_ref[0, c0:c1, :], preferred_element_type=F32)
        o_ref[0, r] = x + g2_ref[0] * acc


def _merge_ffn_call(o_ssm, o_mla, o_fox, x, g1, gm, gf, wo, g, sh, sc, g2, wg, wu, wd, layer):
    tm = 2 * ATT_TILE
    row = lambda b, i: (b, i, 0)
    per_b = lambda b, i: (b, 0, 0)
    c2 = lambda b, i: (0, 0)
    once = pl.Buffered(1)
    resident = lambda shape: pl.BlockSpec(shape, lambda b, i: (layer, 0, 0), pipeline_mode=once)
    att0, att1 = _att_tile_spec(lambda i: 2 * i), _att_tile_spec(lambda i: 2 * i + 1)
    return pl.pallas_call(
        _merge_ffn_kernel,
        grid=(BATCH, SEQ // tm),
        in_specs=[pl.BlockSpec((tm, SSM_WIDTH), lambda b, i: (i, b)),
                  att0, att1, att0, att1,
                  pl.BlockSpec((1, tm, D_MODEL), row),
                  pl.BlockSpec((1, 1, D_MODEL), per_b),
                  pl.BlockSpec((1, ATT_WIDTH), c2),
                  pl.BlockSpec((1, ATT_WIDTH), c2),
                  pl.BlockSpec((D_MODEL, D_MODEL), c2, pipeline_mode=once),
                  pl.BlockSpec((1, D_MODEL), c2),
                  pl.BlockSpec((1, 1, D_MODEL), per_b),
                  pl.BlockSpec((1, 1, D_MODEL), per_b),
                  pl.BlockSpec((1, 1, D_MODEL), per_b),
                  resident((1, D_MODEL, D_FF)),
                  resident((1, D_MODEL, D_FF)),
                  resident((1, D_FF, D_MODEL))],
        out_specs=pl.BlockSpec((1, tm, D_MODEL), row),
        out_shape=jax.ShapeDtypeStruct((BATCH, SEQ, D_MODEL), F32),
        compiler_params=_params(("arbitrary", "arbitrary")),
        name="merge_ffn",
    )(o_ssm, o_mla, o_mla, o_fox, o_fox, x, g1, gm, gf, wo, g, sh, sc, g2, wg, wu, wd)


def _router_kernel(x_ref, g_ref, sh_ref, sc_ref, w_ref, b_ref, comb_ref, rank_ref, rankt_ref, count_ref):
    tm = x_ref.shape[1]
    h = _rms_mod(x_ref[0], g_ref[...], sc_ref[0], sh_ref[0])
    h_hi = h.astype(BF16)
    h_lo = (h - h_hi.astype(F32)).astype(BF16)
    parts = jnp.dot(jnp.concatenate([h_hi, h_lo], axis=1), w_ref[...], preferred_element_type=F32)
    logits = parts + pltpu.roll(parts, LANE - N_EXPERTS, 1) + b_ref[...]
    lane = lax.broadcasted_iota(jnp.int32, logits.shape, 1)
    logits = jnp.where(lane < N_EXPERTS, logits, -jnp.inf)
    m1 = jnp.max(logits, axis=-1, keepdims=True)
    i1 = jnp.min(jnp.where(logits == m1, lane, LANE), axis=-1, keepdims=True)
    rest = jnp.where(lane == i1, -jnp.inf, logits)
    m2 = jnp.max(rest, axis=-1, keepdims=True)
    i2 = jnp.min(jnp.where(rest == m2, lane, LANE), axis=-1, keepdims=True)
    e = jnp.exp(m2 - m1)
    p1 = 1.0 / (1.0 + e)
    comb_ref[0] = jnp.where(lane == i1, p1, 0.0) + jnp.where(lane == i2, e * p1, 0.0)

    chosen = (lane == i1) | (lane == i2)
    chosen_f = jnp.where(chosen, 1.0, 0.0)
    r_i = lax.broadcasted_iota(jnp.int32, (tm, tm), 0)
    c_i = lax.broadcasted_iota(jnp.int32, (tm, tm), 1)
    earlier = jnp.where(c_i < r_i, 1.0, 0.0).astype(BF16)
    rank = jnp.dot(earlier, chosen_f.astype(BF16), preferred_element_type=F32)
    rank = jnp.where(chosen, rank, -1.0)
    rank_ref[0] = rank
    rankt_ref[0] = rank.T[0:SUBLANE, :]
    count_ref[0] = jnp.sum(chosen_f, axis=0, keepdims=True)


def _router_call(x, g, sh, sc, w, b):
    tm = MOE_TILE
    tiles = SEQ // tm
    row = lambda b_, i: (b_, i, 0)
    per_b = lambda b_, i: (b_, 0, 0)
    per_tile = lambda b_, i: (b_ * tiles + i, 0, 0)
    return pl.pallas_call(
        _router_kernel,
        grid=(BATCH, tiles),
        in_specs=[pl.BlockSpec((1, tm, D_MODEL), row),
                  pl.BlockSpec((1, D_MODEL), lambda b_, i: (0, 0)),
                  pl.BlockSpec((1, 1, D_MODEL), per_b),
                  pl.BlockSpec((1, 1, D_MODEL), per_b),
                  pl.BlockSpec((2 * D_MODEL, LANE), lambda b_, i: (0, 0)),
                  pl.BlockSpec((1, LANE), lambda b_, i: (0, 0))],
        out_specs=[pl.BlockSpec((1, tm, LANE), row),
                   pl.BlockSpec((1, tm, LANE), row),
                   pl.BlockSpec((1, SUBLANE, tm), per_tile),
                   pl.BlockSpec((1, 1, LANE), per_tile)],
        out_shape=[jax.ShapeDtypeStruct((BATCH, SEQ, LANE), F32),
                   jax.ShapeDtypeStruct((BATCH, SEQ, LANE), F32),
                   jax.ShapeDtypeStruct((BATCH * tiles, SUBLANE, tm), F32),
                   jax.ShapeDtypeStruct((BATCH * tiles, 1, LANE), F32)],
        compiler_params=_params(("arbitrary", "arbitrary")),
        name="router",
    )(x, g, sh, sc, w, b)


def _moe_kernel(count_ref, x_ref, g_ref, sh_ref, sc_ref, g2_ref, comb_ref, rank_ref, rankt_ref,
                wg_ref, wu_ref, wt_ref, wd_ref, o_ref, h_ref):
    tm = x_ref.shape[1]
    e = pl.program_id(1)

    @pl.when(e == 0)
    def _():
        x = x_ref[0]
        h_ref[...] = _rms_mod(x, g_ref[...], sc_ref[0], sh_ref[0]).astype(BF16)
        o_ref[0] = x

    lane = lax.broadcasted_iota(jnp.int32, (tm, LANE), 1)
    mine = lane == e
    rank_col = jnp.sum(jnp.where(mine, rank_ref[0], 0.0), axis=-1, keepdims=True)
    gate_col = jnp.sum(jnp.where(mine, comb_ref[0], 0.0), axis=-1, keepdims=True)
    rank_row = rankt_ref[0, pl.ds(e, 1), :]
    count = count_ref[pl.program_id(0) * N_EXPERTS + e]

    def expert_pass(first, n_rows):
        base = first.astype(F32)
        slot_sub = lax.broadcasted_iota(jnp.int32, (n_rows, tm), 0).astype(F32)
        slot_lane = lax.broadcasted_iota(jnp.int32, (tm, n_rows), 1).astype(F32)
        pick = jnp.where(rank_row - base == slot_sub, 1.0, 0.0).astype(BF16)
        rows = jnp.dot(pick, h_ref[...], preferred_element_type=F32).astype(BF16)
        gate = jnp.dot(rows, wg_ref[0, 0, :, :EXPERT_MAIN], preferred_element_type=F32)
        up = jnp.dot(rows, wu_ref[0, 0, :, :EXPERT_MAIN], preferred_element_type=F32)
        tail = jnp.dot(rows, wt_ref[0, 0], preferred_element_type=F32)
        gate_t, up_t = tail[:, :LANE], tail[:, LANE:]
        a = jnp.concatenate([gate * jax.nn.sigmoid(gate) * up, gate_t * jax.nn.sigmoid(gate_t) * up_t],
                            axis=1).astype(BF16)
        y = jnp.dot(a, wd_ref[0, 0], preferred_element_type=F32).astype(BF16)
        place = jnp.where(rank_col - base == slot_lane, 1.0, 0.0).astype(BF16)
        back = jnp.dot(place, y, preferred_element_type=F32)
        o_ref[0] += g2_ref[0] * (gate_col * back)

    def full_pass(sb, carry):
        expert_pass(sb * MOE_ROWS, MOE_ROWS)
        return carry

    n_full = count // MOE_ROWS
    lax.fori_loop(0, n_full, full_pass, 0)
    left = count - n_full * MOE_ROWS

    @pl.when(left > MOE_ROWS // 2)
    def _():
        expert_pass(n_full * MOE_ROWS, MOE_ROWS)

    @pl.when((left > 0) & (left <= MOE_ROWS // 2))
    def _():
        expert_pass(n_full * MOE_ROWS, MOE_ROWS // 2)


def _moe_call(x, g, sh, sc, g2, comb, rank, rankt, counts, wg, wu, wt, wd, layer):
    tm = MOE_TILE
    tiles = SEQ // tm
    n_tiles = BATCH * tiles
    row = lambda i, e, cnt: (i, 0, 0)
    per_b = lambda i, e, cnt: (i // tiles, 0, 0)
    expert = lambda i, e, cnt: (layer, e, 0, 0)
    as_tiles = lambda a: a.reshape(n_tiles, tm, a.shape[-1])
    grid_spec = pltpu.PrefetchScalarGridSpec(
        num_scalar_prefetch=1,
        grid=(n_tiles, N_EXPERTS),
        in_specs=[pl.BlockSpec((1, tm, D_MODEL), row),
                  pl.BlockSpec((1, D_MODEL), lambda i, e, cnt: (0, 0)),
                  pl.BlockSpec((1, 1, D_MODEL), per_b),
                  pl.BlockSpec((1, 1, D_MODEL), per_b),
                  pl.BlockSpec((1, 1, D_MODEL), per_b),
                  pl.BlockSpec((1, tm, LANE), row),
                  pl.BlockSpec((1, tm, LANE), row),
                  pl.BlockSpec((1, SUBLANE, tm), row),
                  pl.BlockSpec((1, 1, D_MODEL, D_FF_EXPERT), expert),
                  pl.BlockSpec((1, 1, D_MODEL, D_FF_EXPERT), expert),
                  pl.BlockSpec((1, 1, D_MODEL, 2 * LANE), expert),
                  pl.BlockSpec((1, 1, D_FF_EXPERT, D_MODEL), expert)],
        out_specs=pl.BlockSpec((1, tm, D_MODEL), row),
        scratch_shapes=[pltpu.VMEM((tm, D_MODEL), BF16)],
    )
    out = pl.pallas_call(
        _moe_kernel,
        grid_spec=grid_spec,
        out_shape=jax.ShapeDtypeStruct((n_tiles, tm, D_MODEL), F32),
        compiler_params=_params(("arbitrary", "arbitrary")),
        name="moe_experts",
    )(counts, as_tiles(x), g, sh, sc, g2, as_tiles(comb), as_tiles(rank), rankt, wg, wu, wt, wd)
    return out.reshape(BATCH, SEQ, D_MODEL)


def kernel(x, c, positions, norm_mix, norm_ffn, w_ada, b_ada, w_in, ssm_lam_re, ssm_lam_im, ssm_log_dt, ssm_b_re, ssm_b_im, ssm_c_re, ssm_c_im, ssm_d, ssm_w_glu, ssm_b_glu, mla_q_norm, mla_kv_norm, mla_w_uq, mla_w_ukv, mla_qk_gq, mla_qk_gk, fox_b_f, fox_qk_gq, fox_qk_gk, out_norm, w_out, ffn_w_gate, ffn_w_up, ffn_w_down, moe_w_router, moe_b_router, moe_w_gate, moe_w_up, moe_w_down):
    tabs = _rope_tables(positions)
    w_in_packed = _pack_w_in(w_in)
    moe_wg, moe_wu, moe_wd = (w.astype(BF16) for w in (moe_w_gate, moe_w_up, moe_w_down))
    moe_wt = jnp.concatenate([moe_wg[..., EXPERT_MAIN:], moe_wu[..., EXPERT_MAIN:]], axis=-1)
    ffn_wg, ffn_wu, ffn_wd = (w.astype(BF16) for w in (ffn_w_gate, ffn_w_up, ffn_w_down))
    ada = _ada_call(c, w_ada, b_ada)
    ada = ada.reshape(DEPTH, BATCH, 6, 1, D_MODEL)
    row2 = lambda a: a[None, :]

    for i in range(DEPTH):
        sh1, sc1, g1, sh2, sc2, g2 = (ada[i, :, n] for n in range(6))

        u, mla_qkv, fox_qkv = _front_call(
            x, row2(norm_mix[i]), sh1, sc1, w_in_packed, i,
            (*tabs, row2(mla_q_norm[i]), row2(mla_kv_norm[i]),
             *_mla_weights(mla_w_uq[i], mla_w_ukv[i], mla_qk_gq[i], mla_qk_gk[i])),
            _fox_operands(fox_b_f[i], fox_qk_gq[i], fox_qk_gk[i]))

        bmat, lam, cmat = _s5_operands(ssm_lam_re[i], ssm_lam_im[i], ssm_log_dt[i],
                                       ssm_b_re[i], ssm_b_im[i], ssm_c_re[i], ssm_c_im[i])
        o_ssm = _s5_call(u, bmat, lam, cmat, row2(ssm_d[i]), ssm_w_glu[i].astype(BF16),
                         row2(ssm_b_glu[i]), row2(out_norm[i, :SSM_WIDTH]))

        o_mla = _flash_call(*mla_qkv, CHUNK)
        o_fox = _flash_call(*fox_qkv, 1)

        e1, e2 = SSM_WIDTH, SSM_WIDTH + ATT_WIDTH
        merge_args = (o_ssm, o_mla, o_fox, x, g1, row2(out_norm[i, e1:e2]), row2(out_norm[i, e2:]),
                      w_out[i].astype(BF16))

        j = i // 2
        if i % 2 == 0:
            x = _merge_ffn_call(*merge_args, row2(norm_ffn[i]), sh2, sc2, g2, ffn_wg, ffn_wu, ffn_wd, layer=j)
        else:
            x = _merge_call(*merge_args)
            wr_hi = moe_w_router[j].astype(BF16)
            wr_lo = (moe_w_router[j] - wr_hi.astype(F32)).astype(BF16)
            wr = _pad_lanes(jnp.concatenate([wr_hi, wr_lo], axis=1), LANE)
            comb, rank, rankt, counts = _router_call(x, row2(norm_ffn[i]), sh2, sc2, jnp.concatenate([wr, wr], axis=0),
                                                     _pad_lanes(row2(moe_b_router[j]), LANE))
            counts = counts[:, 0, :N_EXPERTS].astype(jnp.int32).reshape(-1)
            x = _moe_call(x, row2(norm_ffn[i]), sh2, sc2, g2, comb, rank, rankt, counts,
                          moe_wg, moe_wu, moe_wt, moe_wd, layer=j)
    return x
```

```python
import functools
import math

import jax
import jax.numpy as jnp
import numpy as np
from jax import lax
from jax.experimental import pallas as pl
from jax.experimental.pallas import tpu as pltpu

F32 = jnp.float32
BF16 = jnp.bfloat16

D_MODEL = 1024
BATCH = 8
SEQ = 4096
DEPTH = 4
CHUNK = 64
EPS = 1e-6

SSM_WIDTH = 256
SSM_GROUP = 16
N_SSM_GROUPS = 16
SSM_STATE = 64
N_STATE = N_SSM_GROUPS * SSM_STATE

MLA_HEADS = 6
MLA_Q_RANK = 256
MLA_KV_RANK = 128
MLA_NOPE = 64
MLA_ROPE = 32
MLA_V = 64
MLA_QK = 96
ROPE_BASE = 10000.0

FOX_HEADS = 6
FOX_HEAD_DIM = 64
ATT_WIDTH = 384

D_FF = 2816
N_EXPERTS = 8
D_FF_EXPERT = 1408
EXPERT_MAIN = 1280

LANE = 128
SUBLANE = 8
HEAD_PAD = LANE
ONES_LANE = 64
NEG = -1e30

IN_PAD = 1920
KR_LANE = 64

ROW_TILE = 512
S5_STEPS = 64
ATT_TILE = 512
LOG2E = math.log2(math.e)
FF_PARTS = ((0, 1536), (1536, D_FF))
MOE_TILE = 1024
MOE_ROWS = 256
VMEM_LIMIT = 56 * 1024 * 1024


def _params(sem):
    return pltpu.CompilerParams(dimension_semantics=sem, vmem_limit_bytes=VMEM_LIMIT)


def _rms_mod(x, g, sc, sh):
    ms = jnp.mean(x * x, axis=-1, keepdims=True)
    h = x * lax.rsqrt(ms + EPS) * g
    return h * (1.0 + sc) + sh


def _split3(x):
    hi = x.astype(BF16).astype(F32)
    r = x - hi
    mid = r.astype(BF16).astype(F32)
    lo = (r - mid).astype(BF16).astype(F32)
    return hi, mid, lo


def _ada_kernel(c_ref, w_ref, b_ref, o_ref):
    c = c_ref[...]
    ca = (c * jax.nn.sigmoid(c)).astype(BF16)
    o_ref[0] = jnp.dot(ca, w_ref[0].astype(BF16), preferred_element_type=F32) + b_ref[0]


def _ada_call(c, w_ada, b_ada):
    tn = 1536
    return pl.pallas_call(
        _ada_kernel,
        grid=(DEPTH, 6 * D_MODEL // tn),
        in_specs=[pl.BlockSpec((BATCH, D_MODEL), lambda i, j: (0, 0)),
                  pl.BlockSpec((1, D_MODEL, tn), lambda i, j: (i, 0, j)),
                  pl.BlockSpec((1, 1, tn), lambda i, j: (i, 0, j))],
        out_specs=pl.BlockSpec((1, BATCH, tn), lambda i, j: (i, 0, j)),
        out_shape=jax.ShapeDtypeStruct((DEPTH, BATCH, 6 * D_MODEL), F32),
        compiler_params=_params(("arbitrary", "arbitrary")),
        name="ada",
    )(c, w_ada, b_ada.reshape(DEPTH, 1, 6 * D_MODEL))


_IN_GROUPS = ((0, 256), (256, 512), (512, 640), (640, 768), (768, 1152), (1152, 1536), (1536, 1920))


def _inproj_into(x_ref, g_ref, sh_ref, sc_ref, w_ref, proj_ref, u_ref):
    parts = 4
    step = x_ref.shape[1] // parts
    rows = [slice(r * step, (r + 1) * step) for r in range(parts)]
    normed = lambda r: _rms_mod(x_ref[0, rows[r]], g_ref[...], sc_ref[0], sh_ref[0]).astype(BF16)
    h_next = normed(0)
    for r in range(parts):
        h = h_next
        if r + 1 < parts:
            h_next = normed(r + 1)
        proj = jnp.dot(h, w_ref[0], preferred_element_type=F32)
        proj_ref[0, rows[r]] = proj
        u_ref[rows[r]] = proj[:, _IN_GROUPS[0][0]:_IN_GROUPS[0][1]]


def _pack_w_in(w):
    u, cq, ckv, kr, fq, fk, fv, fg = jnp.split(
        w.astype(BF16), (256, 512, 640, 672, 1056, 1440, 1824), axis=2)
    z = lambda n: jnp.zeros(w.shape[:2] + (n,), BF16)
    krfg = jnp.concatenate([fg, z(KR_LANE - FOX_HEADS), kr, z(LANE - KR_LANE - MLA_ROPE)], axis=2)
    return jnp.concatenate([u, cq, ckv, krfg, fq, fk, fv], axis=2)


def _s5_kernel(u2_ref, bmat_ref, lam_ref, cmat_ref, d_ref, wglu_ref, bglu_ref, gn_ref,
               o2_ref, u_ref, o_ref, bu0_ref, bu1_ref, state_ref, *, steps):
    rows = steps * BATCH
    lane_tiles = SSM_WIDTH // LANE
    for b in range(BATCH):
        for c in range(lane_tiles):
            lanes = slice(b * SSM_WIDTH + c * LANE, b * SSM_WIDTH + (c + 1) * LANE)
            u_ref.at[c][pl.ds(b, 2 * steps, stride=BATCH), :] = u2_ref[:, lanes]
    u_rows = lambda rs: jnp.concatenate([u_ref[c, rs, :] for c in range(lane_tiles)], axis=1)

    @pl.when(pl.program_id(0) == 0)
    def _():
        state_ref[...] = jnp.zeros_like(state_ref)

    halves = ((bu0_ref, slice(0, rows)), (bu1_ref, slice(rows, 2 * rows)))
    for bu_ref, rs in halves:
        bu_ref[...] = jnp.dot(u_rows(rs).astype(BF16), bmat_ref[...], preferred_element_type=F32)
    lr = jnp.broadcast_to(lam_ref[0:1, :], (SUBLANE, N_STATE))
    li = jnp.broadcast_to(lam_ref[1:2, :], (SUBLANE, N_STATE))
    sr, si = state_ref[:, 0:N_STATE], state_ref[:, N_STATE:2 * N_STATE]

    for bu_ref, rs in halves:
        for t in range(steps):
            r = slice(t * SUBLANE, (t + 1) * SUBLANE)
            nr = lr * sr - li * si + bu_ref[r, 0:N_STATE]
            ni = lr * si + li * sr + bu_ref[r, N_STATE:2 * N_STATE]
            bu_ref[r, 0:N_STATE] = nr
            bu_ref[r, N_STATE:2 * N_STATE] = ni
            sr, si = nr, ni
        y = jnp.dot(bu_ref[...].astype(BF16), cmat_ref[...], preferred_element_type=F32)
        y = jax.nn.gelu(y + d_ref[...] * u_rows(rs))
        gate = jnp.dot(y.astype(BF16), wglu_ref[...], preferred_element_type=F32) + bglu_ref[...]
        o = y * jax.nn.sigmoid(gate)
        ms = jnp.mean(o * o, axis=-1, keepdims=True)
        o = o * lax.rsqrt(ms + EPS) * gn_ref[...]
        for c in range(lane_tiles):
            o_ref[c, rs, :] = o[:, c * LANE:(c + 1) * LANE]

    state_ref[:, 0:N_STATE] = sr
    state_ref[:, N_STATE:2 * N_STATE] = si
    for b in range(BATCH):
        for c in range(lane_tiles):
            lanes = slice(b * SSM_WIDTH + c * LANE, b * SSM_WIDTH + (c + 1) * LANE)
            o2_ref[:, lanes] = o_ref.at[c][pl.ds(b, 2 * steps, stride=BATCH), :].astype(BF16)


def _s5_call(u2, bmat, lam, cmat, d_skip, wglu, bglu, gn):
    rows = S5_STEPS * BATCH
    const = lambda i: (0, 0)
    return pl.pallas_call(
        functools.partial(_s5_kernel, steps=S5_STEPS),
        grid=(SEQ // (2 * S5_STEPS),),
        in_specs=[pl.BlockSpec((2 * S5_STEPS, BATCH * SSM_WIDTH), lambda i: (i, 0)),
                  pl.BlockSpec((SSM_WIDTH, 2 * N_STATE), const),
                  pl.BlockSpec((2, N_STATE), const),
                  pl.BlockSpec((2 * N_STATE, SSM_WIDTH), const),
                  pl.BlockSpec((1, SSM_WIDTH), const),
                  pl.BlockSpec((SSM_WIDTH, SSM_WIDTH), const),
                  pl.BlockSpec((1, SSM_WIDTH), const),
                  pl.BlockSpec((1, SSM_WIDTH), const)],
        out_specs=pl.BlockSpec((2 * S5_STEPS, BATCH * SSM_WIDTH), lambda i: (i, 0)),
        out_shape=jax.ShapeDtypeStruct((SEQ, BATCH * SSM_WIDTH), BF16),
        scratch_shapes=[pltpu.VMEM((SSM_WIDTH // LANE, 2 * rows, LANE), F32),
                        pltpu.VMEM((SSM_WIDTH // LANE, 2 * rows, LANE), F32),
                        pltpu.VMEM((rows, 2 * N_STATE), F32),
                        pltpu.VMEM((rows, 2 * N_STATE), F32),
                        pltpu.VMEM((SUBLANE, 2 * N_STATE), F32)],
        compiler_params=_params(("arbitrary",)),
        name="s5",
    )(u2, bmat, lam, cmat, d_skip, wglu, bglu, gn)


def _s5_operands(lam_re, lam_im, log_dt, b_re, b_im, c_re, c_im):
    dt = jnp.exp(log_dt)[:, None]
    mag = jnp.exp(lam_re * dt)
    lb_re = mag * jnp.cos(lam_im * dt)
    lb_im = mag * jnp.sin(lam_im * dt)
    den = lam_re * lam_re + lam_im * lam_im
    co_re = ((lb_re - 1.0) * lam_re + lb_im * lam_im) / den
    co_im = (lb_im * lam_re - (lb_re - 1.0) * lam_im) / den
    bb_re = co_re[..., None] * b_re - co_im[..., None] * b_im
    bb_im = co_re[..., None] * b_im + co_im[..., None] * b_re
    eye = jnp.eye(N_SSM_GROUPS, dtype=F32)
    blk_b = lambda m: jnp.einsum("gpc,gh->gchp", m, eye).reshape(SSM_WIDTH, N_STATE)
    bmat = jnp.concatenate([blk_b(bb_re), blk_b(bb_im)], axis=1).astype(BF16)
    blk_c = lambda m: jnp.einsum("gcp,gh->gphc", m, eye).reshape(N_STATE, SSM_WIDTH)
    cmat = jnp.concatenate([blk_c(c_re), -blk_c(c_im)], axis=0).astype(BF16)
    lam = jnp.stack([lb_re.reshape(N_STATE), lb_im.reshape(N_STATE)], axis=0)
    return bmat, lam, cmat


def _rope_tables(positions):
    half = MLA_ROPE // 2
    inv = ROPE_BASE ** (-jnp.arange(half, dtype=F32) / half)
    ang = inv[:, None] * positions.astype(F32).reshape(1, -1)
    shp = positions.shape
    cos, sin = (lax.optimization_barrier(f(ang)).T.reshape(shp + (half,)) for f in (jnp.cos, jnp.sin))
    one = lambda n: jnp.ones(shp + (n,), F32)
    zero = lambda n: jnp.zeros(shp + (n,), F32)
    cos_t = jnp.concatenate([one(MLA_NOPE), cos, cos, zero(LANE - MLA_QK)], axis=-1)
    sin_t = jnp.concatenate([zero(MLA_NOPE), -sin, sin, zero(LANE - MLA_QK)], axis=-1)
    return cos_t, sin_t


def _swap_rope_halves(a):
    half = MLA_ROPE // 2
    lo, hi = a[..., MLA_NOPE:MLA_NOPE + half], a[..., MLA_NOPE + half:MLA_QK]
    return jnp.concatenate([jnp.zeros_like(a[..., :MLA_NOPE]), hi, lo, jnp.zeros_like(a[..., MLA_QK:])], axis=-1)


def _store_key_blocks(kt_ref, h, k):
    kt = k.T
    for s in range(k.shape[0] // ATT_TILE):
        kt_ref[0, h, s] = kt[:, s * ATT_TILE:(s + 1) * ATT_TILE].astype(BF16)


_KT_SPEC = lambda heads, tl: pl.BlockSpec((1, heads, tl // ATT_TILE, HEAD_PAD, ATT_TILE),
                                          lambda b, i: (b, 0, i, 0, 0))
_KT_SHAPE = lambda heads: jax.ShapeDtypeStruct((BATCH, heads, SEQ // ATT_TILE, HEAD_PAD, ATT_TILE), BF16)


def _mla_prep_kernel(cq_ref, ckv_ref, krfg_ref, cos_ref, sin_ref, qn_ref, kvn_ref, wq_ref, wk_ref, wv_ref,
                     gq_ref, gqs_ref, gk_ref, gks_ref, q_ref, k_ref, v_ref):
    tl = cq_ref.shape[1]
    lane = lax.broadcasted_iota(jnp.int32, (tl, LANE), 1)
    cos, sin = cos_ref[0], sin_ref[0]
    q_scale = LOG2E / math.sqrt(MLA_QK)
    q_cos, q_sin = gq_ref[...] * cos * q_scale, gqs_ref[...] * sin * q_scale
    k_cos, k_sin = gk_ref[...] * cos, gks_ref[...] * sin
    ones = jnp.ones((LANE, LANE), BF16)

    def inv_rms(x):
        ss = jnp.dot((x * x).astype(BF16), ones, preferred_element_type=F32)
        return lax.rsqrt(ss / MLA_QK + EPS)

    cq = cq_ref[0]
    cqn = (cq * lax.rsqrt(jnp.mean(cq * cq, axis=-1, keepdims=True) + EPS) * qn_ref[...]).astype(BF16)
    ckv = ckv_ref[0]
    ckvn = (ckv * lax.rsqrt(jnp.mean(ckv * ckv, axis=-1, keepdims=True) + EPS) * kvn_ref[...]).astype(BF16)
    kr = jnp.where((lane >= KR_LANE) & (lane < KR_LANE + MLA_ROPE), krfg_ref[0], 0.0)
    kr_swapped = jnp.where(lane < KR_LANE + MLA_ROPE // 2, pltpu.roll(kr, LANE - 16, 1), pltpu.roll(kr, 16, 1))
    k_rotary = kr_swapped * k_sin

    heads = range(MLA_HEADS)
    qqs = [jnp.dot(cqn, wq_ref[h], preferred_element_type=F32) for h in heads]
    ks = [jnp.dot(ckvn, wk_ref[h], preferred_element_type=F32) + kr for h in heads]
    q_inv = [inv_rms(qq[:, :LANE]) for qq in qqs]
    k_inv = [inv_rms(k) for k in ks]
    for h in heads:
        q, q_swapped = qqs[h][:, :LANE], qqs[h][:, LANE:]
        q_ref[0, h] = (q_inv[h] * (q * q_cos + q_swapped * q_sin)).astype(BF16)
        _store_key_blocks(k_ref, h, k_inv[h] * (ks[h] * k_cos + k_rotary))
        v = jnp.dot(ckvn, wv_ref[h], preferred_element_type=F32)
        v_ref[0, h] = jnp.where(lane == ONES_LANE, 1.0, v).astype(BF16)


def _mla_prep_specs(tl):
    row = lambda b, i: (b, i, 0)
    c2 = lambda b, i: (0, 0)
    c3 = lambda b, i: (0, 0, 0)
    head_out = pl.BlockSpec((1, MLA_HEADS, tl, HEAD_PAD), lambda b, i: (b, 0, i, 0))
    head_shape = jax.ShapeDtypeStruct((BATCH, MLA_HEADS, SEQ, HEAD_PAD), BF16)
    gain = pl.BlockSpec((1, HEAD_PAD), c2)
    in_specs = [pl.BlockSpec((1, tl, MLA_Q_RANK), row),
                pl.BlockSpec((1, tl, MLA_KV_RANK), row),
                pl.BlockSpec((1, tl, LANE), row),
                pl.BlockSpec((1, tl, LANE), row),
                pl.BlockSpec((1, tl, LANE), row),
                pl.BlockSpec((1, MLA_Q_RANK), c2),
                pl.BlockSpec((1, MLA_KV_RANK), c2),
                pl.BlockSpec((MLA_HEADS, MLA_Q_RANK, 2 * HEAD_PAD), c3),
                pl.BlockSpec((MLA_HEADS, MLA_KV_RANK, HEAD_PAD), c3),
                pl.BlockSpec((MLA_HEADS, MLA_KV_RANK, HEAD_PAD), c3),
                gain, gain, gain, gain]
    return (in_specs, [head_out, _KT_SPEC(MLA_HEADS, tl), head_out],
            [head_shape, _KT_SHAPE(MLA_HEADS), head_shape])


def _pad_lanes(a, n=HEAD_PAD):
    return jnp.pad(a, [(0, 0)] * (a.ndim - 1) + [(0, n - a.shape[-1])])


def _mla_weights(w_uq, w_ukv, gq, gk):
    wq = _pad_lanes(w_uq.reshape(MLA_Q_RANK, MLA_HEADS, MLA_QK).transpose(1, 0, 2))
    wq = jnp.concatenate([wq, _swap_rope_halves(wq)], axis=-1).astype(BF16)
    wkv = w_ukv.reshape(MLA_KV_RANK, MLA_HEADS, MLA_NOPE + MLA_V).transpose(1, 0, 2)
    wk = _pad_lanes(wkv[..., :MLA_NOPE]).astype(BF16)
    wv = _pad_lanes(wkv[..., MLA_NOPE:]).astype(BF16)
    gq, gk = _pad_lanes(gq[None, :]), _pad_lanes(gk[None, :])
    return wq, wk, wv, gq, _swap_rope_halves(gq), gk, _swap_rope_halves(gk)


GATE_MID_LANE = 8
GATE_LO_LANE = 16
GATE_ONE_LANE = LANE - 1
Q_GATE_LANE = FOX_HEAD_DIM
K_GATE_LANE = FOX_HEAD_DIM + 3


def _fox_prep_kernel(fq_ref, fk_ref, fv_ref, krfg_ref, bf_ref, gq_ref, gk_ref, pq_ref, pk_ref, pv_ref,
                     q_ref, k_ref, v_ref, carry_ref):
    tl = fq_ref.shape[1]
    lane = lax.broadcasted_iota(jnp.int32, (tl, LANE), 1)

    @pl.when(pl.program_id(1) == 0)
    def _():
        carry_ref[...] = jnp.zeros_like(carry_ref)

    logf = jax.nn.log_sigmoid(krfg_ref[0] + bf_ref[...])
    logf = jnp.where(lane < FOX_HEADS, logf, 0.0)
    r_i = lax.broadcasted_iota(jnp.int32, (tl, tl), 0)
    c_i = lax.broadcasted_iota(jnp.int32, (tl, tl), 1)
    tri = jnp.where(c_i <= r_i, 1.0, 0.0).astype(BF16)
    cum = carry_ref[0:1, :]
    for piece in _split3(logf):
        cum = cum + jnp.dot(tri, piece.astype(BF16), preferred_element_type=F32)
    carry_ref[0:1, :] = cum[tl - 1:tl, :]

    c_hi, c_mid, c_lo = _split3(cum * LOG2E)
    gate_row = (c_hi + pltpu.roll(c_mid, GATE_MID_LANE, 1) + pltpu.roll(c_lo, GATE_LO_LANE, 1)
                + jnp.where(lane == GATE_ONE_LANE, 1.0, 0.0)).astype(BF16)

    p_r = lax.broadcasted_iota(jnp.int32, (LANE, LANE), 0)
    p_c = lax.broadcasted_iota(jnp.int32, (LANE, LANE), 1)
    head_mean = jnp.where(p_r // FOX_HEAD_DIM == p_c // FOX_HEAD_DIM, 1.0 / FOX_HEAD_DIM, 0.0).astype(BF16)

    def mean_sq(ref, j):
        x = ref[0, :, j * LANE:(j + 1) * LANE]
        return jnp.dot((x * x).astype(BF16), head_mean, preferred_element_type=F32)

    def normed(ref, g_ref, j, ms):
        lanes = slice(j * LANE, (j + 1) * LANE)
        return (ref[0, :, lanes] * lax.rsqrt(ms + EPS) * g_ref[:, lanes]).astype(BF16)

    def placed(x, p_ref, j):
        return jnp.dot(jnp.concatenate([x, gate_row], axis=1), p_ref[j], preferred_element_type=F32)

    pairs = range(FOX_HEADS // 2)
    q_ms = [mean_sq(fq_ref, j) for j in pairs]
    k_ms = [mean_sq(fk_ref, j) for j in pairs]
    q_n = [normed(fq_ref, gq_ref, j, q_ms[j]) for j in pairs]
    k_n = [normed(fk_ref, gk_ref, j, k_ms[j]) for j in pairs]
    for j in pairs:
        q = placed(q_n[j], pq_ref, j)
        k = placed(k_n[j], pk_ref, j)
        v = placed(fv_ref[0, :, j * LANE:(j + 1) * LANE].astype(BF16), pv_ref, j)
        for hh in range(2):
            head = slice(hh * HEAD_PAD, (hh + 1) * HEAD_PAD)
            q_ref[0, 2 * j + hh] = q[:, head].astype(BF16)
            _store_key_blocks(k_ref, 2 * j + hh, k[:, head])
            v_ref[0, 2 * j + hh] = v[:, head].astype(BF16)


def _fox_prep_specs(tl):
    row = lambda b, i: (b, i, 0)
    c2 = lambda b, i: (0, 0)
    c3 = lambda b, i: (0, 0, 0)
    head_out = pl.BlockSpec((1, FOX_HEADS, tl, HEAD_PAD), lambda b, i: (b, 0, i, 0))
    head_shape = jax.ShapeDtypeStruct((BATCH, FOX_HEADS, SEQ, HEAD_PAD), BF16)
    place = pl.BlockSpec((FOX_HEADS // 2, 2 * LANE, 2 * HEAD_PAD), c3)
    in_specs = [pl.BlockSpec((1, tl, ATT_WIDTH), row),
                pl.BlockSpec((1, tl, ATT_WIDTH), row),
                pl.BlockSpec((1, tl, ATT_WIDTH), row),
                pl.BlockSpec((1, tl, LANE), row),
                pl.BlockSpec((1, LANE), c2),
                pl.BlockSpec((1, ATT_WIDTH), c2),
                pl.BlockSpec((1, ATT_WIDTH), c2),
                place, place, place]
    return (in_specs, [head_out, _KT_SPEC(FOX_HEADS, tl), head_out],
            [head_shape, _KT_SHAPE(FOX_HEADS), head_shape])


def _front_kernel(*refs, n_mla, n_fox):
    x_ref, g_ref, sh_ref, sc_ref, w_ref = refs[:5]
    mla_rest, fox_rest = refs[5:5 + n_mla], refs[5 + n_mla:5 + n_mla + n_fox]
    u_ref = refs[5 + n_mla + n_fox]
    outs = refs[6 + n_mla + n_fox:12 + n_mla + n_fox]
    proj_ref, carry_ref = refs[-2:]
    _inproj_into(x_ref, g_ref, sh_ref, sc_ref, w_ref, proj_ref, u_ref)
    cols = [proj_ref.at[:, :, c0:c1] for c0, c1 in _IN_GROUPS]
    _mla_prep_kernel(cols[1], cols[2], cols[3], *mla_rest, *outs[:3])
    _fox_prep_kernel(cols[4], cols[5], cols[6], cols[3], *fox_rest, *outs[3:], carry_ref)


def _front_call(x, g, sh, sc, w, layer, mla_rest, fox_rest):
    tl = ROW_TILE
    row = lambda b, i: (b, i, 0)
    per_b = lambda b, i: (b, 0, 0)
    const = lambda b, i: (0, 0)
    mla_in, mla_out, mla_shape = _mla_prep_specs(tl)
    fox_in, fox_out, fox_shape = _fox_prep_specs(tl)
    mla_in, fox_in = mla_in[3:], fox_in[4:]
    outs = pl.pallas_call(
        functools.partial(_front_kernel, n_mla=len(mla_in), n_fox=len(fox_in)),
        grid=(BATCH, SEQ // tl),
        in_specs=[pl.BlockSpec((1, tl, D_MODEL), row),
                  pl.BlockSpec((1, D_MODEL), const),
                  pl.BlockSpec((1, 1, D_MODEL), per_b),
                  pl.BlockSpec((1, 1, D_MODEL), per_b),
                  pl.BlockSpec((1, D_MODEL, IN_PAD), lambda b, i: (layer, 0, 0))] + mla_in + fox_in,
        out_specs=[pl.BlockSpec((tl, SSM_WIDTH), lambda b, i: (i, b))] + mla_out + fox_out,
        out_shape=[jax.ShapeDtypeStruct((SEQ, BATCH * SSM_WIDTH), F32)] + mla_shape + fox_shape,
        scratch_shapes=[pltpu.VMEM((1, tl, IN_PAD), F32), pltpu.VMEM((SUBLANE, LANE), F32)],
        compiler_params=_params(("arbitrary", "arbitrary")),
        name="front",
    )(x, g, sh, sc, w, *mla_rest, *fox_rest)
    return outs[0], outs[1:4], outs[4:]


def _fox_placements():
    pq = np.zeros((FOX_HEADS // 2, 2 * LANE, 2 * HEAD_PAD), np.float32)
    pk = np.zeros_like(pq)
    pv = np.zeros_like(pq)
    one_row = LANE + GATE_ONE_LANE
    for j in range(FOX_HEADS // 2):
        for hh in range(2):
            h, col0 = 2 * j + hh, hh * HEAD_PAD
            for d in range(FOX_HEAD_DIM):
                for p in (pq, pk, pv):
                    p[j, hh * FOX_HEAD_DIM + d, col0 + d] = 1.0
            pv[j, one_row, col0 + ONES_LANE] = 1.0
            for n, piece_lane in enumerate((0, GATE_MID_LANE, GATE_LO_LANE)):
                pq[j, LANE + piece_lane + h, col0 + Q_GATE_LANE + n] = 1.0
                pq[j, one_row, col0 + K_GATE_LANE + n] = 1.0
                pk[j, one_row, col0 + Q_GATE_LANE + n] = 1.0
                pk[j, LANE + piece_lane + h, col0 + K_GATE_LANE + n] = -1.0
    return tuple(jnp.asarray(p, BF16) for p in (pq, pk, pv))


def _fox_operands(bf, gq, gk):
    q_scale = LOG2E / math.sqrt(FOX_HEAD_DIM)
    return (_pad_lanes(bf[None, :], LANE), jnp.tile(gq * q_scale, FOX_HEADS)[None, :],
            jnp.tile(gk, FOX_HEADS)[None, :]) + _fox_placements()


def _flash_kernel(qa_ref, qb_ref, kt_ref, v_ref, gap_ref, o_ref, q_scr, s_ref, m_ref, acc_ref,
                  *, tile, chunk, n_tiles):
    p = pl.program_id(2)
    tiles = (p, n_tiles - 1 - p)
    n_tasks = n_tiles + 1
    half = tile // 2
    top, bottom = slice(0, half), slice(half, tile)
    lane = lax.broadcasted_iota(jnp.int32, (tile, HEAD_PAD), 1)
    q_scr[0] = qa_ref[0]
    q_scr[1] = qb_ref[0]

    def plain_task(t):
        second = t - 2 >= p
        return second, second.astype(jnp.int32), jnp.where(second, t - 2 - p, t - 2)

    def row_max_update(w, hh, rows, s):
        mr = m_ref[w, hh, rows]
        for c in range(s.shape[1] // LANE):
            mr = jnp.maximum(mr, s[:, c * LANE:(c + 1) * LANE])
        m_ref[w, hh, rows] = mr

    m_ref[...] = jnp.full(m_ref.shape, NEG, F32)
    for w in range(2):
        for hh in range(2):
            kt = kt_ref[0, hh, tiles[w]]
            s_top = jnp.dot(q_scr[w, hh, top], kt[:, top], preferred_element_type=F32)
            s_top = jnp.where(gap_ref[top, top] <= 0, s_top, NEG)
            s_ref[hh, w, top, top] = s_top
            row_max_update(w, hh, top, s_top)
            s_bot = jnp.dot(q_scr[w, hh, bottom], kt, preferred_element_type=F32)
            s_bot = jnp.where(gap_ref[bottom, :] <= 0, s_bot, NEG)
            s_ref[hh, w, bottom] = s_bot
            row_max_update(w, hh, bottom, s_bot)
    for t in range(2, n_tasks):
        _, which, j = plain_task(jnp.int32(t))
        for hh in range(2):
            s = jnp.dot(q_scr[which, hh], kt_ref[0, hh, j], preferred_element_type=F32)
            s_ref[hh, t] = s
            row_max_update(which, hh, slice(None), s)

    ms = [[jnp.max(m_ref[w, hh], axis=1, keepdims=True) for hh in range(2)] for w in range(2)]

    acc_ref[...] = jnp.zeros(acc_ref.shape, F32)
    for w in range(2):
        k0 = pl.multiple_of(tiles[w] * tile, tile)
        for hh in range(2):
            pr = jnp.exp2(s_ref[hh, w, top, top] - ms[w][hh][top]).astype(BF16)
            acc_ref[w, hh, top] += jnp.dot(pr, v_ref[0, hh, pl.ds(k0, half), :], preferred_element_type=F32)
            pr = jnp.exp2(s_ref[hh, w, bottom] - ms[w][hh][bottom]).astype(BF16)
            acc_ref[w, hh, bottom] += jnp.dot(pr, v_ref[0, hh, pl.ds(k0, tile), :], preferred_element_type=F32)
    for t in range(2, n_tasks):
        second, which, j = plain_task(jnp.int32(t))
        k0 = pl.multiple_of(j * tile, tile)
        for hh in range(2):
            row_max = jnp.where(second, ms[1][hh], ms[0][hh])
            pr = jnp.exp2(s_ref[hh, t] - row_max).astype(BF16)
            acc_ref[which, hh] += jnp.dot(pr, v_ref[0, hh, pl.ds(k0, tile), :], preferred_element_type=F32)

    for w in range(2):
        outs = [acc_ref[w, hh] / acc_ref[w, hh][:, ONES_LANE:ONES_LANE + 1] for hh in range(2)]
        o_ref[0, w, 0] = jnp.where(lane < 64, outs[0], pltpu.roll(outs[1], 64, 1)).astype(BF16)


def _flash_call(q, kt, v, chunk):
    tile = ATT_TILE
    heads = q.shape[1]
    n_tiles = SEQ // tile
    pos = np.arange(tile, dtype=np.int32) // chunk
    gap = jnp.asarray(pos[None, :] - pos[:, None])
    return pl.pallas_call(
        functools.partial(_flash_kernel, tile=tile, chunk=chunk, n_tiles=n_tiles),
        grid=(BATCH, heads // 2, n_tiles // 2),
        in_specs=[pl.BlockSpec((1, 2, tile, HEAD_PAD), lambda b, hp, p: (b, hp, p, 0)),
                  pl.BlockSpec((1, 2, tile, HEAD_PAD), lambda b, hp, p: (b, hp, n_tiles - 1 - p, 0)),
                  pl.BlockSpec((1, 2, n_tiles, HEAD_PAD, tile), lambda b, hp, p: (b, hp, 0, 0, 0)),
                  pl.BlockSpec((1, 2, SEQ, HEAD_PAD), lambda b, hp, p: (b, hp, 0, 0)),
                  pl.BlockSpec((tile, tile), lambda b, hp, p: (0, 0))],
        out_specs=pl.BlockSpec((1, 2, 1, tile, LANE), lambda b, hp, p: (b, 0, p, 0, hp)),
        out_shape=jax.ShapeDtypeStruct((BATCH, 2, n_tiles // 2, tile, ATT_WIDTH), BF16),
        scratch_shapes=[pltpu.VMEM((2, 2, tile, HEAD_PAD), BF16),
                        pltpu.VMEM((2, n_tiles + 1, tile, tile), F32),
                        pltpu.VMEM((2, 2, tile, LANE), F32),
                        pltpu.VMEM((2, 2, tile, HEAD_PAD), F32)],
        compiler_params=_params(("arbitrary", "arbitrary", "arbitrary")),
        name="flash_chunk%d" % chunk,
    )(q, q, kt, v, gap)


def _mixed(ssm, mla_ref, fox_ref, gm_ref, gf_ref, w_ref):
    def normed(ref, g_ref):
        a = ref[0, 0, 0].astype(F32)
        return (a * lax.rsqrt(jnp.mean(a * a, axis=-1, keepdims=True) + EPS) * g_ref[...]).astype(BF16)

    merged = jnp.concatenate([ssm, normed(mla_ref, gm_ref), normed(fox_ref, gf_ref)], axis=1)
    return jnp.dot(merged, w_ref[...], preferred_element_type=F32)


def _att_tile_spec(tile_of):
    half = SEQ // ATT_TILE // 2

    def index(b, i):
        t = tile_of(i)
        return (b, t // half, jnp.where(t < half, t, 2 * half - 1 - t), 0, 0)
    return pl.BlockSpec((1, 1, 1, ATT_TILE, ATT_WIDTH), index)


def _merge_ffn_kernel(ssm_ref, mla0_ref, mla1_ref, fox0_ref, fox1_ref, x_ref, g1_ref, gm_ref, gf_ref, wo_ref,
                      g_ref, sh_ref, sc_ref, g2_ref, wg_ref, wu_ref, wd_ref, o_ref):
    half = x_ref.shape[1] // 2
    att = ((mla0_ref, fox0_ref), (mla1_ref, fox1_ref))
    for n, r in enumerate((slice(0, half), slice(half, 2 * half))):
        x = x_ref[0, r] + g1_ref[0] * _mixed(ssm_ref[r, :], *att[n], gm_ref, gf_ref, wo_ref)
        h = _rms_mod(x, g_ref[...], sc_ref[0], sh_ref[0]).astype(BF16)
        gate_up = [(jnp.dot(h, wg_ref[0, :, c0:c1], preferred_element_type=F32),
                    jnp.dot(h, wu_ref[0, :, c0:c1], preferred_element_type=F32)) for c0, c1 in FF_PARTS]
        acc = jnp.zeros((half, D_MODEL), F32)
        for (c0, c1), (gate, up) in zip(FF_PARTS, gate_up):
            a = (gate * jax.nn.sigmoid(gate) * up).astype(BF16)
            acc = acc + jnp.dot(a, wd_ref[0, c0:c1, :], preferred_element_type=F32)
        o_ref[0, r] = x + g2_ref[0] * acc


def _merge_ffn_call(o_ssm, o_mla, o_fox, x, g1, gm, gf, wo, g, sh, sc, g2, wg, wu, wd, layer):
    tm = 2 * ATT_TILE
    row = lambda b, i: (b, i, 0)
    per_b = lambda b, i: (b, 0, 0)
    c2 = lambda b, i: (0, 0)
    once = pl.Buffered(1)
    resident = lambda shape: pl.BlockSpec(shape, lambda b, i: (layer, 0, 0), pipeline_mode=once)
    att0, att1 = _att_tile_spec(lambda i: 2 * i), _att_tile_spec(lambda i: 2 * i + 1)
    return pl.pallas_call(
        _merge_ffn_kernel,
        grid=(BATCH, SEQ // tm),
        in_specs=[pl.BlockSpec((tm, SSM_WIDTH), lambda b, i: (i, b)),
                  att0, att1, att0, att1,
                  pl.BlockSpec((1, tm, D_MODEL), row),
                  pl.BlockSpec((1, 1, D_MODEL), per_b),
                  pl.BlockSpec((1, ATT_WIDTH), c2),
                  pl.BlockSpec((1, ATT_WIDTH), c2),
                  pl.BlockSpec((D_MODEL, D_MODEL), c2, pipeline_mode=once),
                  pl.BlockSpec((1, D_MODEL), c2),
                  pl.BlockSpec((1, 1, D_MODEL), per_b),
                  pl.BlockSpec((1, 1, D_MODEL), per_b),
                  pl.BlockSpec((1, 1, D_MODEL), per_b),
                  resident((1, D_MODEL, D_FF)),
                  resident((1, D_MODEL, D_FF)),
                  resident((1, D_FF, D_MODEL))],
        out_specs=pl.BlockSpec((1, tm, D_MODEL), row),
        out_shape=jax.ShapeDtypeStruct((BATCH, SEQ, D_MODEL), F32),
        compiler_params=_params(("arbitrary", "arbitrary")),
        name="merge_ffn",
    )(o_ssm, o_mla, o_mla, o_fox, o_fox, x, g1, gm, gf, wo, g, sh, sc, g2, wg, wu, wd)


def _router_kernel(x_ref, g_ref, sh_ref, sc_ref, w_ref, b_ref, comb_ref, rank_ref, rankt_ref, count_ref):
    tm = x_ref.shape[1]
    h = _rms_mod(x_ref[0], g_ref[...], sc_ref[0], sh_ref[0])
    h_hi = h.astype(BF16)
    h_lo = (h - h_hi.astype(F32)).astype(BF16)
    parts = jnp.dot(jnp.concatenate([h_hi, h_lo], axis=1), w_ref[...], preferred_element_type=F32)
    logits = parts + pltpu.roll(parts, LANE - N_EXPERTS, 1) + b_ref[...]
    lane = lax.broadcasted_iota(jnp.int32, logits.shape, 1)
    logits = jnp.where(lane < N_EXPERTS, logits, -jnp.inf)
    m1 = jnp.max(logits, axis=-1, keepdims=True)
    i1 = jnp.min(jnp.where(logits == m1, lane, LANE), axis=-1, keepdims=True)
    rest = jnp.where(lane == i1, -jnp.inf, logits)
    m2 = jnp.max(rest, axis=-1, keepdims=True)
    i2 = jnp.min(jnp.where(rest == m2, lane, LANE), axis=-1, keepdims=True)
    e = jnp.exp(m2 - m1)
    p1 = 1.0 / (1.0 + e)
    comb_ref[0] = jnp.where(lane == i1, p1, 0.0) + jnp.where(lane == i2, e * p1, 0.0)

    chosen = (lane == i1) | (lane == i2)
    chosen_f = jnp.where(chosen, 1.0, 0.0)
    r_i = lax.broadcasted_iota(jnp.int32, (tm, tm), 0)
    c_i = lax.broadcasted_iota(jnp.int32, (tm, tm), 1)
    earlier = jnp.where(c_i < r_i, 1.0, 0.0).astype(BF16)
    rank = jnp.dot(earlier, chosen_f.astype(BF16), preferred_element_type=F32)
    rank = jnp.where(chosen, rank, -1.0)
    rank_ref[0] = rank
    rankt_ref[0] = rank.T[0:SUBLANE, :]
    count_ref[0] = jnp.sum(chosen_f, axis=0, keepdims=True)


def _merge_router_kernel(ssm_ref, mla0_ref, mla1_ref, fox0_ref, fox1_ref, x_ref, g1_ref, gm_ref, gf_ref, wo_ref,
                         g_ref, sh_ref, sc_ref, w_ref, b_ref, xo_ref, comb_ref, rank_ref, rankt_ref, count_ref):
    half = x_ref.shape[1] // 2
    att = ((mla0_ref, fox0_ref), (mla1_ref, fox1_ref))
    for n, r in enumerate((slice(0, half), slice(half, 2 * half))):
        xo_ref[0, r] = x_ref[0, r] + g1_ref[0] * _mixed(ssm_ref[r, :], *att[n], gm_ref, gf_ref, wo_ref)
    _router_kernel(xo_ref, g_ref, sh_ref, sc_ref, w_ref, b_ref, comb_ref, rank_ref, rankt_ref, count_ref)


def _merge_router_call(o_ssm, o_mla, o_fox, x, g1, gm, gf, wo, g, sh, sc, w, b):
    tm = MOE_TILE
    tiles = SEQ // tm
    row = lambda b_, i: (b_, i, 0)
    per_b = lambda b_, i: (b_, 0, 0)
    per_tile = lambda b_, i: (b_ * tiles + i, 0, 0)
    c2 = lambda b_, i: (0, 0)
    att0, att1 = _att_tile_spec(lambda i: 2 * i), _att_tile_spec(lambda i: 2 * i + 1)
    return pl.pallas_call(
        _merge_router_kernel,
        grid=(BATCH, tiles),
        in_specs=[pl.BlockSpec((tm, SSM_WIDTH), lambda b_, i: (i, b_)),
                  att0, att1, att0, att1,
                  pl.BlockSpec((1, tm, D_MODEL), row),
                  pl.BlockSpec((1, 1, D_MODEL), per_b),
                  pl.BlockSpec((1, ATT_WIDTH), c2),
                  pl.BlockSpec((1, ATT_WIDTH), c2),
                  pl.BlockSpec((D_MODEL, D_MODEL), c2),
                  pl.BlockSpec((1, D_MODEL), c2),
                  pl.BlockSpec((1, 1, D_MODEL), per_b),
                  pl.BlockSpec((1, 1, D_MODEL), per_b),
                  pl.BlockSpec((2 * D_MODEL, LANE), c2),
                  pl.BlockSpec((1, LANE), c2)],
        out_specs=[pl.BlockSpec((1, tm, D_MODEL), row),
                   pl.BlockSpec((1, tm, LANE), row),
                   pl.BlockSpec((1, tm, LANE), row),
                   pl.BlockSpec((1, SUBLANE, tm), per_tile),
                   pl.BlockSpec((1, 1, LANE), per_tile)],
        out_shape=[jax.ShapeDtypeStruct((BATCH, SEQ, D_MODEL), F32),
                   jax.ShapeDtypeStruct((BATCH, SEQ, LANE), F32),
                   jax.ShapeDtypeStruct((BATCH, SEQ, LANE), F32),
                   jax.ShapeDtypeStruct((BATCH * tiles, SUBLANE, tm), F32),
                   jax.ShapeDtypeStruct((BATCH * tiles, 1, LANE), F32)],
        compiler_params=_params(("arbitrary", "arbitrary")),
        name="merge_router",
    )(o_ssm, o_mla, o_mla, o_fox, o_fox, x, g1, gm, gf, wo, g, sh, sc, w, b)


def _moe_kernel(count_ref, x_ref, g_ref, sh_ref, sc_ref, g2_ref, comb_ref, rank_ref, rankt_ref,
                wg_ref, wu_ref, wt_ref, wd_ref, o_ref, h_ref):
    tm = x_ref.shape[1]
    e = pl.program_id(1)

    @pl.when(e == 0)
    def _():
        x = x_ref[0]
        h_ref[...] = _rms_mod(x, g_ref[...], sc_ref[0], sh_ref[0]).astype(BF16)
        o_ref[0] = x

    lane = lax.broadcasted_iota(jnp.int32, (tm, LANE), 1)
    mine = lane == e
    rank_col = jnp.sum(jnp.where(mine, rank_ref[0], 0.0), axis=-1, keepdims=True)
    gate_col = jnp.sum(jnp.where(mine, comb_ref[0], 0.0), axis=-1, keepdims=True)
    rank_row = rankt_ref[0, pl.ds(e, 1), :]
    count = count_ref[pl.program_id(0) * N_EXPERTS + e]

    def expert_pass(first, n_rows):
        base = first.astype(F32)
        slot_sub = lax.broadcasted_iota(jnp.int32, (n_rows, tm), 0).astype(F32)
        slot_lane = lax.broadcasted_iota(jnp.int32, (tm, n_rows), 1).astype(F32)
        pick = jnp.where(rank_row - base == slot_sub, 1.0, 0.0).astype(BF16)
        rows = jnp.dot(pick, h_ref[...], preferred_element_type=F32).astype(BF16)
        gate = jnp.dot(rows, wg_ref[0, 0, :, :EXPERT_MAIN], preferred_element_type=F32)
        up = jnp.dot(rows, wu_ref[0, 0, :, :EXPERT_MAIN], preferred_element_type=F32)
        tail = jnp.dot(rows, wt_ref[0, 0], preferred_element_type=F32)
        gate_t, up_t = tail[:, :LANE], tail[:, LANE:]
        a = jnp.concatenate([gate * jax.nn.sigmoid(gate) * up, gate_t * jax.nn.sigmoid(gate_t) * up_t],
                            axis=1).astype(BF16)
        y = jnp.dot(a, wd_ref[0, 0], preferred_element_type=F32).astype(BF16)
        place = jnp.where(rank_col - base == slot_lane, 1.0, 0.0).astype(BF16)
        back = jnp.dot(place, y, preferred_element_type=F32)
        o_ref[0] += g2_ref[0] * (gate_col * back)

    def full_pass(sb, carry):
        expert_pass(sb * MOE_ROWS, MOE_ROWS)
        return carry

    n_full = count // MOE_ROWS
    lax.fori_loop(0, n_full, full_pass, 0)
    left = count - n_full * MOE_ROWS

    @pl.when(left > MOE_ROWS // 2)
    def _():
        expert_pass(n_full * MOE_ROWS, MOE_ROWS)

    @pl.when((left > 0) & (left <= MOE_ROWS // 2))
    def _():
        expert_pass(n_full * MOE_ROWS, MOE_ROWS // 2)


def _moe_call(x, g, sh, sc, g2, comb, rank, rankt, counts, wg, wu, wt, wd, layer):
    tm = MOE_TILE
    tiles = SEQ // tm
    n_tiles = BATCH * tiles
    row = lambda i, e, cnt: (i, 0, 0)
    per_b = lambda i, e, cnt: (i // tiles, 0, 0)
    expert = lambda i, e, cnt: (layer, e, 0, 0)
    as_tiles = lambda a: a.reshape(n_tiles, tm, a.shape[-1])
    grid_spec = pltpu.PrefetchScalarGridSpec(
        num_scalar_prefetch=1,
        grid=(n_tiles, N_EXPERTS),
        in_specs=[pl.BlockSpec((1, tm, D_MODEL), row),
                  pl.BlockSpec((1, D_MODEL), lambda i, e, cnt: (0, 0)),
                  pl.BlockSpec((1, 1, D_MODEL), per_b),
                  pl.BlockSpec((1, 1, D_MODEL), per_b),
                  pl.BlockSpec((1, 1, D_MODEL), per_b),
                  pl.BlockSpec((1, tm, LANE), row),
                  pl.BlockSpec((1, tm, LANE), row),
                  pl.BlockSpec((1, SUBLANE, tm), row),
                  pl.BlockSpec((1, 1, D_MODEL, D_FF_EXPERT), expert),
                  pl.BlockSpec((1, 1, D_MODEL, D_FF_EXPERT), expert),
                  pl.BlockSpec((1, 1, D_MODEL, 2 * LANE), expert),
                  pl.BlockSpec((1, 1, D_FF_EXPERT, D_MODEL), expert)],
        out_specs=pl.BlockSpec((1, tm, D_MODEL), row),
        scratch_shapes=[pltpu.VMEM((tm, D_MODEL), BF16)],
    )
    out = pl.pallas_call(
        _moe_kernel,
        grid_spec=grid_spec,
        out_shape=jax.ShapeDtypeStruct((n_tiles, tm, D_MODEL), F32),
        compiler_params=_params(("arbitrary", "arbitrary")),
        name="moe_experts",
    )(counts, as_tiles(x), g, sh, sc, g2, as_tiles(comb), as_tiles(rank), rankt, wg, wu, wt, wd)
    return out.reshape(BATCH, SEQ, D_MODEL)


def kernel(x, c, positions, norm_mix, norm_ffn, w_ada, b_ada, w_in, ssm_lam_re, ssm_lam_im, ssm_log_dt, ssm_b_re, ssm_b_im, ssm_c_re, ssm_c_im, ssm_d, ssm_w_glu, ssm_b_glu, mla_q_norm, mla_kv_norm, mla_w_uq, mla_w_ukv, mla_qk_gq, mla_qk_gk, fox_b_f, fox_qk_gq, fox_qk_gk, out_norm, w_out, ffn_w_gate, ffn_w_up, ffn_w_down, moe_w_router, moe_b_router, moe_w_gate, moe_w_up, moe_w_down):
    tabs = _rope_tables(positions)
    w_in_packed = _pack_w_in(w_in)
    moe_wg, moe_wu, moe_wd = (w.astype(BF16) for w in (moe_w_gate, moe_w_up, moe_w_down))
    moe_wt = jnp.concatenate([moe_wg[..., EXPERT_MAIN:], moe_wu[..., EXPERT_MAIN:]], axis=-1)
    ffn_wg, ffn_wu, ffn_wd = (w.astype(BF16) for w in (ffn_w_gate, ffn_w_up, ffn_w_down))
    ada = _ada_call(c, w_ada, b_ada)
    ada = ada.reshape(DEPTH, BATCH, 6, 1, D_MODEL)
    row2 = lambda a: a[None, :]

    for i in range(DEPTH):
        sh1, sc1, g1, sh2, sc2, g2 = (ada[i, :, n] for n in range(6))

        u, mla_qkv, fox_qkv = _front_call(
            x, row2(norm_mix[i]), sh1, sc1, w_in_packed, i,
            (*tabs, row2(mla_q_norm[i]), row2(mla_kv_norm[i]),
             *_mla_weights(mla_w_uq[i], mla_w_ukv[i], mla_qk_gq[i], mla_qk_gk[i])),
            _fox_operands(fox_b_f[i], fox_qk_gq[i], fox_qk_gk[i]))

        bmat, lam, cmat = _s5_operands(ssm_lam_re[i], ssm_lam_im[i], ssm_log_dt[i],
                                       ssm_b_re[i], ssm_b_im[i], ssm_c_re[i], ssm_c_im[i])
        o_ssm = _s5_call(u, bmat, lam, cmat, row2(ssm_d[i]), ssm_w_glu[i].astype(BF16),
                         row2(ssm_b_glu[i]), row2(out_norm[i, :SSM_WIDTH]))

        o_mla = _flash_call(*mla_qkv, CHUNK)
        o_fox = _flash_call(*fox_qkv, 1)

        e1, e2 = SSM_WIDTH, SSM_WIDTH + ATT_WIDTH
        merge_args = (o_ssm, o_mla, o_fox, x, g1, row2(out_norm[i, e1:e2]), row2(out_norm[i, e2:]),
                      w_out[i].astype(BF16))

        j = i // 2
        if i % 2 == 0:
            x = _merge_ffn_call(*merge_args, row2(norm_ffn[i]), sh2, sc2, g2, ffn_wg, ffn_wu, ffn_wd, layer=j)
        else:
            wr_hi = moe_w_router[j].astype(BF16)
            wr_lo = (moe_w_router[j] - wr_hi.astype(F32)).astype(BF16)
            wr = _pad_lanes(jnp.concatenate([wr_hi, wr_lo], axis=1), LANE)
            x, comb, rank, rankt, counts = _merge_router_call(
                *merge_args, row2(norm_ffn[i]), sh2, sc2, jnp.concatenate([wr, wr], axis=0),
                _pad_lanes(row2(moe_b_router[j]), LANE))
            counts = counts[:, 0, :N_EXPERTS].astype(jnp.int32).reshape(-1)
            x = _moe_call(x, row2(norm_ffn[i]), sh2, sc2, g2, comb, rank, rankt, counts,
                          moe_wg, moe_wu, moe_wt, moe_wd, layer=j)
    return x
```

```python
import functools
import math

import jax
import jax.numpy as jnp
import numpy as np
from jax import lax
from jax.experimental import pallas as pl
from jax.experimental.pallas import tpu as pltpu

F32 = jnp.float32
BF16 = jnp.bfloat16

D_MODEL = 1024
BATCH = 8
SEQ = 4096
DEPTH = 4
CHUNK = 64
EPS = 1e-6

SSM_WIDTH = 256
SSM_GROUP = 16
N_SSM_GROUPS = 16
SSM_STATE = 64
N_STATE = N_SSM_GROUPS * SSM_STATE

MLA_HEADS = 6
MLA_Q_RANK = 256
MLA_KV_RANK = 128
MLA_NOPE = 64
MLA_ROPE = 32
MLA_V = 64
MLA_QK = 96
ROPE_BASE = 10000.0

FOX_HEADS = 6
FOX_HEAD_DIM = 64
ATT_WIDTH = 384

D_FF = 2816
N_EXPERTS = 8
D_FF_EXPERT = 1408
EXPERT_MAIN = 1280

LANE = 128
SUBLANE = 8
HEAD_PAD = LANE
ONES_LANE = 64
NEG = -1e30

IN_PAD = 1920
KR_LANE = 64

ROW_TILE = 512
S5_STEPS = 64
ATT_TILE = 512
LOG2E = math.log2(math.e)
FF_PARTS = ((0, 1536), (1536, D_FF))
MOE_TILE = 1024
MOE_ROWS = 256
VMEM_LIMIT = 56 * 1024 * 1024


def _params(sem):
    return pltpu.CompilerParams(dimension_semantics=sem, vmem_limit_bytes=VMEM_LIMIT)


def _rms_mod(x, g, sc, sh):
    ms = jnp.mean(x * x, axis=-1, keepdims=True)
    h = x * lax.rsqrt(ms + EPS) * g
    return h * (1.0 + sc) + sh


def _split3(x):
    hi = x.astype(BF16).astype(F32)
    r = x - hi
    mid = r.astype(BF16).astype(F32)
    lo = (r - mid).astype(BF16).astype(F32)
    return hi, mid, lo


def _ada_kernel(c_ref, w_ref, b_ref, o_ref):
    c = c_ref[...]
    ca = (c * jax.nn.sigmoid(c)).astype(BF16)
    o_ref[0] = jnp.dot(ca, w_ref[0].astype(BF16), preferred_element_type=F32) + b_ref[0]


def _ada_call(c, w_ada, b_ada):
    tn = 1536
    return pl.pallas_call(
        _ada_kernel,
        grid=(DEPTH, 6 * D_MODEL // tn),
        in_specs=[pl.BlockSpec((BATCH, D_MODEL), lambda i, j: (0, 0)),
                  pl.BlockSpec((1, D_MODEL, tn), lambda i, j: (i, 0, j)),
                  pl.BlockSpec((1, 1, tn), lambda i, j: (i, 0, j))],
        out_specs=pl.BlockSpec((1, BATCH, tn), lambda i, j: (i, 0, j)),
        out_shape=jax.ShapeDtypeStruct((DEPTH, BATCH, 6 * D_MODEL), F32),
        compiler_params=_params(("arbitrary", "arbitrary")),
        name="ada",
    )(c, w_ada, b_ada.reshape(DEPTH, 1, 6 * D_MODEL))


_IN_GROUPS = ((0, 256), (256, 512), (512, 640), (640, 768), (768, 1152), (1152, 1536), (1536, 1920))


def _inproj_into(x_ref, g_ref, sh_ref, sc_ref, w_ref, proj_ref, u_ref):
    parts = 4
    step = x_ref.shape[1] // parts
    rows = [slice(r * step, (r + 1) * step) for r in range(parts)]
    normed = lambda r: _rms_mod(x_ref[0, rows[r]], g_ref[...], sc_ref[0], sh_ref[0]).astype(BF16)
    h_next = normed(0)
    for r in range(parts):
        h = h_next
        if r + 1 < parts:
            h_next = normed(r + 1)
        proj = jnp.dot(h, w_ref[0], preferred_element_type=F32)
        proj_ref[0, rows[r]] = proj
        u_ref[rows[r]] = proj[:, _IN_GROUPS[0][0]:_IN_GROUPS[0][1]]


def _pack_w_in(w):
    u, cq, ckv, kr, fq, fk, fv, fg = jnp.split(
        w.astype(BF16), (256, 512, 640, 672, 1056, 1440, 1824), axis=2)
    z = lambda n: jnp.zeros(w.shape[:2] + (n,), BF16)
    krfg = jnp.concatenate([fg, z(KR_LANE - FOX_HEADS), kr, z(LANE - KR_LANE - MLA_ROPE)], axis=2)
    return jnp.concatenate([u, cq, ckv, krfg, fq, fk, fv], axis=2)


def _s5_kernel(u2_ref, bmat_ref, lam_ref, cmat_ref, d_ref, wglu_ref, bglu_ref, gn_ref,
               o2_ref, u_ref, o_ref, bu0_ref, bu1_ref, state_ref, *, steps):
    rows = steps * BATCH
    lane_tiles = SSM_WIDTH // LANE
    for b in range(BATCH):
        for c in range(lane_tiles):
            lanes = slice(b * SSM_WIDTH + c * LANE, b * SSM_WIDTH + (c + 1) * LANE)
            u_ref.at[c][pl.ds(b, 2 * steps, stride=BATCH), :] = u2_ref[:, lanes]
    u_rows = lambda rs: jnp.concatenate([u_ref[c, rs, :] for c in range(lane_tiles)], axis=1)

    @pl.when(pl.program_id(0) == 0)
    def _():
        state_ref[...] = jnp.zeros_like(state_ref)

    halves = ((bu0_ref, slice(0, rows)), (bu1_ref, slice(rows, 2 * rows)))
    for bu_ref, rs in halves:
        bu_ref[...] = jnp.dot(u_rows(rs).astype(BF16), bmat_ref[...], preferred_element_type=F32)
    lr = jnp.broadcast_to(lam_ref[0:1, :], (SUBLANE, N_STATE))
    li = jnp.broadcast_to(lam_ref[1:2, :], (SUBLANE, N_STATE))
    sr, si = state_ref[:, 0:N_STATE], state_ref[:, N_STATE:2 * N_STATE]

    for bu_ref, rs in halves:
        for t in range(steps):
            r = slice(t * SUBLANE, (t + 1) * SUBLANE)
            nr = lr * sr - li * si + bu_ref[r, 0:N_STATE]
            ni = lr * si + li * sr + bu_ref[r, N_STATE:2 * N_STATE]
            bu_ref[r, 0:N_STATE] = nr
            bu_ref[r, N_STATE:2 * N_STATE] = ni
            sr, si = nr, ni
        y = jnp.dot(bu_ref[...].astype(BF16), cmat_ref[...], preferred_element_type=F32)
        y = jax.nn.gelu(y + d_ref[...] * u_rows(rs))
        gate = jnp.dot(y.astype(BF16), wglu_ref[...], preferred_element_type=F32) + bglu_ref[...]
        o = y * jax.nn.sigmoid(gate)
        ms = jnp.mean(o * o, axis=-1, keepdims=True)
        o = o * lax.rsqrt(ms + EPS) * gn_ref[...]
        for c in range(lane_tiles):
            o_ref[c, rs, :] = o[:, c * LANE:(c + 1) * LANE]

    state_ref[:, 0:N_STATE] = sr
    state_ref[:, N_STATE:2 * N_STATE] = si
    for b in range(BATCH):
        for c in range(lane_tiles):
            lanes = slice(b * SSM_WIDTH + c * LANE, b * SSM_WIDTH + (c + 1) * LANE)
            o2_ref[:, lanes] = o_ref.at[c][pl.ds(b, 2 * steps, stride=BATCH), :].astype(BF16)


def _s5_call(u2, bmat, lam, cmat, d_skip, wglu, bglu, gn):
    rows = S5_STEPS * BATCH
    const = lambda i: (0, 0)
    return pl.pallas_call(
        functools.partial(_s5_kernel, steps=S5_STEPS),
        grid=(SEQ // (2 * S5_STEPS),),
        in_specs=[pl.BlockSpec((2 * S5_STEPS, BATCH * SSM_WIDTH), lambda i: (i, 0)),
                  pl.BlockSpec((SSM_WIDTH, 2 * N_STATE), const),
                  pl.BlockSpec((2, N_STATE), const),
                  pl.BlockSpec((2 * N_STATE, SSM_WIDTH), const),
                  pl.BlockSpec((1, SSM_WIDTH), const),
                  pl.BlockSpec((SSM_WIDTH, SSM_WIDTH), const),
                  pl.BlockSpec((1, SSM_WIDTH), const),
                  pl.BlockSpec((1, SSM_WIDTH), const)],
        out_specs=pl.BlockSpec((2 * S5_STEPS, BATCH * SSM_WIDTH), lambda i: (i, 0)),
        out_shape=jax.ShapeDtypeStruct((SEQ, BATCH * SSM_WIDTH), BF16),
        scratch_shapes=[pltpu.VMEM((SSM_WIDTH // LANE, 2 * rows, LANE), F32),
                        pltpu.VMEM((SSM_WIDTH // LANE, 2 * rows, LANE), F32),
                        pltpu.VMEM((rows, 2 * N_STATE), F32),
                        pltpu.VMEM((rows, 2 * N_STATE), F32),
                        pltpu.VMEM((SUBLANE, 2 * N_STATE), F32)],
        compiler_params=_params(("arbitrary",)),
        name="s5",
    )(u2, bmat, lam, cmat, d_skip, wglu, bglu, gn)


def _s5_operands(lam_re, lam_im, log_dt, b_re, b_im, c_re, c_im):
    dt = jnp.exp(log_dt)[:, None]
    mag = jnp.exp(lam_re * dt)
    lb_re = mag * jnp.cos(lam_im * dt)
    lb_im = mag * jnp.sin(lam_im * dt)
    den = lam_re * lam_re + lam_im * lam_im
    co_re = ((lb_re - 1.0) * lam_re + lb_im * lam_im) / den
    co_im = (lb_im * lam_re - (lb_re - 1.0) * lam_im) / den
    bb_re = co_re[..., None] * b_re - co_im[..., None] * b_im
    bb_im = co_re[..., None] * b_im + co_im[..., None] * b_re
    eye = jnp.eye(N_SSM_GROUPS, dtype=F32)
    blk_b = lambda m: jnp.einsum("gpc,gh->gchp", m, eye).reshape(SSM_WIDTH, N_STATE)
    bmat = jnp.concatenate([blk_b(bb_re), blk_b(bb_im)], axis=1).astype(BF16)
    blk_c = lambda m: jnp.einsum("gcp,gh->gphc", m, eye).reshape(N_STATE, SSM_WIDTH)
    cmat = jnp.concatenate([blk_c(c_re), -blk_c(c_im)], axis=0).astype(BF16)
    lam = jnp.stack([lb_re.reshape(N_STATE), lb_im.reshape(N_STATE)], axis=0)
    return bmat, lam, cmat


def _rope_tables(positions):
    half = MLA_ROPE // 2
    inv = ROPE_BASE ** (-jnp.arange(half, dtype=F32) / half)
    ang = inv[:, None] * positions.astype(F32).reshape(1, -1)
    shp = positions.shape
    cos, sin = (lax.optimization_barrier(f(ang)).T.reshape(shp + (half,)) for f in (jnp.cos, jnp.sin))
    one = lambda n: jnp.ones(shp + (n,), F32)
    zero = lambda n: jnp.zeros(shp + (n,), F32)
    cos_t = jnp.concatenate([one(MLA_NOPE), cos, cos, zero(LANE - MLA_QK)], axis=-1)
    sin_t = jnp.concatenate([zero(MLA_NOPE), -sin, sin, zero(LANE - MLA_QK)], axis=-1)
    return cos_t, sin_t


def _swap_rope_halves(a):
    half = MLA_ROPE // 2
    lo, hi = a[..., MLA_NOPE:MLA_NOPE + half], a[..., MLA_NOPE + half:MLA_QK]
    return jnp.concatenate([jnp.zeros_like(a[..., :MLA_NOPE]), hi, lo, jnp.zeros_like(a[..., MLA_QK:])], axis=-1)


def _store_key_blocks(kt_ref, h, k):
    kt = k.T
    for s in range(k.shape[0] // ATT_TILE):
        kt_ref[0, h, s] = kt[:, s * ATT_TILE:(s + 1) * ATT_TILE].astype(BF16)


_KT_SPEC = lambda heads, tl: pl.BlockSpec((1, heads, tl // ATT_TILE, HEAD_PAD, ATT_TILE),
                                          lambda b, i: (b, 0, i, 0, 0))
_KT_SHAPE = lambda heads: jax.ShapeDtypeStruct((BATCH, heads, SEQ // ATT_TILE, HEAD_PAD, ATT_TILE), BF16)


def _mla_prep_kernel(cq_ref, ckv_ref, krfg_ref, cos_ref, sin_ref, qn_ref, kvn_ref, wq_ref, wk_ref, wv_ref,
                     gq_ref, gqs_ref, gk_ref, gks_ref, q_ref, k_ref, v_ref):
    tl = cq_ref.shape[1]
    lane = lax.broadcasted_iota(jnp.int32, (tl, LANE), 1)
    cos, sin = cos_ref[0], sin_ref[0]
    q_scale = LOG2E / math.sqrt(MLA_QK)
    q_cos, q_sin = gq_ref[...] * cos * q_scale, gqs_ref[...] * sin * q_scale
    k_cos, k_sin = gk_ref[...] * cos, gks_ref[...] * sin
    ones = jnp.ones((LANE, LANE), BF16)

    def inv_rms(x):
        ss = jnp.dot((x * x).astype(BF16), ones, preferred_element_type=F32)
        return lax.rsqrt(ss / MLA_QK + EPS)

    cq = cq_ref[0]
    cqn = (cq * lax.rsqrt(jnp.mean(cq * cq, axis=-1, keepdims=True) + EPS) * qn_ref[...]).astype(BF16)
    ckv = ckv_ref[0]
    ckvn = (ckv * lax.rsqrt(jnp.mean(ckv * ckv, axis=-1, keepdims=True) + EPS) * kvn_ref[...]).astype(BF16)
    kr = jnp.where((lane >= KR_LANE) & (lane < KR_LANE + MLA_ROPE), krfg_ref[0], 0.0)
    kr_swapped = jnp.where(lane < KR_LANE + MLA_ROPE // 2, pltpu.roll(kr, LANE - 16, 1), pltpu.roll(kr, 16, 1))
    k_rotary = kr_swapped * k_sin

    heads = range(MLA_HEADS)
    qqs = [jnp.dot(cqn, wq_ref[h], preferred_element_type=F32) for h in heads]
    ks = [jnp.dot(ckvn, wk_ref[h], preferred_element_type=F32) + kr for h in heads]
    q_inv = [inv_rms(qq[:, :LANE]) for qq in qqs]
    k_inv = [inv_rms(k) for k in ks]
    for h in heads:
        q, q_swapped = qqs[h][:, :LANE], qqs[h][:, LANE:]
        q_ref[0, h] = (q_inv[h] * (q * q_cos + q_swapped * q_sin)).astype(BF16)
        _store_key_blocks(k_ref, h, k_inv[h] * (ks[h] * k_cos + k_rotary))
        v = jnp.dot(ckvn, wv_ref[h], preferred_element_type=F32)
        v_ref[0, h] = jnp.where(lane == ONES_LANE, 1.0, v).astype(BF16)


def _mla_prep_specs(tl):
    row = lambda b, i: (b, i, 0)
    c2 = lambda b, i: (0, 0)
    c3 = lambda b, i: (0, 0, 0)
    head_out = pl.BlockSpec((1, MLA_HEADS, tl, HEAD_PAD), lambda b, i: (b, 0, i, 0))
    head_shape = jax.ShapeDtypeStruct((BATCH, MLA_HEADS, SEQ, HEAD_PAD), BF16)
    gain = pl.BlockSpec((1, HEAD_PAD), c2)
    in_specs = [pl.BlockSpec((1, tl, MLA_Q_RANK), row),
                pl.BlockSpec((1, tl, MLA_KV_RANK), row),
                pl.BlockSpec((1, tl, LANE), row),
                pl.BlockSpec((1, tl, LANE), row),
                pl.BlockSpec((1, tl, LANE), row),
                pl.BlockSpec((1, MLA_Q_RANK), c2),
                pl.BlockSpec((1, MLA_KV_RANK), c2),
                pl.BlockSpec((MLA_HEADS, MLA_Q_RANK, 2 * HEAD_PAD), c3),
                pl.BlockSpec((MLA_HEADS, MLA_KV_RANK, HEAD_PAD), c3),
                pl.BlockSpec((MLA_HEADS, MLA_KV_RANK, HEAD_PAD), c3),
                gain, gain, gain, gain]
    return (in_specs, [head_out, _KT_SPEC(MLA_HEADS, tl), head_out],
            [head_shape, _KT_SHAPE(MLA_HEADS), head_shape])


def _pad_lanes(a, n=HEAD_PAD):
    return jnp.pad(a, [(0, 0)] * (a.ndim - 1) + [(0, n - a.shape[-1])])


def _mla_weights(w_uq, w_ukv, gq, gk):
    wq = _pad_lanes(w_uq.reshape(MLA_Q_RANK, MLA_HEADS, MLA_QK).transpose(1, 0, 2))
    wq = jnp.concatenate([wq, _swap_rope_halves(wq)], axis=-1).astype(BF16)
    wkv = w_ukv.reshape(MLA_KV_RANK, MLA_HEADS, MLA_NOPE + MLA_V).transpose(1, 0, 2)
    wk = _pad_lanes(wkv[..., :MLA_NOPE]).astype(BF16)
    wv = _pad_lanes(wkv[..., MLA_NOPE:]).astype(BF16)
    gq, gk = _pad_lanes(gq[None, :]), _pad_lanes(gk[None, :])
    return wq, wk, wv, gq, _swap_rope_halves(gq), gk, _swap_rope_halves(gk)


GATE_MID_LANE = 8
GATE_LO_LANE = 16
GATE_ONE_LANE = LANE - 1
Q_GATE_LANE = FOX_HEAD_DIM
K_GATE_LANE = FOX_HEAD_DIM + 3


def _fox_prep_kernel(fq_ref, fk_ref, fv_ref, krfg_ref, bf_ref, gq_ref, gk_ref, pq_ref, pk_ref, pv_ref,
                     q_ref, k_ref, v_ref, carry_ref):
    tl = fq_ref.shape[1]
    lane = lax.broadcasted_iota(jnp.int32, (tl, LANE), 1)

    @pl.when(pl.program_id(1) == 0)
    def _():
        carry_ref[...] = jnp.zeros_like(carry_ref)

    logf = jax.nn.log_sigmoid(krfg_ref[0] + bf_ref[...])
    logf = jnp.where(lane < FOX_HEADS, logf, 0.0)
    r_i = lax.broadcasted_iota(jnp.int32, (tl, tl), 0)
    c_i = lax.broadcasted_iota(jnp.int32, (tl, tl), 1)
    tri = jnp.where(c_i <= r_i, 1.0, 0.0).astype(BF16)
    cum = carry_ref[0:1, :]
    for piece in _split3(logf):
        cum = cum + jnp.dot(tri, piece.astype(BF16), preferred_element_type=F32)
    carry_ref[0:1, :] = cum[tl - 1:tl, :]

    c_hi, c_mid, c_lo = _split3(cum * LOG2E)
    gate_row = (c_hi + pltpu.roll(c_mid, GATE_MID_LANE, 1) + pltpu.roll(c_lo, GATE_LO_LANE, 1)
                + jnp.where(lane == GATE_ONE_LANE, 1.0, 0.0)).astype(BF16)

    p_r = lax.broadcasted_iota(jnp.int32, (LANE, LANE), 0)
    p_c = lax.broadcasted_iota(jnp.int32, (LANE, LANE), 1)
    head_mean = jnp.where(p_r // FOX_HEAD_DIM == p_c // FOX_HEAD_DIM, 1.0 / FOX_HEAD_DIM, 0.0).astype(BF16)

    def mean_sq(ref, j):
        x = ref[0, :, j * LANE:(j + 1) * LANE]
        return jnp.dot((x * x).astype(BF16), head_mean, preferred_element_type=F32)

    def normed(ref, g_ref, j, ms):
        lanes = slice(j * LANE, (j + 1) * LANE)
        return (ref[0, :, lanes] * lax.rsqrt(ms + EPS) * g_ref[:, lanes]).astype(BF16)

    def placed(x, p_ref, j):
        return jnp.dot(jnp.concatenate([x, gate_row], axis=1), p_ref[j], preferred_element_type=F32)

    pairs = range(FOX_HEADS // 2)
    q_ms = [mean_sq(fq_ref, j) for j in pairs]
    k_ms = [mean_sq(fk_ref, j) for j in pairs]
    q_n = [normed(fq_ref, gq_ref, j, q_ms[j]) for j in pairs]
    k_n = [normed(fk_ref, gk_ref, j, k_ms[j]) for j in pairs]
    for j in pairs:
        q = placed(q_n[j], pq_ref, j)
        k = placed(k_n[j], pk_ref, j)
        v = placed(fv_ref[0, :, j * LANE:(j + 1) * LANE].astype(BF16), pv_ref, j)
        for hh in range(2):
            head = slice(hh * HEAD_PAD, (hh + 1) * HEAD_PAD)
            q_ref[0, 2 * j + hh] = q[:, head].astype(BF16)
            _store_key_blocks(k_ref, 2 * j + hh, k[:, head])
            v_ref[0, 2 * j + hh] = v[:, head].astype(BF16)


def _fox_prep_specs(tl):
    row = lambda b, i: (b, i, 0)
    c2 = lambda b, i: (0, 0)
    c3 = lambda b, i: (0, 0, 0)
    head_out = pl.BlockSpec((1, FOX_HEADS, tl, HEAD_PAD), lambda b, i: (b, 0, i, 0))
    head_shape = jax.ShapeDtypeStruct((BATCH, FOX_HEADS, SEQ, HEAD_PAD), BF16)
    place = pl.BlockSpec((FOX_HEADS // 2, 2 * LANE, 2 * HEAD_PAD), c3)
    in_specs = [pl.BlockSpec((1, tl, ATT_WIDTH), row),
                pl.BlockSpec((1, tl, ATT_WIDTH), row),
                pl.BlockSpec((1, tl, ATT_WIDTH), row),
                pl.BlockSpec((1, tl, LANE), row),
                pl.BlockSpec((1, LANE), c2),
                pl.BlockSpec((1, ATT_WIDTH), c2),
                pl.BlockSpec((1, ATT_WIDTH), c2),
                place, place, place]
    return (in_specs, [head_out, _KT_SPEC(FOX_HEADS, tl), head_out],
            [head_shape, _KT_SHAPE(FOX_HEADS), head_shape])


def _front_kernel(*refs, n_mla, n_fox):
    x_ref, g_ref, sh_ref, sc_ref, w_ref = refs[:5]
    mla_rest, fox_rest = refs[5:5 + n_mla], refs[5 + n_mla:5 + n_mla + n_fox]
    u_ref = refs[5 + n_mla + n_fox]
    outs = refs[6 + n_mla + n_fox:12 + n_mla + n_fox]
    proj_ref, carry_ref = refs[-2:]
    _inproj_into(x_ref, g_ref, sh_ref, sc_ref, w_ref, proj_ref, u_ref)
    cols = [proj_ref.at[:, :, c0:c1] for c0, c1 in _IN_GROUPS]
    _mla_prep_kernel(cols[1], cols[2], cols[3], *mla_rest, *outs[:3])
    _fox_prep_kernel(cols[4], cols[5], cols[6], cols[3], *fox_rest, *outs[3:], carry_ref)


def _front_call(x, g, sh, sc, w, layer, mla_rest, fox_rest):
    tl = ROW_TILE
    row = lambda b, i: (b, i, 0)
    per_b = lambda b, i: (b, 0, 0)
    const = lambda b, i: (0, 0)
    mla_in, mla_out, mla_shape = _mla_prep_specs(tl)
    fox_in, fox_out, fox_shape = _fox_prep_specs(tl)
    mla_in, fox_in = mla_in[3:], fox_in[4:]
    outs = pl.pallas_call(
        functools.partial(_front_kernel, n_mla=len(mla_in), n_fox=len(fox_in)),
        grid=(BATCH, SEQ // tl),
        in_specs=[pl.BlockSpec((1, tl, D_MODEL), row),
                  pl.BlockSpec((1, D_MODEL), const),
                  pl.BlockSpec((1, 1, D_MODEL), per_b),
                  pl.BlockSpec((1, 1, D_MODEL), per_b),
                  pl.BlockSpec((1, D_MODEL, IN_PAD), lambda b, i: (layer, 0, 0))] + mla_in + fox_in,
        out_specs=[pl.BlockSpec((tl, SSM_WIDTH), lambda b, i: (i, b))] + mla_out + fox_out,
        out_shape=[jax.ShapeDtypeStruct((SEQ, BATCH * SSM_WIDTH), F32)] + mla_shape + fox_shape,
        scratch_shapes=[pltpu.VMEM((1, tl, IN_PAD), F32), pltpu.VMEM((SUBLANE, LANE), F32)],
        compiler_params=_params(("arbitrary", "arbitrary")),
        name="front",
    )(x, g, sh, sc, w, *mla_rest, *fox_rest)
    return outs[0], outs[1:4], outs[4:]


def _fox_placements():
    pq = np.zeros((FOX_HEADS // 2, 2 * LANE, 2 * HEAD_PAD), np.float32)
    pk = np.zeros_like(pq)
    pv = np.zeros_like(pq)
    one_row = LANE + GATE_ONE_LANE
    for j in range(FOX_HEADS // 2):
        for hh in range(2):
            h, col0 = 2 * j + hh, hh * HEAD_PAD
            for d in range(FOX_HEAD_DIM):
                for p in (pq, pk, pv):
                    p[j, hh * FOX_HEAD_DIM + d, col0 + d] = 1.0
            pv[j, one_row, col0 + ONES_LANE] = 1.0
            for n, piece_lane in enumerate((0, GATE_MID_LANE, GATE_LO_LANE)):
                pq[j, LANE + piece_lane + h, col0 + Q_GATE_LANE + n] = 1.0
                pq[j, one_row, col0 + K_GATE_LANE + n] = 1.0
                pk[j, one_row, col0 + Q_GATE_LANE + n] = 1.0
                pk[j, LANE + piece_lane + h, col0 + K_GATE_LANE + n] = -1.0
    return tuple(jnp.asarray(p, BF16) for p in (pq, pk, pv))


def _fox_operands(bf, gq, gk):
    q_scale = LOG2E / math.sqrt(FOX_HEAD_DIM)
    return (_pad_lanes(bf[None, :], LANE), jnp.tile(gq * q_scale, FOX_HEADS)[None, :],
            jnp.tile(gk, FOX_HEADS)[None, :]) + _fox_placements()


def _flash_kernel(qa_ref, qb_ref, kt_ref, v_ref, gap_ref, o_ref, q_scr, s_ref, m_ref, acc_ref,
                  *, tile, chunk, n_tiles):
    p = pl.program_id(2)
    tiles = (p, n_tiles - 1 - p)
    n_tasks = n_tiles + 1
    half = tile // 2
    top, bottom = slice(0, half), slice(half, tile)
    lane = lax.broadcasted_iota(jnp.int32, (tile, HEAD_PAD), 1)
    q_scr[0] = qa_ref[0]
    q_scr[1] = qb_ref[0]

    def plain_task(t):
        second = t - 2 >= p
        return second, second.astype(jnp.int32), jnp.where(second, t - 2 - p, t - 2)

    def row_max_update(w, hh, rows, s):
        mr = m_ref[w, hh, rows]
        for c in range(s.shape[1] // LANE):
            mr = jnp.maximum(mr, s[:, c * LANE:(c + 1) * LANE])
        m_ref[w, hh, rows] = mr

    m_ref[...] = jnp.full(m_ref.shape, NEG, F32)
    for w in range(2):
        for hh in range(2):
            kt = kt_ref[0, hh, tiles[w]]
            s_top = jnp.dot(q_scr[w, hh, top], kt[:, top], preferred_element_type=F32)
            s_top = jnp.where(gap_ref[top, top] <= 0, s_top, NEG)
            s_ref[hh, w, top, top] = s_top
            row_max_update(w, hh, top, s_top)
            s_bot = jnp.dot(q_scr[w, hh, bottom], kt, preferred_element_type=F32)
            s_bot = jnp.where(gap_ref[bottom, :] <= 0, s_bot, NEG)
            s_ref[hh, w, bottom] = s_bot
            row_max_update(w, hh, bottom, s_bot)
    for t in range(2, n_tasks):
        _, which, j = plain_task(jnp.int32(t))
        for hh in range(2):
            s = jnp.dot(q_scr[which, hh], kt_ref[0, hh, j], preferred_element_type=F32)
            s_ref[hh, t] = s
            row_max_update(which, hh, slice(None), s)

    ms = [[jnp.max(m_ref[w, hh], axis=1, keepdims=True) for hh in range(2)] for w in range(2)]

    acc_ref[...] = jnp.zeros(acc_ref.shape, F32)
    for w in range(2):
        k0 = pl.multiple_of(tiles[w] * tile, tile)
        for hh in range(2):
            pr = jnp.exp2(s_ref[hh, w, top, top] - ms[w][hh][top]).astype(BF16)
            acc_ref[w, hh, top] += jnp.dot(pr, v_ref[0, hh, pl.ds(k0, half), :], preferred_element_type=F32)
            pr = jnp.exp2(s_ref[hh, w, bottom] - ms[w][hh][bottom]).astype(BF16)
            acc_ref[w, hh, bottom] += jnp.dot(pr, v_ref[0, hh, pl.ds(k0, tile), :], preferred_element_type=F32)
    for t in range(2, n_tasks):
        second, which, j = plain_task(jnp.int32(t))
        k0 = pl.multiple_of(j * tile, tile)
        for hh in range(2):
            row_max = jnp.where(second, ms[1][hh], ms[0][hh])
            pr = jnp.exp2(s_ref[hh, t] - row_max).astype(BF16)
            acc_ref[which, hh] += jnp.dot(pr, v_ref[0, hh, pl.ds(k0, tile), :], preferred_element_type=F32)

    for w in range(2):
        outs = [acc_ref[w, hh] / acc_ref[w, hh][:, ONES_LANE:ONES_LANE + 1] for hh in range(2)]
        o_ref[0, w, 0] = jnp.where(lane < 64, outs[0], pltpu.roll(outs[1], 64, 1)).astype(BF16)


def _flash_call(q, kt, v, chunk):
    tile = ATT_TILE
    heads = q.shape[1]
    n_tiles = SEQ // tile
    pos = np.arange(tile, dtype=np.int32) // chunk
    gap = jnp.asarray(pos[None, :] - pos[:, None])
    return pl.pallas_call(
        functools.partial(_flash_kernel, tile=tile, chunk=chunk, n_tiles=n_tiles),
        grid=(BATCH, heads // 2, n_tiles // 2),
        in_specs=[pl.BlockSpec((1, 2, tile, HEAD_PAD), lambda b, hp, p: (b, hp, p, 0)),
                  pl.BlockSpec((1, 2, tile, HEAD_PAD), lambda b, hp, p: (b, hp, n_tiles - 1 - p, 0)),
                  pl.BlockSpec((1, 2, n_tiles, HEAD_PAD, tile), lambda b, hp, p: (b, hp, 0, 0, 0)),
                  pl.BlockSpec((1, 2, SEQ, HEAD_PAD), lambda b, hp, p: (b, hp, 0, 0)),
                  pl.BlockSpec((tile, tile), lambda b, hp, p: (0, 0))],
        out_specs=pl.BlockSpec((1, 2, 1, tile, LANE), lambda b, hp, p: (b, 0, p, 0, hp)),
        out_shape=jax.ShapeDtypeStruct((BATCH, 2, n_tiles // 2, tile, ATT_WIDTH), BF16),
        scratch_shapes=[pltpu.VMEM((2, 2, tile, HEAD_PAD), BF16),
                        pltpu.VMEM((2, n_tiles + 1, tile, tile), F32),
                        pltpu.VMEM((2, 2, tile, LANE), F32),
                        pltpu.VMEM((2, 2, tile, HEAD_PAD), F32)],
        compiler_params=_params(("arbitrary", "arbitrary", "arbitrary")),
        name="flash_chunk%d" % chunk,
    )(q, q, kt, v, gap)


def _mixed(ssm, mla_ref, fox_ref, gm_ref, gf_ref, w_ref):
    def normed(ref, g_ref):
        a = ref[0, 0, 0].astype(F32)
        return (a * lax.rsqrt(jnp.mean(a * a, axis=-1, keepdims=True) + EPS) * g_ref[...]).astype(BF16)

    merged = jnp.concatenate([ssm, normed(mla_ref, gm_ref), normed(fox_ref, gf_ref)], axis=1)
    return jnp.dot(merged, w_ref[...], preferred_element_type=F32)


def _att_tile_spec(tile_of):
    half = SEQ // ATT_TILE // 2

    def index(b, i):
        t = tile_of(i)
        return (b, t // half, jnp.where(t < half, t, 2 * half - 1 - t), 0, 0)
    return pl.BlockSpec((1, 1, 1, ATT_TILE, ATT_WIDTH), index)


def _merge_ffn_kernel(ssm_ref, mla0_ref, mla1_ref, fox0_ref, fox1_ref, x_ref, g1_ref, gm_ref, gf_ref, wo_ref,
                      g_ref, sh_ref, sc_ref, g2_ref, wg_ref, wu_ref, wd_ref, o_ref):
    half = x_ref.shape[1] // 2
    att = ((mla0_ref, fox0_ref), (mla1_ref, fox1_ref))
    for n, r in enumerate((slice(0, half), slice(half, 2 * half))):
        x = x_ref[0, r] + g1_ref[0] * _mixed(ssm_ref[r, :], *att[n], gm_ref, gf_ref, wo_ref)
        h = _rms_mod(x, g_ref[...], sc_ref[0], sh_ref[0]).astype(BF16)
        gate_up = [(jnp.dot(h, wg_ref[0, :, c0:c1], preferred_element_type=F32),
                    jnp.dot(h, wu_ref[0, :, c0:c1], preferred_element_type=F32)) for c0, c1 in FF_PARTS]
        acc = jnp.zeros((half, D_MODEL), F32)
        for (c0, c1), (gate, up) in zip(FF_PARTS, gate_up):
            a = (gate * jax.nn.sigmoid(gate) * up).astype(BF16)
            acc = acc + jnp.dot(a, wd_ref[0, c0:c1, :], preferred_element_type=F32)
        o_ref[0, r] = x + g2_ref[0] * acc


def _merge_ffn_call(o_ssm, o_mla, o_fox, x, g1, gm, gf, wo, g, sh, sc, g2, wg, wu, wd, layer):
    tm = 2 * ATT_TILE
    row = lambda b, i: (b, i, 0)
    per_b = lambda b, i: (b, 0, 0)
    c2 = lambda b, i: (0, 0)
    once = pl.Buffered(1)
    resident = lambda shape: pl.BlockSpec(shape, lambda b, i: (layer, 0, 0), pipeline_mode=once)
    att0, att1 = _att_tile_spec(lambda i: 2 * i), _att_tile_spec(lambda i: 2 * i + 1)
    return pl.pallas_call(
        _merge_ffn_kernel,
        grid=(BATCH, SEQ // tm),
        in_specs=[pl.BlockSpec((tm, SSM_WIDTH), lambda b, i: (i, b)),
                  att0, att1, att0, att1,
                  pl.BlockSpec((1, tm, D_MODEL), row),
                  pl.BlockSpec((1, 1, D_MODEL), per_b),
                  pl.BlockSpec((1, ATT_WIDTH), c2),
                  pl.BlockSpec((1, ATT_WIDTH), c2),
                  pl.BlockSpec((D_MODEL, D_MODEL), c2, pipeline_mode=once),
                  pl.BlockSpec((1, D_MODEL), c2),
                  pl.BlockSpec((1, 1, D_MODEL), per_b),
                  pl.BlockSpec((1, 1, D_MODEL), per_b),
                  pl.BlockSpec((1, 1, D_MODEL), per_b),
                  resident((1, D_MODEL, D_FF)),
                  resident((1, D_MODEL, D_FF)),
                  resident((1, D_FF, D_MODEL))],
        out_specs=pl.BlockSpec((1, tm, D_MODEL), row),
        out_shape=jax.ShapeDtypeStruct((BATCH, SEQ, D_MODEL), F32),
        compiler_params=_params(("arbitrary", "arbitrary")),
        name="merge_ffn",
    )(o_ssm, o_mla, o_mla, o_fox, o_fox, x, g1, gm, gf, wo, g, sh, sc, g2, wg, wu, wd)


def _router_kernel(x_ref, g_ref, sh_ref, sc_ref, w_ref, b_ref, comb_ref, rank_ref, rankt_ref, count_ref, h_ref):
    tm = x_ref.shape[1]
    h = _rms_mod(x_ref[0], g_ref[...], sc_ref[0], sh_ref[0])
    h_hi = h.astype(BF16)
    h_ref[0] = h_hi
    h_lo = (h - h_hi.astype(F32)).astype(BF16)
    parts = jnp.dot(jnp.concatenate([h_hi, h_lo], axis=1), w_ref[...], preferred_element_type=F32)
    logits = parts + pltpu.roll(parts, LANE - N_EXPERTS, 1) + b_ref[...]
    lane = lax.broadcasted_iota(jnp.int32, logits.shape, 1)
    logits = jnp.where(lane < N_EXPERTS, logits, -jnp.inf)
    m1 = jnp.max(logits, axis=-1, keepdims=True)
    i1 = jnp.min(jnp.where(logits == m1, lane, LANE), axis=-1, keepdims=True)
    rest = jnp.where(lane == i1, -jnp.inf, logits)
    m2 = jnp.max(rest, axis=-1, keepdims=True)
    i2 = jnp.min(jnp.where(rest == m2, lane, LANE), axis=-1, keepdims=True)
    e = jnp.exp(m2 - m1)
    p1 = 1.0 / (1.0 + e)
    comb_ref[0] = jnp.where(lane == i1, p1, 0.0) + jnp.where(lane == i2, e * p1, 0.0)

    chosen = (lane == i1) | (lane == i2)
    chosen_f = jnp.where(chosen, 1.0, 0.0)
    r_i = lax.broadcasted_iota(jnp.int32, (tm, tm), 0)
    c_i = lax.broadcasted_iota(jnp.int32, (tm, tm), 1)
    earlier = jnp.where(c_i < r_i, 1.0, 0.0).astype(BF16)
    rank = jnp.dot(earlier, chosen_f.astype(BF16), preferred_element_type=F32)
    rank = jnp.where(chosen, rank, -1.0)
    rank_ref[0] = rank
    rankt_ref[0] = rank.T[0:SUBLANE, :]
    count_ref[0] = jnp.sum(chosen_f, axis=0, keepdims=True)


def _merge_router_kernel(ssm_ref, mla0_ref, mla1_ref, fox0_ref, fox1_ref, x_ref, g1_ref, gm_ref, gf_ref, wo_ref,
                         g_ref, sh_ref, sc_ref, w_ref, b_ref, xo_ref, comb_ref, rank_ref, rankt_ref, count_ref,
                         h_ref):
    half = x_ref.shape[1] // 2
    att = ((mla0_ref, fox0_ref), (mla1_ref, fox1_ref))
    for n, r in enumerate((slice(0, half), slice(half, 2 * half))):
        xo_ref[0, r] = x_ref[0, r] + g1_ref[0] * _mixed(ssm_ref[r, :], *att[n], gm_ref, gf_ref, wo_ref)
    _router_kernel(xo_ref, g_ref, sh_ref, sc_ref, w_ref, b_ref, comb_ref, rank_ref, rankt_ref, count_ref, h_ref)


def _merge_router_call(o_ssm, o_mla, o_fox, x, g1, gm, gf, wo, g, sh, sc, w, b):
    tm = MOE_TILE
    tiles = SEQ // tm
    row = lambda b_, i: (b_, i, 0)
    per_b = lambda b_, i: (b_, 0, 0)
    per_tile = lambda b_, i: (b_ * tiles + i, 0, 0)
    c2 = lambda b_, i: (0, 0)
    att0, att1 = _att_tile_spec(lambda i: 2 * i), _att_tile_spec(lambda i: 2 * i + 1)
    return pl.pallas_call(
        _merge_router_kernel,
        grid=(BATCH, tiles),
        in_specs=[pl.BlockSpec((tm, SSM_WIDTH), lambda b_, i: (i, b_)),
                  att0, att1, att0, att1,
                  pl.BlockSpec((1, tm, D_MODEL), row),
                  pl.BlockSpec((1, 1, D_MODEL), per_b),
                  pl.BlockSpec((1, ATT_WIDTH), c2),
                  pl.BlockSpec((1, ATT_WIDTH), c2),
                  pl.BlockSpec((D_MODEL, D_MODEL), c2),
                  pl.BlockSpec((1, D_MODEL), c2),
                  pl.BlockSpec((1, 1, D_MODEL), per_b),
                  pl.BlockSpec((1, 1, D_MODEL), per_b),
                  pl.BlockSpec((2 * D_MODEL, LANE), c2),
                  pl.BlockSpec((1, LANE), c2)],
        out_specs=[pl.BlockSpec((1, tm, D_MODEL), row),
                   pl.BlockSpec((1, tm, LANE), row),
                   pl.BlockSpec((1, tm, LANE), row),
                   pl.BlockSpec((1, SUBLANE, tm), per_tile),
                   pl.BlockSpec((1, 1, LANE), per_tile),
                   pl.BlockSpec((1, tm, D_MODEL), row)],
        out_shape=[jax.ShapeDtypeStruct((BATCH, SEQ, D_MODEL), F32),
                   jax.ShapeDtypeStruct((BATCH, SEQ, LANE), F32),
                   jax.ShapeDtypeStruct((BATCH, SEQ, LANE), F32),
                   jax.ShapeDtypeStruct((BATCH * tiles, SUBLANE, tm), F32),
                   jax.ShapeDtypeStruct((BATCH * tiles, 1, LANE), F32),
                   jax.ShapeDtypeStruct((BATCH, SEQ, D_MODEL), BF16)],
        compiler_params=_params(("arbitrary", "arbitrary")),
        name="merge_router",
    )(o_ssm, o_mla, o_mla, o_fox, o_fox, x, g1, gm, gf, wo, g, sh, sc, w, b)


def _moe_kernel(count_ref, x_ref, h_ref, g2_ref, comb_ref, rank_ref, rankt_ref,
                wg_ref, wu_ref, wt_ref, wd_ref, o_ref):
    tm = x_ref.shape[1]
    e = pl.program_id(1)

    @pl.when(e == 0)
    def _():
        o_ref[0] = x_ref[0]

    lane = lax.broadcasted_iota(jnp.int32, (tm, LANE), 1)
    mine = lane == e
    rank_col = jnp.sum(jnp.where(mine, rank_ref[0], 0.0), axis=-1, keepdims=True)
    gate_col = jnp.sum(jnp.where(mine, comb_ref[0], 0.0), axis=-1, keepdims=True)
    rank_row = rankt_ref[0, pl.ds(e, 1), :]
    count = count_ref[pl.program_id(0) * N_EXPERTS + e]

    def expert_pass(first, n_rows):
        base = first.astype(F32)
        slot_sub = lax.broadcasted_iota(jnp.int32, (n_rows, tm), 0).astype(F32)
        slot_lane = lax.broadcasted_iota(jnp.int32, (tm, n_rows), 1).astype(F32)
        pick = jnp.where(rank_row - base == slot_sub, 1.0, 0.0).astype(BF16)
        rows = jnp.dot(pick, h_ref[0], preferred_element_type=F32).astype(BF16)
        gate = jnp.dot(rows, wg_ref[0, 0, :, :EXPERT_MAIN], preferred_element_type=F32)
        up = jnp.dot(rows, wu_ref[0, 0, :, :EXPERT_MAIN], preferred_element_type=F32)
        tail = jnp.dot(rows, wt_ref[0, 0], preferred_element_type=F32)
        gate_t, up_t = tail[:, :LANE], tail[:, LANE:]
        a = jnp.concatenate([gate * jax.nn.sigmoid(gate) * up, gate_t * jax.nn.sigmoid(gate_t) * up_t],
                            axis=1).astype(BF16)
        y = jnp.dot(a, wd_ref[0, 0], preferred_element_type=F32).astype(BF16)
        place = jnp.where(rank_col - base == slot_lane, 1.0, 0.0).astype(BF16)
        back = jnp.dot(place, y, preferred_element_type=F32)
        o_ref[0] += g2_ref[0] * (gate_col * back)

    def full_pass(sb, carry):
        expert_pass(sb * MOE_ROWS, MOE_ROWS)
        return carry

    n_full = count // MOE_ROWS
    lax.fori_loop(0, n_full, full_pass, 0)
    left = count - n_full * MOE_ROWS

    @pl.when(left > MOE_ROWS // 2)
    def _():
        expert_pass(n_full * MOE_ROWS, MOE_ROWS)

    @pl.when((left > 0) & (left <= MOE_ROWS // 2))
    def _():
        expert_pass(n_full * MOE_ROWS, MOE_ROWS // 2)


def _moe_call(x, h, g2, comb, rank, rankt, counts, wg, wu, wt, wd, layer):
    tm = MOE_TILE
    tiles = SEQ // tm
    n_tiles = BATCH * tiles
    row = lambda i, e, cnt: (i, 0, 0)
    per_b = lambda i, e, cnt: (i // tiles, 0, 0)
    expert = lambda i, e, cnt: (layer, e, 0, 0)
    as_tiles = lambda a: a.reshape(n_tiles, tm, a.shape[-1])
    grid_spec = pltpu.PrefetchScalarGridSpec(
        num_scalar_prefetch=1,
        grid=(n_tiles, N_EXPERTS),
        in_specs=[pl.BlockSpec((1, tm, D_MODEL), row),
                  pl.BlockSpec((1, tm, D_MODEL), row),
                  pl.BlockSpec((1, 1, D_MODEL), per_b),
                  pl.BlockSpec((1, tm, LANE), row),
                  pl.BlockSpec((1, tm, LANE), row),
                  pl.BlockSpec((1, SUBLANE, tm), row),
                  pl.BlockSpec((1, 1, D_MODEL, D_FF_EXPERT), expert),
                  pl.BlockSpec((1, 1, D_MODEL, D_FF_EXPERT), expert),
                  pl.BlockSpec((1, 1, D_MODEL, 2 * LANE), expert),
                  pl.BlockSpec((1, 1, D_FF_EXPERT, D_MODEL), expert)],
        out_specs=pl.BlockSpec((1, tm, D_MODEL), row),
    )
    out = pl.pallas_call(
        _moe_kernel,
        grid_spec=grid_spec,
        out_shape=jax.ShapeDtypeStruct((n_tiles, tm, D_MODEL), F32),
        compiler_params=_params(("arbitrary", "arbitrary")),
        name="moe_experts",
    )(counts, as_tiles(x), as_tiles(h), g2, as_tiles(comb), as_tiles(rank), rankt, wg, wu, wt, wd)
    return out.reshape(BATCH, SEQ, D_MODEL)


def kernel(x, c, positions, norm_mix, norm_ffn, w_ada, b_ada, w_in, ssm_lam_re, ssm_lam_im, ssm_log_dt, ssm_b_re, ssm_b_im, ssm_c_re, ssm_c_im, ssm_d, ssm_w_glu, ssm_b_glu, mla_q_norm, mla_kv_norm, mla_w_uq, mla_w_ukv, mla_qk_gq, mla_qk_gk, fox_b_f, fox_qk_gq, fox_qk_gk, out_norm, w_out, ffn_w_gate, ffn_w_up, ffn_w_down, moe_w_router, moe_b_router, moe_w_gate, moe_w_up, moe_w_down):
    tabs = _rope_tables(positions)
    w_in_packed = _pack_w_in(w_in)
    moe_wg, moe_wu, moe_wd = (w.astype(BF16) for w in (moe_w_gate, moe_w_up, moe_w_down))
    moe_wt = jnp.concatenate([moe_wg[..., EXPERT_MAIN:], moe_wu[..., EXPERT_MAIN:]], axis=-1)
    ffn_wg, ffn_wu, ffn_wd = (w.astype(BF16) for w in (ffn_w_gate, ffn_w_up, ffn_w_down))
    ada = _ada_call(c, w_ada, b_ada)
    ada = ada.reshape(DEPTH, BATCH, 6, 1, D_MODEL)
    row2 = lambda a: a[None, :]

    for i in range(DEPTH):
        sh1, sc1, g1, sh2, sc2, g2 = (ada[i, :, n] for n in range(6))

        u, mla_qkv, fox_qkv = _front_call(
            x, row2(norm_mix[i]), sh1, sc1, w_in_packed, i,
            (*tabs, row2(mla_q_norm[i]), row2(mla_kv_norm[i]),
             *_mla_weights(mla_w_uq[i], mla_w_ukv[i], mla_qk_gq[i], mla_qk_gk[i])),
            _fox_operands(fox_b_f[i], fox_qk_gq[i], fox_qk_gk[i]))

        bmat, lam, cmat = _s5_operands(ssm_lam_re[i], ssm_lam_im[i], ssm_log_dt[i],
                                       ssm_b_re[i], ssm_b_im[i], ssm_c_re[i], ssm_c_im[i])
        o_ssm = _s5_call(u, bmat, lam, cmat, row2(ssm_d[i]), ssm_w_glu[i].astype(BF16),
                         row2(ssm_b_glu[i]), row2(out_norm[i, :SSM_WIDTH]))

        o_mla = _flash_call(*mla_qkv, CHUNK)
        o_fox = _flash_call(*fox_qkv, 1)

        e1, e2 = SSM_WIDTH, SSM_WIDTH + ATT_WIDTH
        merge_args = (o_ssm, o_mla, o_fox, x, g1, row2(out_norm[i, e1:e2]), row2(out_norm[i, e2:]),
                      w_out[i].astype(BF16))

        j = i // 2
        if i % 2 == 0:
            x = _merge_ffn_call(*merge_args, row2(norm_ffn[i]), sh2, sc2, g2, ffn_wg, ffn_wu, ffn_wd, layer=j)
        else:
            wr_hi = moe_w_router[j].astype(BF16)
            wr_lo = (moe_w_router[j] - wr_hi.astype(F32)).astype(BF16)
            wr = _pad_lanes(jnp.concatenate([wr_hi, wr_lo], axis=1), LANE)
            x, comb, rank, rankt, counts, h = _merge_router_call(
                *merge_args, row2(norm_ffn[i]), sh2, sc2, jnp.concatenate([wr, wr], axis=0),
                _pad_lanes(row2(moe_b_router[j]), LANE))
            counts = counts[:, 0, :N_EXPERTS].astype(jnp.int32).reshape(-1)
            x = _moe_call(x, h, g2, comb, rank, rankt, counts, moe_wg, moe_wu, moe_wt, moe_wd, layer=j)
    return x
```
